```python
import math
import jax
import jax.numpy as jnp
from jax import lax
import numpy as np

D_MODEL = 1024
BATCH = 4
SEQ = 4096
DEPTH = 1
DEC_BATCH = 32
DEC_SEQ = 4
PAST_LEN = 8192
PAGE_SIZE = 128

N_HEADS = 8
N_KV_HEADS = 2
HEAD_DIM = 64
GROUP = N_HEADS // N_KV_HEADS
ATT_WIDTH = N_HEADS * HEAD_DIM
KV_WIDTH = N_KV_HEADS * HEAD_DIM
N_NSA_BRANCH = 3
CMP_BLOCK = 32
CMP_STRIDE = 16
CMP_HIDDEN = 256
SEL_BLOCK = 64
SEL_TOPK = 16
WINDOW = 512
Q_BLOCK = 128
N_BUCKETS = 32
REL_MAX_DIST = 128
SSM_WIDTH = 512
SSM_GROUP_CH = 16
N_SSM_GROUPS = SSM_WIDTH // SSM_GROUP_CH
SSM_STATE = 64
D_FF = ((8 * D_MODEL + 3 * 256 - 1) // (3 * 256)) * 256
PLE_DIM = 256
EPS = 1e-6
NEG_INF = -1e30
FORCE_SCORE = 1e9

kernel_name = 'nsa_s5_gated_hybrid_step'


def rmsnorm(x, g):
    xf = x.astype(jnp.float32)
    y = xf * lax.rsqrt(jnp.mean(xf * xf, axis=-1, keepdims=True) + EPS)
    return (y * g.astype(jnp.float32)).astype(x.dtype)


def rel_bucket(dist):
    n = jnp.maximum(dist, 0)
    exact = N_BUCKETS // 2
    nf = jnp.maximum(n, 1).astype(jnp.float32)
    large = exact + (jnp.log(nf / exact) / math.log(REL_MAX_DIST / exact) * (N_BUCKETS - exact)).astype(jnp.int32)
    return jnp.where(n < exact, n, jnp.minimum(large, N_BUCKETS - 1))


def head_bias(rel_bias, dist):
    b = rel_bias[rel_bucket(dist)].astype(jnp.float32)
    tq, nk = dist.shape
    return jnp.transpose(b, (2, 0, 1)).reshape(N_KV_HEADS, GROUP, tq, nk)


def masked_softmax(s, mask):
    s = jnp.where(mask, s, NEG_INF)
    e = jnp.exp(s - jnp.max(s, axis=-1, keepdims=True)) * mask
    return e / jnp.maximum(jnp.sum(e, axis=-1, keepdims=True), 1e-30)


def compress(rows, pe, w1, w2):
    b, length = rows.shape[:2]
    n_chunks = length // CMP_STRIDE
    r = CMP_BLOCK // CMP_STRIDE
    n_cmp = n_chunks - r + 1
    ch = rows[:, :n_chunks * CMP_STRIDE].reshape(b, n_chunks, CMP_STRIDE, N_KV_HEADS, HEAD_DIM)
    blocks = jnp.concatenate([ch[:, i:i + n_cmp] for i in range(r)], axis=2)
    blocks = blocks + pe[:, None, :]
    flat = jnp.transpose(blocks, (0, 1, 3, 2, 4)).reshape(b, n_cmp, N_KV_HEADS, CMP_BLOCK * HEAD_DIM)
    return jax.nn.gelu(flat @ w1) @ w2


def sel_blocks(rows):
    b, length = rows.shape[:2]
    n_sel = -(-length // SEL_BLOCK)
    rows = jnp.pad(rows, ((0, 0), (0, n_sel * SEL_BLOCK - length), (0, 0), (0, 0)))
    return jnp.transpose(rows.reshape(b, n_sel, SEL_BLOCK, N_KV_HEADS, HEAD_DIM), (0, 3, 1, 2, 4))


def nsa_context(kc_rows, vc_rows, ks_rows, vs_rows, lw):
    kc = compress(kc_rows, lw['cmp_pe_k'], lw['cmp_w1_k'], lw['cmp_w2_k'])
    vc = compress(vc_rows, lw['cmp_pe_v'], lw['cmp_w1_v'], lw['cmp_w2_v'])
    c_start = jnp.arange(kc.shape[1]) * CMP_STRIDE
    c_end = c_start + CMP_BLOCK - 1
    ksb = sel_blocks(ks_rows)
    vsb = sel_blocks(vs_rows)
    s_start = jnp.arange(ksb.shape[2]) * SEL_BLOCK
    overlap = ((c_start[:, None] < s_start[None, :] + SEL_BLOCK) & (c_end[:, None] >= s_start[None, :])).astype(jnp.float32)
    return kc, vc, c_end, overlap, ksb, vsb


def nsa_block(q, q_pos, kc, vc, c_end, overlap, ksb, vsb, kw, vw, w_pos, rel_bias):
    b, tq = q.shape[:2]
    n_sel = ksb.shape[2]
    scale = HEAD_DIM ** -0.5
    tcol = q_pos[:, None]
    s_c = jnp.einsum('btkgd,bnkd->bkgtn', q, kc).astype(jnp.float32) * scale + head_bias(rel_bias, tcol - c_end[None, :])
    p_c = masked_softmax(s_c, c_end[None, :] <= tcol)
    o_c = jnp.einsum('bkgtn,bnkd->btkgd', p_c.astype(vc.dtype), vc)
    imp = jnp.einsum('bkgtn,nj->bktj', p_c, overlap)
    blk = jnp.arange(n_sel)[None, :]
    cur = tcol // SEL_BLOCK
    forced = (blk == 0) | (blk == cur) | (blk == cur - 1)
    valid = blk * SEL_BLOCK <= tcol
    score = jnp.where(valid, jnp.where(forced, FORCE_SCORE, imp), NEG_INF)
    _, idx = lax.top_k(score, min(SEL_TOPK, n_sel))
    b_ar = jnp.arange(b)[:, None, None, None]
    kv_ar = jnp.arange(N_KV_HEADS)[None, :, None, None]
    kg = ksb[b_ar, kv_ar, idx]
    vg = vsb[b_ar, kv_ar, idx]
    pos_s = idx[..., None] * SEL_BLOCK + jnp.arange(SEL_BLOCK)
    tb = jnp.transpose(rel_bias.reshape(N_BUCKETS, N_KV_HEADS, GROUP), (1, 0, 2))
    bias_s = jnp.moveaxis(tb[kv_ar[..., None], rel_bucket(q_pos[:, None, None] - pos_s)], -1, 2).astype(jnp.float32)
    s_s = jnp.einsum('btkgd,bktnsd->bkgtns', q, kg).astype(jnp.float32) * scale + bias_s
    mask_s = (pos_s <= q_pos[:, None, None])[:, :, None]
    p_s = masked_softmax(s_s.reshape(b, N_KV_HEADS, GROUP, tq, -1), mask_s.reshape(b, N_KV_HEADS, 1, tq, -1)).reshape(s_s.shape)
    o_s = jnp.einsum('bkgtns,bktnsd->btkgd', p_s.astype(vg.dtype), vg)
    dw = tcol - w_pos[None, :]
    s_w = jnp.einsum('btkgd,bwkd->bkgtw', q, kw).astype(jnp.float32) * scale + head_bias(rel_bias, dw)
    p_w = masked_softmax(s_w, (dw >= 0) & (dw < WINDOW) & (w_pos[None, :] >= 0))
    o_w = jnp.einsum('bkgtw,bwkd->btkgd', p_w.astype(vw.dtype), vw)
    return o_c, o_s, o_w


def nsa_prompt(q, ctx, kw, vw, rel_bias):
    kc, vc, c_end, overlap, ksb, vsb = ctx
    b, t = q.shape[:2]
    pad = ((0, 0), (WINDOW, 0), (0, 0), (0, 0))
    kw_pad = jnp.pad(kw, pad)
    vw_pad = jnp.pad(vw, pad)
    band = WINDOW + Q_BLOCK

    def block(i):
        start = i * Q_BLOCK
        qb = lax.dynamic_slice_in_dim(q, start, Q_BLOCK, axis=1)
        kwb = lax.dynamic_slice_in_dim(kw_pad, start, band, axis=1)
        vwb = lax.dynamic_slice_in_dim(vw_pad, start, band, axis=1)
        q_pos = start + jnp.arange(Q_BLOCK)
        w_pos = start - WINDOW + jnp.arange(band)
        return nsa_block(qb, q_pos, kc, vc, c_end, overlap, ksb, vsb, kwb, vwb, w_pos, rel_bias)

    outs = lax.map(block, jnp.arange(t // Q_BLOCK))
    return tuple(jnp.swapaxes(o, 0, 1).reshape(b, t, N_KV_HEADS, GROUP, HEAD_DIM) for o in outs)


def ssm_combine(e1, e2):
    a1r, a1i, b1r, b1i = e1
    a2r, a2i, b2r, b2i = e2
    return (a2r * a1r - a2i * a1i, a2r * a1i + a2i * a1r,
            a2r * b1r - a2i * b1i + b2r, a2r * b1i + a2i * b1r + b2i)


def ssm_branch(u, h0_re, h0_im, lw):
    b, t = u.shape[:2]
    ar = lw['ssm_a_re'].astype(jnp.float32)
    ai = lw['ssm_a_im'].astype(jnp.float32)
    dt = jnp.exp(lw['ssm_log_dt'].astype(jnp.float32))[:, None]
    mag = jnp.exp(ar * dt)
    abr, abi = mag * jnp.cos(ai * dt), mag * jnp.sin(ai * dt)
    den = ar * ar + ai * ai
    nr, ni = abr - 1.0, abi
    fr, fi = (nr * ar + ni * ai) / den, (ni * ar - nr * ai) / den
    br = lw['ssm_b_re'].astype(jnp.float32)
    bi = lw['ssm_b_im'].astype(jnp.float32)
    bbr = fr[..., None] * br - fi[..., None] * bi
    bbi = fr[..., None] * bi + fi[..., None] * br
    uf = u.astype(jnp.float32)
    xr = jnp.einsum('btgc,gpc->btgp', uf, bbr)
    xi = jnp.einsum('btgc,gpc->btgp', uf, bbi)
    h0r = h0_re.astype(jnp.float32)
    h0i = h0_im.astype(jnp.float32)
    xr = xr.at[:, 0].add(abr * h0r - abi * h0i)
    xi = xi.at[:, 0].add(abr * h0i + abi * h0r)
    shp = xr.shape
    _, _, hr, hi = lax.associative_scan(ssm_combine, (jnp.broadcast_to(abr, shp), jnp.broadcast_to(abi, shp), xr, xi), axis=1)
    y = (jnp.einsum('btgp,gcp->btgc', hr, lw['ssm_c_re'].astype(jnp.float32))
         - jnp.einsum('btgp,gcp->btgc', hi, lw['ssm_c_im'].astype(jnp.float32))
         + lw['ssm_d'].astype(jnp.float32) * uf)
    z = jax.nn.gelu(y.reshape(b, t, SSM_WIDTH)).astype(u.dtype)
    out = z * jax.nn.sigmoid(z @ lw['w_glu'] + lw['b_glu'])
    return out, hr[:, -1], hi[:, -1]


def layer_front(h, lw):
    b, t = h.shape[:2]
    z = rmsnorm(h, lw['attn_norm']) @ lw['w_in']
    sizes = [ATT_WIDTH] + [KV_WIDTH] * 6 + [N_HEADS * N_NSA_BRANCH, SSM_WIDTH, 2 * D_MODEL]
    cuts = [int(c) for c in np.cumsum(sizes)[:-1]]
    q, kc, vc, ks, vs, kw, vw, ng, su, mg = jnp.split(z, cuts, axis=-1)
    kvr = (b, t, N_KV_HEADS, HEAD_DIM)
    return (q.reshape(b, t, N_KV_HEADS, GROUP, HEAD_DIM), kc.reshape(kvr), vc.reshape(kvr), ks.reshape(kvr), vs.reshape(kvr),
            kw.reshape(kvr), vw.reshape(kvr), jax.nn.sigmoid(ng).reshape(b, t, N_KV_HEADS, GROUP, N_NSA_BRANCH),
            su.reshape(b, t, N_SSM_GROUPS, SSM_GROUP_CH), jax.nn.sigmoid(mg).reshape(b, t, 2, D_MODEL))


def layer_back(h, outs, ng, so, mg, p_l, lw):
    o_c, o_s, o_w = outs
    b, t = h.shape[:2]
    o = ng[..., 0:1] * o_c + ng[..., 1:2] * o_s + ng[..., 2:3] * o_w
    a = o.reshape(b, t, ATT_WIDTH) @ lw['w_att_br']
    s = so @ lw['w_ssm_br']
    h = h + (mg[:, :, 0] * a + mg[:, :, 1] * s) @ lw['w_o']
    f = rmsnorm(h, lw['ffn_norm'])
    h = h + (jax.nn.silu(f @ lw['w_ffn_gate']) * (f @ lw['w_ffn_up'])) @ lw['w_ffn_down']
    g = jax.nn.sigmoid(rmsnorm(h, lw['ple_norm']) @ lw['w_ple_gate'])
    return h + g * (p_l @ lw['w_ple'])


def with_past(pool, page_table, new):
    rows = pool[page_table]
    bd, n_pages, page = rows.shape[:3]
    return jnp.concatenate([rows.reshape(bd, n_pages * page, N_KV_HEADS, HEAD_DIM), new], axis=1)


def setup_inputs(seed: int = 0) -> dict:
    key = jax.random.key(seed)
    keys = iter(jax.random.split(key, 64))
    nrm = lambda shape, scale=1.0: scale * jax.random.normal(next(keys), shape, jnp.float32)
    n_pages = PAST_LEN // PAGE_SIZE
    n_used = DEC_BATCH * n_pages
    n_phys = (5 * n_used + 3) // 4
    win_buf = min(WINDOW, PAST_LEN)
    paged = (DEPTH, n_phys, PAGE_SIZE, N_KV_HEADS, HEAD_DIM)
    wbuf = (DEPTH, DEC_BATCH, win_buf, N_KV_HEADS, HEAD_DIM)
    sst = (DEPTH, DEC_BATCH, N_SSM_GROUPS, SSM_STATE)
    gp = (DEPTH, N_SSM_GROUPS, SSM_STATE)
    return {
        'x_prompt': nrm((BATCH, SEQ, D_MODEL)),
        'x_sample': nrm((DEC_BATCH, DEC_SEQ, D_MODEL)),
        'cache_k_cmp': nrm(paged),
        'cache_v_cmp': nrm(paged),
        'cache_k_sel': nrm(paged),
        'cache_v_sel': nrm(paged),
        'cache_k_win': nrm(wbuf),
        'cache_v_win': nrm(wbuf),
        'state_ssm_re': nrm(sst, 0.3),
        'state_ssm_im': nrm(sst, 0.3),
        'page_table': jax.random.permutation(next(keys), n_phys)[:n_used].reshape(DEC_BATCH, n_pages).astype(jnp.int32),
        'p_prompt': nrm((DEPTH, BATCH, SEQ, PLE_DIM)),
        'p_sample': nrm((DEPTH, DEC_BATCH, DEC_SEQ, PLE_DIM)),
        'rel_bias': nrm((N_BUCKETS, N_HEADS), 0.5),
        'final_norm': 1.0 + nrm((D_MODEL,), 0.02),
        'attn_norm': 1.0 + nrm((DEPTH, D_MODEL), 0.02),
        'w_in': nrm((DEPTH, D_MODEL, ATT_WIDTH + 6 * KV_WIDTH + N_HEADS * N_NSA_BRANCH + SSM_WIDTH + 2 * D_MODEL), D_MODEL ** -0.5),
        'cmp_pe_k': nrm((DEPTH, CMP_BLOCK, HEAD_DIM), 0.1),
        'cmp_w1_k': nrm((DEPTH, CMP_BLOCK * HEAD_DIM, CMP_HIDDEN), (CMP_BLOCK * HEAD_DIM) ** -0.5),
        'cmp_w2_k': nrm((DEPTH, CMP_HIDDEN, HEAD_DIM), CMP_HIDDEN ** -0.5),
        'cmp_pe_v': nrm((DEPTH, CMP_BLOCK, HEAD_DIM), 0.1),
        'cmp_w1_v': nrm((DEPTH, CMP_BLOCK * HEAD_DIM, CMP_HIDDEN), (CMP_BLOCK * HEAD_DIM) ** -0.5),
        'cmp_w2_v': nrm((DEPTH, CMP_HIDDEN, HEAD_DIM), CMP_HIDDEN ** -0.5),
        'ssm_a_re': -0.5 + nrm(gp, 0.01),
        'ssm_a_im': jnp.pi * jnp.broadcast_to(jnp.arange(SSM_STATE, dtype=jnp.float32), gp) + nrm(gp, 0.01),
        'ssm_log_dt': jax.random.uniform(next(keys), (DEPTH, N_SSM_GROUPS), jnp.float32, math.log(1e-3), math.log(1e-1)),
        'ssm_b_re': nrm((DEPTH, N_SSM_GROUPS, SSM_STATE, SSM_GROUP_CH), (2 * SSM_GROUP_CH) ** -0.5),
        'ssm_b_im': nrm((DEPTH, N_SSM_GROUPS, SSM_STATE, SSM_GROUP_CH), (2 * SSM_GROUP_CH) ** -0.5),
        'ssm_c_re': nrm((DEPTH, N_SSM_GROUPS, SSM_GROUP_CH, SSM_STATE), (2 * SSM_STATE) ** -0.5),
        'ssm_c_im': nrm((DEPTH, N_SSM_GROUPS, SSM_GROUP_CH, SSM_STATE), (2 * SSM_STATE) ** -0.5),
        'ssm_d': nrm((DEPTH, N_SSM_GROUPS, SSM_GROUP_CH)),
        'w_glu': nrm((DEPTH, SSM_WIDTH, SSM_WIDTH), SSM_WIDTH ** -0.5),
        'b_glu': nrm((DEPTH, SSM_WIDTH), 0.01),
        'w_att_br': nrm((DEPTH, ATT_WIDTH, D_MODEL), ATT_WIDTH ** -0.5),
        'w_ssm_br': nrm((DEPTH, SSM_WIDTH, D_MODEL), SSM_WIDTH ** -0.5),
        'w_o': nrm((DEPTH, D_MODEL, D_MODEL), D_MODEL ** -0.5),
        'ffn_norm': 1.0 + nrm((DEPTH, D_MODEL), 0.02),
        'w_ffn_gate': nrm((DEPTH, D_MODEL, D_FF), D_MODEL ** -0.5),
        'w_ffn_up': nrm((DEPTH, D_MODEL, D_FF), D_MODEL ** -0.5),
        'w_ffn_down': nrm((DEPTH, D_FF, D_MODEL), D_FF ** -0.5),
        'ple_norm': 1.0 + nrm((DEPTH, D_MODEL), 0.02),
        'w_ple_gate': nrm((DEPTH, D_MODEL, D_MODEL), D_MODEL ** -0.5),
        'w_ple': nrm((DEPTH, PLE_DIM, D_MODEL), PLE_DIM ** -0.5),
    }


def reference(x_prompt, x_sample, cache_k_cmp, cache_v_cmp, cache_k_sel, cache_v_sel, cache_k_win, cache_v_win,
              state_ssm_re, state_ssm_im, page_table, p_prompt, p_sample, rel_bias, final_norm, attn_norm, w_in,
              cmp_pe_k, cmp_w1_k, cmp_w2_k, cmp_pe_v, cmp_w1_v, cmp_w2_v, ssm_a_re, ssm_a_im, ssm_log_dt,
              ssm_b_re, ssm_b_im, ssm_c_re, ssm_c_im, ssm_d, w_glu, b_glu, w_att_br, w_ssm_br, w_o,
              ffn_norm, w_ffn_gate, w_ffn_up, w_ffn_down, ple_norm, w_ple_gate, w_ple):
    b_p, t_p = x_prompt.shape[:2]
    t_s = x_sample.shape[1]
    past_len = page_table.shape[1] * cache_k_cmp.shape[2]
    win_buf = cache_k_win.shape[2]
    keep_p = min(WINDOW, t_p)
    hp, hs = x_prompt, x_sample
    rec_p, rec_s = [], []
    for l in range(DEPTH):
        lw = {'attn_norm': attn_norm[l], 'w_in': w_in[l],
              'cmp_pe_k': cmp_pe_k[l], 'cmp_w1_k': cmp_w1_k[l], 'cmp_w2_k': cmp_w2_k[l],
              'cmp_pe_v': cmp_pe_v[l], 'cmp_w1_v': cmp_w1_v[l], 'cmp_w2_v': cmp_w2_v[l],
              'ssm_a_re': ssm_a_re[l], 'ssm_a_im': ssm_a_im[l], 'ssm_log_dt': ssm_log_dt[l],
              'ssm_b_re': ssm_b_re[l], 'ssm_b_im': ssm_b_im[l], 'ssm_c_re': ssm_c_re[l], 'ssm_c_im': ssm_c_im[l],
              'ssm_d': ssm_d[l], 'w_glu': w_glu[l], 'b_glu': b_glu[l],
              'w_att_br': w_att_br[l], 'w_ssm_br': w_ssm_br[l], 'w_o': w_o[l],
              'ffn_norm': ffn_norm[l], 'w_ffn_gate': w_ffn_gate[l], 'w_ffn_up': w_ffn_up[l], 'w_ffn_down': w_ffn_down[l],
              'ple_norm': ple_norm[l], 'w_ple_gate': w_ple_gate[l], 'w_ple': w_ple[l]}
        q, kc, vc, ks, vs, kw, vw, ng, su, mg = layer_front(hp, lw)
        outs = nsa_prompt(q, nsa_context(kc, vc, ks, vs, lw), kw, vw, rel_bias)
        h0 = jnp.zeros((b_p, N_SSM_GROUPS, SSM_STATE), jnp.float32)
        so, sr, si = ssm_branch(su, h0, h0, lw)
        hp = layer_back(hp, outs, ng, so, mg, p_prompt[l], lw)
        rec_p.append((kc, vc, ks, vs, kw[:, t_p - keep_p:], vw[:, t_p - keep_p:], sr, si))
        q, kc, vc, ks, vs, kw, vw, ng, su, mg = layer_front(hs, lw)
        ctx = nsa_context(with_past(cache_k_cmp[l], page_table, kc), with_past(cache_v_cmp[l], page_table, vc),
                          with_past(cache_k_sel[l], page_table, ks), with_past(cache_v_sel[l], page_table, vs), lw)
        kw_all = jnp.concatenate([cache_k_win[l], kw], axis=1)
        vw_all = jnp.concatenate([cache_v_win[l], vw], axis=1)
        w_pos = past_len - win_buf + jnp.arange(win_buf + t_s)
        q_pos = past_len + jnp.arange(t_s)
        outs = nsa_block(q, q_pos, *ctx, kw_all, vw_all, w_pos, rel_bias)
        so, sr, si = ssm_branch(su, state_ssm_re[l], state_ssm_im[l], lw)
        hs = layer_back(hs, outs, ng, so, mg, p_sample[l], lw)
        rec_s.append((kc, vc, ks, vs, kw_all[:, t_s:], vw_all[:, t_s:], sr, si))
    y_prompt = rmsnorm(hp, final_norm)
    y_sample = rmsnorm(hs, final_norm)
    nk_cmp_p, nv_cmp_p, nk_sel_p, nv_sel_p, nk_win_p, nv_win_p, ns_re_p, ns_im_p = [jnp.stack(t) for t in zip(*rec_p)]
    nk_cmp_s, nv_cmp_s, nk_sel_s, nv_sel_s, nk_win_s, nv_win_s, ns_re_s, ns_im_s = [jnp.stack(t) for t in zip(*rec_s)]
    return (y_prompt, y_sample,
            nk_cmp_p, nv_cmp_p, nk_sel_p, nv_sel_p, nk_win_p, nv_win_p, ns_re_p, ns_im_p,
            nk_cmp_s, nv_cmp_s, nk_sel_s, nv_sel_s, nk_win_s, nv_win_s, ns_re_s, ns_im_s)
```

```python
import functools
import math

import numpy as np
import jax
import jax.numpy as jnp
from jax import lax
from jax.experimental import pallas as pl
from jax.experimental.pallas import tpu as pltpu

F32 = jnp.float32
BF16 = jnp.bfloat16

N_HEADS = 8
N_KV = 2
HD = 64
GROUP = N_HEADS // N_KV
N_BRANCH = 3
CMP_BLOCK = 32
CMP_STRIDE = 16
CMP_HIDDEN = 256
SEL_BLOCK = 64
SEL_TOPK = 16
WINDOW = 512
QB = 128
N_BUCKETS = 32
REL_MAX_DIST = 128
SSM_CH = 16
SSM_P = 64
EPS = 1e-6
NEG = -1e30
NEG_TEST = -1e29
FORCE = 1e9
LANES = 128
VMEM_LIMIT = 56 * 1024 * 1024


def _cparams(sem):
    return pltpu.CompilerParams(dimension_semantics=sem, vmem_limit_bytes=VMEM_LIMIT)


def _const_spec(shape):
    nd = len(shape)
    return pl.BlockSpec(shape, lambda *_: (0,) * nd)


def _rms(x, g):
    return x * lax.rsqrt(jnp.mean(x * x, axis=-1, keepdims=True) + EPS) * g


def _gelu(x):
    return x * (0.5 * (1.0 + jnp.tanh(math.sqrt(2.0 / math.pi) * (x + 0.044715 * (x * x * x)))))


def _sigmoid(x):
    return 1.0 / (1.0 + jnp.exp(-x))


def _dot(a, b):
    return jnp.dot(a, b, preferred_element_type=F32)


def _dot_t(a, b):
    return lax.dot_general(a, b, (((1,), (1,)), ((), ())), preferred_element_type=F32)


def _masked_softmax(s):
    valid = s > NEG_TEST
    m = jnp.max(s, axis=-1, keepdims=True)
    e = jnp.where(valid, jnp.exp(s - m), 0.0)
    return e / jnp.maximum(jnp.sum(e, axis=-1, keepdims=True), 1e-30)


def _front_kernel(x_ref, g_ref, wa_ref, wng_ref, wsu_ref, wmg_ref,
                  q_ref, kc_ref, vc_ref, ks_ref, vs_ref, kw_ref, vw_ref,
                  ksb_ref, vsb_ref, kwb_ref, vwb_ref, ng_ref, su_ref, mg_ref):
    u = _rms(x_ref[...], g_ref[...]).astype(BF16)
    za = _dot(u, wa_ref[...])
    aw = N_HEADS * HD
    q_ref[...] = (za[:, :aw] * (HD ** -0.5)).astype(BF16)
    kvw = N_KV * HD
    f32_outs = (kc_ref, vc_ref, ks_ref, vs_ref, kw_ref, vw_ref)
    bf_outs = (None, None, ksb_ref, vsb_ref, kwb_ref, vwb_ref)
    for i in range(6):
        blk = za[:, aw + i * kvw: aw + (i + 1) * kvw]
        f32_outs[i][...] = blk
        if bf_outs[i] is not None:
            bf_outs[i][...] = blk.astype(BF16)
    ng_ref[...] = _sigmoid(_dot(u, wng_ref[...]))
    su_ref[...] = _dot(u, wsu_ref[...]).astype(su_ref.dtype)
    mg_ref[...] = _sigmoid(_dot(u, wmg_ref[...])).astype(BF16)


def _front(x2d, nb, t, g, wa, wng, wsu, wmg, tm):
    n, d = x2d.shape
    nt = t // tm
    aw, kvw = N_HEADS * HD, N_KV * HD
    sw, mw = wsu.shape[1], wmg.shape[1]
    row = lambda b, i: (b * nt + i, 0)
    outs = ([jax.ShapeDtypeStruct((n, aw), BF16)]
            + [jax.ShapeDtypeStruct((n, kvw), F32)] * 6
            + [jax.ShapeDtypeStruct((n, kvw), BF16)] * 4
            + [jax.ShapeDtypeStruct((n, LANES), F32),
               jax.ShapeDtypeStruct((t, nb * sw), BF16),
               jax.ShapeDtypeStruct((n, mw), BF16)])
    out_specs = ([pl.BlockSpec((tm, aw), row)]
                 + [pl.BlockSpec((tm, kvw), row)] * 10
                 + [pl.BlockSpec((tm, LANES), row),
                    pl.BlockSpec((tm, sw), lambda b, i: (i, b)),
                    pl.BlockSpec((tm, mw), row)])
    return pl.pallas_call(
        _front_kernel,
        grid=(nb, nt),
        in_specs=[pl.BlockSpec((tm, d), row), _const_spec(g.shape), _const_spec(wa.shape),
                  _const_spec(wng.shape), _const_spec(wsu.shape), _const_spec(wmg.shape)],
        out_specs=out_specs,
        out_shape=outs,
        compiler_params=_cparams(("parallel", "parallel")),
        name="front",
    )(x2d, g, wa, wng, wsu, wmg)


def _compress_compute(load_rows, c, w1_ref, w2_ref, pe_ref, a_scr):
    rc_n = min(c, 256)
    lo = lax.broadcasted_iota(jnp.int32, (rc_n, LANES), 1) < HD
    w1 = w1_ref[...]
    for rc in range(c // rc_n):
        x = load_rows(rc * rc_n, rc_n)
        cols = [x[:, r * LANES:(r + 1) * LANES] for r in range(CMP_STRIDE)]
        rol = [pltpu.roll(col, HD, 1) for col in cols]
        for kh in range(N_KV):
            if kh == 0:
                parts = [jnp.where(lo, cols[2 * j], rol[2 * j + 1]) for j in range(CMP_STRIDE // 2)]
            else:
                parts = [jnp.where(lo, rol[2 * j], cols[2 * j + 1]) for j in range(CMP_STRIDE // 2)]
            xh = jnp.concatenate(parts, axis=1).astype(BF16)
            a_scr[kh, rc * rc_n:(rc + 1) * rc_n, :] = _dot(xh, w1)
    pw = _dot(pe_ref[...], w1)
    peb = pw[0:1, :CMP_HIDDEN] + pw[1:2, CMP_HIDDEN:]
    w2 = w2_ref[...]
    outs = []
    for kh in range(N_KV):
        a = a_scr[kh]
        hid = a[:, :CMP_HIDDEN] + pltpu.roll(a[:, CMP_HIDDEN:], c - 1, 0) + peb
        outs.append(_dot(_gelu(hid).astype(BF16), w2))
    return jnp.concatenate(outs, axis=1)


def _compress_prompt_kernel(xk_ref, xv_ref, w1k_ref, w2k_ref, pek_ref, w1v_ref, w2v_ref, pev_ref,
                            ok_ref, ov_ref, a_scr):
    c = xk_ref.shape[1]
    ok_ref[0] = _compress_compute(lambda r0, rn: xk_ref[0, r0:r0 + rn, :], c,
                                  w1k_ref, w2k_ref, pek_ref, a_scr).astype(BF16)
    ov_ref[0] = _compress_compute(lambda r0, rn: xv_ref[0, r0:r0 + rn, :], c,
                                  w1v_ref, w2v_ref, pev_ref, a_scr).astype(BF16)


def _compress_prompt(xk, xv, cw):
    nb, c, cw_lanes = xk.shape
    wspecs = [_const_spec(w.shape) for w in cw]
    blk = pl.BlockSpec((1, c, cw_lanes), lambda b: (b, 0, 0))
    oblk = pl.BlockSpec((1, c, LANES), lambda b: (b, 0, 0))
    return pl.pallas_call(
        _compress_prompt_kernel,
        grid=(nb,),
        in_specs=[blk, blk] + wspecs,
        out_specs=[oblk, oblk],
        out_shape=[jax.ShapeDtypeStruct((nb, c, LANES), BF16)] * 2,
        scratch_shapes=[pltpu.VMEM((N_KV, c, 2 * CMP_HIDDEN), F32)],
        compiler_params=_cparams(("parallel",)),
        name="compress_prompt",
    )(xk, xv, *cw)


def _page_gather_start(pt_ref, seq, pools, bufs, sems, slot, n_pages, rows):
    def body(p, carry):
        page = pt_ref[seq, p]
        for i, (pool, buf) in enumerate(zip(pools, bufs)):
            pltpu.make_async_copy(pool.at[page], buf.at[slot, pl.ds(p * rows, rows)], sems.at[i, slot]).start()
        return carry
    lax.fori_loop(0, n_pages, body, 0)


def _page_gather_wait(pools, bufs, sems, slot, n_pages, rows):
    def body(p, carry):
        for i, (pool, buf) in enumerate(zip(pools, bufs)):
            pltpu.make_async_copy(pool.at[0], buf.at[slot, pl.ds(p * rows, rows)], sems.at[i, slot]).wait()
        return carry
    lax.fori_loop(0, n_pages, body, 0)


def _paged_prefetch(pt_ref, pools, bufs, sems, n_pages, rows):
    s = pl.program_id(0)
    slot = s % 2

    @pl.when(s == 0)
    def _():
        _page_gather_start(pt_ref, 0, pools, bufs, sems, 0, n_pages, rows)

    @pl.when(s + 1 < pl.num_programs(0))
    def _():
        _page_gather_start(pt_ref, s + 1, pools, bufs, sems, 1 - slot, n_pages, rows)

    _page_gather_wait(pools, bufs, sems, slot, n_pages, rows)
    return slot


def _compress_sample_kernel(pt_ref, kpool, vpool, w1k_ref, w2k_ref, pek_ref, w1v_ref, w2v_ref, pev_ref,
                            ok_ref, ov_ref, kbuf, vbuf, sems, a_scr):
    n_pages = pt_ref.shape[1]
    rows = kpool.shape[1]
    c = n_pages * rows
    slot = _paged_prefetch(pt_ref, (kpool, vpool), (kbuf, vbuf), sems, n_pages, rows)
    ok_ref[0] = _compress_compute(lambda r0, rn: kbuf[slot, pl.ds(r0, rn), :], c,
                                  w1k_ref, w2k_ref, pek_ref, a_scr).astype(BF16)
    ov_ref[0] = _compress_compute(lambda r0, rn: vbuf[slot, pl.ds(r0, rn), :], c,
                                  w1v_ref, w2v_ref, pev_ref, a_scr).astype(BF16)


def _compress_sample(page_table, kpool, vpool, cw):
    ns, n_pages = page_table.shape
    rows, width = kpool.shape[1:]
    c = n_pages * rows
    any_spec = pl.BlockSpec(memory_space=pl.ANY)
    wspecs = [pl.BlockSpec(w.shape, lambda s, pt, nd=w.ndim: (0,) * nd) for w in cw]
    oblk = pl.BlockSpec((1, c, LANES), lambda s, pt: (s, 0, 0))
    return pl.pallas_call(
        _compress_sample_kernel,
        grid_spec=pltpu.PrefetchScalarGridSpec(
            num_scalar_prefetch=1,
            grid=(ns,),
            in_specs=[any_spec, any_spec] + wspecs,
            out_specs=[oblk, oblk],
            scratch_shapes=[pltpu.VMEM((2, c, width), F32), pltpu.VMEM((2, c, width), F32),
                            pltpu.SemaphoreType.DMA((2, 2)),
                            pltpu.VMEM((N_KV, c, 2 * CMP_HIDDEN), F32)]),
        out_shape=[jax.ShapeDtypeStruct((ns, c, LANES), BF16)] * 2,
        compiler_params=_cparams(("arbitrary",)),
        name="compress_sample",
    )(page_table, kpool, vpool, *cw)


def _rank_select(score, blk, n_real, axis):
    rank = jnp.zeros(score.shape, F32)
    for kk in range(n_real):
        if axis == 0:
            other = jnp.broadcast_to(score[kk:kk + 1, :], score.shape)
        else:
            other = jnp.broadcast_to(score[:, kk:kk + 1], score.shape)
        beats = (other > score) | ((other == score) & (blk > kk))
        rank = rank + jnp.where(beats, 1.0, 0.0)
    return jnp.where(rank < SEL_TOPK, 1.0, 0.0)


def _block_scores(imp, blk, t):
    cur = t // SEL_BLOCK
    forced = (blk == 0) | (blk == cur) | (blk == cur - 1)
    valid = blk * SEL_BLOCK <= t
    return jnp.where(valid, jnp.where(forced, FORCE, imp), NEG)


def _flash_step(carry, s, vt):
    m, l, acc = carry
    m_new = jnp.maximum(m, jnp.max(s, axis=-1, keepdims=True))
    p = jnp.where(s > NEG_TEST, jnp.exp(s - m_new), 0.0)
    alpha = jnp.exp(m - m_new)
    l = alpha * l + jnp.sum(p, axis=-1, keepdims=True)
    acc = alpha * acc + _dot(p.astype(BF16), vt)
    return m_new, l, acc


def _flash_init(rows):
    return (jnp.full((rows, 1), NEG, F32), jnp.zeros((rows, 1), F32), jnp.zeros((rows, HD), F32))


def _flash_done(carry):
    _, l, acc = carry
    return acc / jnp.maximum(l, 1e-30)


def _nsa_prompt_kernel(q_ref, ng_ref, kc_ref, vc_ref, ks_ref, vs_ref, kw_ref, vw_ref,
                       ut_ref, tiles_ref, ovl_ref, e_ref, o_ref, *, n_sel):
    ib = pl.program_id(1)
    q = q_ref[0]
    ng = ng_ref[0]
    ncp = kc_ref.shape[1]
    rows = GROUP * QB
    t_row = ib * QB + lax.broadcasted_iota(jnp.int32, (n_sel, QB), 1)
    blk_t = lax.broadcasted_iota(jnp.int32, (n_sel, QB), 0)
    heads_out = []
    for k in range(N_KV):
        lanes = slice(k * HD, (k + 1) * HD)
        qk = jnp.concatenate([q[:, (GROUP * k + g) * HD:(GROUP * k + g + 1) * HD] for g in range(GROUP)], axis=0)

        kc = kc_ref[0][:, lanes]
        vc = vc_ref[0][:, lanes]
        bias_t = ut_ref[k, pl.ds(pl.multiple_of(ncp - (QB // CMP_STRIDE) * ib, 8), ncp), :]
        p_c = _masked_softmax(_dot_t(qk, kc) + bias_t.T)
        o_c = _dot(p_c.astype(BF16), vc)
        psum = p_c[0:QB]
        for g in range(1, GROUP):
            psum = psum + p_c[g * QB:(g + 1) * QB]
        imp = jnp.dot(psum, ovl_ref[...], preferred_element_type=F32, precision=lax.Precision.HIGHEST)

        score = _block_scores(imp.T[0:n_sel], blk_t, t_row)
        sel_t = _rank_select(score, blk_t, n_sel, 0)
        if n_sel < LANES:
            sel_t = jnp.concatenate([sel_t, jnp.zeros((LANES - n_sel, QB), F32)], axis=0)
        sel = sel_t.T.astype(BF16)

        def sel_body(jt, carry):
            k0 = pl.multiple_of(jt * QB, QB)
            kt = ks_ref[0, pl.ds(k0, QB), lanes]
            vt = vs_ref[0, pl.ds(k0, QB), lanes]
            tidx = jnp.where(jt == ib, 0, jnp.where(jt == ib - 1, 1, 2))
            keep = _dot(sel, e_ref[jt])
            keep = jnp.concatenate([keep] * GROUP, axis=0)
            s = jnp.where(keep > 0.5, _dot_t(qk, kt) + tiles_ref[tidx, k], NEG)
            return _flash_step(carry, s, vt)

        o_s = _flash_done(lax.fori_loop(0, ib + 1, sel_body, _flash_init(rows)))

        s_parts, v_parts = [], []
        for w, tidx in enumerate((0, 1, 2, 2, 3)):
            jt = ib - w
            k0 = pl.multiple_of(jnp.maximum(jt, 0) * QB, QB)
            kt = kw_ref[0, pl.ds(k0, QB), lanes]
            v_parts.append(vw_ref[0, pl.ds(k0, QB), lanes])
            s_parts.append(jnp.where(jt >= 0, _dot_t(qk, kt) + tiles_ref[tidx, k], NEG))
        p_w = _masked_softmax(jnp.concatenate(s_parts, axis=1))
        o_w = _dot(p_w.astype(BF16), jnp.concatenate(v_parts, axis=0))

        for g in range(GROUP):
            j = (GROUP * k + g) * N_BRANCH
            r = slice(g * QB, (g + 1) * QB)
            heads_out.append(ng[:, j:j + 1] * o_c[r] + ng[:, j + 1:j + 2] * o_s[r] + ng[:, j + 2:j + 3] * o_w[r])
    o_ref[0] = jnp.concatenate(heads_out, axis=1).astype(BF16)


def _nsa_prompt(q, ng, kc, vc, ks, vs, kw, vw, ut, tiles, ovl, e):
    nb, t, aw = q.shape
    nq = t // QB
    ncp = kc.shape[1]
    kvw = N_KV * HD
    qblk = lambda b, i: (b, i, 0)
    full = lambda b, i: (b, 0, 0)
    return pl.pallas_call(
        functools.partial(_nsa_prompt_kernel, n_sel=t // SEL_BLOCK),
        grid=(nb, nq),
        in_specs=[pl.BlockSpec((1, QB, aw), qblk), pl.BlockSpec((1, QB, LANES), qblk),
                  pl.BlockSpec((1, ncp, kvw), full), pl.BlockSpec((1, ncp, kvw), full),
                  pl.BlockSpec((1, t, kvw), full), pl.BlockSpec((1, t, kvw), full),
                  pl.BlockSpec((1, t, kvw), full), pl.BlockSpec((1, t, kvw), full),
                  _const_spec(ut.shape), _const_spec(tiles.shape), _const_spec(ovl.shape), _const_spec(e.shape)],
        out_specs=pl.BlockSpec((1, QB, aw), qblk),
        out_shape=jax.ShapeDtypeStruct((nb, t, aw), BF16),
        compiler_params=_cparams(("parallel", "arbitrary")),
        name="nsa_prompt",
    )(q, ng, kc, vc, ks, vs, kw, vw, ut, tiles, ovl, e)


def _nsa_sample_kernel(pt_ref, qk_ref, gate_ref, kc_ref, vc_ref, kpool, vpool, ksn_ref, vsn_ref,
                       kwin_ref, vwin_ref, kwn_ref, vwn_ref, bc_ref, tiles_ref, bw_ref, ovl_ref, e_ref,
                       o_ref, kbuf, vbuf, sems, *, n_sel, past, t_new):
    n_pages = pt_ref.shape[1]
    page = kpool.shape[1]
    nt = past // QB
    wb = kwin_ref.shape[1]
    rows = GROUP * t_new
    slot = _paged_prefetch(pt_ref, (kpool, vpool), (kbuf, vbuf), sems, n_pages, page)
    nbp = ovl_ref.shape[1]
    blk = lax.broadcasted_iota(jnp.int32, (8, nbp), 1)
    tpos = past + lax.broadcasted_iota(jnp.int32, (8, nbp), 0)
    pad_new = jnp.zeros((QB - ksn_ref.shape[1], N_KV * HD), F32)
    for k in range(N_KV):
        lanes = slice(k * HD, (k + 1) * HD)
        qk = qk_ref[0, k]

        p_c = _masked_softmax(_dot_t(qk, kc_ref[0][:, lanes]) + bc_ref[k])
        o_c = _dot(p_c.astype(BF16), vc_ref[0][:, lanes])
        psum = p_c[0:t_new]
        for g in range(1, GROUP):
            psum = psum + p_c[g * t_new:(g + 1) * t_new]
        psum = jnp.concatenate([psum, jnp.zeros((8 - t_new, psum.shape[1]), F32)], axis=0)
        imp = jnp.dot(psum, ovl_ref[...], preferred_element_type=F32, precision=lax.Precision.HIGHEST)
        sel = _rank_select(_block_scores(imp, blk, tpos), blk, n_sel, 1)
        sel = jnp.concatenate([sel[0:t_new]] * GROUP, axis=0).astype(BF16)

        def sel_body(jt, carry):
            k0 = pl.multiple_of(jt * QB, QB)
            kt = kbuf[slot, pl.ds(k0, QB), lanes].astype(BF16)
            vt = vbuf[slot, pl.ds(k0, QB), lanes].astype(BF16)
            tidx = jnp.where(jt == nt - 1, 1, 0)
            keep = _dot(sel[:, 0:LANES], e_ref[jt])
            s = jnp.where(keep > 0.5, _dot_t(qk, kt) + tiles_ref[tidx, k], NEG)
            return _flash_step(carry, s, vt)

        carry = lax.fori_loop(0, nt, sel_body, _flash_init(rows))
        kn = jnp.concatenate([ksn_ref[0], pad_new], axis=0)[:, lanes].astype(BF16)
        vn = jnp.concatenate([vsn_ref[0], pad_new], axis=0)[:, lanes].astype(BF16)
        keep = _dot(sel[:, LANES:2 * LANES], e_ref[0])
        s = jnp.where(keep > 0.5, _dot_t(qk, kn) + tiles_ref[2, k], NEG)
        o_s = _flash_done(_flash_step(carry, s, vn))

        kwn = jnp.concatenate([kwn_ref[0], pad_new], axis=0)[:, lanes].astype(BF16)
        vwn = jnp.concatenate([vwn_ref[0], pad_new], axis=0)[:, lanes].astype(BF16)
        s_w = jnp.concatenate([_dot_t(qk, kwin_ref[0][:, lanes].astype(BF16)), _dot_t(qk, kwn)], axis=1)
        p_w = _masked_softmax(s_w + bw_ref[k]).astype(BF16)
        o_w = _dot(p_w[:, 0:wb], vwin_ref[0][:, lanes].astype(BF16)) + _dot(p_w[:, wb:], vwn)

        gate = gate_ref[0, k]
        o_ref[0, k] = gate[:, 0:1] * o_c + gate[:, 1:2] * o_s + gate[:, 2:3] * o_w


def _nsa_sample(page_table, qk, gate, kc, vc, kpool, vpool, ksn, vsn, kwin, vwin, kwn, vwn,
                bc, tiles, bw, ovl, e, n_sel, past, t_new):
    ns, n_pages = page_table.shape
    page, kvw = kpool.shape[1:]
    rows = qk.shape[2]
    seq3 = lambda s, pt: (s, 0, 0)
    seq4 = lambda s, pt: (s, 0, 0, 0)
    any_spec = pl.BlockSpec(memory_space=pl.ANY)
    cs = lambda a: pl.BlockSpec(a.shape, lambda s, pt, nd=a.ndim: (0,) * nd)
    return pl.pallas_call(
        functools.partial(_nsa_sample_kernel, n_sel=n_sel, past=past, t_new=t_new),
        grid_spec=pltpu.PrefetchScalarGridSpec(
            num_scalar_prefetch=1,
            grid=(ns,),
            in_specs=[pl.BlockSpec((1,) + qk.shape[1:], seq4), pl.BlockSpec((1,) + gate.shape[1:], seq4),
                      pl.BlockSpec((1,) + kc.shape[1:], seq3), pl.BlockSpec((1,) + vc.shape[1:], seq3),
                      any_spec, any_spec,
                      pl.BlockSpec((1,) + ksn.shape[1:], seq3), pl.BlockSpec((1,) + vsn.shape[1:], seq3),
                      pl.BlockSpec((1,) + kwin.shape[1:], seq3), pl.BlockSpec((1,) + vwin.shape[1:], seq3),
                      pl.BlockSpec((1,) + kwn.shape[1:], seq3), pl.BlockSpec((1,) + vwn.shape[1:], seq3),
                      cs(bc), cs(tiles), cs(bw), cs(ovl), cs(e)],
            out_specs=pl.BlockSpec((1, N_KV, rows, HD), seq4),
            scratch_shapes=[pltpu.VMEM((2, n_pages * page, kvw), F32), pltpu.VMEM((2, n_pages * page, kvw), F32),
                            pltpu.SemaphoreType.DMA((2, 2))]),
        out_shape=jax.ShapeDtypeStruct((ns, N_KV, rows, HD), F32),
        compiler_params=_cparams(("arbitrary",)),
        name="nsa_sample",
    )(page_table, qk, gate, kc, vc, kpool, vpool, ksn, vsn, kwin, vwin, kwn, vwn, bc, tiles, bw, ovl, e)


def _ssm_param_kernel(ar_ref, ai_ref, ldt_ref, br_ref, bi_ref, abr_ref, abi_ref, bbr_ref, bbi_ref):
    ar = ar_ref[...]
    ai = ai_ref[...]
    dt = jnp.exp(ldt_ref[...])
    mag = jnp.exp(ar * dt)
    abr = mag * jnp.cos(ai * dt)
    abi = mag * jnp.sin(ai * dt)
    den = ar * ar + ai * ai
    nr, ni = abr - 1.0, abi
    fr = (nr * ar + ni * ai) / den
    fi = (ni * ar - nr * ai) / den
    abr_ref[...] = abr
    abi_ref[...] = abi
    for g in range(ar.shape[0]):
        br = br_ref[g]
        bi = bi_ref[g]
        frg = fr[g:g + 1, :]
        fig = fi[g:g + 1, :]
        bbr_ref[g] = frg * br - fig * bi
        bbi_ref[g] = frg * bi + fig * br


def _ssm_params(a_re, a_im, log_dt, b_re_t, b_im_t):
    g, p = a_re.shape
    return pl.pallas_call(
        _ssm_param_kernel,
        out_shape=[jax.ShapeDtypeStruct((g, p), F32)] * 2 + [jax.ShapeDtypeStruct(b_re_t.shape, F32)] * 2,
        name="ssm_params",
    )(a_re, a_im, log_dt.reshape(g, 1), b_re_t, b_im_t)


def _ssm_kernel(u_ref, h0r_ref, h0i_ref, ar_ref, ai_ref, bd_ref, cd_ref, d_ref, wglu_ref, bglu_ref,
                so_ref, hr_ref, hi_ref, xr_scr, xi_scr, *, bt):
    i = pl.program_id(0)
    rows, width = u_ref.shape
    n_slab = bd_ref.shape[0]
    sw = bd_ref.shape[2] // 2
    u = u_ref[...]

    @pl.when(i == 0)
    def _():
        hr_ref[...] = h0r_ref[...]
        hi_ref[...] = h0i_ref[...]

    for sl in range(n_slab):
        x = _dot(u[:, sl * LANES:(sl + 1) * LANES], bd_ref[sl])
        xr_scr[:, sl * sw:(sl + 1) * sw] = x[:, :sw]
        xi_scr[:, sl * sw:(sl + 1) * sw] = x[:, sw:]

    per = 8 // math.gcd(bt, 8)
    grp = per * bt
    lc = 512
    for c0 in range(0, xr_scr.shape[1], lc):
        cl = slice(c0, c0 + lc)
        a_r = jnp.broadcast_to(ar_ref[:, cl], (bt, lc))
        a_i = jnp.broadcast_to(ai_ref[:, cl], (bt, lc))

        def step(j, carry):
            h_r, h_i = carry
            r0 = pl.multiple_of(j * grp, grp)
            xr = xr_scr[pl.ds(r0, grp), cl]
            xi = xi_scr[pl.ds(r0, grp), cl]
            out_r, out_i = [], []
            for s in range(per):
                n_r = a_r * h_r - a_i * h_i + xr[s * bt:(s + 1) * bt]
                n_i = a_r * h_i + a_i * h_r + xi[s * bt:(s + 1) * bt]
                h_r, h_i = n_r, n_i
                out_r.append(h_r)
                out_i.append(h_i)
            xr_scr[pl.ds(r0, grp), cl] = jnp.concatenate(out_r, axis=0) if per > 1 else out_r[0]
            xi_scr[pl.ds(r0, grp), cl] = jnp.concatenate(out_i, axis=0) if per > 1 else out_i[0]
            return h_r, h_i

        h_r, h_i = lax.fori_loop(0, rows // grp, step, (hr_ref[:, cl], hi_ref[:, cl]))
        hr_ref[:, cl] = h_r
        hi_ref[:, cl] = h_i

    ys = []
    for sl in range(n_slab):
        hcat = jnp.concatenate([xr_scr[:, sl * sw:(sl + 1) * sw], xi_scr[:, sl * sw:(sl + 1) * sw]], axis=1)
        ys.append(_dot(hcat.astype(BF16), cd_ref[sl]))
    y = jnp.concatenate(ys, axis=1) + d_ref[...] * u.astype(F32)
    z = _gelu(y)
    so_ref[...] = (z * _sigmoid(_dot(z.astype(BF16), wglu_ref[...]) + bglu_ref[...])).astype(so_ref.dtype)


def _ssm(u, h0r, h0i, ar, ai, bd, cd, dvec, wglu, bglu, bt, tt):
    n, width = u.shape
    rows = tt * bt
    nstate = ar.shape[1]
    cst = [_const_spec(a.shape) for a in (h0r, h0i, ar, ai, bd, cd, dvec, wglu, bglu)]
    st_spec = _const_spec((bt, nstate))
    return pl.pallas_call(
        functools.partial(_ssm_kernel, bt=bt),
        grid=(n // rows,),
        in_specs=[pl.BlockSpec((rows, width), lambda i: (i, 0))] + cst,
        out_specs=[pl.BlockSpec((rows, width), lambda i: (i, 0)), st_spec, st_spec],
        out_shape=[jax.ShapeDtypeStruct((n, width), BF16), jax.ShapeDtypeStruct((bt, nstate), F32),
                   jax.ShapeDtypeStruct((bt, nstate), F32)],
        scratch_shapes=[pltpu.VMEM((rows, nstate), F32), pltpu.VMEM((rows, nstate), F32)],
        compiler_params=_cparams(("arbitrary",)),
        name="ssm",
    )(u, h0r, h0i, ar, ai, bd, cd, dvec, wglu, bglu)


def _back_kernel(h_ref, o_ref, so_ref, mg_ref, p_ref, watt_ref, wssm_ref, wo_ref, fn_ref, wg_ref, wu_ref, wd_ref,
                 pn_ref, wpg_ref, wple_ref, fin_ref, y_ref, *, ff_chunk):
    d = h_ref.shape[1]
    a = _dot(o_ref[...], watt_ref[...])
    s = _dot(so_ref[...], wssm_ref[...])
    mg = mg_ref[...].astype(F32)
    h = h_ref[...] + _dot((mg[:, :d] * a + mg[:, d:] * s).astype(BF16), wo_ref[...])
    f = _rms(h, fn_ref[...]).astype(BF16)
    ffn = jnp.zeros_like(h)
    for c0 in range(0, wg_ref.shape[1], ff_chunk):
        gate = _dot(f, wg_ref[:, c0:c0 + ff_chunk])
        up = _dot(f, wu_ref[:, c0:c0 + ff_chunk])
        ffn = ffn + _dot((gate * _sigmoid(gate) * up).astype(BF16), wd_ref[c0:c0 + ff_chunk, :])
    h = h + ffn
    g = _sigmoid(_dot(_rms(h, pn_ref[...]).astype(BF16), wpg_ref[...]))
    h = h + g * _dot(p_ref[...].astype(BF16), wple_ref[...])
    y_ref[...] = _rms(h, fin_ref[...])


def _back(h2d, o2d, so_tb, mg, p2d, weights, nb, t, tm, ff_chunk):
    n, d = h2d.shape
    nt = t // tm
    row = lambda b, i: (b * nt + i, 0)
    wspecs = [pl.BlockSpec(w.shape, lambda b, i, nd=w.ndim: (0,) * nd, pipeline_mode=pl.Buffered(1))
              for w in weights]
    sw = o2d.shape[1]
    return pl.pallas_call(
        functools.partial(_back_kernel, ff_chunk=ff_chunk),
        grid=(nb, nt),
        in_specs=[pl.BlockSpec((tm, d), row), pl.BlockSpec((tm, sw), row),
                  pl.BlockSpec((tm, sw), lambda b, i: (i, b)),
                  pl.BlockSpec((tm, mg.shape[1]), row), pl.BlockSpec((tm, p2d.shape[1]), row)] + wspecs,
        out_specs=pl.BlockSpec((tm, d), row),
        out_shape=jax.ShapeDtypeStruct((n, d), F32),
        compiler_params=_cparams(("parallel", "parallel")),
        name="back",
    )(h2d, o2d, so_tb, mg, p2d, *weights)


def _bucket_np(dist):
    n = np.maximum(dist, 0)
    exact = N_BUCKETS // 2
    nf = np.maximum(n, 1).astype(np.float64)
    large = exact + (np.log(nf / exact) / math.log(REL_MAX_DIST / exact) * (N_BUCKETS - exact)).astype(np.int64)
    return np.where(n < exact, n, np.minimum(large, N_BUCKETS - 1)).astype(np.int32)


def _bias_table(rel_bias, dist, valid):
    onehot = jax.nn.one_hot(jnp.asarray(_bucket_np(dist)), N_BUCKETS, dtype=F32)
    b = jnp.einsum('...b,bh->h...', onehot, rel_bias.astype(F32), precision=lax.Precision.HIGHEST)
    return jnp.where(jnp.asarray(valid)[None], b, NEG)


def _group_rows(b):
    h, r, n = b.shape
    return b.reshape(N_KV, GROUP * r, n)


def _prompt_tables(rel_bias, t):
    i = np.arange(QB)[:, None]
    j = np.arange(QB)[None, :]
    ones = np.ones((QB, QB), bool)
    far = np.full((QB, QB), 4 * QB)
    tiles = jnp.stack([
        _group_rows(_bias_table(rel_bias, i - j, i >= j)),
        _group_rows(_bias_table(rel_bias, QB + i - j, ones)),
        _group_rows(_bias_table(rel_bias, far, ones)),
        _group_rows(_bias_table(rel_bias, far, j > i)),
    ])
    ncp = t // CMP_STRIDE
    m = np.arange(2 * ncp)[:, None] - ncp
    ii = np.arange(QB)[None, :]
    dist = ii - CMP_STRIDE * m - (CMP_BLOCK - 1)
    ut = _bias_table(rel_bias, dist, dist >= 0)
    ut = ut.reshape(N_KV, GROUP, 2 * ncp, QB).transpose(0, 2, 1, 3).reshape(N_KV, 2 * ncp, GROUP * QB)
    n_sel = t // SEL_BLOCK
    n = np.arange(ncp)[:, None]
    jb = np.arange(LANES)[None, :]
    ovl = ((n * CMP_STRIDE < jb * SEL_BLOCK + SEL_BLOCK) & (n * CMP_STRIDE + CMP_BLOCK - 1 >= jb * SEL_BLOCK)
           & (jb < n_sel) & (n < ncp - 1))
    return tiles, ut, jnp.asarray(ovl.astype(np.float32))


def _expand_table(n_tiles):
    r = np.arange(LANES)[None, :, None]
    c = np.arange(QB)[None, None, :]
    j = np.arange(n_tiles)[:, None, None]
    return jnp.asarray((r == (QB // SEL_BLOCK) * (j % (LANES * SEL_BLOCK // QB)) + c // SEL_BLOCK), BF16)


def _sample_tables(rel_bias, past, t_new, win_buf):
    def rows(b):
        return b.reshape(N_KV, GROUP * t_new, b.shape[-1])
    tok = np.arange(t_new)[:, None]
    nc = past // CMP_STRIDE
    n = np.arange(nc)[None, :]
    c_end = n * CMP_STRIDE + CMP_BLOCK - 1
    n_cmp = (past + t_new) // CMP_STRIDE - 1
    bc = rows(_bias_table(rel_bias, past + tok - c_end, (c_end <= past + tok) & (n < n_cmp)))
    j = np.arange(QB)[None, :]
    ones = np.ones((t_new, QB), bool)
    tiles = jnp.stack([
        rows(_bias_table(rel_bias, np.full((t_new, QB), 4 * QB), ones)),
        rows(_bias_table(rel_bias, QB + tok - j, ones)),
        rows(_bias_table(rel_bias, tok - j, (j <= tok) & (j < t_new))),
    ])
    jw = np.arange(win_buf + QB)[None, :]
    dw = np.where(jw < win_buf, win_buf + tok - jw, tok - (jw - win_buf))
    valid = np.where(jw < win_buf, (dw >= 0) & (dw < WINDOW), (dw >= 0) & (jw - win_buf < t_new))
    bw = rows(_bias_table(rel_bias, dw, valid))
    n_sel = -(-(past + t_new) // SEL_BLOCK)
    nbp = 2 * LANES
    nn = np.arange(nc)[:, None]
    jb = np.arange(nbp)[None, :]
    ovl = ((nn * CMP_STRIDE < jb * SEL_BLOCK + SEL_BLOCK) & (nn * CMP_STRIDE + CMP_BLOCK - 1 >= jb * SEL_BLOCK)
           & (jb < n_sel) & (nn < n_cmp))
    return bc, tiles, bw, jnp.asarray(ovl.astype(np.float32)), n_sel


def _block_diag(blocks):
    g, r, c = blocks.shape
    eye = jnp.eye(g, dtype=blocks.dtype)
    return jnp.einsum('grc,gh->grhc', blocks, eye).reshape(g * r, g * c)


def kernel(x_prompt, x_sample, cache_k_cmp, cache_v_cmp, cache_k_sel, cache_v_sel, cache_k_win, cache_v_win, state_ssm_re, state_ssm_im, page_table, p_prompt, p_sample, rel_bias, final_norm, attn_norm, w_in, cmp_pe_k, cmp_w1_k, cmp_w2_k, cmp_pe_v, cmp_w1_v, cmp_w2_v, ssm_a_re, ssm_a_im, ssm_log_dt, ssm_b_re, ssm_b_im, ssm_c_re, ssm_c_im, ssm_d, w_glu, b_glu, w_att_br, w_ssm_br, w_o, ffn_norm, w_ffn_gate, w_ffn_up, w_ffn_down, ple_norm, w_ple_gate, w_ple):
    depth = w_in.shape[0]
    assert depth == 1, "single-layer trunk"
    nb, t, d = x_prompt.shape
    ns, t_new = x_sample.shape[:2]
    n_phys, page = cache_k_cmp.shape[1:3]
    n_pages = page_table.shape[1]
    past = n_pages * page
    win_buf = cache_k_win.shape[2]
    aw, kvw = N_HEADS * HD, N_KV * HD
    n_groups = ssm_a_re.shape[1]
    ssm_w = n_groups * SSM_CH
    nstate = n_groups * SSM_P
    assert page == QB and t % (CMP_STRIDE * LANES) == 0 and past % (CMP_STRIDE * LANES) == 0
    assert win_buf == WINDOW and t >= WINDOW and t_new <= 8 and ssm_w % LANES == 0

    l = 0
    w = w_in[l]
    c0 = aw + 6 * kvw
    n_gate = N_HEADS * N_BRANCH
    wa = w[:, :c0].astype(BF16)
    wng = jnp.pad(w[:, c0:c0 + n_gate], ((0, 0), (0, LANES - n_gate))).astype(BF16)
    wsu = w[:, c0 + n_gate:c0 + n_gate + ssm_w].astype(BF16)
    wmg = w[:, c0 + n_gate + ssm_w:].astype(BF16)
    g_attn = attn_norm[l].reshape(1, d)

    def cmp_weights(pe, w1, w2):
        half = CMP_STRIDE * HD
        w1cat = jnp.concatenate([w1[:half], w1[half:]], axis=1).astype(BF16)
        pe2 = jnp.pad(pe.reshape(2, half), ((0, 6), (0, 0))).astype(BF16)
        return w1cat, w2.astype(BF16), pe2
    cw = cmp_weights(cmp_pe_k[l], cmp_w1_k[l], cmp_w2_k[l]) + cmp_weights(cmp_pe_v[l], cmp_w1_v[l], cmp_w2_v[l])

    abr, abi, bbr_t, bbi_t = _ssm_params(ssm_a_re[l], ssm_a_im[l], ssm_log_dt[l],
                                         jnp.swapaxes(ssm_b_re[l], 1, 2), jnp.swapaxes(ssm_b_im[l], 1, 2))
    n_slab = ssm_w // LANES
    sw = nstate // n_slab
    bd_r = _block_diag(bbr_t)
    bd_i = _block_diag(bbi_t)
    bd = jnp.stack([jnp.concatenate([bd_r[s * LANES:(s + 1) * LANES, s * sw:(s + 1) * sw],
                                     bd_i[s * LANES:(s + 1) * LANES, s * sw:(s + 1) * sw]], axis=1)
                    for s in range(n_slab)]).astype(BF16)
    cd_r = _block_diag(jnp.swapaxes(ssm_c_re[l], 1, 2))
    cd_i = _block_diag(jnp.swapaxes(ssm_c_im[l], 1, 2))
    cd = jnp.stack([jnp.concatenate([cd_r[s * sw:(s + 1) * sw, s * LANES:(s + 1) * LANES],
                                     -cd_i[s * sw:(s + 1) * sw, s * LANES:(s + 1) * LANES]], axis=0)
                    for s in range(n_slab)]).astype(BF16)
    a_r = abr.reshape(1, nstate)
    a_i = abi.reshape(1, nstate)
    dvec = ssm_d[l].reshape(1, ssm_w)
    wglu = w_glu[l].astype(BF16)
    bglu = b_glu[l].reshape(1, ssm_w)

    back_w = (w_att_br[l].astype(BF16), w_ssm_br[l].astype(BF16), w_o[l].astype(BF16),
              ffn_norm[l].reshape(1, d), w_ffn_gate[l].astype(BF16), w_ffn_up[l].astype(BF16),
              w_ffn_down[l].astype(BF16), ple_norm[l].reshape(1, d), w_ple_gate[l].astype(BF16),
              w_ple[l].astype(BF16), final_norm.reshape(1, d))
    d_ff = w_ffn_gate.shape[2]
    ff_chunk = d_ff // 2 if (d_ff // 2) % LANES == 0 else d_ff

    xp = x_prompt.reshape(nb * t, d)
    (q, kc, vc, ks, vs, kw, vw, ksb, vsb, kwb, vwb, ng, su, mg) = _front(xp, nb, t, g_attn, wa, wng, wsu, wmg, 512)
    ncp = t // CMP_STRIDE
    kcc, vcc = _compress_prompt(kc.reshape(nb, ncp, CMP_STRIDE * kvw), vc.reshape(nb, ncp, CMP_STRIDE * kvw), cw)
    tiles, ut, ovl = _prompt_tables(rel_bias, t)
    e_p = _expand_table(t // QB)
    o_p = _nsa_prompt(q.reshape(nb, t, aw), ng.reshape(nb, t, LANES), kcc, vcc,
                      ksb.reshape(nb, t, kvw), vsb.reshape(nb, t, kvw), kwb.reshape(nb, t, kvw),
                      vwb.reshape(nb, t, kvw), ut, tiles, ovl, e_p)
    zeros_state = jnp.zeros((nb, nstate), F32)
    so_p, sr_p, si_p = _ssm(su.reshape(t * nb, ssm_w), zeros_state, zeros_state, a_r, a_i, bd, cd, dvec, wglu, bglu,
                            nb, 256)
    y_p = _back(xp, o_p.reshape(nb * t, aw), so_p.reshape(t, nb * ssm_w), mg, p_prompt[l].reshape(nb * t, -1),
                back_w, nb, t, 256, ff_chunk)

    n_s = ns * t_new
    xs = x_sample.reshape(n_s, d)
    (q_s, kc_s, vc_s, ks_s, vs_s, kw_s, vw_s, _, _, _, _, ng_s, su_s, mg_s) = _front(
        xs, 1, n_s, g_attn, wa, wng, wsu, wmg, n_s)
    chunk_rows = page // CMP_STRIDE
    kcc_s, vcc_s = _compress_sample(page_table,
                                    cache_k_cmp[l].reshape(n_phys, chunk_rows, CMP_STRIDE * kvw),
                                    cache_v_cmp[l].reshape(n_phys, chunk_rows, CMP_STRIDE * kvw), cw)
    bc, tiles_s, bw, ovl_s, n_sel_s = _sample_tables(rel_bias, past, t_new, win_buf)
    e_s = _expand_table(past // QB)
    rows_s = GROUP * t_new
    qk_s = q_s.reshape(ns, t_new, N_KV, GROUP, HD).transpose(0, 2, 3, 1, 4).reshape(ns, N_KV, rows_s, HD)
    gate_s = ng_s[:, :n_gate].reshape(ns, t_new, N_KV, GROUP, N_BRANCH).transpose(0, 2, 3, 1, 4)
    gate_s = jnp.pad(gate_s.reshape(ns, N_KV, rows_s, N_BRANCH), ((0, 0), (0, 0), (0, 0), (0, LANES - N_BRANCH)))
    pad8 = lambda a: jnp.pad(a.reshape(ns, t_new, kvw), ((0, 0), (0, 8 - t_new), (0, 0)))
    o_s = _nsa_sample(page_table, qk_s, gate_s, kcc_s, vcc_s,
                      cache_k_sel[l].reshape(n_phys, page, kvw), cache_v_sel[l].reshape(n_phys, page, kvw),
                      pad8(ks_s), pad8(vs_s),
                      cache_k_win[l].reshape(ns, win_buf, kvw), cache_v_win[l].reshape(ns, win_buf, kvw),
                      pad8(kw_s), pad8(vw_s), bc, tiles_s, bw, ovl_s, e_s, n_sel_s, past, t_new)
    o_s = o_s.reshape(ns, N_KV, GROUP, t_new, HD).transpose(0, 3, 1, 2, 4).reshape(n_s, aw).astype(BF16)
    su_ts = su_s.reshape(ns, t_new, ssm_w).transpose(1, 0, 2).reshape(n_s, ssm_w)
    so_ts, sr_s, si_s = _ssm(su_ts, state_ssm_re[l].reshape(ns, nstate), state_ssm_im[l].reshape(ns, nstate),
                             a_r, a_i, bd, cd, dvec, wglu, bglu, ns, t_new)
    so_s = so_ts.reshape(t_new, ns, ssm_w).transpose(1, 0, 2).reshape(n_s, ssm_w)
    y_s = _back(xs, o_s, so_s, mg_s, p_sample[l].reshape(n_s, -1), back_w, 1, n_s, n_s, ff_chunk)

    kv5 = lambda a, b_, t_: a.reshape(1, b_, t_, N_KV, HD)
    keep = min(WINDOW, t)
    win_p = lambda a: a.reshape(nb, t, kvw)[:, t - keep:].reshape(1, nb, keep, N_KV, HD)
    win_s = lambda cache, new: jnp.concatenate(
        [cache[l], new.reshape(ns, t_new, N_KV, HD)], axis=1)[:, t_new:][None]
    st = lambda a, b_: a.reshape(1, b_, n_groups, SSM_P)
    return (y_p.reshape(nb, t, d), y_s.reshape(ns, t_new, d),
            kv5(kc, nb, t), kv5(vc, nb, t), kv5(ks, nb, t), kv5(vs, nb, t), win_p(kw), win_p(vw),
            st(sr_p, nb), st(si_p, nb),
            kv5(kc_s, ns, t_new), kv5(vc_s, ns, t_new), kv5(ks_s, ns, t_new), kv5(vs_s, ns, t_new),
            win_s(cache_k_win, kw_s), win_s(cache_v_win, vw_s),
            st(sr_s, ns), st(si_s, ns))
```

```python
import functools
import math

import numpy as np
import jax
import jax.numpy as jnp
from jax import lax
from jax.experimental import pallas as pl
from jax.experimental.pallas import tpu as pltpu

F32 = jnp.float32
BF16 = jnp.bfloat16

N_HEADS = 8
N_KV = 2
HD = 64
GROUP = N_HEADS // N_KV
N_BRANCH = 3
CMP_BLOCK = 32
CMP_STRIDE = 16
CMP_HIDDEN = 256
SEL_BLOCK = 64
SEL_TOPK = 16
WINDOW = 512
QB = 128
N_BUCKETS = 32
REL_MAX_DIST = 128
SSM_CH = 16
SSM_P = 64
EPS = 1e-6
NEG = -1e30
NEG_TEST = -1e29
FORCE = 1e9
BIG = 1e30
LANES = 128
VMEM_LIMIT = 56 * 1024 * 1024


def _cparams(sem):
    return pltpu.CompilerParams(dimension_semantics=sem, vmem_limit_bytes=VMEM_LIMIT)


def _const_spec(shape):
    nd = len(shape)
    return pl.BlockSpec(shape, lambda *_: (0,) * nd)


def _rms(x, g):
    return x * lax.rsqrt(jnp.mean(x * x, axis=-1, keepdims=True) + EPS) * g


def _gelu(x):
    return x * (0.5 * (1.0 + jnp.tanh(math.sqrt(2.0 / math.pi) * (x + 0.044715 * (x * x * x)))))


def _sigmoid(x):
    return 1.0 / (1.0 + jnp.exp(-x))


def _dot(a, b):
    return jnp.dot(a, b, preferred_element_type=F32)


def _dot_t(a, b):
    return lax.dot_general(a, b, (((1,), (1,)), ((), ())), preferred_element_type=F32)


def _masked_softmax(s):
    valid = s > NEG_TEST
    m = jnp.max(s, axis=-1, keepdims=True)
    e = jnp.where(valid, jnp.exp(s - m), 0.0)
    return e / jnp.maximum(jnp.sum(e, axis=-1, keepdims=True), 1e-30)


def _front_kernel(x_ref, g_ref, wa_ref, wng_ref, wsu_ref, wmg_ref,
                  q_ref, kc_ref, vc_ref, ks_ref, vs_ref, kw_ref, vw_ref,
                  ksb_ref, vsb_ref, kwb_ref, vwb_ref, ng_ref, su_ref, mg_ref):
    u = _rms(x_ref[...], g_ref[...]).astype(BF16)
    za = _dot(u, wa_ref[...])
    aw = N_HEADS * HD
    q_ref[...] = (za[:, :aw] * (HD ** -0.5)).astype(BF16)
    kvw = N_KV * HD
    f32_outs = (kc_ref, vc_ref, ks_ref, vs_ref, kw_ref, vw_ref)
    bf_outs = (None, None, ksb_ref, vsb_ref, kwb_ref, vwb_ref)
    for i in range(6):
        blk = za[:, aw + i * kvw: aw + (i + 1) * kvw]
        f32_outs[i][...] = blk
        if bf_outs[i] is not None:
            bf_outs[i][...] = blk.astype(BF16)
    ng_ref[...] = _sigmoid(_dot(u, wng_ref[...]))
    su_ref[...] = _dot(u, wsu_ref[...]).astype(su_ref.dtype)
    mg_ref[...] = _sigmoid(_dot(u, wmg_ref[...])).astype(BF16)


def _front(x2d, nb, t, g, wa, wng, wsu, wmg, tm):
    n, d = x2d.shape
    nt = t // tm
    aw, kvw = N_HEADS * HD, N_KV * HD
    sw, mw = wsu.shape[1], wmg.shape[1]
    row = lambda b, i: (b * nt + i, 0)
    outs = ([jax.ShapeDtypeStruct((n, aw), BF16)]
            + [jax.ShapeDtypeStruct((n, kvw), F32)] * 6
            + [jax.ShapeDtypeStruct((n, kvw), BF16)] * 4
            + [jax.ShapeDtypeStruct((n, LANES), F32),
               jax.ShapeDtypeStruct((t, nb * sw), BF16),
               jax.ShapeDtypeStruct((n, mw), BF16)])
    out_specs = ([pl.BlockSpec((tm, aw), row)]
                 + [pl.BlockSpec((tm, kvw), row)] * 10
                 + [pl.BlockSpec((tm, LANES), row),
                    pl.BlockSpec((tm, sw), lambda b, i: (i, b)),
                    pl.BlockSpec((tm, mw), row)])
    return pl.pallas_call(
        _front_kernel,
        grid=(nb, nt),
        in_specs=[pl.BlockSpec((tm, d), row), _const_spec(g.shape), _const_spec(wa.shape),
                  _const_spec(wng.shape), _const_spec(wsu.shape), _const_spec(wmg.shape)],
        out_specs=out_specs,
        out_shape=outs,
        compiler_params=_cparams(("parallel", "parallel")),
        name="front",
    )(x2d, g, wa, wng, wsu, wmg)


def _compress_compute(load_rows, c, w1_ref, w2_ref, pe_ref, a_scr):
    rc_n = min(c, 256)
    lo = lax.broadcasted_iota(jnp.int32, (rc_n, LANES), 1) < HD
    w1 = w1_ref[...]
    for rc in range(c // rc_n):
        x = load_rows(rc * rc_n, rc_n)
        cols = [x[:, r * LANES:(r + 1) * LANES] for r in range(CMP_STRIDE)]
        rol = [pltpu.roll(col, HD, 1) for col in cols]
        for kh in range(N_KV):
            if kh == 0:
                parts = [jnp.where(lo, cols[2 * j], rol[2 * j + 1]) for j in range(CMP_STRIDE // 2)]
            else:
                parts = [jnp.where(lo, rol[2 * j], cols[2 * j + 1]) for j in range(CMP_STRIDE // 2)]
            xh = jnp.concatenate(parts, axis=1).astype(BF16)
            a_scr[kh, rc * rc_n:(rc + 1) * rc_n, :] = _dot(xh, w1)
    pw = _dot(pe_ref[...], w1)
    peb = pw[0:1, :CMP_HIDDEN] + pw[1:2, CMP_HIDDEN:]
    w2 = w2_ref[...]
    outs = []
    for kh in range(N_KV):
        a = a_scr[kh]
        hid = a[:, :CMP_HIDDEN] + pltpu.roll(a[:, CMP_HIDDEN:], c - 1, 0) + peb
        outs.append(_dot(_gelu(hid).astype(BF16), w2))
    return jnp.concatenate(outs, axis=1)


def _compress_prompt_kernel(xk_ref, xv_ref, w1k_ref, w2k_ref, pek_ref, w1v_ref, w2v_ref, pev_ref,
                            ok_ref, ov_ref, a_scr):
    c = xk_ref.shape[1]
    ok_ref[0] = _compress_compute(lambda r0, rn: xk_ref[0, r0:r0 + rn, :], c,
                                  w1k_ref, w2k_ref, pek_ref, a_scr).astype(BF16)
    ov_ref[0] = _compress_compute(lambda r0, rn: xv_ref[0, r0:r0 + rn, :], c,
                                  w1v_ref, w2v_ref, pev_ref, a_scr).astype(BF16)


def _compress_prompt(xk, xv, cw):
    nb, c, cw_lanes = xk.shape
    wspecs = [_const_spec(w.shape) for w in cw]
    blk = pl.BlockSpec((1, c, cw_lanes), lambda b: (b, 0, 0))
    oblk = pl.BlockSpec((1, c, LANES), lambda b: (b, 0, 0))
    return pl.pallas_call(
        _compress_prompt_kernel,
        grid=(nb,),
        in_specs=[blk, blk] + wspecs,
        out_specs=[oblk, oblk],
        out_shape=[jax.ShapeDtypeStruct((nb, c, LANES), BF16)] * 2,
        scratch_shapes=[pltpu.VMEM((N_KV, c, 2 * CMP_HIDDEN), F32)],
        compiler_params=_cparams(("parallel",)),
        name="compress_prompt",
    )(xk, xv, *cw)


def _page_gather_start(pt_ref, seq, pools, bufs, sems, slot, n_pages, rows):
    def body(p, carry):
        page = pt_ref[seq, p]
        for i, (pool, buf) in enumerate(zip(pools, bufs)):
            pltpu.make_async_copy(pool.at[page], buf.at[slot, pl.ds(p * rows, rows)], sems.at[i, slot]).start()
        return carry
    lax.fori_loop(0, n_pages, body, 0)


def _page_gather_wait(pools, bufs, sems, slot, n_pages, rows):
    def body(p, carry):
        for i, (pool, buf) in enumerate(zip(pools, bufs)):
            pltpu.make_async_copy(pool.at[0], buf.at[slot, pl.ds(p * rows, rows)], sems.at[i, slot]).wait()
        return carry
    lax.fori_loop(0, n_pages, body, 0)


def _paged_prefetch(pt_ref, pools, bufs, sems, n_pages, rows):
    s = pl.program_id(0)
    slot = s % 2

    @pl.when(s == 0)
    def _():
        _page_gather_start(pt_ref, 0, pools, bufs, sems, 0, n_pages, rows)

    @pl.when(s + 1 < pl.num_programs(0))
    def _():
        _page_gather_start(pt_ref, s + 1, pools, bufs, sems, 1 - slot, n_pages, rows)

    _page_gather_wait(pools, bufs, sems, slot, n_pages, rows)
    return slot


def _compress_sample_kernel(pt_ref, kpool, vpool, w1k_ref, w2k_ref, pek_ref, w1v_ref, w2v_ref, pev_ref,
                            ok_ref, ov_ref, kbuf, vbuf, sems, a_scr):
    n_pages = pt_ref.shape[1]
    rows = kpool.shape[1]
    c = n_pages * rows
    slot = _paged_prefetch(pt_ref, (kpool, vpool), (kbuf, vbuf), sems, n_pages, rows)
    ok_ref[0] = _compress_compute(lambda r0, rn: kbuf[slot, pl.ds(r0, rn), :], c,
                                  w1k_ref, w2k_ref, pek_ref, a_scr).astype(BF16)
    ov_ref[0] = _compress_compute(lambda r0, rn: vbuf[slot, pl.ds(r0, rn), :], c,
                                  w1v_ref, w2v_ref, pev_ref, a_scr).astype(BF16)


def _compress_sample(page_table, kpool, vpool, cw):
    ns, n_pages = page_table.shape
    rows, width = kpool.shape[1:]
    c = n_pages * rows
    any_spec = pl.BlockSpec(memory_space=pl.ANY)
    wspecs = [pl.BlockSpec(w.shape, lambda s, pt, nd=w.ndim: (0,) * nd) for w in cw]
    oblk = pl.BlockSpec((1, c, LANES), lambda s, pt: (s, 0, 0))
    return pl.pallas_call(
        _compress_sample_kernel,
        grid_spec=pltpu.PrefetchScalarGridSpec(
            num_scalar_prefetch=1,
            grid=(ns,),
            in_specs=[any_spec, any_spec] + wspecs,
            out_specs=[oblk, oblk],
            scratch_shapes=[pltpu.VMEM((2, c, width), F32), pltpu.VMEM((2, c, width), F32),
                            pltpu.SemaphoreType.DMA((2, 2)),
                            pltpu.VMEM((N_KV, c, 2 * CMP_HIDDEN), F32)]),
        out_shape=[jax.ShapeDtypeStruct((ns, c, LANES), BF16)] * 2,
        compiler_params=_cparams(("arbitrary",)),
        name="compress_sample",
    )(page_table, kpool, vpool, *cw)


def _rank_select(score, blk, n_real, axis):
    rank = jnp.zeros(score.shape, F32)
    for kk in range(n_real):
        if axis == 0:
            other = jnp.broadcast_to(score[kk:kk + 1, :], score.shape)
        else:
            other = jnp.broadcast_to(score[:, kk:kk + 1], score.shape)
        beats = (other > score) | ((other == score) & (blk > kk))
        rank = rank + jnp.where(beats, 1.0, 0.0)
    return jnp.where(rank < SEL_TOPK, 1.0, 0.0)


def _block_scores(imp, blk, t):
    cur = t // SEL_BLOCK
    forced = (blk == 0) | (blk == cur) | (blk == cur - 1)
    valid = blk * SEL_BLOCK <= t
    return jnp.where(valid, jnp.where(forced, FORCE, imp), NEG)


def _flash_step(carry, s, vt):
    m, l, acc = carry
    m_new = jnp.maximum(m, jnp.max(s, axis=-1, keepdims=True))
    p = jnp.where(s > NEG_TEST, jnp.exp(s - m_new), 0.0)
    alpha = jnp.exp(m - m_new)
    l = alpha * l + jnp.sum(p, axis=-1, keepdims=True)
    acc = alpha * acc + _dot(p.astype(BF16), vt)
    return m_new, l, acc


def _flash_init(rows):
    return (jnp.full((rows, 1), NEG, F32), jnp.zeros((rows, 1), F32), jnp.zeros((rows, HD), F32))


def _flash_done(carry):
    _, l, acc = carry
    return acc / jnp.maximum(l, 1e-30)


def _nsa_prompt_kernel(q_ref, ng_ref, kc_ref, vc_ref, ks_ref, vs_ref, kw_ref, vw_ref,
                       ut_ref, tiles_ref, ovl_ref, e_ref, o_ref, *, n_sel):
    ib = pl.program_id(1)
    q = q_ref[0]
    ng = ng_ref[0]
    ncp = kc_ref.shape[1]
    rows = GROUP * QB
    t_row = ib * QB + lax.broadcasted_iota(jnp.int32, (n_sel, QB), 1)
    blk_t = lax.broadcasted_iota(jnp.int32, (n_sel, QB), 0)
    heads_out = []
    for k in range(N_KV):
        lanes = slice(k * HD, (k + 1) * HD)
        qk = jnp.concatenate([q[:, (GROUP * k + g) * HD:(GROUP * k + g + 1) * HD] for g in range(GROUP)], axis=0)

        kc = kc_ref[0][:, lanes]
        vc = vc_ref[0][:, lanes]
        bias_t = ut_ref[k, pl.ds(pl.multiple_of(ncp - (QB // CMP_STRIDE) * ib, 8), ncp), :]
        p_c = _masked_softmax(_dot_t(qk, kc) + bias_t.T)
        o_c = _dot(p_c.astype(BF16), vc)
        psum = p_c[0:QB]
        for g in range(1, GROUP):
            psum = psum + p_c[g * QB:(g + 1) * QB]
        imp = jnp.dot(psum, ovl_ref[...], preferred_element_type=F32, precision=lax.Precision.HIGHEST)

        score = _block_scores(imp.T[0:n_sel], blk_t, t_row)
        sel_t = _rank_select(score, blk_t, n_sel, 0)
        if n_sel < LANES:
            sel_t = jnp.concatenate([sel_t, jnp.zeros((LANES - n_sel, QB), F32)], axis=0)
        sel = sel_t.T.astype(BF16)

        def sel_body(jt, carry):
            k0 = pl.multiple_of(jt * QB, QB)
            kt = ks_ref[0, pl.ds(k0, QB), lanes]
            vt = vs_ref[0, pl.ds(k0, QB), lanes]
            tidx = jnp.where(jt == ib, 0, jnp.where(jt == ib - 1, 1, 2))
            keep = _dot(sel, e_ref[jt])
            keep = jnp.concatenate([keep] * GROUP, axis=0)
            s = jnp.where(keep > 0.5, _dot_t(qk, kt) + tiles_ref[tidx, k], NEG)
            return _flash_step(carry, s, vt)

        o_s = _flash_done(lax.fori_loop(0, ib + 1, sel_body, _flash_init(rows)))

        s_parts, v_parts = [], []
        for w, tidx in enumerate((0, 1, 2, 2, 3)):
            jt = ib - w
            k0 = pl.multiple_of(jnp.maximum(jt, 0) * QB, QB)
            kt = kw_ref[0, pl.ds(k0, QB), lanes]
            v_parts.append(vw_ref[0, pl.ds(k0, QB), lanes])
            s_parts.append(jnp.where(jt >= 0, _dot_t(qk, kt) + tiles_ref[tidx, k], NEG))
        p_w = _masked_softmax(jnp.concatenate(s_parts, axis=1))
        o_w = _dot(p_w.astype(BF16), jnp.concatenate(v_parts, axis=0))

        for g in range(GROUP):
            j = (GROUP * k + g) * N_BRANCH
            r = slice(g * QB, (g + 1) * QB)
            heads_out.append(ng[:, j:j + 1] * o_c[r] + ng[:, j + 1:j + 2] * o_s[r] + ng[:, j + 2:j + 3] * o_w[r])
    o_ref[0] = jnp.concatenate(heads_out, axis=1).astype(BF16)


def _nsa_prompt(q, ng, kc, vc, ks, vs, kw, vw, ut, tiles, ovl, e):
    nb, t, aw = q.shape
    nq = t // QB
    ncp = kc.shape[1]
    kvw = N_KV * HD
    qblk = lambda b, i: (b, i, 0)
    full = lambda b, i: (b, 0, 0)
    return pl.pallas_call(
        functools.partial(_nsa_prompt_kernel, n_sel=t // SEL_BLOCK),
        grid=(nb, nq),
        in_specs=[pl.BlockSpec((1, QB, aw), qblk), pl.BlockSpec((1, QB, LANES), qblk),
                  pl.BlockSpec((1, ncp, kvw), full), pl.BlockSpec((1, ncp, kvw), full),
                  pl.BlockSpec((1, t, kvw), full), pl.BlockSpec((1, t, kvw), full),
                  pl.BlockSpec((1, t, kvw), full), pl.BlockSpec((1, t, kvw), full),
                  _const_spec(ut.shape), _const_spec(tiles.shape), _const_spec(ovl.shape), _const_spec(e.shape)],
        out_specs=pl.BlockSpec((1, QB, aw), qblk),
        out_shape=jax.ShapeDtypeStruct((nb, t, aw), BF16),
        compiler_params=_cparams(("parallel", "arbitrary")),
        name="nsa_prompt",
    )(q, ng, kc, vc, ks, vs, kw, vw, ut, tiles, ovl, e)


def _nsa_sample_kernel(pt_ref, q_ref, gate_ref, kc_ref, vc_ref, kpool, vpool, ksn_ref, vsn_ref,
                       kwin_ref, vwin_ref, kwn_ref, vwn_ref, bc_ref, bs_ref, bw_ref, ovl_ref, e_ref,
                       o_ref, kbuf, vbuf, sems, *, n_sel, past, t_new):
    n_pages = pt_ref.shape[1]
    page = kpool.shape[1]
    wb = kwin_ref.shape[1]
    slot = _paged_prefetch(pt_ref, (kpool, vpool), (kbuf, vbuf), sems, n_pages, page)
    q = q_ref[0]
    nbp = ovl_ref.shape[1]
    nb_past = past // SEL_BLOCK
    pad_new = jnp.zeros((QB - ksn_ref.shape[1], N_KV * HD), F32)
    new_tile = lambda ref: jnp.concatenate([ref[0], pad_new], axis=0).astype(BF16)

    p_c = _masked_softmax(_dot_t(q, kc_ref[0]) + bc_ref[...])
    o_c = _dot(p_c.astype(BF16), vc_ref[0])
    parts = []
    for k in range(N_KV):
        base = k * GROUP * t_new
        ps = p_c[base:base + t_new]
        for g in range(1, GROUP):
            ps = ps + p_c[base + g * t_new:base + (g + 1) * t_new]
        parts.append(ps)
    psum = jnp.concatenate(parts, axis=0)
    imp = jnp.dot(psum, ovl_ref[...], preferred_element_type=F32, precision=lax.Precision.HIGHEST)
    blk = lax.broadcasted_iota(jnp.int32, imp.shape, 1)
    tpos = past + lax.broadcasted_iota(jnp.int32, imp.shape, 0) % t_new
    sel = _rank_select(_block_scores(imp, blk, tpos), blk, n_sel, 1)
    sel = jnp.concatenate([sel[k * t_new:(k + 1) * t_new] for k in range(N_KV) for _ in range(GROUP)], axis=0)

    mask_add = _dot((sel[:, 0:LANES] - 1.0).astype(BF16), e_ref[...])
    s_past = _dot_t(q, kbuf[slot].astype(BF16)) + bs_ref[:, 0:past] + mask_add
    s_new = _dot_t(q, new_tile(ksn_ref)) + bs_ref[:, past:]
    s_new = jnp.where(sel[:, nb_past:nb_past + 1] > 0.5, s_new, NEG)
    p_s = _masked_softmax(jnp.concatenate([s_past, s_new], axis=1)).astype(BF16)
    o_s = _dot(p_s[:, 0:past], vbuf[slot].astype(BF16)) + _dot(p_s[:, past:], new_tile(vsn_ref))

    s_w = jnp.concatenate([_dot_t(q, kwin_ref[0].astype(BF16)), _dot_t(q, new_tile(kwn_ref))], axis=1)
    p_w = _masked_softmax(s_w + bw_ref[...]).astype(BF16)
    o_w = _dot(p_w[:, 0:wb], vwin_ref[0].astype(BF16)) + _dot(p_w[:, wb:], new_tile(vwn_ref))

    gate = gate_ref[0]
    o_ref[0] = gate[:, 0:1] * o_c + gate[:, 1:2] * o_s + gate[:, 2:3] * o_w


def _nsa_sample(page_table, q, gate, kc, vc, kpool, vpool, ksn, vsn, kwin, vwin, kwn, vwn,
                bc, bs, bw, ovl, e, n_sel, past, t_new):
    ns, n_pages = page_table.shape
    page, kvw = kpool.shape[1:]
    rows = q.shape[1]
    seq3 = lambda s, pt: (s, 0, 0)
    any_spec = pl.BlockSpec(memory_space=pl.ANY)
    cs = lambda a: pl.BlockSpec(a.shape, lambda s, pt, nd=a.ndim: (0,) * nd)
    per_seq = lambda a: pl.BlockSpec((1,) + a.shape[1:], seq3)
    return pl.pallas_call(
        functools.partial(_nsa_sample_kernel, n_sel=n_sel, past=past, t_new=t_new),
        grid_spec=pltpu.PrefetchScalarGridSpec(
            num_scalar_prefetch=1,
            grid=(ns,),
            in_specs=[per_seq(q), per_seq(gate), per_seq(kc), per_seq(vc), any_spec, any_spec,
                      per_seq(ksn), per_seq(vsn), per_seq(kwin), per_seq(vwin), per_seq(kwn), per_seq(vwn),
                      cs(bc), cs(bs), cs(bw), cs(ovl), cs(e)],
            out_specs=pl.BlockSpec((1, rows, kvw), seq3),
            scratch_shapes=[pltpu.VMEM((2, n_pages * page, kvw), F32), pltpu.VMEM((2, n_pages * page, kvw), F32),
                            pltpu.SemaphoreType.DMA((2, 2))]),
        out_shape=jax.ShapeDtypeStruct((ns, rows, kvw), F32),
        compiler_params=_cparams(("arbitrary",)),
        name="nsa_sample",
    )(page_table, q, gate, kc, vc, kpool, vpool, ksn, vsn, kwin, vwin, kwn, vwn, bc, bs, bw, ovl, e)


def _ssm_param_kernel(ar_ref, ai_ref, ldt_ref, br_ref, bi_ref, abr_ref, abi_ref, bbr_ref, bbi_ref):
    ar = ar_ref[...]
    ai = ai_ref[...]
    dt = jnp.exp(ldt_ref[...])
    mag = jnp.exp(ar * dt)
    abr = mag * jnp.cos(ai * dt)
    abi = mag * jnp.sin(ai * dt)
    den = ar * ar + ai * ai
    nr, ni = abr - 1.0, abi
    fr = (nr * ar + ni * ai) / den
    fi = (ni * ar - nr * ai) / den
    abr_ref[...] = abr
    abi_ref[...] = abi
    for g in range(ar.shape[0]):
        br = br_ref[g]
        bi = bi_ref[g]
        frg = fr[g:g + 1, :]
        fig = fi[g:g + 1, :]
        bbr_ref[g] = frg * br - fig * bi
        bbi_ref[g] = frg * bi + fig * br


def _ssm_params(a_re, a_im, log_dt, b_re_t, b_im_t):
    g, p = a_re.shape
    return pl.pallas_call(
        _ssm_param_kernel,
        out_shape=[jax.ShapeDtypeStruct((g, p), F32)] * 2 + [jax.ShapeDtypeStruct(b_re_t.shape, F32)] * 2,
        name="ssm_params",
    )(a_re, a_im, log_dt.reshape(g, 1), b_re_t, b_im_t)


def _ssm_kernel(u_ref, h0r_ref, h0i_ref, ar_ref, ai_ref, bd_ref, cd_ref, d_ref, wglu_ref, bglu_ref,
                so_ref, hr_ref, hi_ref, xr_scr, xi_scr, *, bt):
    i = pl.program_id(0)
    rows, width = u_ref.shape
    n_slab = bd_ref.shape[0]
    sw = bd_ref.shape[2] // 2
    u = u_ref[...]

    @pl.when(i == 0)
    def _():
        hr_ref[...] = h0r_ref[...]
        hi_ref[...] = h0i_ref[...]

    for sl in range(n_slab):
        x = _dot(u[:, sl * LANES:(sl + 1) * LANES], bd_ref[sl])
        xr_scr[:, sl * sw:(sl + 1) * sw] = x[:, :sw]
        xi_scr[:, sl * sw:(sl + 1) * sw] = x[:, sw:]

    per = 8 // math.gcd(bt, 8)
    grp = per * bt
    lc = 512
    for c0 in range(0, xr_scr.shape[1], lc):
        cl = slice(c0, c0 + lc)
        a_r = jnp.broadcast_to(ar_ref[:, cl], (bt, lc))
        a_i = jnp.broadcast_to(ai_ref[:, cl], (bt, lc))

        def step(j, carry):
            h_r, h_i = carry
            r0 = pl.multiple_of(j * grp, grp)
            xr = xr_scr[pl.ds(r0, grp), cl]
            xi = xi_scr[pl.ds(r0, grp), cl]
            out_r, out_i = [], []
            for s in range(per):
                n_r = a_r * h_r - a_i * h_i + xr[s * bt:(s + 1) * bt]
                n_i = a_r * h_i + a_i * h_r + xi[s * bt:(s + 1) * bt]
                h_r, h_i = n_r, n_i
                out_r.append(h_r)
                out_i.append(h_i)
            xr_scr[pl.ds(r0, grp), cl] = jnp.concatenate(out_r, axis=0) if per > 1 else out_r[0]
            xi_scr[pl.ds(r0, grp), cl] = jnp.concatenate(out_i, axis=0) if per > 1 else out_i[0]
            return h_r, h_i

        h_r, h_i = lax.fori_loop(0, rows // grp, step, (hr_ref[:, cl], hi_ref[:, cl]))
        hr_ref[:, cl] = h_r
        hi_ref[:, cl] = h_i

    ys = []
    for sl in range(n_slab):
        hcat = jnp.concatenate([xr_scr[:, sl * sw:(sl + 1) * sw], xi_scr[:, sl * sw:(sl + 1) * sw]], axis=1)
        ys.append(_dot(hcat.astype(BF16), cd_ref[sl]))
    y = jnp.concatenate(ys, axis=1) + d_ref[...] * u.astype(F32)
    z = _gelu(y)
    so_ref[...] = (z * _sigmoid(_dot(z.astype(BF16), wglu_ref[...]) + bglu_ref[...])).astype(so_ref.dtype)


def _ssm(u, h0r, h0i, ar, ai, bd, cd, dvec, wglu, bglu, bt, tt):
    n, width = u.shape
    rows = tt * bt
    nstate = ar.shape[1]
    cst = [_const_spec(a.shape) for a in (h0r, h0i, ar, ai, bd, cd, dvec, wglu, bglu)]
    st_spec = _const_spec((bt, nstate))
    return pl.pallas_call(
        functools.partial(_ssm_kernel, bt=bt),
        grid=(n // rows,),
        in_specs=[pl.BlockSpec((rows, width), lambda i: (i, 0))] + cst,
        out_specs=[pl.BlockSpec((rows, width), lambda i: (i, 0)), st_spec, st_spec],
        out_shape=[jax.ShapeDtypeStruct((n, width), BF16), jax.ShapeDtypeStruct((bt, nstate), F32),
                   jax.ShapeDtypeStruct((bt, nstate), F32)],
        scratch_shapes=[pltpu.VMEM((rows, nstate), F32), pltpu.VMEM((rows, nstate), F32)],
        compiler_params=_cparams(("arbitrary",)),
        name="ssm",
    )(u, h0r, h0i, ar, ai, bd, cd, dvec, wglu, bglu)


def _back_kernel(h_ref, o_ref, so_ref, mg_ref, p_ref, watt_ref, wssm_ref, wo_ref, fn_ref, wg_ref, wu_ref, wd_ref,
                 pn_ref, wpg_ref, wple_ref, fin_ref, y_ref, *, ff_chunk):
    d = h_ref.shape[1]
    a = _dot(o_ref[...], watt_ref[...])
    s = _dot(so_ref[...], wssm_ref[...])
    mg = mg_ref[...].astype(F32)
    h = h_ref[...] + _dot((mg[:, :d] * a + mg[:, d:] * s).astype(BF16), wo_ref[...])
    f = _rms(h, fn_ref[...]).astype(BF16)
    ffn = jnp.zeros_like(h)
    for c0 in range(0, wg_ref.shape[1], ff_chunk):
        gate = _dot(f, wg_ref[:, c0:c0 + ff_chunk])
        up = _dot(f, wu_ref[:, c0:c0 + ff_chunk])
        ffn = ffn + _dot((gate * _sigmoid(gate) * up).astype(BF16), wd_ref[c0:c0 + ff_chunk, :])
    h = h + ffn
    g = _sigmoid(_dot(_rms(h, pn_ref[...]).astype(BF16), wpg_ref[...]))
    h = h + g * _dot(p_ref[...].astype(BF16), wple_ref[...])
    y_ref[...] = _rms(h, fin_ref[...])


def _back(h2d, o2d, so_tb, mg, p2d, weights, nb, t, tm, ff_chunk):
    n, d = h2d.shape
    nt = t // tm
    row = lambda b, i: (b * nt + i, 0)
    wspecs = [pl.BlockSpec(w.shape, lambda b, i, nd=w.ndim: (0,) * nd, pipeline_mode=pl.Buffered(1))
              for w in weights]
    sw = o2d.shape[1]
    return pl.pallas_call(
        functools.partial(_back_kernel, ff_chunk=ff_chunk),
        grid=(nb, nt),
        in_specs=[pl.BlockSpec((tm, d), row), pl.BlockSpec((tm, sw), row),
                  pl.BlockSpec((tm, sw), lambda b, i: (i, b)),
                  pl.BlockSpec((tm, mg.shape[1]), row), pl.BlockSpec((tm, p2d.shape[1]), row)] + wspecs,
        out_specs=pl.BlockSpec((tm, d), row),
        out_shape=jax.ShapeDtypeStruct((n, d), F32),
        compiler_params=_cparams(("parallel", "parallel")),
        name="back",
    )(h2d, o2d, so_tb, mg, p2d, *weights)


def _bucket_np(dist):
    n = np.maximum(dist, 0)
    exact = N_BUCKETS // 2
    nf = np.maximum(n, 1).astype(np.float64)
    large = exact + (np.log(nf / exact) / math.log(REL_MAX_DIST / exact) * (N_BUCKETS - exact)).astype(np.int64)
    return np.where(n < exact, n, np.minimum(large, N_BUCKETS - 1)).astype(np.int32)


def _bias_table(rel_bias, dist, valid):
    onehot = jax.nn.one_hot(jnp.asarray(_bucket_np(dist)), N_BUCKETS, dtype=F32)
    b = jnp.einsum('...b,bh->h...', onehot, rel_bias.astype(F32), precision=lax.Precision.HIGHEST)
    return jnp.where(jnp.asarray(valid)[None], b, NEG)


def _group_rows(b):
    h, r, n = b.shape
    return b.reshape(N_KV, GROUP * r, n)


def _prompt_tables(rel_bias, t):
    i = np.arange(QB)[:, None]
    j = np.arange(QB)[None, :]
    ones = np.ones((QB, QB), bool)
    far = np.full((QB, QB), 4 * QB)
    tiles = jnp.stack([
        _group_rows(_bias_table(rel_bias, i - j, i >= j)),
        _group_rows(_bias_table(rel_bias, QB + i - j, ones)),
        _group_rows(_bias_table(rel_bias, far, ones)),
        _group_rows(_bias_table(rel_bias, far, j > i)),
    ])
    ncp = t // CMP_STRIDE
    m = np.arange(2 * ncp)[:, None] - ncp
    ii = np.arange(QB)[None, :]
    dist = ii - CMP_STRIDE * m - (CMP_BLOCK - 1)
    ut = _bias_table(rel_bias, dist, dist >= 0)
    ut = ut.reshape(N_KV, GROUP, 2 * ncp, QB).transpose(0, 2, 1, 3).reshape(N_KV, 2 * ncp, GROUP * QB)
    n_sel = t // SEL_BLOCK
    n = np.arange(ncp)[:, None]
    jb = np.arange(LANES)[None, :]
    ovl = ((n * CMP_STRIDE < jb * SEL_BLOCK + SEL_BLOCK) & (n * CMP_STRIDE + CMP_BLOCK - 1 >= jb * SEL_BLOCK)
           & (jb < n_sel) & (n < ncp - 1))
    return tiles, ut, jnp.asarray(ovl.astype(np.float32))


def _expand_table(n_tiles):
    r = np.arange(LANES)[None, :, None]
    c = np.arange(QB)[None, None, :]
    j = np.arange(n_tiles)[:, None, None]
    return jnp.asarray((r == (QB // SEL_BLOCK) * (j % (LANES * SEL_BLOCK // QB)) + c // SEL_BLOCK), BF16)


def _sample_tables(rel_bias, past, t_new, win_buf):
    def rows(b):
        return b.reshape(N_HEADS * t_new, b.shape[-1])
    tok = np.arange(t_new)[:, None]
    nc = past // CMP_STRIDE
    n = np.arange(nc)[None, :]
    c_end = n * CMP_STRIDE + CMP_BLOCK - 1
    n_cmp = (past + t_new) // CMP_STRIDE - 1
    bc = rows(_bias_table(rel_bias, past + tok - c_end, (c_end <= past + tok) & (n < n_cmp)))
    js = np.arange(past + QB)[None, :]
    ds = np.where(js < past, past + tok - js, tok - (js - past))
    bs = rows(_bias_table(rel_bias, ds, np.where(js < past, True, (ds >= 0) & (js - past < t_new))))
    jw = np.arange(win_buf + QB)[None, :]
    dw = np.where(jw < win_buf, win_buf + tok - jw, tok - (jw - win_buf))
    valid = np.where(jw < win_buf, (dw >= 0) & (dw < WINDOW), (dw >= 0) & (jw - win_buf < t_new))
    bw = rows(_bias_table(rel_bias, dw, valid))
    n_sel = -(-(past + t_new) // SEL_BLOCK)
    nbp = 2 * LANES
    nn = np.arange(nc)[:, None]
    jb = np.arange(nbp)[None, :]
    ovl = ((nn * CMP_STRIDE < jb * SEL_BLOCK + SEL_BLOCK) & (nn * CMP_STRIDE + CMP_BLOCK - 1 >= jb * SEL_BLOCK)
           & (jb < n_sel) & (nn < n_cmp))
    e = (np.arange(LANES)[:, None] == np.arange(past)[None, :] // SEL_BLOCK).astype(np.float32) * BIG
    return bc, bs, bw, jnp.asarray(ovl.astype(np.float32)), jnp.asarray(e, BF16), n_sel


def _block_diag(blocks):
    g, r, c = blocks.shape
    eye = jnp.eye(g, dtype=blocks.dtype)
    return jnp.einsum('grc,gh->grhc', blocks, eye).reshape(g * r, g * c)


def kernel(x_prompt, x_sample, cache_k_cmp, cache_v_cmp, cache_k_sel, cache_v_sel, cache_k_win, cache_v_win, state_ssm_re, state_ssm_im, page_table, p_prompt, p_sample, rel_bias, final_norm, attn_norm, w_in, cmp_pe_k, cmp_w1_k, cmp_w2_k, cmp_pe_v, cmp_w1_v, cmp_w2_v, ssm_a_re, ssm_a_im, ssm_log_dt, ssm_b_re, ssm_b_im, ssm_c_re, ssm_c_im, ssm_d, w_glu, b_glu, w_att_br, w_ssm_br, w_o, ffn_norm, w_ffn_gate, w_ffn_up, w_ffn_down, ple_norm, w_ple_gate, w_ple):
    depth = w_in.shape[0]
    assert depth == 1, "single-layer trunk"
    nb, t, d = x_prompt.shape
    ns, t_new = x_sample.shape[:2]
    n_phys, page = cache_k_cmp.shape[1:3]
    n_pages = page_table.shape[1]
    past = n_pages * page
    win_buf = cache_k_win.shape[2]
    aw, kvw = N_HEADS * HD, N_KV * HD
    n_groups = ssm_a_re.shape[1]
    ssm_w = n_groups * SSM_CH
    nstate = n_groups * SSM_P
    assert page == QB and t % (CMP_STRIDE * LANES) == 0 and past % (CMP_STRIDE * LANES) == 0
    assert win_buf == WINDOW and t >= WINDOW and t_new <= 8 and ssm_w % LANES == 0

    l = 0
    w = w_in[l]
    c0 = aw + 6 * kvw
    n_gate = N_HEADS * N_BRANCH
    wa = w[:, :c0].astype(BF16)
    wng = jnp.pad(w[:, c0:c0 + n_gate], ((0, 0), (0, LANES - n_gate))).astype(BF16)
    wsu = w[:, c0 + n_gate:c0 + n_gate + ssm_w].astype(BF16)
    wmg = w[:, c0 + n_gate + ssm_w:].astype(BF16)
    g_attn = attn_norm[l].reshape(1, d)

    def cmp_weights(pe, w1, w2):
        half = CMP_STRIDE * HD
        w1cat = jnp.concatenate([w1[:half], w1[half:]], axis=1).astype(BF16)
        pe2 = jnp.pad(pe.reshape(2, half), ((0, 6), (0, 0))).astype(BF16)
        return w1cat, w2.astype(BF16), pe2
    cw = cmp_weights(cmp_pe_k[l], cmp_w1_k[l], cmp_w2_k[l]) + cmp_weights(cmp_pe_v[l], cmp_w1_v[l], cmp_w2_v[l])

    abr, abi, bbr_t, bbi_t = _ssm_params(ssm_a_re[l], ssm_a_im[l], ssm_log_dt[l],
                                         jnp.swapaxes(ssm_b_re[l], 1, 2), jnp.swapaxes(ssm_b_im[l], 1, 2))
    n_slab = ssm_w // LANES
    sw = nstate // n_slab
    bd_r = _block_diag(bbr_t)
    bd_i = _block_diag(bbi_t)
    bd = jnp.stack([jnp.concatenate([bd_r[s * LANES:(s + 1) * LANES, s * sw:(s + 1) * sw],
                                     bd_i[s * LANES:(s + 1) * LANES, s * sw:(s + 1) * sw]], axis=1)
                    for s in range(n_slab)]).astype(BF16)
    cd_r = _block_diag(jnp.swapaxes(ssm_c_re[l], 1, 2))
    cd_i = _block_diag(jnp.swapaxes(ssm_c_im[l], 1, 2))
    cd = jnp.stack([jnp.concatenate([cd_r[s * sw:(s + 1) * sw, s * LANES:(s + 1) * LANES],
                                     -cd_i[s * sw:(s + 1) * sw, s * LANES:(s + 1) * LANES]], axis=0)
                    for s in range(n_slab)]).astype(BF16)
    a_r = abr.reshape(1, nstate)
    a_i = abi.reshape(1, nstate)
    dvec = ssm_d[l].reshape(1, ssm_w)
    wglu = w_glu[l].astype(BF16)
    bglu = b_glu[l].reshape(1, ssm_w)

    back_w = (w_att_br[l].astype(BF16), w_ssm_br[l].astype(BF16), w_o[l].astype(BF16),
              ffn_norm[l].reshape(1, d), w_ffn_gate[l].astype(BF16), w_ffn_up[l].astype(BF16),
              w_ffn_down[l].astype(BF16), ple_norm[l].reshape(1, d), w_ple_gate[l].astype(BF16),
              w_ple[l].astype(BF16), final_norm.reshape(1, d))
    d_ff = w_ffn_gate.shape[2]
    ff_chunk = d_ff // 2 if (d_ff // 2) % LANES == 0 else d_ff

    xp = x_prompt.reshape(nb * t, d)
    (q, kc, vc, ks, vs, kw, vw, ksb, vsb, kwb, vwb, ng, su, mg) = _front(xp, nb, t, g_attn, wa, wng, wsu, wmg, 512)
    ncp = t // CMP_STRIDE
    kcc, vcc = _compress_prompt(kc.reshape(nb, ncp, CMP_STRIDE * kvw), vc.reshape(nb, ncp, CMP_STRIDE * kvw), cw)
    tiles, ut, ovl = _prompt_tables(rel_bias, t)
    e_p = _expand_table(t // QB)
    o_p = _nsa_prompt(q.reshape(nb, t, aw), ng.reshape(nb, t, LANES), kcc, vcc,
                      ksb.reshape(nb, t, kvw), vsb.reshape(nb, t, kvw), kwb.reshape(nb, t, kvw),
                      vwb.reshape(nb, t, kvw), ut, tiles, ovl, e_p)
    zeros_state = jnp.zeros((nb, nstate), F32)
    so_p, sr_p, si_p = _ssm(su.reshape(t * nb, ssm_w), zeros_state, zeros_state, a_r, a_i, bd, cd, dvec, wglu, bglu,
                            nb, 256)
    y_p = _back(xp, o_p.reshape(nb * t, aw), so_p.reshape(t, nb * ssm_w), mg, p_prompt[l].reshape(nb * t, -1),
                back_w, nb, t, 256, ff_chunk)

    n_s = ns * t_new
    xs = x_sample.reshape(n_s, d)
    (q_s, kc_s, vc_s, ks_s, vs_s, kw_s, vw_s, _, _, _, _, ng_s, su_s, mg_s) = _front(
        xs, 1, n_s, g_attn, wa, wng, wsu, wmg, n_s)
    chunk_rows = page // CMP_STRIDE
    kcc_s, vcc_s = _compress_sample(page_table,
                                    cache_k_cmp[l].reshape(n_phys, chunk_rows, CMP_STRIDE * kvw),
                                    cache_v_cmp[l].reshape(n_phys, chunk_rows, CMP_STRIDE * kvw), cw)
    assert past // SEL_BLOCK <= LANES and t_new < CMP_STRIDE
    bc, bs, bw, ovl_s, e_s, n_sel_s = _sample_tables(rel_bias, past, t_new, win_buf)
    rows_s = N_HEADS * t_new
    eye_kv = jnp.eye(N_KV, dtype=BF16)
    q_rows = q_s.reshape(ns, t_new, N_KV, GROUP, HD).transpose(0, 2, 3, 1, 4)
    q_rows = jnp.einsum('skgtd,kj->skgtjd', q_rows, eye_kv).reshape(ns, rows_s, kvw)
    gate_s = ng_s[:, :n_gate].reshape(ns, t_new, N_KV, GROUP, N_BRANCH).transpose(0, 2, 3, 1, 4)
    gate_s = jnp.pad(gate_s.reshape(ns, rows_s, N_BRANCH), ((0, 0), (0, 0), (0, LANES - N_BRANCH)))
    pad8 = lambda a: jnp.pad(a.reshape(ns, t_new, kvw), ((0, 0), (0, 8 - t_new), (0, 0)))
    o_s = _nsa_sample(page_table, q_rows, gate_s, kcc_s, vcc_s,
                      cache_k_sel[l].reshape(n_phys, page, kvw), cache_v_sel[l].reshape(n_phys, page, kvw),
                      pad8(ks_s), pad8(vs_s),
                      cache_k_win[l].reshape(ns, win_buf, kvw), cache_v_win[l].reshape(ns, win_buf, kvw),
                      pad8(kw_s), pad8(vw_s), bc, bs, bw, ovl_s, e_s, n_sel_s, past, t_new)
    o_s = o_s.reshape(ns, N_KV, GROUP, t_new, N_KV, HD)
    o_s = jnp.stack([o_s[:, k, :, :, k, :] for k in range(N_KV)], axis=1)
    o_s = o_s.transpose(0, 3, 1, 2, 4).reshape(n_s, aw).astype(BF16)
    su_ts = su_s.reshape(ns, t_new, ssm_w).transpose(1, 0, 2).reshape(n_s, ssm_w)
    so_ts, sr_s, si_s = _ssm(su_ts, state_ssm_re[l].reshape(ns, nstate), state_ssm_im[l].reshape(ns, nstate),
                             a_r, a_i, bd, cd, dvec, wglu, bglu, ns, t_new)
    so_s = so_ts.reshape(t_new, ns, ssm_w).transpose(1, 0, 2).reshape(n_s, ssm_w)
    y_s = _back(xs, o_s, so_s, mg_s, p_sample[l].reshape(n_s, -1), back_w, 1, n_s, n_s, ff_chunk)

    kv5 = lambda a, b_, t_: a.reshape(1, b_, t_, N_KV, HD)
    keep = min(WINDOW, t)
    win_p = lambda a: a.reshape(nb, t, kvw)[:, t - keep:].reshape(1, nb, keep, N_KV, HD)
    win_s = lambda cache, new: jnp.concatenate(
        [cache[l], new.reshape(ns, t_new, N_KV, HD)], axis=1)[:, t_new:][None]
    st = lambda a, b_: a.reshape(1, b_, n_groups, SSM_P)
    return (y_p.reshape(nb, t, d), y_s.reshape(ns, t_new, d),
            kv5(kc, nb, t), kv5(vc, nb, t), kv5(ks, nb, t), kv5(vs, nb, t), win_p(kw), win_p(vw),
            st(sr_p, nb), st(si_p, nb),
            kv5(kc_s, ns, t_new), kv5(vc_s, ns, t_new), kv5(ks_s, ns, t_new), kv5(vs_s, ns, t_new),
            win_s(cache_k_win, kw_s), win_s(cache_v_win, vw_s),
            st(sr_s, ns), st(si_s, ns))
```

```python
import functools
import math

import numpy as np
import jax
import jax.numpy as jnp
from jax import lax
from jax.experimental import pallas as pl
from jax.experimental.pallas import tpu as pltpu

F32 = jnp.float32
BF16 = jnp.bfloat16

N_HEADS = 8
N_KV = 2
HD = 64
GROUP = N_HEADS // N_KV
N_BRANCH = 3
CMP_BLOCK = 32
CMP_STRIDE = 16
CMP_HIDDEN = 256
SEL_BLOCK = 64
SEL_TOPK = 16
WINDOW = 512
QB = 128
SEL_KEYS = 2 * QB
SEL_SPLIT = 2
N_BUCKETS = 32
REL_MAX_DIST = 128
SSM_CH = 16
SSM_P = 64
EPS = 1e-6
NEG = -1e30
NEG_TEST = -1e29
FORCE = 1e9
BIG = 1e30
LANES = 128
VMEM_LIMIT = 56 * 1024 * 1024
AW = N_HEADS * HD
KVW = N_KV * HD
NG_ROWS = 32


def _cparams(sem):
    return pltpu.CompilerParams(dimension_semantics=sem, vmem_limit_bytes=VMEM_LIMIT)


def _const_spec(shape):
    nd = len(shape)
    return pl.BlockSpec(shape, lambda *_: (0,) * nd)


def _rms(x, g):
    return x * lax.rsqrt(jnp.mean(x * x, axis=-1, keepdims=True) + EPS) * g


def _gelu(x):
    return x * (0.5 * (1.0 + jnp.tanh(math.sqrt(2.0 / math.pi) * (x + 0.044715 * (x * x * x)))))


def _sigmoid(x):
    return 1.0 / (1.0 + jnp.exp(-x))


def _dot(a, b):
    return jnp.dot(a, b, preferred_element_type=F32)


def _dot_t(a, b):
    return lax.dot_general(a, b, (((1,), (1,)), ((), ())), preferred_element_type=F32)


def _masked_softmax(s, axis=-1):
    valid = s > NEG_TEST
    m = jnp.max(s, axis=axis, keepdims=True)
    e = jnp.where(valid, jnp.exp(s - m), 0.0)
    return e / jnp.maximum(jnp.sum(e, axis=axis, keepdims=True), 1e-30)


def _softmax_cols(s):
    m = jnp.max(s, axis=0, keepdims=True)
    e = jnp.exp(s - m)
    inv = jnp.where(m > NEG_TEST, 1.0 / jnp.maximum(jnp.sum(e, axis=0, keepdims=True), 1e-30), 0.0)
    return e * inv


def _front_project(x_ref, g_ref, wa_ref, wng_ref, wsu_ref, wmg_ref, su_ref, mg_ref):
    u = _rms(x_ref[...], g_ref[...]).astype(BF16)
    za = _dot(u, wa_ref[...])
    q = za[:, :AW] * (HD ** -0.5)
    rows = [za[:, AW + i * KVW: AW + (i + 1) * KVW] for i in range(6)]
    ng = _sigmoid(_dot(u, wng_ref[...]))
    su_ref[...] = _dot(u, wsu_ref[...]).astype(su_ref.dtype)
    mg_ref[...] = _sigmoid(_dot(u, wmg_ref[...])).astype(BF16)
    return q, rows, ng


def _front_sample_kernel(x_ref, g_ref, wa_ref, wng_ref, wsu_ref, wmg_ref,
                         q_ref, kc_ref, vc_ref, ks_ref, vs_ref, kw_ref, vw_ref, ng_ref, su_ref, mg_ref):
    q, rows, ng = _front_project(x_ref, g_ref, wa_ref, wng_ref, wsu_ref, wmg_ref, su_ref, mg_ref)
    q_ref[...] = q.astype(BF16)
    for ref, r in zip((kc_ref, vc_ref, ks_ref, vs_ref, kw_ref, vw_ref), rows):
        ref[...] = r
    ng_ref[...] = ng


def _front_prompt_kernel(x_ref, g_ref, wa_ref, wng_ref, wsu_ref, wmg_ref,
                         qt_ref, kc5_ref, vc5_ref, ks5_ref, vs5_ref, kc_ref, vc_ref, kw_ref, vw_ref,
                         ksb_ref, kwb_ref, vst_ref, vwt_ref, ngt_ref, su_ref, mg_ref):
    q, rows, ng = _front_project(x_ref, g_ref, wa_ref, wng_ref, wsu_ref, wmg_ref, su_ref, mg_ref)
    kc, vc, ks, vs, kw, vw = rows
    qt_ref[0] = q.T.astype(BF16)
    for ref, r in zip((kc5_ref, vc5_ref, ks5_ref, vs5_ref), (kc, vc, ks, vs)):
        for k in range(N_KV):
            ref[0, 0, :, k, :] = r[:, k * HD:(k + 1) * HD]
    kc_ref[...] = kc
    vc_ref[...] = vc
    kw_ref[...] = kw
    vw_ref[...] = vw
    ksb_ref[...] = ks.astype(BF16)
    kwb_ref[...] = kw.astype(BF16)
    for ref, r in ((vst_ref, vs), (vwt_ref, vw)):
        rt = r.T.astype(BF16)
        for j in range(rt.shape[1] // QB):
            ref[0, j] = rt[:, j * QB:(j + 1) * QB]
    ngt_ref[0] = ng.T[0:NG_ROWS]


def _front_sample(x2d, fw):
    g, wa, wng, wsu, wmg = fw
    n, d = x2d.shape
    sw, mw = wsu.shape[1], wmg.shape[1]
    shapes = ([jax.ShapeDtypeStruct((n, AW), BF16)] + [jax.ShapeDtypeStruct((n, KVW), F32)] * 6
              + [jax.ShapeDtypeStruct((n, LANES), F32), jax.ShapeDtypeStruct((n, sw), BF16),
                 jax.ShapeDtypeStruct((n, mw), BF16)])
    return pl.pallas_call(
        _front_sample_kernel,
        grid=(1,),
        in_specs=[_const_spec(a.shape) for a in (x2d, g, wa, wng, wsu, wmg)],
        out_specs=[_const_spec(s.shape) for s in shapes],
        out_shape=shapes,
        compiler_params=_cparams(("arbitrary",)),
        name="front_sample",
    )(x2d, g, wa, wng, wsu, wmg)


def _front_prompt(x2d, nb, t, fw, tm):
    g, wa, wng, wsu, wmg = fw
    n, d = x2d.shape
    nt = t // tm
    sw, mw = wsu.shape[1], wmg.shape[1]
    row = lambda b, i: (b * nt + i, 0)
    kv5 = jax.ShapeDtypeStruct((1, nb, t, N_KV, HD), F32)
    kv5_spec = pl.BlockSpec((1, 1, tm, N_KV, HD), lambda b, i: (0, b, i, 0, 0))
    vt = jax.ShapeDtypeStruct((nb, t // QB, KVW, QB), BF16)
    vt_spec = pl.BlockSpec((1, tm // QB, KVW, QB), lambda b, i: (b, i, 0, 0))
    shapes = ([jax.ShapeDtypeStruct((nb, AW, t), BF16)] + [kv5] * 4 + [jax.ShapeDtypeStruct((n, KVW), F32)] * 4
              + [jax.ShapeDtypeStruct((n, KVW), BF16)] * 2 + [vt] * 2
              + [jax.ShapeDtypeStruct((nb, NG_ROWS, t), F32), jax.ShapeDtypeStruct((t, nb * sw), BF16),
                 jax.ShapeDtypeStruct((n, mw), BF16)])
    specs = ([pl.BlockSpec((1, AW, tm), lambda b, i: (b, 0, i))] + [kv5_spec] * 4
             + [pl.BlockSpec((tm, KVW), row)] * 6 + [vt_spec] * 2
             + [pl.BlockSpec((1, NG_ROWS, tm), lambda b, i: (b, 0, i)),
                pl.BlockSpec((tm, sw), lambda b, i: (i, b)), pl.BlockSpec((tm, mw), row)])
    return pl.pallas_call(
        _front_prompt_kernel,
        grid=(nb, nt),
        in_specs=[pl.BlockSpec((tm, d), row)] + [_const_spec(a.shape) for a in (g, wa, wng, wsu, wmg)],
        out_specs=specs,
        out_shape=shapes,
        compiler_params=_cparams(("parallel", "parallel")),
        name="front_prompt",
    )(x2d, g, wa, wng, wsu, wmg)


def _compress_compute(load_rows, c, w1_ref, w2_ref, pe_ref, a_scr):
    rc_n = min(c, 256)
    lo = lax.broadcasted_iota(jnp.int32, (rc_n, LANES), 1) < HD
    w1 = w1_ref[...]
    for rc in range(c // rc_n):
        x = load_rows(rc * rc_n, rc_n)
        cols = [x[:, r * LANES:(r + 1) * LANES] for r in range(CMP_STRIDE)]
        rol = [pltpu.roll(col, HD, 1) for col in cols]
        for kh in range(N_KV):
            if kh == 0:
                parts = [jnp.where(lo, cols[2 * j], rol[2 * j + 1]) for j in range(CMP_STRIDE // 2)]
            else:
                parts = [jnp.where(lo, rol[2 * j], cols[2 * j + 1]) for j in range(CMP_STRIDE // 2)]
            xh = jnp.concatenate(parts, axis=1).astype(BF16)
            a_scr[kh, rc * rc_n:(rc + 1) * rc_n, :] = _dot(xh, w1)
    pw = _dot(pe_ref[...], w1)
    peb = pw[0:1, :CMP_HIDDEN] + pw[1:2, CMP_HIDDEN:]
    w2 = w2_ref[...]
    outs = []
    for kh in range(N_KV):
        a = a_scr[kh]
        hid = a[:, :CMP_HIDDEN] + pltpu.roll(a[:, CMP_HIDDEN:], c - 1, 0) + peb
        outs.append(_dot(_gelu(hid).astype(BF16), w2))
    return jnp.concatenate(outs, axis=1)


def _compress_prompt_kernel(xk_ref, xv_ref, w1k_ref, w2k_ref, pek_ref, w1v_ref, w2v_ref, pev_ref,
                            ok_ref, ovt_ref, a_scr):
    c = xk_ref.shape[1]
    ok_ref[0] = _compress_compute(lambda r0, rn: xk_ref[0, r0:r0 + rn, :], c,
                                  w1k_ref, w2k_ref, pek_ref, a_scr).astype(BF16)
    ovt_ref[0] = _compress_compute(lambda r0, rn: xv_ref[0, r0:r0 + rn, :], c,
                                   w1v_ref, w2v_ref, pev_ref, a_scr).T.astype(BF16)


def _compress_prompt(xk, xv, cw):
    nb, c, cw_lanes = xk.shape
    wspecs = [_const_spec(w.shape) for w in cw]
    blk = pl.BlockSpec((1, c, cw_lanes), lambda b: (b, 0, 0))
    return pl.pallas_call(
        _compress_prompt_kernel,
        grid=(nb,),
        in_specs=[blk, blk] + wspecs,
        out_specs=[pl.BlockSpec((1, c, KVW), lambda b: (b, 0, 0)), pl.BlockSpec((1, KVW, c), lambda b: (b, 0, 0))],
        out_shape=[jax.ShapeDtypeStruct((nb, c, KVW), BF16), jax.ShapeDtypeStruct((nb, KVW, c), BF16)],
        scratch_shapes=[pltpu.VMEM((N_KV, c, 2 * CMP_HIDDEN), F32)],
        compiler_params=_cparams(("parallel",)),
        name="compress_prompt",
    )(xk, xv, *cw)


def _page_gather_start(pt_ref, seq, pools, bufs, sems, slot, n_pages, rows):
    def body(p, carry):
        page = pt_ref[seq, p]
        for i, (pool, buf) in enumerate(zip(pools, bufs)):
            pltpu.make_async_copy(pool.at[page], buf.at[slot, pl.ds(p * rows, rows)], sems.at[i, slot]).start()
        return carry
    lax.fori_loop(0, n_pages, body, 0)


def _page_gather_wait(pools, bufs, sems, slot, n_pages, rows):
    def body(p, carry):
        for i, (pool, buf) in enumerate(zip(pools, bufs)):
            pltpu.make_async_copy(pool.at[0], buf.at[slot, pl.ds(p * rows, rows)], sems.at[i, slot]).wait()
        return carry
    lax.fori_loop(0, n_pages, body, 0)


def _paged_prefetch(pt_ref, pools, bufs, sems, n_pages, rows):
    s = pl.program_id(0)
    slot = s % 2

    @pl.when(s == 0)
    def _():
        _page_gather_start(pt_ref, 0, pools, bufs, sems, 0, n_pages, rows)

    @pl.when(s + 1 < pl.num_programs(0))
    def _():
        _page_gather_start(pt_ref, s + 1, pools, bufs, sems, 1 - slot, n_pages, rows)

    _page_gather_wait(pools, bufs, sems, slot, n_pages, rows)
    return slot


def _compress_sample_kernel(pt_ref, kpool, vpool, w1k_ref, w2k_ref, pek_ref, w1v_ref, w2v_ref, pev_ref,
                            ok_ref, ov_ref, kbuf, vbuf, sems, a_scr):
    n_pages = pt_ref.shape[1]
    rows = kpool.shape[1]
    c = n_pages * rows
    slot = _paged_prefetch(pt_ref, (kpool, vpool), (kbuf, vbuf), sems, n_pages, rows)
    ok_ref[0] = _compress_compute(lambda r0, rn: kbuf[slot, pl.ds(r0, rn), :], c,
                                  w1k_ref, w2k_ref, pek_ref, a_scr).astype(BF16)
    ov_ref[0] = _compress_compute(lambda r0, rn: vbuf[slot, pl.ds(r0, rn), :], c,
                                  w1v_ref, w2v_ref, pev_ref, a_scr).astype(BF16)


def _compress_sample(page_table, kpool, vpool, cw):
    ns, n_pages = page_table.shape
    rows, width = kpool.shape[1:]
    c = n_pages * rows
    any_spec = pl.BlockSpec(memory_space=pl.ANY)
    wspecs = [pl.BlockSpec(w.shape, lambda s, pt, nd=w.ndim: (0,) * nd) for w in cw]
    oblk = pl.BlockSpec((1, c, KVW), lambda s, pt: (s, 0, 0))
    return pl.pallas_call(
        _compress_sample_kernel,
        grid_spec=pltpu.PrefetchScalarGridSpec(
            num_scalar_prefetch=1,
            grid=(ns,),
            in_specs=[any_spec, any_spec] + wspecs,
            out_specs=[oblk, oblk],
            scratch_shapes=[pltpu.VMEM((2, c, width), F32), pltpu.VMEM((2, c, width), F32),
                            pltpu.SemaphoreType.DMA((2, 2)),
                            pltpu.VMEM((N_KV, c, 2 * CMP_HIDDEN), F32)]),
        out_shape=[jax.ShapeDtypeStruct((ns, c, KVW), BF16)] * 2,
        compiler_params=_cparams(("arbitrary",)),
        name="compress_sample",
    )(page_table, kpool, vpool, *cw)


def _rank_select(score, blk, n_real, axis):
    size = 8 if axis == 0 else LANES
    total = score.shape[axis]
    chunk = (lambda a, c: a[c * size:(c + 1) * size]) if axis == 0 else (lambda a, c: a[:, c * size:(c + 1) * size])
    n_chunks = -(-total // size)
    sc = [chunk(score, c) for c in range(n_chunks)]
    bl = [chunk(blk, c) for c in range(n_chunks)]
    rank = [jnp.zeros(s.shape, F32) for s in sc]
    for kk in range(n_real):
        col = score[kk:kk + 1, :] if axis == 0 else score[:, kk:kk + 1]
        for c in range(n_chunks):
            other = jnp.broadcast_to(col, sc[c].shape)
            if c * size > kk:
                beats = other >= sc[c]
            elif min((c + 1) * size, total) - 1 < kk:
                beats = other > sc[c]
            else:
                beats = (other > sc[c]) | ((other == sc[c]) & (bl[c] > kk))
            rank[c] = rank[c] + jnp.where(beats, 1.0, 0.0)
    return jnp.where(jnp.concatenate(rank, axis=axis) < SEL_TOPK, 1.0, 0.0)


def _block_scores(imp, blk, t):
    cur = t // SEL_BLOCK
    forced = (blk == 0) | (blk == cur) | (blk == cur - 1)
    valid = blk * SEL_BLOCK <= t
    return jnp.where(valid, jnp.where(forced, FORCE, imp), NEG)


def _nsa_prompt_kernel(qt_ref, ngt_ref, kc_ref, vct_ref, ks_ref, vst_ref, kw_ref, vwt_ref,
                       ut_ref, at_ref, ovlt_ref, stat_ref, crow_ref, o_ref, *, n_sel):
    ib = pl.program_id(1)
    qt = qt_ref[0]
    ngt = ngt_ref[0]
    ncp = kc_ref.shape[1]
    cols = GROUP * QB
    sel_rows = LANES // 2
    t_row = ib * QB + lax.broadcasted_iota(jnp.int32, (n_sel, QB), 1)
    blk_t = lax.broadcasted_iota(jnp.int32, (n_sel, QB), 0)
    zeros_q = jnp.zeros((HD, cols), F32)
    vrows = [slice(k * HD, (k + 1) * HD) for k in range(N_KV)]
    q_sel, q_win, o_c = [], [], []
    for k in range(N_KV):
        qk = jnp.concatenate([qt[(GROUP * k + g) * HD:(GROUP * k + g + 1) * HD, :] for g in range(GROUP)],
                             axis=1).astype(F32)
        qa = jnp.concatenate([qk, zeros_q] if k == 0 else [zeros_q, qk], axis=0)

        bias_c = ut_ref[k, pl.ds(pl.multiple_of(ncp - (QB // CMP_STRIDE) * ib, 8), ncp), :]
        p_c = _softmax_cols(_dot(kc_ref[0], qa.astype(BF16)) + bias_c)
        o_c.append(_dot(vct_ref[0][vrows[k], :], p_c.astype(BF16)))
        psum = p_c[:, 0:QB]
        for g in range(1, GROUP):
            psum = psum + p_c[:, g * QB:(g + 1) * QB]
        imp = jnp.dot(ovlt_ref[...], psum, preferred_element_type=F32, precision=lax.Precision.HIGHEST)

        sel = _rank_select(_block_scores(imp[0:n_sel], blk_t, t_row), blk_t, n_sel, 0)
        selm1 = jnp.concatenate([sel - 1.0] * GROUP, axis=1)
        if n_sel < sel_rows:
            selm1 = jnp.concatenate([selm1, jnp.zeros((sel_rows - n_sel, cols), F32)], axis=0)
        tail = jnp.concatenate([crow_ref[k], jnp.zeros((LANES - sel_rows - 8, cols), F32)], axis=0)
        q_sel.append(jnp.concatenate([qa, selm1, tail], axis=0).astype(BF16))
        q_win.append(jnp.concatenate([qa, jnp.zeros((sel_rows, cols), F32), tail], axis=0).astype(BF16))

    tiles_per_step = SEL_KEYS // QB

    def sel_body(jp, carry):
        scores, vts = [], []
        for sp in range(SEL_SPLIT):
            j0 = (jp * SEL_SPLIT + sp) * tiles_per_step
            k0 = pl.multiple_of(j0 * QB, SEL_KEYS)
            lhs = jnp.concatenate([ks_ref[0, pl.ds(k0, SEL_KEYS), :], stat_ref[pl.ds(k0, SEL_KEYS), :]], axis=1)
            tiles = [j0 + h for h in range(tiles_per_step)]
            tidx = [jnp.where(jt == ib, 0, jnp.where(jt == ib - 1, 1, jnp.where(jt < ib, 2, 4))) for jt in tiles]
            for k in range(N_KV):
                scores.append(_dot(lhs, q_sel[k]) + jnp.concatenate([at_ref[ti, k] for ti in tidx], axis=0))
                vts.append(jnp.concatenate([vst_ref[0, jt, vrows[k], :] for jt in tiles], axis=1))
        stats = []
        for (m, l, acc), s in zip(carry, scores):
            m_new = jnp.maximum(m, jnp.max(s, axis=0, keepdims=True))
            p = jnp.exp(s - m_new)
            alpha = jnp.exp(m - m_new)
            stats.append((m_new, alpha * l + jnp.sum(p, axis=0, keepdims=True), alpha, p.astype(BF16)))
        return tuple((m_new, l, alpha * acc + _dot(vt, p))
                     for (_, _, acc), (m_new, l, alpha, p), vt in zip(carry, stats, vts))

    init = (jnp.full((1, cols), NEG, F32), jnp.zeros((1, cols), F32), jnp.zeros((HD, cols), F32))
    step_tiles = SEL_SPLIT * tiles_per_step
    sel_state = lax.fori_loop(0, (ib + step_tiles) // step_tiles, sel_body, (init,) * (N_KV * SEL_SPLIT))

    out_rows = []
    for k in range(N_KV):
        parts = [sel_state[sp * N_KV + k] for sp in range(SEL_SPLIT)]
        m_s = parts[0][0]
        for m_p, _, _ in parts[1:]:
            m_s = jnp.maximum(m_s, m_p)
        l_s = jnp.zeros((1, cols), F32)
        acc_s = jnp.zeros((HD, cols), F32)
        for m_p, l_p, acc_p in parts:
            w_p = jnp.exp(m_p - m_s)
            l_s = l_s + w_p * l_p
            acc_s = acc_s + w_p * acc_p
        o_s = acc_s / jnp.maximum(l_s, 1e-30)

        s_parts, tiles_j = [], []
        for w, tidx in enumerate((0, 1, None, None, 3)):
            jt = ib - w
            jc = jnp.maximum(jt, 0)
            k0 = pl.multiple_of(jc * QB, QB)
            lhs = jnp.concatenate([kw_ref[0, pl.ds(k0, QB), :], stat_ref[pl.ds(k0, QB), :]], axis=1)
            s = _dot(lhs, q_win[k])
            if tidx is not None:
                s = s + at_ref[tidx, k]
            s_parts.append(jnp.where(jt >= 0, s, NEG))
            tiles_j.append(jc)
        p_w = _softmax_cols(jnp.concatenate(s_parts, axis=0)).astype(BF16)
        o_w = jnp.zeros((HD, cols), F32)
        for w, jc in enumerate(tiles_j):
            o_w = o_w + _dot(vwt_ref[0, jc, vrows[k], :], p_w[w * QB:(w + 1) * QB])

        def gate_row(br):
            return jnp.concatenate([ngt[(GROUP * k + g) * N_BRANCH + br:(GROUP * k + g) * N_BRANCH + br + 1, :]
                                    for g in range(GROUP)], axis=1)
        o_k = gate_row(0) * o_c[k] + gate_row(1) * o_s + gate_row(2) * o_w
        out_rows += [o_k[:, g * QB:(g + 1) * QB] for g in range(GROUP)]
    o_ref[0] = jnp.concatenate(out_rows, axis=0).T.astype(BF16)


def _nsa_prompt(qt, ngt, kc, vct, ks, vst, kw, vwt, tables):
    nb, aw, t = qt.shape
    nq = t // QB
    ncp = kc.shape[1]
    full3 = lambda b, i: (b, 0, 0)
    full4 = lambda b, i: (b, 0, 0, 0)
    return pl.pallas_call(
        functools.partial(_nsa_prompt_kernel, n_sel=t // SEL_BLOCK),
        grid=(nb, nq),
        in_specs=[pl.BlockSpec((1, aw, QB), lambda b, i: (b, 0, i)),
                  pl.BlockSpec((1, NG_ROWS, QB), lambda b, i: (b, 0, i)),
                  pl.BlockSpec((1, ncp, KVW), full3), pl.BlockSpec((1, KVW, ncp), full3),
                  pl.BlockSpec((1, t, KVW), full3), pl.BlockSpec((1, nq, KVW, QB), full4),
                  pl.BlockSpec((1, t, KVW), full3), pl.BlockSpec((1, nq, KVW, QB), full4)]
                 + [_const_spec(a.shape) for a in tables],
        out_specs=pl.BlockSpec((1, QB, aw), lambda b, i: (b, i, 0)),
        out_shape=jax.ShapeDtypeStruct((nb, t, aw), BF16),
        compiler_params=_cparams(("parallel", "arbitrary")),
        name="nsa_prompt",
    )(qt, ngt, kc, vct, ks, vst, kw, vwt, *tables)


def _nsa_sample_kernel(pt_ref, q_ref, gate_ref, kc_ref, vc_ref, kpool, vpool, ksn_ref, vsn_ref,
                       kwin_ref, vwin_ref, kwn_ref, vwn_ref, bc_ref, bs_ref, bw_ref, ovl_ref, e_ref,
                       o_ref, kbuf, vbuf, sems, *, n_sel, past, t_new):
    n_pages = pt_ref.shape[1]
    page = kpool.shape[1]
    wb = kwin_ref.shape[1]
    slot = _paged_prefetch(pt_ref, (kpool, vpool), (kbuf, vbuf), sems, n_pages, page)
    q = q_ref[0]
    nb_past = past // SEL_BLOCK
    pad_new = jnp.zeros((QB - ksn_ref.shape[1], KVW), F32)
    new_tile = lambda ref: jnp.concatenate([ref[0], pad_new], axis=0).astype(BF16)

    p_c = _masked_softmax(_dot_t(q, kc_ref[0]) + bc_ref[...])
    o_c = _dot(p_c.astype(BF16), vc_ref[0])
    parts = []
    for k in range(N_KV):
        base = k * GROUP * t_new
        ps = p_c[base:base + t_new]
        for g in range(1, GROUP):
            ps = ps + p_c[base + g * t_new:base + (g + 1) * t_new]
        parts.append(ps)
    psum = jnp.concatenate(parts, axis=0)
    imp = jnp.dot(psum, ovl_ref[...], preferred_element_type=F32, precision=lax.Precision.HIGHEST)
    blk = lax.broadcasted_iota(jnp.int32, imp.shape, 1)
    tpos = past + lax.broadcasted_iota(jnp.int32, imp.shape, 0) % t_new
    sel = _rank_select(_block_scores(imp, blk, tpos), blk, n_sel, 1)
    sel = jnp.concatenate([sel[k * t_new:(k + 1) * t_new] for k in range(N_KV) for _ in range(GROUP)], axis=0)

    mask_add = _dot((sel[:, 0:LANES] - 1.0).astype(BF16), e_ref[...])
    s_past = _dot_t(q, kbuf[slot].astype(BF16)) + bs_ref[:, 0:past] + mask_add
    s_new = _dot_t(q, new_tile(ksn_ref)) + bs_ref[:, past:]
    s_new = jnp.where(sel[:, nb_past:nb_past + 1] > 0.5, s_new, NEG)
    p_s = _masked_softmax(jnp.concatenate([s_past, s_new], axis=1)).astype(BF16)
    o_s = _dot(p_s[:, 0:past], vbuf[slot].astype(BF16)) + _dot(p_s[:, past:], new_tile(vsn_ref))

    s_w = jnp.concatenate([_dot_t(q, kwin_ref[0].astype(BF16)), _dot_t(q, new_tile(kwn_ref))], axis=1)
    p_w = _masked_softmax(s_w + bw_ref[...]).astype(BF16)
    o_w = _dot(p_w[:, 0:wb], vwin_ref[0].astype(BF16)) + _dot(p_w[:, wb:], new_tile(vwn_ref))

    gate = gate_ref[0]
    o_ref[0] = gate[:, 0:1] * o_c + gate[:, 1:2] * o_s + gate[:, 2:3] * o_w


def _nsa_sample(page_table, q, gate, kc, vc, kpool, vpool, ksn, vsn, kwin, vwin, kwn, vwn,
                bc, bs, bw, ovl, e, n_sel, past, t_new):
    ns, n_pages = page_table.shape
    page, kvw = kpool.shape[1:]
    rows = q.shape[1]
    seq3 = lambda s, pt: (s, 0, 0)
    any_spec = pl.BlockSpec(memory_space=pl.ANY)
    cs = lambda a: pl.BlockSpec(a.shape, lambda s, pt, nd=a.ndim: (0,) * nd)
    per_seq = lambda a: pl.BlockSpec((1,) + a.shape[1:], seq3)
    return pl.pallas_call(
        functools.partial(_nsa_sample_kernel, n_sel=n_sel, past=past, t_new=t_new),
        grid_spec=pltpu.PrefetchScalarGridSpec(
            num_scalar_prefetch=1,
            grid=(ns,),
            in_specs=[per_seq(q), per_seq(gate), per_seq(kc), per_seq(vc), any_spec, any_spec,
                      per_seq(ksn), per_seq(vsn), per_seq(kwin), per_seq(vwin), per_seq(kwn), per_seq(vwn),
                      cs(bc), cs(bs), cs(bw), cs(ovl), cs(e)],
            out_specs=pl.BlockSpec((1, rows, kvw), seq3),
            scratch_shapes=[pltpu.VMEM((2, n_pages * page, kvw), F32), pltpu.VMEM((2, n_pages * page, kvw), F32),
                            pltpu.SemaphoreType.DMA((2, 2))]),
        out_shape=jax.ShapeDtypeStruct((ns, rows, kvw), F32),
        compiler_params=_cparams(("arbitrary",)),
        name="nsa_sample",
    )(page_table, q, gate, kc, vc, kpool, vpool, ksn, vsn, kwin, vwin, kwn, vwn, bc, bs, bw, ovl, e)


def _ssm_param_kernel(ar_ref, ai_ref, ldt_ref, br_ref, bi_ref, abr_ref, abi_ref, bbr_ref, bbi_ref):
    ar = ar_ref[...]
    ai = ai_ref[...]
    dt = jnp.exp(ldt_ref[...])
    mag = jnp.exp(ar * dt)
    abr = mag * jnp.cos(ai * dt)
    abi = mag * jnp.sin(ai * dt)
    den = ar * ar + ai * ai
    nr, ni = abr - 1.0, abi
    fr = (nr * ar + ni * ai) / den
    fi = (ni * ar - nr * ai) / den
    abr_ref[...] = abr
    abi_ref[...] = abi
    for g in range(ar.shape[0]):
        br = br_ref[g]
        bi = bi_ref[g]
        frg = fr[g:g + 1, :]
        fig = fi[g:g + 1, :]
        bbr_ref[g] = frg * br - fig * bi
        bbi_ref[g] = frg * bi + fig * br


def _ssm_params(a_re, a_im, log_dt, b_re_t, b_im_t):
    g, p = a_re.shape
    return pl.pallas_call(
        _ssm_param_kernel,
        out_shape=[jax.ShapeDtypeStruct((g, p), F32)] * 2 + [jax.ShapeDtypeStruct(b_re_t.shape, F32)] * 2,
        name="ssm_params",
    )(a_re, a_im, log_dt.reshape(g, 1), b_re_t, b_im_t)


def _ssm_kernel(u_ref, h0r_ref, h0i_ref, ar_ref, ai_ref, bd_ref, cd_ref, d_ref, wglu_ref, bglu_ref,
                so_ref, hr_ref, hi_ref, xr_scr, xi_scr, *, bt):
    i = pl.program_id(0)
    rows, width = u_ref.shape
    n_slab = bd_ref.shape[0]
    sw = bd_ref.shape[2] // 2
    u = u_ref[...]

    @pl.when(i == 0)
    def _():
        hr_ref[...] = h0r_ref[...]
        hi_ref[...] = h0i_ref[...]

    for sl in range(n_slab):
        x = _dot(u[:, sl * LANES:(sl + 1) * LANES], bd_ref[sl])
        xr_scr[:, sl * sw:(sl + 1) * sw] = x[:, :sw]
        xi_scr[:, sl * sw:(sl + 1) * sw] = x[:, sw:]

    per = 8 // math.gcd(bt, 8)
    grp = per * bt
    lc = 512
    for c0 in range(0, xr_scr.shape[1], lc):
        cl = slice(c0, c0 + lc)
        a_r = jnp.broadcast_to(ar_ref[:, cl], (bt, lc))
        a_i = jnp.broadcast_to(ai_ref[:, cl], (bt, lc))

        def step(j, carry):
            h_r, h_i = carry
            r0 = pl.multiple_of(j * grp, grp)
            xr = xr_scr[pl.ds(r0, grp), cl]
            xi = xi_scr[pl.ds(r0, grp), cl]
            out_r, out_i = [], []
            for s in range(per):
                n_r = a_r * h_r - a_i * h_i + xr[s * bt:(s + 1) * bt]
                n_i = a_r * h_i + a_i * h_r + xi[s * bt:(s + 1) * bt]
                h_r, h_i = n_r, n_i
                out_r.append(h_r)
                out_i.append(h_i)
            xr_scr[pl.ds(r0, grp), cl] = jnp.concatenate(out_r, axis=0) if per > 1 else out_r[0]
            xi_scr[pl.ds(r0, grp), cl] = jnp.concatenate(out_i, axis=0) if per > 1 else out_i[0]
            return h_r, h_i

        h_r, h_i = lax.fori_loop(0, rows // grp, step, (hr_ref[:, cl], hi_ref[:, cl]))
        hr_ref[:, cl] = h_r
        hi_ref[:, cl] = h_i

    ys = []
    for sl in range(n_slab):
        hcat = jnp.concatenate([xr_scr[:, sl * sw:(sl + 1) * sw], xi_scr[:, sl * sw:(sl + 1) * sw]], axis=1)
        ys.append(_dot(hcat.astype(BF16), cd_ref[sl]))
    y = jnp.concatenate(ys, axis=1) + d_ref[...] * u.astype(F32)
    z = _gelu(y)
    so_ref[...] = (z * _sigmoid(_dot(z.astype(BF16), wglu_ref[...]) + bglu_ref[...])).astype(so_ref.dtype)


def _ssm(u, h0r, h0i, ar, ai, bd, cd, dvec, wglu, bglu, bt, tt):
    n, width = u.shape
    rows = tt * bt
    nstate = ar.shape[1]
    cst = [_const_spec(a.shape) for a in (h0r, h0i, ar, ai, bd, cd, dvec, wglu, bglu)]
    st_spec = _const_spec((bt, nstate))
    return pl.pallas_call(
        functools.partial(_ssm_kernel, bt=bt),
        grid=(n // rows,),
        in_specs=[pl.BlockSpec((rows, width), lambda i: (i, 0))] + cst,
        out_specs=[pl.BlockSpec((rows, width), lambda i: (i, 0)), st_spec, st_spec],
        out_shape=[jax.ShapeDtypeStruct((n, width), BF16), jax.ShapeDtypeStruct((bt, nstate), F32),
                   jax.ShapeDtypeStruct((bt, nstate), F32)],
        scratch_shapes=[pltpu.VMEM((rows, nstate), F32), pltpu.VMEM((rows, nstate), F32)],
        compiler_params=_cparams(("arbitrary",)),
        name="ssm",
    )(u, h0r, h0i, ar, ai, bd, cd, dvec, wglu, bglu)


def _back_kernel(h_ref, o_ref, so_ref, mg_ref, p_ref, watt_ref, wssm_ref, wo_ref, fn_ref, wg_ref, wu_ref, wd_ref,
                 pn_ref, wpg_ref, wple_ref, fin_ref, y_ref, *, ff_chunk):
    d = h_ref.shape[1]
    a = _dot(o_ref[...], watt_ref[...])
    s = _dot(so_ref[...], wssm_ref[...])
    mg = mg_ref[...].astype(F32)
    h = h_ref[...] + _dot((mg[:, :d] * a + mg[:, d:] * s).astype(BF16), wo_ref[...])
    f = _rms(h, fn_ref[...]).astype(BF16)
    ffn = jnp.zeros_like(h)
    for c0 in range(0, wg_ref.shape[1], ff_chunk):
        gate = _dot(f, wg_ref[:, c0:c0 + ff_chunk])
        up = _dot(f, wu_ref[:, c0:c0 + ff_chunk])
        ffn = ffn + _dot((gate * _sigmoid(gate) * up).astype(BF16), wd_ref[c0:c0 + ff_chunk, :])
    h = h + ffn
    g = _sigmoid(_dot(_rms(h, pn_ref[...]).astype(BF16), wpg_ref[...]))
    h = h + g * _dot(p_ref[...].astype(BF16), wple_ref[...])
    y_ref[...] = _rms(h, fin_ref[...])


def _back(h2d, o2d, so_tb, mg, p2d, weights, nb, t, tm, ff_chunk):
    n, d = h2d.shape
    nt = t // tm
    row = lambda b, i: (b * nt + i, 0)
    wspecs = [pl.BlockSpec(w.shape, lambda b, i, nd=w.ndim: (0,) * nd, pipeline_mode=pl.Buffered(1))
              for w in weights]
    sw = o2d.shape[1]
    return pl.pallas_call(
        functools.partial(_back_kernel, ff_chunk=ff_chunk),
        grid=(nb, nt),
        in_specs=[pl.BlockSpec((tm, d), row), pl.BlockSpec((tm, sw), row),
                  pl.BlockSpec((tm, sw), lambda b, i: (i, b)),
                  pl.BlockSpec((tm, mg.shape[1]), row), pl.BlockSpec((tm, p2d.shape[1]), row)] + wspecs,
        out_specs=pl.BlockSpec((tm, d), row),
        out_shape=jax.ShapeDtypeStruct((n, d), F32),
        compiler_params=_cparams(("parallel", "parallel")),
        name="back",
    )(h2d, o2d, so_tb, mg, p2d, *weights)


def _bucket_np(dist):
    n = np.maximum(dist, 0)
    exact = N_BUCKETS // 2
    nf = np.maximum(n, 1).astype(np.float64)
    large = exact + (np.log(nf / exact) / math.log(REL_MAX_DIST / exact) * (N_BUCKETS - exact)).astype(np.int64)
    return np.where(n < exact, n, np.minimum(large, N_BUCKETS - 1)).astype(np.int32)


def _bias_table(rel_bias, dist, valid, offset=None):
    onehot = jax.nn.one_hot(jnp.asarray(_bucket_np(dist)), N_BUCKETS, dtype=F32)
    b = jnp.einsum('...b,bh->h...', onehot, rel_bias.astype(F32), precision=lax.Precision.HIGHEST)
    if offset is not None:
        b = b - offset.reshape((N_HEADS,) + (1,) * dist.ndim)
    return jnp.where(jnp.asarray(valid)[None], b, NEG)


def _prompt_tables(rel_bias, t):
    def cols(b):
        r = b.shape[1]
        return b.reshape(N_KV, GROUP, r, QB).transpose(0, 2, 1, 3).reshape(N_KV, r, GROUP * QB)
    c = rel_bias[N_BUCKETS - 1].astype(F32)
    c_hi = c.astype(BF16)
    c_lo = (c - c_hi.astype(F32)).astype(BF16)
    c_eff = c_hi.astype(F32) + c_lo.astype(F32)
    crow = jnp.stack([c_hi.astype(F32), c_lo.astype(F32)] + [jnp.zeros_like(c)] * 6, axis=1)
    crow = jnp.broadcast_to(crow[:, :, None], (N_HEADS, 8, QB))
    crow = cols(crow)
    j = np.arange(QB)[:, None]
    i = np.arange(QB)[None, :]
    ones = np.ones((QB, QB), bool)
    zeros = jnp.zeros((N_KV, QB, GROUP * QB), F32)
    at = jnp.stack([
        cols(_bias_table(rel_bias, i - j, i >= j, c_eff)),
        cols(_bias_table(rel_bias, QB + i - j, ones, c_eff)),
        zeros,
        jnp.where(jnp.asarray(np.tile(j > i, (1, GROUP)))[None], zeros, NEG),
        zeros + NEG,
    ])
    ncp = t // CMP_STRIDE
    m = np.arange(2 * ncp)[:, None] - ncp
    dist = i - CMP_STRIDE * m - (CMP_BLOCK - 1)
    ut = cols(_bias_table(rel_bias, dist, dist >= 0))
    n_sel = t // SEL_BLOCK
    n = np.arange(ncp)[None, :]
    jb = np.arange(LANES)[:, None]
    ovlt = ((n * CMP_STRIDE < jb * SEL_BLOCK + SEL_BLOCK) & (n * CMP_STRIDE + CMP_BLOCK - 1 >= jb * SEL_BLOCK)
            & (jb < n_sel) & (n < ncp - 1))
    key = np.arange(t)[:, None]
    lane = np.arange(LANES)[None, :]
    stat = np.where(lane < LANES // 2, (lane == key // SEL_BLOCK) * BIG,
                    ((lane == LANES // 2) | (lane == LANES // 2 + 1)) * 1.0).astype(np.float32)
    return ut, at, jnp.asarray(ovlt.astype(np.float32)), jnp.asarray(stat, BF16), crow


def _sample_tables(rel_bias, past, t_new, win_buf):
    def rows(b):
        return b.reshape(N_HEADS * t_new, b.shape[-1])
    tok = np.arange(t_new)[:, None]
    nc = past // CMP_STRIDE
    n = np.arange(nc)[None, :]
    c_end = n * CMP_STRIDE + CMP_BLOCK - 1
    n_cmp = (past + t_new) // CMP_STRIDE - 1
    bc = rows(_bias_table(rel_bias, past + tok - c_end, (c_end <= past + tok) & (n < n_cmp)))
    js = np.arange(past + QB)[None, :]
    ds = np.where(js < past, past + tok - js, tok - (js - past))
    bs = rows(_bias_table(rel_bias, ds, np.where(js < past, True, (ds >= 0) & (js - past < t_new))))
    jw = np.arange(win_buf + QB)[None, :]
    dw = np.where(jw < win_buf, win_buf + tok - jw, tok - (jw - win_buf))
    valid = np.where(jw < win_buf, (dw >= 0) & (dw < WINDOW), (dw >= 0) & (jw - win_buf < t_new))
    bw = rows(_bias_table(rel_bias, dw, valid))
    n_sel = -(-(past + t_new) // SEL_BLOCK)
    nbp = 2 * LANES
    nn = np.arange(nc)[:, None]
    jb = np.arange(nbp)[None, :]
    ovl = ((nn * CMP_STRIDE < jb * SEL_BLOCK + SEL_BLOCK) & (nn * CMP_STRIDE + CMP_BLOCK - 1 >= jb * SEL_BLOCK)
           & (jb < n_sel) & (nn < n_cmp))
    e = (np.arange(LANES)[:, None] == np.arange(past)[None, :] // SEL_BLOCK).astype(np.float32) * BIG
    return bc, bs, bw, jnp.asarray(ovl.astype(np.float32)), jnp.asarray(e, BF16), n_sel


def _block_diag(blocks):
    g, r, c = blocks.shape
    eye = jnp.eye(g, dtype=blocks.dtype)
    return jnp.einsum('grc,gh->grhc', blocks, eye).reshape(g * r, g * c)


def _layer_params(rel_bias, final_norm, attn_norm, w_in, cmp_pe_k, cmp_w1_k, cmp_w2_k, cmp_pe_v, cmp_w1_v, cmp_w2_v,
                  ssm_a_re, ssm_a_im, ssm_log_dt, ssm_b_re, ssm_b_im, ssm_c_re, ssm_c_im, ssm_d, w_glu, b_glu,
                  w_att_br, w_ssm_br, w_o, ffn_norm, w_ffn_gate, w_ffn_up, w_ffn_down, ple_norm, w_ple_gate, w_ple):
    l = 0
    d = w_in.shape[1]
    n_groups = ssm_a_re.shape[1]
    ssm_w = n_groups * SSM_CH
    nstate = n_groups * SSM_P
    assert ssm_w % LANES == 0
    w = w_in[l]
    c0 = AW + 6 * KVW
    n_gate = N_HEADS * N_BRANCH
    front_w = (attn_norm[l].reshape(1, d), w[:, :c0].astype(BF16),
               jnp.pad(w[:, c0:c0 + n_gate], ((0, 0), (0, LANES - n_gate))).astype(BF16),
               w[:, c0 + n_gate:c0 + n_gate + ssm_w].astype(BF16), w[:, c0 + n_gate + ssm_w:].astype(BF16))

    def cmp_weights(pe, w1, w2):
        half = CMP_STRIDE * HD
        w1cat = jnp.concatenate([w1[:half], w1[half:]], axis=1).astype(BF16)
        pe2 = jnp.pad(pe.reshape(2, half), ((0, 6), (0, 0))).astype(BF16)
        return w1cat, w2.astype(BF16), pe2
    cw = cmp_weights(cmp_pe_k[l], cmp_w1_k[l], cmp_w2_k[l]) + cmp_weights(cmp_pe_v[l], cmp_w1_v[l], cmp_w2_v[l])

    abr, abi, bbr_t, bbi_t = _ssm_params(ssm_a_re[l], ssm_a_im[l], ssm_log_dt[l],
                                         jnp.swapaxes(ssm_b_re[l], 1, 2), jnp.swapaxes(ssm_b_im[l], 1, 2))
    n_slab = ssm_w // LANES
    sw = nstate // n_slab
    bd_r = _block_diag(bbr_t)
    bd_i = _block_diag(bbi_t)
    bd = jnp.stack([jnp.concatenate([bd_r[s * LANES:(s + 1) * LANES, s * sw:(s + 1) * sw],
                                     bd_i[s * LANES:(s + 1) * LANES, s * sw:(s + 1) * sw]], axis=1)
                    for s in range(n_slab)]).astype(BF16)
    cd_r = _block_diag(jnp.swapaxes(ssm_c_re[l], 1, 2))
    cd_i = _block_diag(jnp.swapaxes(ssm_c_im[l], 1, 2))
    cd = jnp.stack([jnp.concatenate([cd_r[s * sw:(s + 1) * sw, s * LANES:(s + 1) * LANES],
                                     -cd_i[s * sw:(s + 1) * sw, s * LANES:(s + 1) * LANES]], axis=0)
                    for s in range(n_slab)]).astype(BF16)
    ssm_p = (abr.reshape(1, nstate), abi.reshape(1, nstate), bd, cd, ssm_d[l].reshape(1, ssm_w),
             w_glu[l].astype(BF16), b_glu[l].reshape(1, ssm_w))

    back_w = (w_att_br[l].astype(BF16), w_ssm_br[l].astype(BF16), w_o[l].astype(BF16),
              ffn_norm[l].reshape(1, d), w_ffn_gate[l].astype(BF16), w_ffn_up[l].astype(BF16),
              w_ffn_down[l].astype(BF16), ple_norm[l].reshape(1, d), w_ple_gate[l].astype(BF16),
              w_ple[l].astype(BF16), final_norm.reshape(1, d))
    d_ff = w_ffn_gate.shape[2]
    ff_chunk = d_ff // 2 if (d_ff // 2) % LANES == 0 else d_ff
    return dict(front=front_w, cmp=cw, ssm=ssm_p, back=back_w, ff_chunk=ff_chunk, rel_bias=rel_bias,
                n_groups=n_groups, ssm_w=ssm_w, nstate=nstate, n_gate=n_gate)


def _prompt_group(x_prompt, p_l, prm):
    nb, t, d = x_prompt.shape
    ssm_w, nstate = prm["ssm_w"], prm["nstate"]
    assert t % (CMP_STRIDE * LANES) == 0 and t >= WINDOW and t // SEL_BLOCK <= LANES // 2
    xp = x_prompt.reshape(nb * t, d)
    (qt, kc5, vc5, ks5, vs5, kc, vc, kw, vw, ksb, kwb, vst, vwt, ngt, su, mg) = _front_prompt(
        xp, nb, t, prm["front"], 512)
    ncp = t // CMP_STRIDE
    kcc, vcct = _compress_prompt(kc.reshape(nb, ncp, CMP_STRIDE * KVW), vc.reshape(nb, ncp, CMP_STRIDE * KVW),
                                 prm["cmp"])
    o = _nsa_prompt(qt, ngt, kcc, vcct, ksb.reshape(nb, t, KVW), vst, kwb.reshape(nb, t, KVW), vwt,
                    _prompt_tables(prm["rel_bias"], t))
    zeros_state = jnp.zeros((nb, nstate), F32)
    so, sr, si = _ssm(su.reshape(t * nb, ssm_w), zeros_state, zeros_state, *prm["ssm"], nb, 256)
    y = _back(xp, o.reshape(nb * t, AW), so.reshape(t, nb * ssm_w), mg, p_l.reshape(nb * t, -1),
              prm["back"], nb, t, 256, prm["ff_chunk"])
    return dict(y=y, o=o, so=so, rows5=(kc5, vc5, ks5, vs5), win=(kw, vw), state=(sr, si))


def _sample_group(x_sample, p_l, page_table, pools, wins, states, prm):
    ns, t_new, d = x_sample.shape
    k_cmp, v_cmp, k_sel, v_sel = pools
    k_win, v_win = wins
    n_phys, page = k_cmp.shape[:2]
    past = page_table.shape[1] * page
    win_buf = k_win.shape[1]
    ssm_w, nstate, n_gate = prm["ssm_w"], prm["nstate"], prm["n_gate"]
    assert page == QB and past % (CMP_STRIDE * LANES) == 0 and past // SEL_BLOCK <= LANES
    assert win_buf == WINDOW and past >= win_buf and t_new <= 8 and t_new < CMP_STRIDE
    n_s = ns * t_new
    xs = x_sample.reshape(n_s, d)
    q_s, kc_s, vc_s, ks_s, vs_s, kw_s, vw_s, ng_s, su_s, mg_s = _front_sample(xs, prm["front"])
    chunk_rows = page // CMP_STRIDE
    kcc_s, vcc_s = _compress_sample(page_table, k_cmp.reshape(n_phys, chunk_rows, CMP_STRIDE * KVW),
                                    v_cmp.reshape(n_phys, chunk_rows, CMP_STRIDE * KVW), prm["cmp"])
    bc, bs, bw, ovl_s, e_s, n_sel_s = _sample_tables(prm["rel_bias"], past, t_new, win_buf)
    rows_s = N_HEADS * t_new
    eye_kv = jnp.eye(N_KV, dtype=BF16)
    q_rows = q_s.reshape(ns, t_new, N_KV, GROUP, HD).transpose(0, 2, 3, 1, 4)
    q_rows = jnp.einsum('skgtd,kj->skgtjd', q_rows, eye_kv).reshape(ns, rows_s, KVW)
    gate_s = ng_s[:, :n_gate].reshape(ns, t_new, N_KV, GROUP, N_BRANCH).transpose(0, 2, 3, 1, 4)
    gate_s = jnp.pad(gate_s.reshape(ns, rows_s, N_BRANCH), ((0, 0), (0, 0), (0, LANES - N_BRANCH)))
    pad8 = lambda a: jnp.pad(a.reshape(ns, t_new, KVW), ((0, 0), (0, 8 - t_new), (0, 0)))
    o_s = _nsa_sample(page_table, q_rows, gate_s, kcc_s, vcc_s,
                      k_sel.reshape(n_phys, page, KVW), v_sel.reshape(n_phys, page, KVW), pad8(ks_s), pad8(vs_s),
                      k_win.reshape(ns, win_buf, KVW), v_win.reshape(ns, win_buf, KVW),
                      pad8(kw_s), pad8(vw_s), bc, bs, bw, ovl_s, e_s, n_sel_s, past, t_new)
    o_s = o_s.reshape(ns, N_KV, GROUP, t_new, N_KV, HD)
    o_s = jnp.stack([o_s[:, k, :, :, k, :] for k in range(N_KV)], axis=1)
    o_s = o_s.transpose(0, 3, 1, 2, 4).reshape(n_s, AW).astype(BF16)
    su_ts = su_s.reshape(ns, t_new, ssm_w).transpose(1, 0, 2).reshape(n_s, ssm_w)
    so_ts, sr, si = _ssm(su_ts, states[0].reshape(ns, nstate), states[1].reshape(ns, nstate), *prm["ssm"], ns, t_new)
    so_s = so_ts.reshape(t_new, ns, ssm_w).transpose(1, 0, 2).reshape(n_s, ssm_w)
    y = _back(xs, o_s, so_s, mg_s, p_l.reshape(n_s, -1), prm["back"], 1, n_s, n_s, prm["ff_chunk"])
    return dict(y=y, o=o_s, so=so_s, rows=(kc_s, vc_s, ks_s, vs_s, kw_s, vw_s), state=(sr, si))


def kernel(x_prompt, x_sample, cache_k_cmp, cache_v_cmp, cache_k_sel, cache_v_sel, cache_k_win, cache_v_win, state_ssm_re, state_ssm_im, page_table, p_prompt, p_sample, rel_bias, final_norm, attn_norm, w_in, cmp_pe_k, cmp_w1_k, cmp_w2_k, cmp_pe_v, cmp_w1_v, cmp_w2_v, ssm_a_re, ssm_a_im, ssm_log_dt, ssm_b_re, ssm_b_im, ssm_c_re, ssm_c_im, ssm_d, w_glu, b_glu, w_att_br, w_ssm_br, w_o, ffn_norm, w_ffn_gate, w_ffn_up, w_ffn_down, ple_norm, w_ple_gate, w_ple):
    assert w_in.shape[0] == 1, "single-layer trunk"
    l = 0
    nb, t, d = x_prompt.shape
    ns, t_new = x_sample.shape[:2]
    prm = _layer_params(rel_bias, final_norm, attn_norm, w_in, cmp_pe_k, cmp_w1_k, cmp_w2_k, cmp_pe_v, cmp_w1_v,
                        cmp_w2_v, ssm_a_re, ssm_a_im, ssm_log_dt, ssm_b_re, ssm_b_im, ssm_c_re, ssm_c_im, ssm_d,
                        w_glu, b_glu, w_att_br, w_ssm_br, w_o, ffn_norm, w_ffn_gate, w_ffn_up, w_ffn_down,
                        ple_norm, w_ple_gate, w_ple)
    pg = _prompt_group(x_prompt, p_prompt[l], prm)
    sg = _sample_group(x_sample, p_sample[l], page_table,
                       (cache_k_cmp[l], cache_v_cmp[l], cache_k_sel[l], cache_v_sel[l]),
                       (cache_k_win[l], cache_v_win[l]), (state_ssm_re[l], state_ssm_im[l]), prm)

    kv5 = lambda a, b_, t_: a.reshape(1, b_, t_, N_KV, HD)
    keep = min(WINDOW, t)
    win_p = lambda a: a.reshape(nb, t, KVW)[:, t - keep:].reshape(1, nb, keep, N_KV, HD)
    win_s = lambda cache, new: jnp.concatenate(
        [cache[l], new.reshape(ns, t_new, N_KV, HD)], axis=1)[:, t_new:][None]
    st = lambda a, b_: a.reshape(1, b_, prm["n_groups"], SSM_P)
    kc_s, vc_s, ks_s, vs_s, kw_s, vw_s = sg["rows"]
    return (pg["y"].reshape(nb, t, d), sg["y"].reshape(ns, t_new, d),
            *pg["rows5"], win_p(pg["win"][0]), win_p(pg["win"][1]),
            st(pg["state"][0], nb), st(pg["state"][1], nb),
            kv5(kc_s, ns, t_new), kv5(vc_s, ns, t_new), kv5(ks_s, ns, t_new), kv5(vs_s, ns, t_new),
            win_s(cache_k_win, kw_s), win_s(cache_v_win, vw_s),
            st(sg["state"][0], ns), st(sg["state"][1], ns))
```

```python
import functools
import math

import numpy as np
import jax
import jax.numpy as jnp
from jax import lax
from jax.experimental import pallas as pl
from jax.experimental.pallas import tpu as pltpu

F32 = jnp.float32
BF16 = jnp.bfloat16

N_HEADS = 8
N_KV = 2
HD = 64
GROUP = N_HEADS // N_KV
N_BRANCH = 3
CMP_BLOCK = 32
CMP_STRIDE = 16
CMP_HIDDEN = 256
SEL_BLOCK = 64
SEL_TOPK = 16
WINDOW = 512
QB = 128
SEL_KEYS = 2 * QB
SEL_SPLIT = 2
N_BUCKETS = 32
REL_MAX_DIST = 128
SSM_CH = 16
SSM_P = 64
EPS = 1e-6
NEG = -1e30
NEG_TEST = -1e29
FORCE = 1e9
BIG = 1e30
LOG2E = math.log2(math.e)
LANES = 128
VMEM_LIMIT = 56 * 1024 * 1024
AW = N_HEADS * HD
KVW = N_KV * HD
NG_ROWS = 32


def _cparams(sem):
    return pltpu.CompilerParams(dimension_semantics=sem, vmem_limit_bytes=VMEM_LIMIT)


def _const_spec(shape):
    nd = len(shape)
    return pl.BlockSpec(shape, lambda *_: (0,) * nd)


def _rms(x, g):
    return x * lax.rsqrt(jnp.mean(x * x, axis=-1, keepdims=True) + EPS) * g


def _gelu(x):
    return x * (0.5 * (1.0 + jnp.tanh(math.sqrt(2.0 / math.pi) * (x + 0.044715 * (x * x * x)))))


def _sigmoid(x):
    return 1.0 / (1.0 + jnp.exp(-x))


def _dot(a, b):
    return jnp.dot(a, b, preferred_element_type=F32)


def _dot_t(a, b):
    return lax.dot_general(a, b, (((1,), (1,)), ((), ())), preferred_element_type=F32)


def _masked_softmax(s, axis=-1):
    valid = s > NEG_TEST
    m = jnp.max(s, axis=axis, keepdims=True)
    e = jnp.where(valid, jnp.exp(s - m), 0.0)
    return e / jnp.maximum(jnp.sum(e, axis=axis, keepdims=True), 1e-30)


def _softmax2_cols(s):
    m = jnp.max(s, axis=0, keepdims=True)
    e = jnp.exp2(s - m)
    inv = jnp.where(m > NEG_TEST, 1.0 / jnp.maximum(jnp.sum(e, axis=0, keepdims=True), 1e-30), 0.0)
    return e * inv


def _front_project(x_ref, g_ref, wa_ref, wng_ref, wsu_ref, wmg_ref, su_ref, mg_ref, q_scale):
    u = _rms(x_ref[...], g_ref[...]).astype(BF16)
    za = _dot(u, wa_ref[...])
    q = za[:, :AW] * q_scale
    rows = [za[:, AW + i * KVW: AW + (i + 1) * KVW] for i in range(6)]
    ng = _sigmoid(_dot(u, wng_ref[...]))
    su_ref[...] = _dot(u, wsu_ref[...]).astype(su_ref.dtype)
    mg_ref[...] = _sigmoid(_dot(u, wmg_ref[...])).astype(BF16)
    return q, rows, ng


def _front_sample_kernel(x_ref, g_ref, wa_ref, wng_ref, wsu_ref, wmg_ref,
                         q_ref, kc_ref, vc_ref, ks_ref, vs_ref, kw_ref, vw_ref, ng_ref, su_ref, mg_ref):
    q, rows, ng = _front_project(x_ref, g_ref, wa_ref, wng_ref, wsu_ref, wmg_ref, su_ref, mg_ref, HD ** -0.5)
    q_ref[...] = q.astype(BF16)
    for ref, r in zip((kc_ref, vc_ref, ks_ref, vs_ref, kw_ref, vw_ref), rows):
        ref[...] = r
    ng_ref[...] = ng


def _front_prompt_kernel(x_ref, g_ref, wa_ref, wng_ref, wsu_ref, wmg_ref,
                         qt_ref, kc5_ref, vc5_ref, ks5_ref, vs5_ref, kc_ref, vc_ref, kw_ref, vw_ref,
                         ksb_ref, kwb_ref, vst_ref, vwt_ref, ngt_ref, su_ref, mg_ref):
    q, rows, ng = _front_project(x_ref, g_ref, wa_ref, wng_ref, wsu_ref, wmg_ref, su_ref, mg_ref,
                                 HD ** -0.5 * LOG2E)
    kc, vc, ks, vs, kw, vw = rows
    qt_ref[0] = q.T.astype(BF16)
    for ref, r in zip((kc5_ref, vc5_ref, ks5_ref, vs5_ref), (kc, vc, ks, vs)):
        for k in range(N_KV):
            ref[0, 0, :, k, :] = r[:, k * HD:(k + 1) * HD]
    kc_ref[...] = kc
    vc_ref[...] = vc
    kw_ref[...] = kw
    vw_ref[...] = vw
    ksb_ref[...] = ks.astype(BF16)
    kwb_ref[...] = kw.astype(BF16)
    for ref, r in ((vst_ref, vs), (vwt_ref, vw)):
        rt = r.T.astype(BF16)
        for j in range(rt.shape[1] // QB):
            ref[0, j] = rt[:, j * QB:(j + 1) * QB]
    ngt_ref[0] = ng.T[0:NG_ROWS]


def _front_sample(x2d, fw):
    g, wa, wng, wsu, wmg = fw
    n, d = x2d.shape
    sw, mw = wsu.shape[1], wmg.shape[1]
    shapes = ([jax.ShapeDtypeStruct((n, AW), BF16)] + [jax.ShapeDtypeStruct((n, KVW), F32)] * 6
              + [jax.ShapeDtypeStruct((n, LANES), F32), jax.ShapeDtypeStruct((n, sw), BF16),
                 jax.ShapeDtypeStruct((n, mw), BF16)])
    return pl.pallas_call(
        _front_sample_kernel,
        grid=(1,),
        in_specs=[_const_spec(a.shape) for a in (x2d, g, wa, wng, wsu, wmg)],
        out_specs=[_const_spec(s.shape) for s in shapes],
        out_shape=shapes,
        compiler_params=_cparams(("arbitrary",)),
        name="front_sample",
    )(x2d, g, wa, wng, wsu, wmg)


def _front_prompt(x2d, nb, t, fw, tm):
    g, wa, wng, wsu, wmg = fw
    n, d = x2d.shape
    nt = t // tm
    sw, mw = wsu.shape[1], wmg.shape[1]
    row = lambda b, i: (b * nt + i, 0)
    kv5 = jax.ShapeDtypeStruct((1, nb, t, N_KV, HD), F32)
    kv5_spec = pl.BlockSpec((1, 1, tm, N_KV, HD), lambda b, i: (0, b, i, 0, 0))
    vt = jax.ShapeDtypeStruct((nb, t // QB, KVW, QB), BF16)
    vt_spec = pl.BlockSpec((1, tm // QB, KVW, QB), lambda b, i: (b, i, 0, 0))
    shapes = ([jax.ShapeDtypeStruct((nb, AW, t), BF16)] + [kv5] * 4 + [jax.ShapeDtypeStruct((n, KVW), F32)] * 4
              + [jax.ShapeDtypeStruct((n, KVW), BF16)] * 2 + [vt] * 2
              + [jax.ShapeDtypeStruct((nb, NG_ROWS, t), F32), jax.ShapeDtypeStruct((t, nb * sw), BF16),
                 jax.ShapeDtypeStruct((n, mw), BF16)])
    specs = ([pl.BlockSpec((1, AW, tm), lambda b, i: (b, 0, i))] + [kv5_spec] * 4
             + [pl.BlockSpec((tm, KVW), row)] * 6 + [vt_spec] * 2
             + [pl.BlockSpec((1, NG_ROWS, tm), lambda b, i: (b, 0, i)),
                pl.BlockSpec((tm, sw), lambda b, i: (i, b)), pl.BlockSpec((tm, mw), row)])
    return pl.pallas_call(
        _front_prompt_kernel,
        grid=(nb, nt),
        in_specs=[pl.BlockSpec((tm, d), row)] + [_const_spec(a.shape) for a in (g, wa, wng, wsu, wmg)],
        out_specs=specs,
        out_shape=shapes,
        compiler_params=_cparams(("parallel", "parallel")),
        name="front_prompt",
    )(x2d, g, wa, wng, wsu, wmg)


def _chunk_rows(load, r0, rn):
    return jnp.concatenate([load(pl.ds(CMP_STRIDE * r0 + r, rn, stride=CMP_STRIDE)) for r in range(CMP_STRIDE)],
                           axis=1)


def _compress_compute(load_rows, c, w1_ref, w2_ref, pe_ref, a_scr):
    rc_n = min(c, 256)
    lo = lax.broadcasted_iota(jnp.int32, (rc_n, LANES), 1) < HD
    w1 = w1_ref[...]
    for rc in range(c // rc_n):
        x = load_rows(rc * rc_n, rc_n)
        cols = [x[:, r * LANES:(r + 1) * LANES] for r in range(CMP_STRIDE)]
        rol = [pltpu.roll(col, HD, 1) for col in cols]
        for kh in range(N_KV):
            if kh == 0:
                parts = [jnp.where(lo, cols[2 * j], rol[2 * j + 1]) for j in range(CMP_STRIDE // 2)]
            else:
                parts = [jnp.where(lo, rol[2 * j], cols[2 * j + 1]) for j in range(CMP_STRIDE // 2)]
            xh = jnp.concatenate(parts, axis=1).astype(BF16)
            a_scr[kh, rc * rc_n:(rc + 1) * rc_n, :] = _dot(xh, w1)
    pw = _dot(pe_ref[...], w1)
    peb = pw[0:1, :CMP_HIDDEN] + pw[1:2, CMP_HIDDEN:]
    w2 = w2_ref[...]
    outs = []
    for kh in range(N_KV):
        a = a_scr[kh]
        hid = a[:, :CMP_HIDDEN] + pltpu.roll(a[:, CMP_HIDDEN:], c - 1, 0) + peb
        outs.append(_dot(_gelu(hid).astype(BF16), w2))
    return jnp.concatenate(outs, axis=1)


def _compress_prompt_kernel(xk_ref, xv_ref, w1k_ref, w2k_ref, pek_ref, w1v_ref, w2v_ref, pev_ref,
                            ok_ref, ovt_ref, a_scr):
    c = xk_ref.shape[1] // CMP_STRIDE
    ok_ref[0] = _compress_compute(lambda r0, rn: _chunk_rows(lambda idx: xk_ref[0, idx, :], r0, rn), c,
                                  w1k_ref, w2k_ref, pek_ref, a_scr).astype(BF16)
    ovt_ref[0] = _compress_compute(lambda r0, rn: _chunk_rows(lambda idx: xv_ref[0, idx, :], r0, rn), c,
                                   w1v_ref, w2v_ref, pev_ref, a_scr).T.astype(BF16)


def _compress_prompt(xk, xv, cw):
    nb, t, kvw = xk.shape
    c = t // CMP_STRIDE
    wspecs = [_const_spec(w.shape) for w in cw]
    blk = pl.BlockSpec((1, t, kvw), lambda b: (b, 0, 0))
    return pl.pallas_call(
        _compress_prompt_kernel,
        grid=(nb,),
        in_specs=[blk, blk] + wspecs,
        out_specs=[pl.BlockSpec((1, c, KVW), lambda b: (b, 0, 0)), pl.BlockSpec((1, KVW, c), lambda b: (b, 0, 0))],
        out_shape=[jax.ShapeDtypeStruct((nb, c, KVW), BF16), jax.ShapeDtypeStruct((nb, KVW, c), BF16)],
        scratch_shapes=[pltpu.VMEM((N_KV, c, 2 * CMP_HIDDEN), F32)],
        compiler_params=_cparams(("parallel",)),
        name="compress_prompt",
    )(xk, xv, *cw)


def _page_copies(pool, buf, sem, page, p, slot, rows):
    dst_rows = pl.ds(p * rows, rows)
    if pool.ndim == 3:
        return [pltpu.make_async_copy(pool.at[page], buf.at[slot, dst_rows], sem)]
    return [pltpu.make_async_copy(pool.at[page, :, k, :], buf.at[slot, k, dst_rows, :], sem)
            for k in range(pool.shape[2])]


def _page_gather_start(pt_ref, seq, pools, bufs, sems, slot, n_pages, rows):
    def body(p, carry):
        page = pt_ref[seq, p]
        for i, (pool, buf) in enumerate(zip(pools, bufs)):
            for cp in _page_copies(pool, buf, sems.at[i, slot], page, p, slot, rows):
                cp.start()
        return carry
    lax.fori_loop(0, n_pages, body, 0)


def _page_gather_wait(pools, bufs, sems, slot, n_pages, rows):
    def body(p, carry):
        for i, (pool, buf) in enumerate(zip(pools, bufs)):
            for cp in _page_copies(pool, buf, sems.at[i, slot], 0, p, slot, rows):
                cp.wait()
        return carry
    lax.fori_loop(0, n_pages, body, 0)


def _paged_prefetch(pt_ref, pools, bufs, sems, n_pages, rows):
    s = pl.program_id(0)
    slot = s % 2

    @pl.when(s == 0)
    def _():
        _page_gather_start(pt_ref, 0, pools, bufs, sems, 0, n_pages, rows)

    @pl.when(s + 1 < pl.num_programs(0))
    def _():
        _page_gather_start(pt_ref, s + 1, pools, bufs, sems, 1 - slot, n_pages, rows)

    _page_gather_wait(pools, bufs, sems, slot, n_pages, rows)
    return slot


def _compress_sample_kernel(pt_ref, kpool, vpool, w1k_ref, w2k_ref, pek_ref, w1v_ref, w2v_ref, pev_ref,
                            ok_ref, ov_ref, kbuf, vbuf, sems, a_scr):
    n_pages = pt_ref.shape[1]
    rows = kpool.shape[1]
    c = n_pages * rows // CMP_STRIDE
    slot = _paged_prefetch(pt_ref, (kpool, vpool), (kbuf, vbuf), sems, n_pages, rows)
    ok_ref[0] = _compress_compute(lambda r0, rn: _chunk_rows(lambda idx: kbuf[slot, idx, :], r0, rn), c,
                                  w1k_ref, w2k_ref, pek_ref, a_scr).astype(BF16)
    ov_ref[0] = _compress_compute(lambda r0, rn: _chunk_rows(lambda idx: vbuf[slot, idx, :], r0, rn), c,
                                  w1v_ref, w2v_ref, pev_ref, a_scr).astype(BF16)


def _compress_sample(page_table, kpool, vpool, cw):
    ns, n_pages = page_table.shape
    rows, width = kpool.shape[1:]
    tokens = n_pages * rows
    c = tokens // CMP_STRIDE
    any_spec = pl.BlockSpec(memory_space=pl.ANY)
    wspecs = [pl.BlockSpec(w.shape, lambda s, pt, nd=w.ndim: (0,) * nd) for w in cw]
    oblk = pl.BlockSpec((1, c, KVW), lambda s, pt: (s, 0, 0))
    return pl.pallas_call(
        _compress_sample_kernel,
        grid_spec=pltpu.PrefetchScalarGridSpec(
            num_scalar_prefetch=1,
            grid=(ns,),
            in_specs=[any_spec, any_spec] + wspecs,
            out_specs=[oblk, oblk],
            scratch_shapes=[pltpu.VMEM((2, tokens, width), F32), pltpu.VMEM((2, tokens, width), F32),
                            pltpu.SemaphoreType.DMA((2, 2)),
                            pltpu.VMEM((N_KV, c, 2 * CMP_HIDDEN), F32)]),
        out_shape=[jax.ShapeDtypeStruct((ns, c, KVW), BF16)] * 2,
        compiler_params=_cparams(("arbitrary",)),
        name="compress_sample",
    )(page_table, kpool, vpool, *cw)


def _rank_select(score, blk, n_real, axis):
    size = 8 if axis == 0 else LANES
    total = score.shape[axis]
    chunk = (lambda a, c: a[c * size:(c + 1) * size]) if axis == 0 else (lambda a, c: a[:, c * size:(c + 1) * size])
    n_chunks = -(-total // size)
    sc = [chunk(score, c) for c in range(n_chunks)]
    bl = [chunk(blk, c) for c in range(n_chunks)]
    rank = [jnp.zeros(s.shape, F32) for s in sc]
    for kk in range(n_real):
        col = score[kk:kk + 1, :] if axis == 0 else score[:, kk:kk + 1]
        for c in range(n_chunks):
            other = jnp.broadcast_to(col, sc[c].shape)
            if c * size > kk:
                beats = other >= sc[c]
            elif min((c + 1) * size, total) - 1 < kk:
                beats = other > sc[c]
            else:
                beats = (other > sc[c]) | ((other == sc[c]) & (bl[c] > kk))
            rank[c] = rank[c] + jnp.where(beats, 1.0, 0.0)
    return jnp.where(jnp.concatenate(rank, axis=axis) < SEL_TOPK, 1.0, 0.0)


def _block_scores(imp, blk, t):
    cur = t // SEL_BLOCK
    forced = (blk == 0) | (blk == cur) | (blk == cur - 1)
    valid = blk * SEL_BLOCK <= t
    return jnp.where(valid, jnp.where(forced, FORCE, imp), NEG)


def _nsa_prompt_kernel(qt_ref, ngt_ref, kc_ref, vct_ref, ks_ref, vst_ref, kw_ref, vwt_ref,
                       ut_ref, at_ref, ovlt_ref, stat_ref, crow_ref, o_ref, *, n_sel):
    ib = pl.program_id(1)
    qt = qt_ref[0]
    ngt = ngt_ref[0]
    ncp = kc_ref.shape[1]
    cols = GROUP * QB
    sel_rows = LANES // 2
    t_row = ib * QB + lax.broadcasted_iota(jnp.int32, (n_sel, QB), 1)
    blk_t = lax.broadcasted_iota(jnp.int32, (n_sel, QB), 0)
    zeros_q = jnp.zeros((HD, cols), F32)
    vrows = [slice(k * HD, (k + 1) * HD) for k in range(N_KV)]
    q_sel, q_win, o_c = [], [], []
    for k in range(N_KV):
        qk = jnp.concatenate([qt[(GROUP * k + g) * HD:(GROUP * k + g + 1) * HD, :] for g in range(GROUP)],
                             axis=1).astype(F32)
        qa = jnp.concatenate([qk, zeros_q] if k == 0 else [zeros_q, qk], axis=0)

        bias_c = ut_ref[k, pl.ds(pl.multiple_of(ncp - (QB // CMP_STRIDE) * ib, 8), ncp), :]
        p_c = _softmax2_cols(_dot(kc_ref[0], qa.astype(BF16)) + bias_c)
        o_c.append(_dot(vct_ref[0][vrows[k], :], p_c.astype(BF16)))
        psum = p_c[:, 0:QB]
        for g in range(1, GROUP):
            psum = psum + p_c[:, g * QB:(g + 1) * QB]
        imp = jnp.dot(ovlt_ref[...], psum, preferred_element_type=F32, precision=lax.Precision.HIGHEST)

        sel = _rank_select(_block_scores(imp[0:n_sel], blk_t, t_row), blk_t, n_sel, 0)
        selm1 = jnp.concatenate([sel - 1.0] * GROUP, axis=1)
        if n_sel < sel_rows:
            selm1 = jnp.concatenate([selm1, jnp.zeros((sel_rows - n_sel, cols), F32)], axis=0)
        tail = jnp.concatenate([crow_ref[k], jnp.zeros((LANES - sel_rows - 8, cols), F32)], axis=0)
        q_sel.append(jnp.concatenate([qa, selm1, tail], axis=0).astype(BF16))
        q_win.append(jnp.concatenate([qa, jnp.zeros((sel_rows, cols), F32), tail], axis=0).astype(BF16))

    tiles_per_step = SEL_KEYS // QB

    ones_rows = jnp.where(lax.broadcasted_iota(jnp.int32, (16, SEL_KEYS), 0) == 0, 1.0, 0.0).astype(BF16)
    step_tiles = SEL_SPLIT * tiles_per_step

    def make_sel_body(near):
        def sel_body(jp, carry):
            scores, vts = [], []
            for sp in range(SEL_SPLIT):
                j0 = (jp * SEL_SPLIT + sp) * tiles_per_step
                k0 = pl.multiple_of(j0 * QB, SEL_KEYS)
                lhs = jnp.concatenate([ks_ref[0, pl.ds(k0, SEL_KEYS), :], stat_ref[pl.ds(k0, SEL_KEYS), :]], axis=1)
                tiles = [j0 + h for h in range(tiles_per_step)]
                tidx = [jnp.where(jt == ib, 0, jnp.where(jt == ib - 1, 1, jnp.where(jt < ib, 2, 4))) for jt in tiles]
                for k in range(N_KV):
                    s = _dot(lhs, q_sel[k])
                    if near:
                        s = s + jnp.concatenate([at_ref[ti, k] for ti in tidx], axis=0)
                    scores.append(s)
                    vts.append(jnp.concatenate([vst_ref[0, jt, vrows[k], :] for jt in tiles], axis=1))
            stats = []
            for (m, _), s in zip(carry, scores):
                m_new = jnp.maximum(m, jnp.max(s, axis=0, keepdims=True))
                stats.append((m_new, jnp.exp2(m - m_new), jnp.exp2(s - m_new).astype(BF16)))
            return tuple((m_new, alpha * acc + _dot(jnp.concatenate([vt, ones_rows], axis=0), p))
                         for (_, acc), (m_new, alpha, p), vt in zip(carry, stats, vts))
        return sel_body

    init = (jnp.full((1, cols), NEG, F32), jnp.zeros((HD + ones_rows.shape[0], cols), F32))
    n_far = jnp.maximum(ib - 1, 0) // step_tiles
    sel_state = lax.fori_loop(0, n_far, make_sel_body(False), (init,) * (N_KV * SEL_SPLIT))
    sel_state = lax.fori_loop(n_far, (ib + step_tiles) // step_tiles, make_sel_body(True), sel_state)

    out_rows = []
    for k in range(N_KV):
        parts = [sel_state[sp * N_KV + k] for sp in range(SEL_SPLIT)]
        m_s = parts[0][0]
        for m_p, _ in parts[1:]:
            m_s = jnp.maximum(m_s, m_p)
        acc_s = jnp.zeros(init[1].shape, F32)
        for m_p, acc_p in parts:
            acc_s = acc_s + jnp.exp2(m_p - m_s) * acc_p
        o_s = acc_s[0:HD] / jnp.maximum(acc_s[HD:HD + 1], 1e-30)

        s_parts, tiles_j = [], []
        for w, tidx in enumerate((0, 1, None, None, 3)):
            jt = ib - w
            jc = jnp.maximum(jt, 0)
            k0 = pl.multiple_of(jc * QB, QB)
            lhs = jnp.concatenate([kw_ref[0, pl.ds(k0, QB), :], stat_ref[pl.ds(k0, QB), :]], axis=1)
            s = _dot(lhs, q_win[k])
            if tidx is not None:
                s = s + at_ref[tidx, k]
            s_parts.append(jnp.where(jt >= 0, s, NEG))
            tiles_j.append(jc)
        p_w = _softmax2_cols(jnp.concatenate(s_parts, axis=0)).astype(BF16)
        o_w = jnp.zeros((HD, cols), F32)
        for w, jc in enumerate(tiles_j):
            o_w = o_w + _dot(vwt_ref[0, jc, vrows[k], :], p_w[w * QB:(w + 1) * QB])

        def gate_row(br):
            return jnp.concatenate([ngt[(GROUP * k + g) * N_BRANCH + br:(GROUP * k + g) * N_BRANCH + br + 1, :]
                                    for g in range(GROUP)], axis=1)
        o_k = gate_row(0) * o_c[k] + gate_row(1) * o_s + gate_row(2) * o_w
        out_rows += [o_k[:, g * QB:(g + 1) * QB] for g in range(GROUP)]
    o_ref[0] = jnp.concatenate(out_rows, axis=0).T.astype(BF16)


def _nsa_prompt(qt, ngt, kc, vct, ks, vst, kw, vwt, tables):
    nb, aw, t = qt.shape
    nq = t // QB
    ncp = kc.shape[1]
    full3 = lambda b, i: (b, 0, 0)
    full4 = lambda b, i: (b, 0, 0, 0)
    return pl.pallas_call(
        functools.partial(_nsa_prompt_kernel, n_sel=t // SEL_BLOCK),
        grid=(nb, nq),
        in_specs=[pl.BlockSpec((1, aw, QB), lambda b, i: (b, 0, i)),
                  pl.BlockSpec((1, NG_ROWS, QB), lambda b, i: (b, 0, i)),
                  pl.BlockSpec((1, ncp, KVW), full3), pl.BlockSpec((1, KVW, ncp), full3),
                  pl.BlockSpec((1, t, KVW), full3), pl.BlockSpec((1, nq, KVW, QB), full4),
                  pl.BlockSpec((1, t, KVW), full3), pl.BlockSpec((1, nq, KVW, QB), full4)]
                 + [_const_spec(a.shape) for a in tables],
        out_specs=pl.BlockSpec((1, QB, aw), lambda b, i: (b, i, 0)),
        out_shape=jax.ShapeDtypeStruct((nb, t, aw), BF16),
        compiler_params=_cparams(("parallel", "arbitrary")),
        name="nsa_prompt",
    )(qt, ngt, kc, vct, ks, vst, kw, vwt, *tables)


def _nsa_sample_kernel(pt_ref, q_ref, gate_ref, kc_ref, vc_ref, kpool, vpool, ksn_ref, vsn_ref,
                       kwin_ref, vwin_ref, kwn_ref, vwn_ref, bc_ref, bs_ref, bw_ref, ovl_ref, e_ref,
                       o_ref, kbuf, vbuf, sems, *, n_sel, past, t_new):
    n_pages = pt_ref.shape[1]
    page = kpool.shape[1]
    wb = kwin_ref.shape[1]
    slot = _paged_prefetch(pt_ref, (kpool, vpool), (kbuf, vbuf), sems, n_pages, page)
    q = q_ref[0]
    nb_past = past // SEL_BLOCK
    pad_new = jnp.zeros((QB - ksn_ref.shape[1], KVW), F32)
    new_tile = lambda ref: jnp.concatenate([ref[0], pad_new], axis=0).astype(BF16)

    p_c = _masked_softmax(_dot_t(q, kc_ref[0]) + bc_ref[...])
    o_c = _dot(p_c.astype(BF16), vc_ref[0])
    parts = []
    for k in range(N_KV):
        base = k * GROUP * t_new
        ps = p_c[base:base + t_new]
        for g in range(1, GROUP):
            ps = ps + p_c[base + g * t_new:base + (g + 1) * t_new]
        parts.append(ps)
    psum = jnp.concatenate(parts, axis=0)
    imp = jnp.dot(psum, ovl_ref[...], preferred_element_type=F32, precision=lax.Precision.HIGHEST)
    blk = lax.broadcasted_iota(jnp.int32, imp.shape, 1)
    tpos = past + lax.broadcasted_iota(jnp.int32, imp.shape, 0) % t_new
    sel = _rank_select(_block_scores(imp, blk, tpos), blk, n_sel, 1)
    sel = jnp.concatenate([sel[k * t_new:(k + 1) * t_new] for k in range(N_KV) for _ in range(GROUP)], axis=0)

    mask_add = _dot((sel[:, 0:LANES] - 1.0).astype(BF16), e_ref[...])
    rows_k = GROUP * t_new
    head = lambda a, k: a[k * rows_k:(k + 1) * rows_k]
    s_past = jnp.concatenate([_dot_t(head(q, k)[:, k * HD:(k + 1) * HD], kbuf[slot, k].astype(BF16))
                              for k in range(N_KV)], axis=0) + bs_ref[:, 0:past] + mask_add
    s_new = _dot_t(q, new_tile(ksn_ref)) + bs_ref[:, past:]
    s_new = jnp.where(sel[:, nb_past:nb_past + 1] > 0.5, s_new, NEG)
    p_s = _masked_softmax(jnp.concatenate([s_past, s_new], axis=1)).astype(BF16)
    zeros_o = jnp.zeros((rows_k, HD), F32)
    o_past = [_dot(head(p_s, k)[:, 0:past], vbuf[slot, k].astype(BF16)) for k in range(N_KV)]
    o_s = jnp.concatenate([jnp.concatenate([o_past[0], zeros_o], axis=1),
                           jnp.concatenate([zeros_o, o_past[1]], axis=1)], axis=0)
    o_s = o_s + _dot(p_s[:, past:], new_tile(vsn_ref))

    s_w = jnp.concatenate([_dot_t(q, kwin_ref[0].astype(BF16)), _dot_t(q, new_tile(kwn_ref))], axis=1)
    p_w = _masked_softmax(s_w + bw_ref[...]).astype(BF16)
    o_w = _dot(p_w[:, 0:wb], vwin_ref[0].astype(BF16)) + _dot(p_w[:, wb:], new_tile(vwn_ref))

    gate = gate_ref[0]
    o_ref[0] = gate[:, 0:1] * o_c + gate[:, 1:2] * o_s + gate[:, 2:3] * o_w


def _nsa_sample(page_table, q, gate, kc, vc, kpool, vpool, ksn, vsn, kwin, vwin, kwn, vwn,
                bc, bs, bw, ovl, e, n_sel, past, t_new):
    ns, n_pages = page_table.shape
    page = kpool.shape[1]
    rows, kvw = q.shape[1:]
    buf_shape = (2, kpool.shape[2], n_pages * page, kpool.shape[3])
    seq3 = lambda s, pt: (s, 0, 0)
    any_spec = pl.BlockSpec(memory_space=pl.ANY)
    cs = lambda a: pl.BlockSpec(a.shape, lambda s, pt, nd=a.ndim: (0,) * nd, pipeline_mode=pl.Buffered(1))
    per_seq = lambda a: pl.BlockSpec((1,) + a.shape[1:], seq3)
    return pl.pallas_call(
        functools.partial(_nsa_sample_kernel, n_sel=n_sel, past=past, t_new=t_new),
        grid_spec=pltpu.PrefetchScalarGridSpec(
            num_scalar_prefetch=1,
            grid=(ns,),
            in_specs=[per_seq(q), per_seq(gate), per_seq(kc), per_seq(vc), any_spec, any_spec,
                      per_seq(ksn), per_seq(vsn), per_seq(kwin), per_seq(vwin), per_seq(kwn), per_seq(vwn),
                      cs(bc), cs(bs), cs(bw), cs(ovl), cs(e)],
            out_specs=pl.BlockSpec((1, rows, kvw), seq3),
            scratch_shapes=[pltpu.VMEM(buf_shape, F32), pltpu.VMEM(buf_shape, F32), pltpu.SemaphoreType.DMA((2, 2))]),
        out_shape=jax.ShapeDtypeStruct((ns, rows, kvw), F32),
        compiler_params=_cparams(("arbitrary",)),
        name="nsa_sample",
    )(page_table, q, gate, kc, vc, kpool, vpool, ksn, vsn, kwin, vwin, kwn, vwn, bc, bs, bw, ovl, e)


def _ssm_param_kernel(ar_ref, ai_ref, ldt_ref, br_ref, bi_ref, abr_ref, abi_ref, bbr_ref, bbi_ref):
    ar = ar_ref[...]
    ai = ai_ref[...]
    dt = jnp.exp(ldt_ref[...])
    mag = jnp.exp(ar * dt)
    abr = mag * jnp.cos(ai * dt)
    abi = mag * jnp.sin(ai * dt)
    den = ar * ar + ai * ai
    nr, ni = abr - 1.0, abi
    fr = (nr * ar + ni * ai) / den
    fi = (ni * ar - nr * ai) / den
    abr_ref[...] = abr
    abi_ref[...] = abi
    for g in range(ar.shape[0]):
        br = br_ref[g]
        bi = bi_ref[g]
        frg = fr[g:g + 1, :]
        fig = fi[g:g + 1, :]
        bbr_ref[g] = frg * br - fig * bi
        bbi_ref[g] = frg * bi + fig * br


def _ssm_params(a_re, a_im, log_dt, b_re_t, b_im_t):
    g, p = a_re.shape
    return pl.pallas_call(
        _ssm_param_kernel,
        out_shape=[jax.ShapeDtypeStruct((g, p), F32)] * 2 + [jax.ShapeDtypeStruct(b_re_t.shape, F32)] * 2,
        name="ssm_params",
    )(a_re, a_im, log_dt.reshape(g, 1), b_re_t, b_im_t)


def _ssm_kernel(u_ref, h0r_ref, h0i_ref, ar_ref, ai_ref, bd_ref, cd_ref, d_ref, wglu_ref, bglu_ref,
                so_ref, hr_ref, hi_ref, xr_scr, xi_scr, *, bt):
    i = pl.program_id(0)
    rows, width = u_ref.shape
    n_slab = bd_ref.shape[0]
    sw = bd_ref.shape[2] // 2
    u = u_ref[...]

    @pl.when(i == 0)
    def _():
        hr_ref[...] = h0r_ref[...]
        hi_ref[...] = h0i_ref[...]

    for sl in range(n_slab):
        x = _dot(u[:, sl * LANES:(sl + 1) * LANES], bd_ref[sl])
        xr_scr[:, sl * sw:(sl + 1) * sw] = x[:, :sw]
        xi_scr[:, sl * sw:(sl + 1) * sw] = x[:, sw:]

    per = 8 // math.gcd(bt, 8)
    grp = per * bt
    lc = 512
    for c0 in range(0, xr_scr.shape[1], lc):
        cl = slice(c0, c0 + lc)
        a_r = jnp.broadcast_to(ar_ref[:, cl], (bt, lc))
        a_i = jnp.broadcast_to(ai_ref[:, cl], (bt, lc))

        def step(j, carry):
            h_r, h_i = carry
            r0 = pl.multiple_of(j * grp, grp)
            xr = xr_scr[pl.ds(r0, grp), cl]
            xi = xi_scr[pl.ds(r0, grp), cl]
            out_r, out_i = [], []
            for s in range(per):
                n_r = a_r * h_r - a_i * h_i + xr[s * bt:(s + 1) * bt]
                n_i = a_r * h_i + a_i * h_r + xi[s * bt:(s + 1) * bt]
                h_r, h_i = n_r, n_i
                out_r.append(h_r)
                out_i.append(h_i)
            xr_scr[pl.ds(r0, grp), cl] = jnp.concatenate(out_r, axis=0) if per > 1 else out_r[0]
            xi_scr[pl.ds(r0, grp), cl] = jnp.concatenate(out_i, axis=0) if per > 1 else out_i[0]
            return h_r, h_i

        h_r, h_i = lax.fori_loop(0, rows // grp, step, (hr_ref[:, cl], hi_ref[:, cl]))
        hr_ref[:, cl] = h_r
        hi_ref[:, cl] = h_i

    ys = []
    for sl in range(n_slab):
        hcat = jnp.concatenate([xr_scr[:, sl * sw:(sl + 1) * sw], xi_scr[:, sl * sw:(sl + 1) * sw]], axis=1)
        ys.append(_dot(hcat.astype(BF16), cd_ref[sl]))
    y = jnp.concatenate(ys, axis=1) + d_ref[...] * u.astype(F32)
    z = _gelu(y)
    so_ref[...] = (z * _sigmoid(_dot(z.astype(BF16), wglu_ref[...]) + bglu_ref[...])).astype(so_ref.dtype)


def _ssm(u, h0r, h0i, ar, ai, bd, cd, dvec, wglu, bglu, bt, tt):
    n, width = u.shape
    rows = tt * bt
    nstate = ar.shape[1]
    cst = [_const_spec(a.shape) for a in (h0r, h0i, ar, ai, bd, cd, dvec, wglu, bglu)]
    st_spec = _const_spec((bt, nstate))
    return pl.pallas_call(
        functools.partial(_ssm_kernel, bt=bt),
        grid=(n // rows,),
        in_specs=[pl.BlockSpec((rows, width), lambda i: (i, 0))] + cst,
        out_specs=[pl.BlockSpec((rows, width), lambda i: (i, 0)), st_spec, st_spec],
        out_shape=[jax.ShapeDtypeStruct((n, width), BF16), jax.ShapeDtypeStruct((bt, nstate), F32),
                   jax.ShapeDtypeStruct((bt, nstate), F32)],
        scratch_shapes=[pltpu.VMEM((rows, nstate), F32), pltpu.VMEM((rows, nstate), F32)],
        compiler_params=_cparams(("arbitrary",)),
        name="ssm",
    )(u, h0r, h0i, ar, ai, bd, cd, dvec, wglu, bglu)


def _back_kernel(h_ref, o_ref, so_ref, mg_ref, p_ref, watt_ref, wssm_ref, wo_ref, fn_ref, wg_ref, wu_ref, wd_ref,
                 pn_ref, wpg_ref, wple_ref, fin_ref, y_ref, *, ff_chunk):
    d = h_ref.shape[1]
    a = _dot(o_ref[...], watt_ref[...])
    s = _dot(so_ref[...], wssm_ref[...])
    mg = mg_ref[...].astype(F32)
    h = h_ref[...] + _dot((mg[:, :d] * a + mg[:, d:] * s).astype(BF16), wo_ref[...])
    f = _rms(h, fn_ref[...]).astype(BF16)
    ffn = jnp.zeros_like(h)
    for c0 in range(0, wg_ref.shape[1], ff_chunk):
        gate = _dot(f, wg_ref[:, c0:c0 + ff_chunk])
        up = _dot(f, wu_ref[:, c0:c0 + ff_chunk])
        ffn = ffn + _dot((gate * _sigmoid(gate) * up).astype(BF16), wd_ref[c0:c0 + ff_chunk, :])
    h = h + ffn
    g = _sigmoid(_dot(_rms(h, pn_ref[...]).astype(BF16), wpg_ref[...]))
    h = h + g * _dot(p_ref[...].astype(BF16), wple_ref[...])
    y_ref[...] = _rms(h, fin_ref[...])


def _back(h2d, o2d, so_tb, mg, p2d, weights, nb, t, tm, ff_chunk):
    n, d = h2d.shape
    nt = t // tm
    row = lambda b, i: (b * nt + i, 0)
    wspecs = [pl.BlockSpec(w.shape, lambda b, i, nd=w.ndim: (0,) * nd, pipeline_mode=pl.Buffered(1))
              for w in weights]
    sw = o2d.shape[1]
    return pl.pallas_call(
        functools.partial(_back_kernel, ff_chunk=ff_chunk),
        grid=(nb, nt),
        in_specs=[pl.BlockSpec((tm, d), row), pl.BlockSpec((tm, sw), row),
                  pl.BlockSpec((tm, sw), lambda b, i: (i, b)),
                  pl.BlockSpec((tm, mg.shape[1]), row), pl.BlockSpec((tm, p2d.shape[1]), row)] + wspecs,
        out_specs=pl.BlockSpec((tm, d), row),
        out_shape=jax.ShapeDtypeStruct((n, d), F32),
        compiler_params=_cparams(("parallel", "parallel")),
        name="back",
    )(h2d, o2d, so_tb, mg, p2d, *weights)


def _bucket_np(dist):
    n = np.maximum(dist, 0)
    exact = N_BUCKETS // 2
    nf = np.maximum(n, 1).astype(np.float64)
    large = exact + (np.log(nf / exact) / math.log(REL_MAX_DIST / exact) * (N_BUCKETS - exact)).astype(np.int64)
    return np.where(n < exact, n, np.minimum(large, N_BUCKETS - 1)).astype(np.int32)


def _bias_table(rel_bias, dist, valid, offset=None):
    onehot = jax.nn.one_hot(jnp.asarray(_bucket_np(dist)), N_BUCKETS, dtype=F32)
    b = jnp.einsum('...b,bh->h...', onehot, rel_bias.astype(F32), precision=lax.Precision.HIGHEST)
    if offset is not None:
        b = b - offset.reshape((N_HEADS,) + (1,) * dist.ndim)
    return jnp.where(jnp.asarray(valid)[None], b, NEG)


def _prompt_tables(rel_bias, t):
    def cols(b):
        r = b.shape[1]
        return b.reshape(N_KV, GROUP, r, QB).transpose(0, 2, 1, 3).reshape(N_KV, r, GROUP * QB)
    rel_bias = rel_bias.astype(F32) * LOG2E
    c = rel_bias[N_BUCKETS - 1]
    c_hi = c.astype(BF16)
    c_lo = (c - c_hi.astype(F32)).astype(BF16)
    c_eff = c_hi.astype(F32) + c_lo.astype(F32)
    crow = jnp.stack([c_hi.astype(F32), c_lo.astype(F32)] + [jnp.zeros_like(c)] * 6, axis=1)
    crow = jnp.broadcast_to(crow[:, :, None], (N_HEADS, 8, QB))
    crow = cols(crow)
    j = np.arange(QB)[:, None]
    i = np.arange(QB)[None, :]
    ones = np.ones((QB, QB), bool)
    zeros = jnp.zeros((N_KV, QB, GROUP * QB), F32)
    at = jnp.stack([
        cols(_bias_table(rel_bias, i - j, i >= j, c_eff)),
        cols(_bias_table(rel_bias, QB + i - j, ones, c_eff)),
        zeros,
        jnp.where(jnp.asarray(np.tile(j > i, (1, GROUP)))[None], zeros, NEG),
        zeros + NEG,
    ])
    ncp = t // CMP_STRIDE
    m = np.arange(2 * ncp)[:, None] - ncp
    dist = i - CMP_STRIDE * m - (CMP_BLOCK - 1)
    ut = cols(_bias_table(rel_bias, dist, dist >= 0))
    n_sel = t // SEL_BLOCK
    n = np.arange(ncp)[None, :]
    jb = np.arange(LANES)[:, None]
    ovlt = ((n * CMP_STRIDE < jb * SEL_BLOCK + SEL_BLOCK) & (n * CMP_STRIDE + CMP_BLOCK - 1 >= jb * SEL_BLOCK)
            & (jb < n_sel) & (n < ncp - 1))
    key = np.arange(t)[:, None]
    lane = np.arange(LANES)[None, :]
    stat = np.where(lane < LANES // 2, (lane == key // SEL_BLOCK) * BIG,
                    ((lane == LANES // 2) | (lane == LANES // 2 + 1)) * 1.0).astype(np.float32)
    return ut, at, jnp.asarray(ovlt.astype(np.float32)), jnp.asarray(stat, BF16), crow


def _sample_tables(rel_bias, past, t_new, win_buf):
    def rows(b):
        return b.reshape(N_HEADS * t_new, b.shape[-1])
    tok = np.arange(t_new)[:, None]
    nc = past // CMP_STRIDE
    n = np.arange(nc)[None, :]
    c_end = n * CMP_STRIDE + CMP_BLOCK - 1
    n_cmp = (past + t_new) // CMP_STRIDE - 1
    bc = rows(_bias_table(rel_bias, past + tok - c_end, (c_end <= past + tok) & (n < n_cmp)))
    js = np.arange(past + QB)[None, :]
    ds = np.where(js < past, past + tok - js, tok - (js - past))
    bs = rows(_bias_table(rel_bias, ds, np.where(js < past, True, (ds >= 0) & (js - past < t_new))))
    jw = np.arange(win_buf + QB)[None, :]
    dw = np.where(jw < win_buf, win_buf + tok - jw, tok - (jw - win_buf))
    valid = np.where(jw < win_buf, (dw >= 0) & (dw < WINDOW), (dw >= 0) & (jw - win_buf < t_new))
    bw = rows(_bias_table(rel_bias, dw, valid))
    n_sel = -(-(past + t_new) // SEL_BLOCK)
    nbp = 2 * LANES
    nn = np.arange(nc)[:, None]
    jb = np.arange(nbp)[None, :]
    ovl = ((nn * CMP_STRIDE < jb * SEL_BLOCK + SEL_BLOCK) & (nn * CMP_STRIDE + CMP_BLOCK - 1 >= jb * SEL_BLOCK)
           & (jb < n_sel) & (nn < n_cmp))
    e = (np.arange(LANES)[:, None] == np.arange(past)[None, :] // SEL_BLOCK).astype(np.float32) * BIG
    return bc, bs, bw, jnp.asarray(ovl.astype(np.float32)), jnp.asarray(e, BF16), n_sel


def _block_diag(blocks):
    g, r, c = blocks.shape
    eye = jnp.eye(g, dtype=blocks.dtype)
    return jnp.einsum('grc,gh->grhc', blocks, eye).reshape(g * r, g * c)


def _layer_params(rel_bias, final_norm, attn_norm, w_in, cmp_pe_k, cmp_w1_k, cmp_w2_k, cmp_pe_v, cmp_w1_v, cmp_w2_v,
                  ssm_a_re, ssm_a_im, ssm_log_dt, ssm_b_re, ssm_b_im, ssm_c_re, ssm_c_im, ssm_d, w_glu, b_glu,
                  w_att_br, w_ssm_br, w_o, ffn_norm, w_ffn_gate, w_ffn_up, w_ffn_down, ple_norm, w_ple_gate, w_ple):
    l = 0
    d = w_in.shape[1]
    n_groups = ssm_a_re.shape[1]
    ssm_w = n_groups * SSM_CH
    nstate = n_groups * SSM_P
    assert ssm_w % LANES == 0
    w = w_in[l]
    c0 = AW + 6 * KVW
    n_gate = N_HEADS * N_BRANCH
    front_w = (attn_norm[l].reshape(1, d), w[:, :c0].astype(BF16),
               jnp.pad(w[:, c0:c0 + n_gate], ((0, 0), (0, LANES - n_gate))).astype(BF16),
               w[:, c0 + n_gate:c0 + n_gate + ssm_w].astype(BF16), w[:, c0 + n_gate + ssm_w:].astype(BF16))

    def cmp_weights(pe, w1, w2):
        half = CMP_STRIDE * HD
        w1cat = jnp.concatenate([w1[:half], w1[half:]], axis=1).astype(BF16)
        pe2 = jnp.pad(pe.reshape(2, half), ((0, 6), (0, 0))).astype(BF16)
        return w1cat, w2.astype(BF16), pe2
    cw = cmp_weights(cmp_pe_k[l], cmp_w1_k[l], cmp_w2_k[l]) + cmp_weights(cmp_pe_v[l], cmp_w1_v[l], cmp_w2_v[l])

    abr, abi, bbr_t, bbi_t = _ssm_params(ssm_a_re[l], ssm_a_im[l], ssm_log_dt[l],
                                         jnp.swapaxes(ssm_b_re[l], 1, 2), jnp.swapaxes(ssm_b_im[l], 1, 2))
    n_slab = ssm_w // LANES
    sw = nstate // n_slab
    bd_r = _block_diag(bbr_t)
    bd_i = _block_diag(bbi_t)
    bd = jnp.stack([jnp.concatenate([bd_r[s * LANES:(s + 1) * LANES, s * sw:(s + 1) * sw],
                                     bd_i[s * LANES:(s + 1) * LANES, s * sw:(s + 1) * sw]], axis=1)
                    for s in range(n_slab)]).astype(BF16)
    cd_r = _block_diag(jnp.swapaxes(ssm_c_re[l], 1, 2))
    cd_i = _block_diag(jnp.swapaxes(ssm_c_im[l], 1, 2))
    cd = jnp.stack([jnp.concatenate([cd_r[s * sw:(s + 1) * sw, s * LANES:(s + 1) * LANES],
                                     -cd_i[s * sw:(s + 1) * sw, s * LANES:(s + 1) * LANES]], axis=0)
                    for s in range(n_slab)]).astype(BF16)
    ssm_p = (abr.reshape(1, nstate), abi.reshape(1, nstate), bd, cd, ssm_d[l].reshape(1, ssm_w),
             w_glu[l].astype(BF16), b_glu[l].reshape(1, ssm_w))

    back_w = (w_att_br[l].astype(BF16), w_ssm_br[l].astype(BF16), w_o[l].astype(BF16),
              ffn_norm[l].reshape(1, d), w_ffn_gate[l].astype(BF16), w_ffn_up[l].astype(BF16),
              w_ffn_down[l].astype(BF16), ple_norm[l].reshape(1, d), w_ple_gate[l].astype(BF16),
              w_ple[l].astype(BF16), final_norm.reshape(1, d))
    d_ff = w_ffn_gate.shape[2]
    ff_chunk = d_ff // 2 if (d_ff // 2) % LANES == 0 else d_ff
    return dict(front=front_w, cmp=cw, ssm=ssm_p, back=back_w, ff_chunk=ff_chunk, rel_bias=rel_bias,
                n_groups=n_groups, ssm_w=ssm_w, nstate=nstate, n_gate=n_gate)


def _prompt_group(x_prompt, p_l, prm):
    nb, t, d = x_prompt.shape
    ssm_w, nstate = prm["ssm_w"], prm["nstate"]
    assert t % (CMP_STRIDE * LANES) == 0 and t >= WINDOW and t // SEL_BLOCK <= LANES // 2
    xp = x_prompt.reshape(nb * t, d)
    (qt, kc5, vc5, ks5, vs5, kc, vc, kw, vw, ksb, kwb, vst, vwt, ngt, su, mg) = _front_prompt(
        xp, nb, t, prm["front"], 512)
    kcc, vcct = _compress_prompt(kc.reshape(nb, t, KVW), vc.reshape(nb, t, KVW), prm["cmp"])
    o = _nsa_prompt(qt, ngt, kcc, vcct, ksb.reshape(nb, t, KVW), vst, kwb.reshape(nb, t, KVW), vwt,
                    _prompt_tables(prm["rel_bias"], t))
    zeros_state = jnp.zeros((nb, nstate), F32)
    so, sr, si = _ssm(su.reshape(t * nb, ssm_w), zeros_state, zeros_state, *prm["ssm"], nb, 256)
    y = _back(xp, o.reshape(nb * t, AW), so.reshape(t, nb * ssm_w), mg, p_l.reshape(nb * t, -1),
              prm["back"], nb, t, 256, prm["ff_chunk"])
    return dict(y=y, o=o, so=so, rows5=(kc5, vc5, ks5, vs5), win=(kw, vw), state=(sr, si))


def _sample_group(x_sample, p_l, page_table, pools, wins, states, prm):
    ns, t_new, d = x_sample.shape
    k_cmp, v_cmp, k_sel, v_sel = pools
    k_win, v_win = wins
    n_phys, page = k_cmp.shape[:2]
    past = page_table.shape[1] * page
    win_buf = k_win.shape[1]
    ssm_w, nstate, n_gate = prm["ssm_w"], prm["nstate"], prm["n_gate"]
    assert page == QB and past % (CMP_STRIDE * LANES) == 0 and past // SEL_BLOCK <= LANES
    assert win_buf == WINDOW and past >= win_buf and t_new <= 8 and t_new < CMP_STRIDE
    n_s = ns * t_new
    xs = x_sample.reshape(n_s, d)
    q_s, kc_s, vc_s, ks_s, vs_s, kw_s, vw_s, ng_s, su_s, mg_s = _front_sample(xs, prm["front"])
    kcc_s, vcc_s = _compress_sample(page_table, k_cmp.reshape(n_phys, page, KVW), v_cmp.reshape(n_phys, page, KVW),
                                    prm["cmp"])
    bc, bs, bw, ovl_s, e_s, n_sel_s = _sample_tables(prm["rel_bias"], past, t_new, win_buf)
    rows_s = N_HEADS * t_new
    eye_kv = jnp.eye(N_KV, dtype=BF16)
    q_rows = q_s.reshape(ns, t_new, N_KV, GROUP, HD).transpose(0, 2, 3, 1, 4)
    q_rows = jnp.einsum('skgtd,kj->skgtjd', q_rows, eye_kv).reshape(ns, rows_s, KVW)
    gate_s = ng_s[:, :n_gate].reshape(ns, t_new, N_KV, GROUP, N_BRANCH).transpose(0, 2, 3, 1, 4)
    gate_s = jnp.pad(gate_s.reshape(ns, rows_s, N_BRANCH), ((0, 0), (0, 0), (0, LANES - N_BRANCH)))
    pad8 = lambda a: jnp.pad(a.reshape(ns, t_new, KVW), ((0, 0), (0, 8 - t_new), (0, 0)))
    o_s = _nsa_sample(page_table, q_rows, gate_s, kcc_s, vcc_s,
                      k_sel, v_sel, pad8(ks_s), pad8(vs_s),
                      k_win.reshape(ns, win_buf, KVW), v_win.reshape(ns, win_buf, KVW),
                      pad8(kw_s), pad8(vw_s), bc, bs, bw, ovl_s, e_s, n_sel_s, past, t_new)
    o_s = o_s.reshape(ns, N_KV, GROUP, t_new, N_KV, HD)
    o_s = jnp.stack([o_s[:, k, :, :, k, :] for k in range(N_KV)], axis=1)
    o_s = o_s.transpose(0, 3, 1, 2, 4).reshape(n_s, AW).astype(BF16)
    su_ts = su_s.reshape(ns, t_new, ssm_w).transpose(1, 0, 2).reshape(n_s, ssm_w)
    so_ts, sr, si = _ssm(su_ts, states[0].reshape(ns, nstate), states[1].reshape(ns, nstate), *prm["ssm"], ns, t_new)
    so_s = so_ts.reshape(t_new, ns, ssm_w).transpose(1, 0, 2).reshape(n_s, ssm_w)
    y = _back(xs, o_s, so_s, mg_s, p_l.reshape(n_s, -1), prm["back"], 1, n_s, n_s, prm["ff_chunk"])
    return dict(y=y, o=o_s, so=so_s, rows=(kc_s, vc_s, ks_s, vs_s, kw_s, vw_s), state=(sr, si))


def kernel(x_prompt, x_sample, cache_k_cmp, cache_v_cmp, cache_k_sel, cache_v_sel, cache_k_win, cache_v_win, state_ssm_re, state_ssm_im, page_table, p_prompt, p_sample, rel_bias, final_norm, attn_norm, w_in, cmp_pe_k, cmp_w1_k, cmp_w2_k, cmp_pe_v, cmp_w1_v, cmp_w2_v, ssm_a_re, ssm_a_im, ssm_log_dt, ssm_b_re, ssm_b_im, ssm_c_re, ssm_c_im, ssm_d, w_glu, b_glu, w_att_br, w_ssm_br, w_o, ffn_norm, w_ffn_gate, w_ffn_up, w_ffn_down, ple_norm, w_ple_gate, w_ple):
    assert w_in.shape[0] == 1, "single-layer trunk"
    l = 0
    nb, t, d = x_prompt.shape
    ns, t_new = x_sample.shape[:2]
    prm = _layer_params(rel_bias, final_norm, attn_norm, w_in, cmp_pe_k, cmp_w1_k, cmp_w2_k, cmp_pe_v, cmp_w1_v,
                        cmp_w2_v, ssm_a_re, ssm_a_im, ssm_log_dt, ssm_b_re, ssm_b_im, ssm_c_re, ssm_c_im, ssm_d,
                        w_glu, b_glu, w_att_br, w_ssm_br, w_o, ffn_norm, w_ffn_gate, w_ffn_up, w_ffn_down,
                        ple_norm, w_ple_gate, w_ple)
    pg = _prompt_group(x_prompt, p_prompt[l], prm)
    sg = _sample_group(x_sample, p_sample[l], page_table,
                       (cache_k_cmp[l], cache_v_cmp[l], cache_k_sel[l], cache_v_sel[l]),
                       (cache_k_win[l], cache_v_win[l]), (state_ssm_re[l], state_ssm_im[l]), prm)

    kv5 = lambda a, b_, t_: a.reshape(1, b_, t_, N_KV, HD)
    keep = min(WINDOW, t)
    win_p = lambda a: a.reshape(nb, t, KVW)[:, t - keep:].reshape(1, nb, keep, N_KV, HD)
    win_s = lambda cache, new: jnp.concatenate(
        [cache[l], new.reshape(ns, t_new, N_KV, HD)], axis=1)[:, t_new:][None]
    st = lambda a, b_: a.reshape(1, b_, prm["n_groups"], SSM_P)
    kc_s, vc_s, ks_s, vs_s, kw_s, vw_s = sg["rows"]
    return (pg["y"].reshape(nb, t, d), sg["y"].reshape(ns, t_new, d),
            *pg["rows5"], win_p(pg["win"][0]), win_p(pg["win"][1]),
            st(pg["state"][0], nb), st(pg["state"][1], nb),
            kv5(kc_s, ns, t_new), kv5(vc_s, ns, t_new), kv5(ks_s, ns, t_new), kv5(vs_s, ns, t_new),
            win_s(cache_k_win, kw_s), win_s(cache_v_win, vw_s),
            st(sg["state"][0], ns), st(sg["state"][1], ns))
```

```python
import functools
import math

import numpy as np
import jax
import jax.numpy as jnp
from jax import lax
from jax.experimental import pallas as pl
from jax.experimental.pallas import tpu as pltpu

F32 = jnp.float32
BF16 = jnp.bfloat16

N_HEADS = 8
N_KV = 2
HD = 64
GROUP = N_HEADS // N_KV
N_BRANCH = 3
CMP_BLOCK = 32
CMP_STRIDE = 16
CMP_HIDDEN = 256
SEL_BLOCK = 64
SEL_TOPK = 16
WINDOW = 512
QB = 128
SEL_KEYS = 2 * QB
SEL_SPLIT = 2
N_BUCKETS = 32
REL_MAX_DIST = 128
SSM_CH = 16
SSM_P = 64
EPS = 1e-6
NEG = -1e30
NEG_TEST = -1e29
FORCE = 1e9
BIG = 1e30
LOG2E = math.log2(math.e)
LANES = 128
VMEM_LIMIT = 56 * 1024 * 1024
AW = N_HEADS * HD
KVW = N_KV * HD
NG_ROWS = 32


def _cparams(sem):
    return pltpu.CompilerParams(dimension_semantics=sem, vmem_limit_bytes=VMEM_LIMIT)


def _const_spec(shape):
    nd = len(shape)
    return pl.BlockSpec(shape, lambda *_: (0,) * nd)


def _rms(x, g):
    return x * lax.rsqrt(jnp.mean(x * x, axis=-1, keepdims=True) + EPS) * g


def _gelu(x):
    return x * (0.5 * (1.0 + jnp.tanh(math.sqrt(2.0 / math.pi) * (x + 0.044715 * (x * x * x)))))


def _sigmoid(x):
    return 1.0 / (1.0 + jnp.exp(-x))


def _dot(a, b):
    return jnp.dot(a, b, preferred_element_type=F32)


def _dot_t(a, b):
    return lax.dot_general(a, b, (((1,), (1,)), ((), ())), preferred_element_type=F32)


def _masked_softmax(s, axis=-1):
    valid = s > NEG_TEST
    m = jnp.max(s, axis=axis, keepdims=True)
    e = jnp.where(valid, jnp.exp(s - m), 0.0)
    return e / jnp.maximum(jnp.sum(e, axis=axis, keepdims=True), 1e-30)


def _softmax2_cols(s):
    m = jnp.max(s, axis=0, keepdims=True)
    e = jnp.exp2(s - m)
    inv = jnp.where(m > NEG_TEST, 1.0 / jnp.maximum(jnp.sum(e, axis=0, keepdims=True), 1e-30), 0.0)
    return e * inv


def _front_project(x_ref, g_ref, wa_ref, wng_ref, wsu_ref, wmg_ref, su_ref, mg_ref, q_scale):
    u = _rms(x_ref[...], g_ref[...]).astype(BF16)
    za = _dot(u, wa_ref[...])
    q = za[:, :AW] * q_scale
    rows = [za[:, AW + i * KVW: AW + (i + 1) * KVW] for i in range(6)]
    ng = _sigmoid(_dot(u, wng_ref[...]))
    su_ref[...] = _dot(u, wsu_ref[...]).astype(su_ref.dtype)
    mg_ref[...] = _sigmoid(_dot(u, wmg_ref[...])).astype(BF16)
    return q, rows, ng


def _front_sample_kernel(x_ref, g_ref, wa_ref, wng_ref, wsu_ref, wmg_ref,
                         q_ref, kc_ref, vc_ref, ks_ref, vs_ref, kw_ref, vw_ref, ng_ref, su_ref, mg_ref):
    q, rows, ng = _front_project(x_ref, g_ref, wa_ref, wng_ref, wsu_ref, wmg_ref, su_ref, mg_ref, HD ** -0.5)
    q_ref[...] = q.astype(BF16)
    for ref, r in zip((kc_ref, vc_ref, ks_ref, vs_ref, kw_ref, vw_ref), rows):
        ref[...] = r
    ng_ref[...] = ng


def _front_prompt_kernel(x_ref, g_ref, wa_ref, wng_ref, wsu_ref, wmg_ref,
                         qt_ref, kct_ref, vct_ref, kst_ref, vst32_ref, kc_ref, vc_ref, kw_ref, vw_ref,
                         ksb_ref, kwb_ref, vst_ref, vwt_ref, ngt_ref, su_ref, mg_ref):
    q, rows, ng = _front_project(x_ref, g_ref, wa_ref, wng_ref, wsu_ref, wmg_ref, su_ref, mg_ref,
                                 HD ** -0.5 * LOG2E)
    kc, vc, ks, vs, kw, vw = rows
    qt_ref[0] = q.T.astype(BF16)
    for ref, r in zip((kct_ref, vct_ref, kst_ref, vst32_ref), (kc, vc, ks, vs)):
        ref[0] = r.T
    kc_ref[...] = kc
    vc_ref[...] = vc
    kw_ref[...] = kw
    vw_ref[...] = vw
    ksb_ref[...] = ks.astype(BF16)
    kwb_ref[...] = kw.astype(BF16)
    for ref, r in ((vst_ref, vs), (vwt_ref, vw)):
        rt = r.T.astype(BF16)
        for j in range(rt.shape[1] // QB):
            ref[0, j] = rt[:, j * QB:(j + 1) * QB]
    ngt_ref[0] = ng.T[0:NG_ROWS]


def _front_sample(x2d, fw):
    g, wa, wng, wsu, wmg = fw
    n, d = x2d.shape
    sw, mw = wsu.shape[1], wmg.shape[1]
    shapes = ([jax.ShapeDtypeStruct((n, AW), BF16)] + [jax.ShapeDtypeStruct((n, KVW), F32)] * 6
              + [jax.ShapeDtypeStruct((n, LANES), F32), jax.ShapeDtypeStruct((n, sw), BF16),
                 jax.ShapeDtypeStruct((n, mw), BF16)])
    return pl.pallas_call(
        _front_sample_kernel,
        grid=(1,),
        in_specs=[_const_spec(a.shape) for a in (x2d, g, wa, wng, wsu, wmg)],
        out_specs=[_const_spec(s.shape) for s in shapes],
        out_shape=shapes,
        compiler_params=_cparams(("arbitrary",)),
        name="front_sample",
    )(x2d, g, wa, wng, wsu, wmg)


def _front_prompt(x2d, nb, t, fw, tm):
    g, wa, wng, wsu, wmg = fw
    n, d = x2d.shape
    nt = t // tm
    sw, mw = wsu.shape[1], wmg.shape[1]
    row = lambda b, i: (b * nt + i, 0)
    kv5 = jax.ShapeDtypeStruct((nb, KVW, t), F32)
    kv5_spec = pl.BlockSpec((1, KVW, tm), lambda b, i: (b, 0, i))
    vt = jax.ShapeDtypeStruct((nb, t // QB, KVW, QB), BF16)
    vt_spec = pl.BlockSpec((1, tm // QB, KVW, QB), lambda b, i: (b, i, 0, 0))
    shapes = ([jax.ShapeDtypeStruct((nb, AW, t), BF16)] + [kv5] * 4 + [jax.ShapeDtypeStruct((n, KVW), F32)] * 4
              + [jax.ShapeDtypeStruct((n, KVW), BF16)] * 2 + [vt] * 2
              + [jax.ShapeDtypeStruct((nb, NG_ROWS, t), F32), jax.ShapeDtypeStruct((t, nb * sw), BF16),
                 jax.ShapeDtypeStruct((n, mw), BF16)])
    specs = ([pl.BlockSpec((1, AW, tm), lambda b, i: (b, 0, i))] + [kv5_spec] * 4
             + [pl.BlockSpec((tm, KVW), row)] * 6 + [vt_spec] * 2
             + [pl.BlockSpec((1, NG_ROWS, tm), lambda b, i: (b, 0, i)),
                pl.BlockSpec((tm, sw), lambda b, i: (i, b)), pl.BlockSpec((tm, mw), row)])
    return pl.pallas_call(
        _front_prompt_kernel,
        grid=(nb, nt),
        in_specs=[pl.BlockSpec((tm, d), row)] + [_const_spec(a.shape) for a in (g, wa, wng, wsu, wmg)],
        out_specs=specs,
        out_shape=shapes,
        compiler_params=_cparams(("parallel", "parallel")),
        name="front_prompt",
    )(x2d, g, wa, wng, wsu, wmg)


def _chunk_rows(load, r0, rn):
    return jnp.concatenate([load(pl.ds(CMP_STRIDE * r0 + r, rn, stride=CMP_STRIDE)) for r in range(CMP_STRIDE)],
                           axis=1)


def _compress_compute(load_rows, c, w1_ref, w2_ref, pe_ref, a_scr):
    rc_n = min(c, 256)
    lo = lax.broadcasted_iota(jnp.int32, (rc_n, LANES), 1) < HD
    w1 = w1_ref[...]
    for rc in range(c // rc_n):
        x = load_rows(rc * rc_n, rc_n)
        cols = [x[:, r * LANES:(r + 1) * LANES] for r in range(CMP_STRIDE)]
        rol = [pltpu.roll(col, HD, 1) for col in cols]
        for kh in range(N_KV):
            if kh == 0:
                parts = [jnp.where(lo, cols[2 * j], rol[2 * j + 1]) for j in range(CMP_STRIDE // 2)]
            else:
                parts = [jnp.where(lo, rol[2 * j], cols[2 * j + 1]) for j in range(CMP_STRIDE // 2)]
            xh = jnp.concatenate(parts, axis=1).astype(BF16)
            a_scr[kh, rc * rc_n:(rc + 1) * rc_n, :] = _dot(xh, w1)
    pw = _dot(pe_ref[...], w1)
    peb = pw[0:1, :CMP_HIDDEN] + pw[1:2, CMP_HIDDEN:]
    w2 = w2_ref[...]
    outs = []
    for kh in range(N_KV):
        a = a_scr[kh]
        hid = a[:, :CMP_HIDDEN] + pltpu.roll(a[:, CMP_HIDDEN:], c - 1, 0) + peb
        outs.append(_dot(_gelu(hid).astype(BF16), w2))
    return jnp.concatenate(outs, axis=1)


def _compress_prompt_kernel(xk_ref, xv_ref, w1k_ref, w2k_ref, pek_ref, w1v_ref, w2v_ref, pev_ref,
                            ok_ref, ovt_ref, a_scr):
    c = xk_ref.shape[1] // CMP_STRIDE
    ok_ref[0] = _compress_compute(lambda r0, rn: _chunk_rows(lambda idx: xk_ref[0, idx, :], r0, rn), c,
                                  w1k_ref, w2k_ref, pek_ref, a_scr).astype(BF16)
    ovt_ref[0] = _compress_compute(lambda r0, rn: _chunk_rows(lambda idx: xv_ref[0, idx, :], r0, rn), c,
                                   w1v_ref, w2v_ref, pev_ref, a_scr).T.astype(BF16)


def _compress_prompt(xk, xv, cw):
    nb, t, kvw = xk.shape
    c = t // CMP_STRIDE
    wspecs = [_const_spec(w.shape) for w in cw]
    blk = pl.BlockSpec((1, t, kvw), lambda b: (b, 0, 0))
    return pl.pallas_call(
        _compress_prompt_kernel,
        grid=(nb,),
        in_specs=[blk, blk] + wspecs,
        out_specs=[pl.BlockSpec((1, c, KVW), lambda b: (b, 0, 0)), pl.BlockSpec((1, KVW, c), lambda b: (b, 0, 0))],
        out_shape=[jax.ShapeDtypeStruct((nb, c, KVW), BF16), jax.ShapeDtypeStruct((nb, KVW, c), BF16)],
        scratch_shapes=[pltpu.VMEM((N_KV, c, 2 * CMP_HIDDEN), F32)],
        compiler_params=_cparams(("parallel",)),
        name="compress_prompt",
    )(xk, xv, *cw)


def _page_copy(pool, buf, sem, page, p, slot):
    return pltpu.make_async_copy(pool.at[page], buf.at[slot, p], sem)


def _page_gather_start(pt_ref, seq, pools, bufs, sems, slot, n_pages):
    def body(p, carry):
        page = pt_ref[seq, p]
        for i, (pool, buf) in enumerate(zip(pools, bufs)):
            _page_copy(pool, buf, sems.at[i, slot], page, p, slot).start()
        return carry
    lax.fori_loop(0, n_pages, body, 0)


def _page_gather_wait(pools, bufs, sems, slot, n_pages):
    def body(p, carry):
        for i, (pool, buf) in enumerate(zip(pools, bufs)):
            _page_copy(pool, buf, sems.at[i, slot], 0, p, slot).wait()
        return carry
    lax.fori_loop(0, n_pages, body, 0)


def _paged_prefetch(pt_ref, pools, bufs, sems, n_pages):
    s = pl.program_id(0)
    slot = s % 2

    @pl.when(s == 0)
    def _():
        _page_gather_start(pt_ref, 0, pools, bufs, sems, 0, n_pages)

    @pl.when(s + 1 < pl.num_programs(0))
    def _():
        _page_gather_start(pt_ref, s + 1, pools, bufs, sems, 1 - slot, n_pages)

    _page_gather_wait(pools, bufs, sems, slot, n_pages)
    return slot


def _compress_sample_kernel(pt_ref, kpool, vpool, w1k_ref, w2k_ref, pek_ref, w1v_ref, w2v_ref, pev_ref,
                            ok_ref, ov_ref, kbuf, vbuf, sems, a_scr, rows_scr):
    n_pages = pt_ref.shape[1]
    page = kpool.shape[2]
    c = n_pages * page // CMP_STRIDE
    slot = _paged_prefetch(pt_ref, (kpool, vpool), (kbuf, vbuf), sems, n_pages)
    for buf, out_ref, w1_ref, w2_ref, pe_ref in ((kbuf, ok_ref, w1k_ref, w2k_ref, pek_ref),
                                                (vbuf, ov_ref, w1v_ref, w2v_ref, pev_ref)):
        for p in range(n_pages):
            rows_scr[p * page:(p + 1) * page, :] = buf[slot, p].T
        out_ref[0] = _compress_compute(lambda r0, rn: _chunk_rows(lambda idx: rows_scr[idx, :], r0, rn), c,
                                       w1_ref, w2_ref, pe_ref, a_scr).astype(BF16)


def _compress_sample(page_table, kpool, vpool, cw):
    ns, n_pages = page_table.shape
    width, page = kpool.shape[1:]
    tokens = n_pages * page
    c = tokens // CMP_STRIDE
    buf_shape = (2, n_pages, width, page)
    any_spec = pl.BlockSpec(memory_space=pl.ANY)
    wspecs = [pl.BlockSpec(w.shape, lambda s, pt, nd=w.ndim: (0,) * nd) for w in cw]
    oblk = pl.BlockSpec((1, c, KVW), lambda s, pt: (s, 0, 0))
    return pl.pallas_call(
        _compress_sample_kernel,
        grid_spec=pltpu.PrefetchScalarGridSpec(
            num_scalar_prefetch=1,
            grid=(ns,),
            in_specs=[any_spec, any_spec] + wspecs,
            out_specs=[oblk, oblk],
            scratch_shapes=[pltpu.VMEM(buf_shape, F32), pltpu.VMEM(buf_shape, F32), pltpu.SemaphoreType.DMA((2, 2)),
                            pltpu.VMEM((N_KV, c, 2 * CMP_HIDDEN), F32), pltpu.VMEM((tokens, width), F32)]),
        out_shape=[jax.ShapeDtypeStruct((ns, c, KVW), BF16)] * 2,
        compiler_params=_cparams(("arbitrary",)),
        name="compress_sample",
    )(page_table, kpool, vpool, *cw)


def _rank_select(score, blk, n_real, axis):
    size = 8 if axis == 0 else LANES
    total = score.shape[axis]
    chunk = (lambda a, c: a[c * size:(c + 1) * size]) if axis == 0 else (lambda a, c: a[:, c * size:(c + 1) * size])
    n_chunks = -(-total // size)
    sc = [chunk(score, c) for c in range(n_chunks)]
    bl = [chunk(blk, c) for c in range(n_chunks)]
    rank = [jnp.zeros(s.shape, F32) for s in sc]
    for kk in range(n_real):
        col = score[kk:kk + 1, :] if axis == 0 else score[:, kk:kk + 1]
        for c in range(n_chunks):
            other = jnp.broadcast_to(col, sc[c].shape)
            if c * size > kk:
                beats = other >= sc[c]
            elif min((c + 1) * size, total) - 1 < kk:
                beats = other > sc[c]
            else:
                beats = (other > sc[c]) | ((other == sc[c]) & (bl[c] > kk))
            rank[c] = rank[c] + jnp.where(beats, 1.0, 0.0)
    return jnp.where(jnp.concatenate(rank, axis=axis) < SEL_TOPK, 1.0, 0.0)


def _block_scores(imp, blk, t):
    cur = t // SEL_BLOCK
    forced = (blk == 0) | (blk == cur) | (blk == cur - 1)
    valid = blk * SEL_BLOCK <= t
    return jnp.where(valid, jnp.where(forced, FORCE, imp), NEG)


def _nsa_prompt_kernel(qt_ref, ngt_ref, kc_ref, vct_ref, ks_ref, vst_ref, kw_ref, vwt_ref,
                       ut_ref, at_ref, ovlt_ref, stat_ref, crow_ref, o_ref, *, n_sel):
    ib = pl.program_id(1)
    qt = qt_ref[0]
    ngt = ngt_ref[0]
    ncp = kc_ref.shape[1]
    cols = GROUP * QB
    sel_rows = LANES // 2
    t_row = ib * QB + lax.broadcasted_iota(jnp.int32, (n_sel, QB), 1)
    blk_t = lax.broadcasted_iota(jnp.int32, (n_sel, QB), 0)
    zeros_q = jnp.zeros((HD, cols), F32)
    vrows = [slice(k * HD, (k + 1) * HD) for k in range(N_KV)]
    q_sel, q_win, o_c = [], [], []
    for k in range(N_KV):
        qk = jnp.concatenate([qt[(GROUP * k + g) * HD:(GROUP * k + g + 1) * HD, :] for g in range(GROUP)],
                             axis=1).astype(F32)
        qa = jnp.concatenate([qk, zeros_q] if k == 0 else [zeros_q, qk], axis=0)

        bias_c = ut_ref[k, pl.ds(pl.multiple_of(ncp - (QB // CMP_STRIDE) * ib, 8), ncp), :]
        p_c = _softmax2_cols(_dot(kc_ref[0], qa.astype(BF16)) + bias_c)
        o_c.append(_dot(vct_ref[0][vrows[k], :], p_c.astype(BF16)))
        psum = p_c[:, 0:QB]
        for g in range(1, GROUP):
            psum = psum + p_c[:, g * QB:(g + 1) * QB]
        imp = jnp.dot(ovlt_ref[...], psum, preferred_element_type=F32, precision=lax.Precision.HIGHEST)

        sel = _rank_select(_block_scores(imp[0:n_sel], blk_t, t_row), blk_t, n_sel, 0)
        selm1 = jnp.concatenate([sel - 1.0] * GROUP, axis=1)
        if n_sel < sel_rows:
            selm1 = jnp.concatenate([selm1, jnp.zeros((sel_rows - n_sel, cols), F32)], axis=0)
        tail = jnp.concatenate([crow_ref[k], jnp.zeros((LANES - sel_rows - 8, cols), F32)], axis=0)
        q_sel.append(jnp.concatenate([qa, selm1, tail], axis=0).astype(BF16))
        q_win.append(jnp.concatenate([qa, jnp.zeros((sel_rows, cols), F32), tail], axis=0).astype(BF16))

    tiles_per_step = SEL_KEYS // QB

    ones_rows = jnp.where(lax.broadcasted_iota(jnp.int32, (16, SEL_KEYS), 0) == 0, 1.0, 0.0).astype(BF16)
    step_tiles = SEL_SPLIT * tiles_per_step

    def make_sel_body(near):
        def sel_body(jp, carry):
            scores, vts = [], []
            for sp in range(SEL_SPLIT):
                j0 = (jp * SEL_SPLIT + sp) * tiles_per_step
                k0 = pl.multiple_of(j0 * QB, SEL_KEYS)
                lhs = jnp.concatenate([ks_ref[0, pl.ds(k0, SEL_KEYS), :], stat_ref[pl.ds(k0, SEL_KEYS), :]], axis=1)
                tiles = [j0 + h for h in range(tiles_per_step)]
                tidx = [jnp.where(jt == ib, 0, jnp.where(jt == ib - 1, 1, jnp.where(jt < ib, 2, 4))) for jt in tiles]
                for k in range(N_KV):
                    s = _dot(lhs, q_sel[k])
                    if near:
                        s = s + jnp.concatenate([at_ref[ti, k] for ti in tidx], axis=0)
                    scores.append(s)
                    vts.append(jnp.concatenate([vst_ref[0, jt, vrows[k], :] for jt in tiles], axis=1))
            stats = []
            for (m, _), s in zip(carry, scores):
                m_new = jnp.maximum(m, jnp.max(s, axis=0, keepdims=True))
                stats.append((m_new, jnp.exp2(m - m_new), jnp.exp2(s - m_new).astype(BF16)))
            return tuple((m_new, alpha * acc + _dot(jnp.concatenate([vt, ones_rows], axis=0), p))
                         for (_, acc), (m_new, alpha, p), vt in zip(carry, stats, vts))
        return sel_body

    init = (jnp.full((1, cols), NEG, F32), jnp.zeros((HD + ones_rows.shape[0], cols), F32))
    n_far = jnp.maximum(ib - 1, 0) // step_tiles
    sel_state = lax.fori_loop(0, n_far, make_sel_body(False), (init,) * (N_KV * SEL_SPLIT))
    sel_state = lax.fori_loop(n_far, (ib + step_tiles) // step_tiles, make_sel_body(True), sel_state)

    out_rows = []
    for k in range(N_KV):
        parts = [sel_state[sp * N_KV + k] for sp in range(SEL_SPLIT)]
        m_s = parts[0][0]
        for m_p, _ in parts[1:]:
            m_s = jnp.maximum(m_s, m_p)
        acc_s = jnp.zeros(init[1].shape, F32)
        for m_p, acc_p in parts:
            acc_s = acc_s + jnp.exp2(m_p - m_s) * acc_p
        o_s = acc_s[0:HD] / jnp.maximum(acc_s[HD:HD + 1], 1e-30)

        s_parts, tiles_j = [], []
        for w, tidx in enumerate((0, 1, None, None, 3)):
            jt = ib - w
            jc = jnp.maximum(jt, 0)
            k0 = pl.multiple_of(jc * QB, QB)
            lhs = jnp.concatenate([kw_ref[0, pl.ds(k0, QB), :], stat_ref[pl.ds(k0, QB), :]], axis=1)
            s = _dot(lhs, q_win[k])
            if tidx is not None:
                s = s + at_ref[tidx, k]
            s_parts.append(jnp.where(jt >= 0, s, NEG))
            tiles_j.append(jc)
        p_w = _softmax2_cols(jnp.concatenate(s_parts, axis=0)).astype(BF16)
        o_w = jnp.zeros((HD, cols), F32)
        for w, jc in enumerate(tiles_j):
            o_w = o_w + _dot(vwt_ref[0, jc, vrows[k], :], p_w[w * QB:(w + 1) * QB])

        def gate_row(br):
            return jnp.concatenate([ngt[(GROUP * k + g) * N_BRANCH + br:(GROUP * k + g) * N_BRANCH + br + 1, :]
                                    for g in range(GROUP)], axis=1)
        o_k = gate_row(0) * o_c[k] + gate_row(1) * o_s + gate_row(2) * o_w
        out_rows += [o_k[:, g * QB:(g + 1) * QB] for g in range(GROUP)]
    o_ref[0] = jnp.concatenate(out_rows, axis=0).T.astype(BF16)


def _nsa_prompt(qt, ngt, kc, vct, ks, vst, kw, vwt, tables):
    nb, aw, t = qt.shape
    nq = t // QB
    ncp = kc.shape[1]
    full3 = lambda b, i: (b, 0, 0)
    full4 = lambda b, i: (b, 0, 0, 0)
    return pl.pallas_call(
        functools.partial(_nsa_prompt_kernel, n_sel=t // SEL_BLOCK),
        grid=(nb, nq),
        in_specs=[pl.BlockSpec((1, aw, QB), lambda b, i: (b, 0, i)),
                  pl.BlockSpec((1, NG_ROWS, QB), lambda b, i: (b, 0, i)),
                  pl.BlockSpec((1, ncp, KVW), full3), pl.BlockSpec((1, KVW, ncp), full3),
                  pl.BlockSpec((1, t, KVW), full3), pl.BlockSpec((1, nq, KVW, QB), full4),
                  pl.BlockSpec((1, t, KVW), full3), pl.BlockSpec((1, nq, KVW, QB), full4)]
                 + [_const_spec(a.shape) for a in tables],
        out_specs=pl.BlockSpec((1, QB, aw), lambda b, i: (b, i, 0)),
        out_shape=jax.ShapeDtypeStruct((nb, t, aw), BF16),
        compiler_params=_cparams(("parallel", "arbitrary")),
        name="nsa_prompt",
    )(qt, ngt, kc, vct, ks, vst, kw, vwt, *tables)


def _nsa_sample_kernel(pt_ref, q_ref, gate_ref, kc_ref, vc_ref, kpool, vpool, ksn_ref, vsn_ref,
                       kwin_ref, vwin_ref, kwn_ref, vwn_ref, bc_ref, bs_ref, bw_ref, ovl_ref, e_ref,
                       o_ref, kbuf, vbuf, sems, *, n_sel, past, t_new):
    n_pages = pt_ref.shape[1]
    wb = kwin_ref.shape[2]
    slot = _paged_prefetch(pt_ref, (kpool, vpool), (kbuf, vbuf), sems, n_pages)
    past_t = lambda buf: jnp.concatenate([buf[slot, p] for p in range(n_pages)], axis=1).astype(BF16)
    q = q_ref[0]
    nb_past = past // SEL_BLOCK
    pad_new = jnp.zeros((QB - ksn_ref.shape[1], KVW), F32)
    new_tile = lambda ref: jnp.concatenate([ref[0], pad_new], axis=0).astype(BF16)

    p_c = _masked_softmax(_dot_t(q, kc_ref[0]) + bc_ref[...])
    o_c = _dot(p_c.astype(BF16), vc_ref[0])
    parts = []
    for k in range(N_KV):
        base = k * GROUP * t_new
        ps = p_c[base:base + t_new]
        for g in range(1, GROUP):
            ps = ps + p_c[base + g * t_new:base + (g + 1) * t_new]
        parts.append(ps)
    psum = jnp.concatenate(parts, axis=0)
    imp = jnp.dot(psum, ovl_ref[...], preferred_element_type=F32, precision=lax.Precision.HIGHEST)
    blk = lax.broadcasted_iota(jnp.int32, imp.shape, 1)
    tpos = past + lax.broadcasted_iota(jnp.int32, imp.shape, 0) % t_new
    sel = _rank_select(_block_scores(imp, blk, tpos), blk, n_sel, 1)
    sel = jnp.concatenate([sel[k * t_new:(k + 1) * t_new] for k in range(N_KV) for _ in range(GROUP)], axis=0)

    mask_add = _dot((sel[:, 0:LANES] - 1.0).astype(BF16), e_ref[...])
    s_past = _dot(q, past_t(kbuf)) + bs_ref[:, 0:past] + mask_add
    s_new = _dot_t(q, new_tile(ksn_ref)) + bs_ref[:, past:]
    s_new = jnp.where(sel[:, nb_past:nb_past + 1] > 0.5, s_new, NEG)
    p_s = _masked_softmax(jnp.concatenate([s_past, s_new], axis=1)).astype(BF16)
    o_s = _dot_t(p_s[:, 0:past], past_t(vbuf)) + _dot(p_s[:, past:], new_tile(vsn_ref))

    s_w = jnp.concatenate([_dot(q, kwin_ref[0].astype(BF16)), _dot_t(q, new_tile(kwn_ref))], axis=1)
    p_w = _masked_softmax(s_w + bw_ref[...]).astype(BF16)
    o_w = _dot_t(p_w[:, 0:wb], vwin_ref[0].astype(BF16)) + _dot(p_w[:, wb:], new_tile(vwn_ref))

    gate = gate_ref[0]
    o_ref[0] = gate[:, 0:1] * o_c + gate[:, 1:2] * o_s + gate[:, 2:3] * o_w


def _nsa_sample(page_table, q, gate, kc, vc, kpool, vpool, ksn, vsn, kwin, vwin, kwn, vwn,
                bc, bs, bw, ovl, e, n_sel, past, t_new):
    ns, n_pages = page_table.shape
    rows, kvw = q.shape[1:]
    buf_shape = (2, n_pages) + kpool.shape[1:]
    seq3 = lambda s, pt: (s, 0, 0)
    any_spec = pl.BlockSpec(memory_space=pl.ANY)
    cs = lambda a: pl.BlockSpec(a.shape, lambda s, pt, nd=a.ndim: (0,) * nd, pipeline_mode=pl.Buffered(1))
    per_seq = lambda a: pl.BlockSpec((1,) + a.shape[1:], seq3)
    return pl.pallas_call(
        functools.partial(_nsa_sample_kernel, n_sel=n_sel, past=past, t_new=t_new),
        grid_spec=pltpu.PrefetchScalarGridSpec(
            num_scalar_prefetch=1,
            grid=(ns,),
            in_specs=[per_seq(q), per_seq(gate), per_seq(kc), per_seq(vc), any_spec, any_spec,
                      per_seq(ksn), per_seq(vsn), per_seq(kwin), per_seq(vwin), per_seq(kwn), per_seq(vwn),
                      cs(bc), cs(bs), cs(bw), cs(ovl), cs(e)],
            out_specs=pl.BlockSpec((1, rows, kvw), seq3),
            scratch_shapes=[pltpu.VMEM(buf_shape, F32), pltpu.VMEM(buf_shape, F32), pltpu.SemaphoreType.DMA((2, 2))]),
        out_shape=jax.ShapeDtypeStruct((ns, rows, kvw), F32),
        compiler_params=_cparams(("arbitrary",)),
        name="nsa_sample",
    )(page_table, q, gate, kc, vc, kpool, vpool, ksn, vsn, kwin, vwin, kwn, vwn, bc, bs, bw, ovl, e)


def _ssm_param_kernel(ar_ref, ai_ref, ldt_ref, br_ref, bi_ref, abr_ref, abi_ref, bbr_ref, bbi_ref):
    ar = ar_ref[...]
    ai = ai_ref[...]
    dt = jnp.exp(ldt_ref[...])
    mag = jnp.exp(ar * dt)
    abr = mag * jnp.cos(ai * dt)
    abi = mag * jnp.sin(ai * dt)
    den = ar * ar + ai * ai
    nr, ni = abr - 1.0, abi
    fr = (nr * ar + ni * ai) / den
    fi = (ni * ar - nr * ai) / den
    abr_ref[...] = abr
    abi_ref[...] = abi
    for g in range(ar.shape[0]):
        br = br_ref[g]
        bi = bi_ref[g]
        frg = fr[g:g + 1, :]
        fig = fi[g:g + 1, :]
        bbr_ref[g] = frg * br - fig * bi
        bbi_ref[g] = frg * bi + fig * br


def _ssm_params(a_re, a_im, log_dt, b_re_t, b_im_t):
    g, p = a_re.shape
    return pl.pallas_call(
        _ssm_param_kernel,
        out_shape=[jax.ShapeDtypeStruct((g, p), F32)] * 2 + [jax.ShapeDtypeStruct(b_re_t.shape, F32)] * 2,
        name="ssm_params",
    )(a_re, a_im, log_dt.reshape(g, 1), b_re_t, b_im_t)


def _ssm_kernel(u_ref, h0r_ref, h0i_ref, ar_ref, ai_ref, bd_ref, cd_ref, d_ref, wglu_ref, bglu_ref,
                so_ref, hr_ref, hi_ref, xr_scr, xi_scr, *, bt):
    i = pl.program_id(0)
    rows, width = u_ref.shape
    n_slab = bd_ref.shape[0]
    sw = bd_ref.shape[2] // 2
    u = u_ref[...]

    @pl.when(i == 0)
    def _():
        hr_ref[...] = h0r_ref[...]
        hi_ref[...] = h0i_ref[...]

    for sl in range(n_slab):
        x = _dot(u[:, sl * LANES:(sl + 1) * LANES], bd_ref[sl])
        xr_scr[:, sl * sw:(sl + 1) * sw] = x[:, :sw]
        xi_scr[:, sl * sw:(sl + 1) * sw] = x[:, sw:]

    per = 8 // math.gcd(bt, 8)
    grp = per * bt
    lc = 512
    for c0 in range(0, xr_scr.shape[1], lc):
        cl = slice(c0, c0 + lc)
        a_r = jnp.broadcast_to(ar_ref[:, cl], (bt, lc))
        a_i = jnp.broadcast_to(ai_ref[:, cl], (bt, lc))

        def step(j, carry):
            h_r, h_i = carry
            r0 = pl.multiple_of(j * grp, grp)
            xr = xr_scr[pl.ds(r0, grp), cl]
            xi = xi_scr[pl.ds(r0, grp), cl]
            out_r, out_i = [], []
            for s in range(per):
                n_r = a_r * h_r - a_i * h_i + xr[s * bt:(s + 1) * bt]
                n_i = a_r * h_i + a_i * h_r + xi[s * bt:(s + 1) * bt]
                h_r, h_i = n_r, n_i
                out_r.append(h_r)
                out_i.append(h_i)
            xr_scr[pl.ds(r0, grp), cl] = jnp.concatenate(out_r, axis=0) if per > 1 else out_r[0]
            xi_scr[pl.ds(r0, grp), cl] = jnp.concatenate(out_i, axis=0) if per > 1 else out_i[0]
            return h_r, h_i

        h_r, h_i = lax.fori_loop(0, rows // grp, step, (hr_ref[:, cl], hi_ref[:, cl]))
        hr_ref[:, cl] = h_r
        hi_ref[:, cl] = h_i

    ys = []
    for sl in range(n_slab):
        hcat = jnp.concatenate([xr_scr[:, sl * sw:(sl + 1) * sw], xi_scr[:, sl * sw:(sl + 1) * sw]], axis=1)
        ys.append(_dot(hcat.astype(BF16), cd_ref[sl]))
    y = jnp.concatenate(ys, axis=1) + d_ref[...] * u.astype(F32)
    z = _gelu(y)
    so_ref[...] = (z * _sigmoid(_dot(z.astype(BF16), wglu_ref[...]) + bglu_ref[...])).astype(so_ref.dtype)


def _ssm(u, h0r, h0i, ar, ai, bd, cd, dvec, wglu, bglu, bt, tt):
    n, width = u.shape
    rows = tt * bt
    nstate = ar.shape[1]
    cst = [_const_spec(a.shape) for a in (h0r, h0i, ar, ai, bd, cd, dvec, wglu, bglu)]
    st_spec = _const_spec((bt, nstate))
    return pl.pallas_call(
        functools.partial(_ssm_kernel, bt=bt),
        grid=(n // rows,),
        in_specs=[pl.BlockSpec((rows, width), lambda i: (i, 0))] + cst,
        out_specs=[pl.BlockSpec((rows, width), lambda i: (i, 0)), st_spec, st_spec],
        out_shape=[jax.ShapeDtypeStruct((n, width), BF16), jax.ShapeDtypeStruct((bt, nstate), F32),
                   jax.ShapeDtypeStruct((bt, nstate), F32)],
        scratch_shapes=[pltpu.VMEM((rows, nstate), F32), pltpu.VMEM((rows, nstate), F32)],
        compiler_params=_cparams(("arbitrary",)),
        name="ssm",
    )(u, h0r, h0i, ar, ai, bd, cd, dvec, wglu, bglu)


def _back_kernel(h_ref, o_ref, so_ref, mg_ref, p_ref, watt_ref, wssm_ref, wo_ref, fn_ref, wg_ref, wu_ref, wd_ref,
                 pn_ref, wpg_ref, wple_ref, fin_ref, y_ref, *, ff_chunk):
    d = h_ref.shape[1]
    a = _dot(o_ref[...], watt_ref[...])
    s = _dot(so_ref[...], wssm_ref[...])
    mg = mg_ref[...].astype(F32)
    h = h_ref[...] + _dot((mg[:, :d] * a + mg[:, d:] * s).astype(BF16), wo_ref[...])
    f = _rms(h, fn_ref[...]).astype(BF16)
    ffn = jnp.zeros_like(h)
    for c0 in range(0, wg_ref.shape[1], ff_chunk):
        gate = _dot(f, wg_ref[:, c0:c0 + ff_chunk])
        up = _dot(f, wu_ref[:, c0:c0 + ff_chunk])
        ffn = ffn + _dot((gate * _sigmoid(gate) * up).astype(BF16), wd_ref[c0:c0 + ff_chunk, :])
    h = h + ffn
    g = _sigmoid(_dot(_rms(h, pn_ref[...]).astype(BF16), wpg_ref[...]))
    h = h + g * _dot(p_ref[...].astype(BF16), wple_ref[...])
    y_ref[...] = _rms(h, fin_ref[...])


def _back(h2d, o2d, so_tb, mg, p2d, weights, nb, t, tm, ff_chunk):
    n, d = h2d.shape
    nt = t // tm
    row = lambda b, i: (b * nt + i, 0)
    wspecs = [pl.BlockSpec(w.shape, lambda b, i, nd=w.ndim: (0,) * nd, pipeline_mode=pl.Buffered(1))
              for w in weights]
    sw = o2d.shape[1]
    return pl.pallas_call(
        functools.partial(_back_kernel, ff_chunk=ff_chunk),
        grid=(nb, nt),
        in_specs=[pl.BlockSpec((tm, d), row), pl.BlockSpec((tm, sw), row),
                  pl.BlockSpec((tm, sw), lambda b, i: (i, b)),
                  pl.BlockSpec((tm, mg.shape[1]), row), pl.BlockSpec((tm, p2d.shape[1]), row)] + wspecs,
        out_specs=pl.BlockSpec((tm, d), row),
        out_shape=jax.ShapeDtypeStruct((n, d), F32),
        compiler_params=_cparams(("parallel", "parallel")),
        name="back",
    )(h2d, o2d, so_tb, mg, p2d, *weights)


def _bucket_np(dist):
    n = np.maximum(dist, 0)
    exact = N_BUCKETS // 2
    nf = np.maximum(n, 1).astype(np.float64)
    large = exact + (np.log(nf / exact) / math.log(REL_MAX_DIST / exact) * (N_BUCKETS - exact)).astype(np.int64)
    return np.where(n < exact, n, np.minimum(large, N_BUCKETS - 1)).astype(np.int32)


def _bias_table(rel_bias, dist, valid, offset=None):
    onehot = jax.nn.one_hot(jnp.asarray(_bucket_np(dist)), N_BUCKETS, dtype=F32)
    b = jnp.einsum('...b,bh->h...', onehot, rel_bias.astype(F32), precision=lax.Precision.HIGHEST)
    if offset is not None:
        b = b - offset.reshape((N_HEADS,) + (1,) * dist.ndim)
    return jnp.where(jnp.asarray(valid)[None], b, NEG)


def _prompt_tables(rel_bias, t):
    def cols(b):
        r = b.shape[1]
        return b.reshape(N_KV, GROUP, r, QB).transpose(0, 2, 1, 3).reshape(N_KV, r, GROUP * QB)
    rel_bias = rel_bias.astype(F32) * LOG2E
    c = rel_bias[N_BUCKETS - 1]
    c_hi = c.astype(BF16)
    c_lo = (c - c_hi.astype(F32)).astype(BF16)
    c_eff = c_hi.astype(F32) + c_lo.astype(F32)
    crow = jnp.stack([c_hi.astype(F32), c_lo.astype(F32)] + [jnp.zeros_like(c)] * 6, axis=1)
    crow = jnp.broadcast_to(crow[:, :, None], (N_HEADS, 8, QB))
    crow = cols(crow)
    j = np.arange(QB)[:, None]
    i = np.arange(QB)[None, :]
    ones = np.ones((QB, QB), bool)
    zeros = jnp.zeros((N_KV, QB, GROUP * QB), F32)
    at = jnp.stack([
        cols(_bias_table(rel_bias, i - j, i >= j, c_eff)),
        cols(_bias_table(rel_bias, QB + i - j, ones, c_eff)),
        zeros,
        jnp.where(jnp.asarray(np.tile(j > i, (1, GROUP)))[None], zeros, NEG),
        zeros + NEG,
    ])
    ncp = t // CMP_STRIDE
    m = np.arange(2 * ncp)[:, None] - ncp
    dist = i - CMP_STRIDE * m - (CMP_BLOCK - 1)
    ut = cols(_bias_table(rel_bias, dist, dist >= 0))
    n_sel = t // SEL_BLOCK
    n = np.arange(ncp)[None, :]
    jb = np.arange(LANES)[:, None]
    ovlt = ((n * CMP_STRIDE < jb * SEL_BLOCK + SEL_BLOCK) & (n * CMP_STRIDE + CMP_BLOCK - 1 >= jb * SEL_BLOCK)
            & (jb < n_sel) & (n < ncp - 1))
    key = np.arange(t)[:, None]
    lane = np.arange(LANES)[None, :]
    stat = np.where(lane < LANES // 2, (lane == key // SEL_BLOCK) * BIG,
                    ((lane == LANES // 2) | (lane == LANES // 2 + 1)) * 1.0).astype(np.float32)
    return ut, at, jnp.asarray(ovlt.astype(np.float32)), jnp.asarray(stat, BF16), crow


def _sample_tables(rel_bias, past, t_new, win_buf):
    def rows(b):
        return b.reshape(N_HEADS * t_new, b.shape[-1])
    tok = np.arange(t_new)[:, None]
    nc = past // CMP_STRIDE
    n = np.arange(nc)[None, :]
    c_end = n * CMP_STRIDE + CMP_BLOCK - 1
    n_cmp = (past + t_new) // CMP_STRIDE - 1
    bc = rows(_bias_table(rel_bias, past + tok - c_end, (c_end <= past + tok) & (n < n_cmp)))
    js = np.arange(past + QB)[None, :]
    ds = np.where(js < past, past + tok - js, tok - (js - past))
    bs = rows(_bias_table(rel_bias, ds, np.where(js < past, True, (ds >= 0) & (js - past < t_new))))
    jw = np.arange(win_buf + QB)[None, :]
    dw = np.where(jw < win_buf, win_buf + tok - jw, tok - (jw - win_buf))
    valid = np.where(jw < win_buf, (dw >= 0) & (dw < WINDOW), (dw >= 0) & (jw - win_buf < t_new))
    bw = rows(_bias_table(rel_bias, dw, valid))
    n_sel = -(-(past + t_new) // SEL_BLOCK)
    nbp = 2 * LANES
    nn = np.arange(nc)[:, None]
    jb = np.arange(nbp)[None, :]
    ovl = ((nn * CMP_STRIDE < jb * SEL_BLOCK + SEL_BLOCK) & (nn * CMP_STRIDE + CMP_BLOCK - 1 >= jb * SEL_BLOCK)
           & (jb < n_sel) & (nn < n_cmp))
    e = (np.arange(LANES)[:, None] == np.arange(past)[None, :] // SEL_BLOCK).astype(np.float32) * BIG
    return bc, bs, bw, jnp.asarray(ovl.astype(np.float32)), jnp.asarray(e, BF16), n_sel


def _block_diag(blocks):
    g, r, c = blocks.shape
    eye = jnp.eye(g, dtype=blocks.dtype)
    return jnp.einsum('grc,gh->grhc', blocks, eye).reshape(g * r, g * c)


def _layer_params(rel_bias, final_norm, attn_norm, w_in, cmp_pe_k, cmp_w1_k, cmp_w2_k, cmp_pe_v, cmp_w1_v, cmp_w2_v,
                  ssm_a_re, ssm_a_im, ssm_log_dt, ssm_b_re, ssm_b_im, ssm_c_re, ssm_c_im, ssm_d, w_glu, b_glu,
                  w_att_br, w_ssm_br, w_o, ffn_norm, w_ffn_gate, w_ffn_up, w_ffn_down, ple_norm, w_ple_gate, w_ple):
    l = 0
    d = w_in.shape[1]
    n_groups = ssm_a_re.shape[1]
    ssm_w = n_groups * SSM_CH
    nstate = n_groups * SSM_P
    assert ssm_w % LANES == 0
    w = w_in[l]
    c0 = AW + 6 * KVW
    n_gate = N_HEADS * N_BRANCH
    front_w = (attn_norm[l].reshape(1, d), w[:, :c0].astype(BF16),
               jnp.pad(w[:, c0:c0 + n_gate], ((0, 0), (0, LANES - n_gate))).astype(BF16),
               w[:, c0 + n_gate:c0 + n_gate + ssm_w].astype(BF16), w[:, c0 + n_gate + ssm_w:].astype(BF16))

    def cmp_weights(pe, w1, w2):
        half = CMP_STRIDE * HD
        w1cat = jnp.concatenate([w1[:half], w1[half:]], axis=1).astype(BF16)
        pe2 = jnp.pad(pe.reshape(2, half), ((0, 6), (0, 0))).astype(BF16)
        return w1cat, w2.astype(BF16), pe2
    cw = cmp_weights(cmp_pe_k[l], cmp_w1_k[l], cmp_w2_k[l]) + cmp_weights(cmp_pe_v[l], cmp_w1_v[l], cmp_w2_v[l])

    abr, abi, bbr_t, bbi_t = _ssm_params(ssm_a_re[l], ssm_a_im[l], ssm_log_dt[l],
                                         jnp.swapaxes(ssm_b_re[l], 1, 2), jnp.swapaxes(ssm_b_im[l], 1, 2))
    n_slab = ssm_w // LANES
    sw = nstate // n_slab
    bd_r = _block_diag(bbr_t)
    bd_i = _block_diag(bbi_t)
    bd = jnp.stack([jnp.concatenate([bd_r[s * LANES:(s + 1) * LANES, s * sw:(s + 1) * sw],
                                     bd_i[s * LANES:(s + 1) * LANES, s * sw:(s + 1) * sw]], axis=1)
                    for s in range(n_slab)]).astype(BF16)
    cd_r = _block_diag(jnp.swapaxes(ssm_c_re[l], 1, 2))
    cd_i = _block_diag(jnp.swapaxes(ssm_c_im[l], 1, 2))
    cd = jnp.stack([jnp.concatenate([cd_r[s * sw:(s + 1) * sw, s * LANES:(s + 1) * LANES],
                                     -cd_i[s * sw:(s + 1) * sw, s * LANES:(s + 1) * LANES]], axis=0)
                    for s in range(n_slab)]).astype(BF16)
    ssm_p = (abr.reshape(1, nstate), abi.reshape(1, nstate), bd, cd, ssm_d[l].reshape(1, ssm_w),
             w_glu[l].astype(BF16), b_glu[l].reshape(1, ssm_w))

    back_w = (w_att_br[l].astype(BF16), w_ssm_br[l].astype(BF16), w_o[l].astype(BF16),
              ffn_norm[l].reshape(1, d), w_ffn_gate[l].astype(BF16), w_ffn_up[l].astype(BF16),
              w_ffn_down[l].astype(BF16), ple_norm[l].reshape(1, d), w_ple_gate[l].astype(BF16),
              w_ple[l].astype(BF16), final_norm.reshape(1, d))
    d_ff = w_ffn_gate.shape[2]
    ff_chunk = d_ff // 2 if (d_ff // 2) % LANES == 0 else d_ff
    return dict(front=front_w, cmp=cw, ssm=ssm_p, back=back_w, ff_chunk=ff_chunk, rel_bias=rel_bias,
                n_groups=n_groups, ssm_w=ssm_w, nstate=nstate, n_gate=n_gate)


def _prompt_group(x_prompt, p_l, prm):
    nb, t, d = x_prompt.shape
    ssm_w, nstate = prm["ssm_w"], prm["nstate"]
    assert t % (CMP_STRIDE * LANES) == 0 and t >= WINDOW and t // SEL_BLOCK <= LANES // 2
    xp = x_prompt.reshape(nb * t, d)
    (qt, kct, vct, kst, vst32, kc, vc, kw, vw, ksb, kwb, vst, vwt, ngt, su, mg) = _front_prompt(
        xp, nb, t, prm["front"], 512)
    kcc, vcct = _compress_prompt(kc.reshape(nb, t, KVW), vc.reshape(nb, t, KVW), prm["cmp"])
    o = _nsa_prompt(qt, ngt, kcc, vcct, ksb.reshape(nb, t, KVW), vst, kwb.reshape(nb, t, KVW), vwt,
                    _prompt_tables(prm["rel_bias"], t))
    zeros_state = jnp.zeros((nb, nstate), F32)
    so, sr, si = _ssm(su.reshape(t * nb, ssm_w), zeros_state, zeros_state, *prm["ssm"], nb, 256)
    y = _back(xp, o.reshape(nb * t, AW), so.reshape(t, nb * ssm_w), mg, p_l.reshape(nb * t, -1),
              prm["back"], nb, t, 256, prm["ff_chunk"])
    return dict(y=y, o=o, so=so, rows_t=(kct, vct, kst, vst32), win=(kw, vw), state=(sr, si))


def _sample_group(x_sample, p_l, page_table, pools, wins, states, prm):
    ns, t_new, d = x_sample.shape
    tok_minor = lambda a: jnp.transpose(a, (0, 2, 3, 1)).reshape(a.shape[0], KVW, a.shape[1])
    k_cmp, v_cmp, k_sel, v_sel = pools
    k_win, v_win = wins
    n_phys, page = k_cmp.shape[:2]
    past = page_table.shape[1] * page
    win_buf = k_win.shape[1]
    ssm_w, nstate, n_gate = prm["ssm_w"], prm["nstate"], prm["n_gate"]
    assert page == QB and past % (CMP_STRIDE * LANES) == 0 and past // SEL_BLOCK <= LANES
    assert win_buf == WINDOW and past >= win_buf and t_new <= 8 and t_new < CMP_STRIDE
    n_s = ns * t_new
    xs = x_sample.reshape(n_s, d)
    q_s, kc_s, vc_s, ks_s, vs_s, kw_s, vw_s, ng_s, su_s, mg_s = _front_sample(xs, prm["front"])
    kcc_s, vcc_s = _compress_sample(page_table, tok_minor(k_cmp), tok_minor(v_cmp), prm["cmp"])
    bc, bs, bw, ovl_s, e_s, n_sel_s = _sample_tables(prm["rel_bias"], past, t_new, win_buf)
    rows_s = N_HEADS * t_new
    eye_kv = jnp.eye(N_KV, dtype=BF16)
    q_rows = q_s.reshape(ns, t_new, N_KV, GROUP, HD).transpose(0, 2, 3, 1, 4)
    q_rows = jnp.einsum('skgtd,kj->skgtjd', q_rows, eye_kv).reshape(ns, rows_s, KVW)
    gate_s = ng_s[:, :n_gate].reshape(ns, t_new, N_KV, GROUP, N_BRANCH).transpose(0, 2, 3, 1, 4)
    gate_s = jnp.pad(gate_s.reshape(ns, rows_s, N_BRANCH), ((0, 0), (0, 0), (0, LANES - N_BRANCH)))
    pad8 = lambda a: jnp.pad(a.reshape(ns, t_new, KVW), ((0, 0), (0, 8 - t_new), (0, 0)))
    o_s = _nsa_sample(page_table, q_rows, gate_s, kcc_s, vcc_s,
                      tok_minor(k_sel), tok_minor(v_sel), pad8(ks_s), pad8(vs_s), tok_minor(k_win), tok_minor(v_win),
                      pad8(kw_s), pad8(vw_s), bc, bs, bw, ovl_s, e_s, n_sel_s, past, t_new)
    o_s = o_s.reshape(ns, N_KV, GROUP, t_new, N_KV, HD)
    o_s = jnp.stack([o_s[:, k, :, :, k, :] for k in range(N_KV)], axis=1)
    o_s = o_s.transpose(0, 3, 1, 2, 4).reshape(n_s, AW).astype(BF16)
    su_ts = su_s.reshape(ns, t_new, ssm_w).transpose(1, 0, 2).reshape(n_s, ssm_w)
    so_ts, sr, si = _ssm(su_ts, states[0].reshape(ns, nstate), states[1].reshape(ns, nstate), *prm["ssm"], ns, t_new)
    so_s = so_ts.reshape(t_new, ns, ssm_w).transpose(1, 0, 2).reshape(n_s, ssm_w)
    y = _back(xs, o_s, so_s, mg_s, p_l.reshape(n_s, -1), prm["back"], 1, n_s, n_s, prm["ff_chunk"])
    return dict(y=y, o=o_s, so=so_s, rows=(kc_s, vc_s, ks_s, vs_s, kw_s, vw_s), state=(sr, si))


def kernel(x_prompt, x_sample, cache_k_cmp, cache_v_cmp, cache_k_sel, cache_v_sel, cache_k_win, cache_v_win, state_ssm_re, state_ssm_im, page_table, p_prompt, p_sample, rel_bias, final_norm, attn_norm, w_in, cmp_pe_k, cmp_w1_k, cmp_w2_k, cmp_pe_v, cmp_w1_v, cmp_w2_v, ssm_a_re, ssm_a_im, ssm_log_dt, ssm_b_re, ssm_b_im, ssm_c_re, ssm_c_im, ssm_d, w_glu, b_glu, w_att_br, w_ssm_br, w_o, ffn_norm, w_ffn_gate, w_ffn_up, w_ffn_down, ple_norm, w_ple_gate, w_ple):
    assert w_in.shape[0] == 1, "single-layer trunk"
    l = 0
    nb, t, d = x_prompt.shape
    ns, t_new = x_sample.shape[:2]
    prm = _layer_params(rel_bias, final_norm, attn_norm, w_in, cmp_pe_k, cmp_w1_k, cmp_w2_k, cmp_pe_v, cmp_w1_v,
                        cmp_w2_v, ssm_a_re, ssm_a_im, ssm_log_dt, ssm_b_re, ssm_b_im, ssm_c_re, ssm_c_im, ssm_d,
                        w_glu, b_glu, w_att_br, w_ssm_br, w_o, ffn_norm, w_ffn_gate, w_ffn_up, w_ffn_down,
                        ple_norm, w_ple_gate, w_ple)
    pg = _prompt_group(x_prompt, p_prompt[l], prm)
    sg = _sample_group(x_sample, p_sample[l], page_table,
                       (cache_k_cmp[l], cache_v_cmp[l], cache_k_sel[l], cache_v_sel[l]),
                       (cache_k_win[l], cache_v_win[l]), (state_ssm_re[l], state_ssm_im[l]), prm)

    kv5 = lambda a, b_, t_: a.reshape(1, b_, t_, N_KV, HD)
    kv5_t = lambda a: jnp.transpose(a.reshape(1, nb, N_KV, HD, t), (0, 1, 4, 2, 3))
    keep = min(WINDOW, t)
    win_p = lambda a: a.reshape(nb, t, KVW)[:, t - keep:].reshape(1, nb, keep, N_KV, HD)
    win_s = lambda cache, new: jnp.concatenate(
        [cache[l], new.reshape(ns, t_new, N_KV, HD)], axis=1)[:, t_new:][None]
    st = lambda a, b_: a.reshape(1, b_, prm["n_groups"], SSM_P)
    kc_s, vc_s, ks_s, vs_s, kw_s, vw_s = sg["rows"]
    return (pg["y"].reshape(nb, t, d), sg["y"].reshape(ns, t_new, d),
            *[kv5_t(a) for a in pg["rows_t"]], win_p(pg["win"][0]), win_p(pg["win"][1]),
            st(pg["state"][0], nb), st(pg["state"][1], nb),
            kv5(kc_s, ns, t_new), kv5(vc_s, ns, t_new), kv5(ks_s, ns, t_new), kv5(vs_s, ns, t_new),
            win_s(cache_k_win, kw_s), win_s(cache_v_win, vw_s),
            st(sg["state"][0], ns), st(sg["state"][1], ns))
```

```python
import functools
import math

import numpy as np
import jax
import jax.numpy as jnp
from jax import lax
from jax.experimental import pallas as pl
from jax.experimental.pallas import tpu as pltpu

F32 = jnp.float32
BF16 = jnp.bfloat16

N_HEADS = 8
N_KV = 2
HD = 64
GROUP = N_HEADS // N_KV
N_BRANCH = 3
CMP_BLOCK = 32
CMP_STRIDE = 16
CMP_HIDDEN = 256
CMP_PITCH = 24
SEL_BLOCK = 64
SEL_TOPK = 16
WINDOW = 512
QB = 128
SEL_KEYS = 4 * QB
SEL_SPLIT = 2
N_BUCKETS = 32
REL_MAX_DIST = 128
SSM_CH = 16
SSM_P = 64
EPS = 1e-6
NEG = -1e30
NEG_TEST = -1e29
FORCE = 1e9
BIG = 1e30
LOG2E = math.log2(math.e)
LANES = 128
VMEM_LIMIT = 56 * 1024 * 1024
AW = N_HEADS * HD
KVW = N_KV * HD
NG_ROWS = 32


def _cparams(sem):
    return pltpu.CompilerParams(dimension_semantics=sem, vmem_limit_bytes=VMEM_LIMIT)


def _const_spec(shape):
    nd = len(shape)
    return pl.BlockSpec(shape, lambda *_: (0,) * nd)


def _rms(x, g):
    return x * lax.rsqrt(jnp.mean(x * x, axis=-1, keepdims=True) + EPS) * g


def _gelu(x):
    return x * (0.5 * (1.0 + jnp.tanh(math.sqrt(2.0 / math.pi) * (x + 0.044715 * (x * x * x)))))


def _sigmoid(x):
    return 1.0 / (1.0 + jnp.exp(-x))


def _dot(a, b):
    return jnp.dot(a, b, preferred_element_type=F32)


def _dot_t(a, b):
    return lax.dot_general(a, b, (((1,), (1,)), ((), ())), preferred_element_type=F32)


def _masked_softmax(s, axis=-1):
    valid = s > NEG_TEST
    m = jnp.max(s, axis=axis, keepdims=True)
    e = jnp.where(valid, jnp.exp(s - m), 0.0)
    return e / jnp.maximum(jnp.sum(e, axis=axis, keepdims=True), 1e-30)


def _softmax2_cols(s):
    m = jnp.max(s, axis=0, keepdims=True)
    e = jnp.exp2(s - m)
    inv = jnp.where(m > NEG_TEST, 1.0 / jnp.maximum(jnp.sum(e, axis=0, keepdims=True), 1e-30), 0.0)
    return e * inv


def _front_project(x_ref, g_ref, wa_ref, wng_ref, wsu_ref, wmg_ref, su_ref, mg_ref, q_scale):
    u = _rms(x_ref[...], g_ref[...]).astype(BF16)
    za = _dot(u, wa_ref[...])
    q = za[:, :AW] * q_scale
    rows = [za[:, AW + i * KVW: AW + (i + 1) * KVW] for i in range(6)]
    ng = _sigmoid(_dot(u, wng_ref[...]))
    su_ref[...] = _dot(u, wsu_ref[...]).astype(su_ref.dtype)
    mg_ref[...] = _sigmoid(_dot(u, wmg_ref[...])).astype(BF16)
    return q, rows, ng


def _front_sample_kernel(x_ref, g_ref, wa_ref, wng_ref, wsu_ref, wmg_ref,
                         q_ref, kc_ref, vc_ref, ks_ref, vs_ref, kw_ref, vw_ref, ng_ref, su_ref, mg_ref):
    q, rows, ng = _front_project(x_ref, g_ref, wa_ref, wng_ref, wsu_ref, wmg_ref, su_ref, mg_ref, HD ** -0.5)
    q_ref[...] = q.astype(BF16)
    for ref, r in zip((kc_ref, vc_ref, ks_ref, vs_ref, kw_ref, vw_ref), rows):
        ref[...] = r
    ng_ref[...] = ng


def _front_prompt_kernel(x_ref, g_ref, wa_ref, wng_ref, wsu_ref, wmg_ref,
                         qt_ref, kct_ref, vct_ref, kst_ref, vst32_ref, kc_ref, vc_ref, kw_ref, vw_ref,
                         ksb_ref, kwb_ref, vst_ref, vwt_ref, ngt_ref, su_ref, mg_ref):
    q, rows, ng = _front_project(x_ref, g_ref, wa_ref, wng_ref, wsu_ref, wmg_ref, su_ref, mg_ref,
                                 HD ** -0.5 * LOG2E)
    kc, vc, ks, vs, kw, vw = rows
    qt_ref[0] = q.T.astype(BF16)
    for ref, r in zip((kct_ref, vct_ref, kst_ref, vst32_ref), (kc, vc, ks, vs)):
        ref[0] = r.T
    kc_ref[...] = kc
    vc_ref[...] = vc
    kw_ref[...] = kw
    vw_ref[...] = vw
    ksb_ref[...] = ks.astype(BF16)
    kwb_ref[...] = kw.astype(BF16)
    for ref, r in ((vst_ref, vs), (vwt_ref, vw)):
        rt = r.T.astype(BF16)
        for j in range(rt.shape[1] // QB):
            ref[0, j] = rt[:, j * QB:(j + 1) * QB]
    ngt_ref[0] = ng.T[0:NG_ROWS]


def _front_sample(x2d, fw):
    g, wa, wng, wsu, wmg = fw
    n, d = x2d.shape
    sw, mw = wsu.shape[1], wmg.shape[1]
    shapes = ([jax.ShapeDtypeStruct((n, AW), BF16)] + [jax.ShapeDtypeStruct((n, KVW), F32)] * 6
              + [jax.ShapeDtypeStruct((n, LANES), F32), jax.ShapeDtypeStruct((n, sw), BF16),
                 jax.ShapeDtypeStruct((n, mw), BF16)])
    return pl.pallas_call(
        _front_sample_kernel,
        grid=(1,),
        in_specs=[_const_spec(a.shape) for a in (x2d, g, wa, wng, wsu, wmg)],
        out_specs=[_const_spec(s.shape) for s in shapes],
        out_shape=shapes,
        compiler_params=_cparams(("arbitrary",)),
        name="front_sample",
    )(x2d, g, wa, wng, wsu, wmg)


def _front_prompt(x2d, nb, t, fw, tm):
    g, wa, wng, wsu, wmg = fw
    n, d = x2d.shape
    nt = t // tm
    sw, mw = wsu.shape[1], wmg.shape[1]
    row = lambda b, i: (b * nt + i, 0)
    kv5 = jax.ShapeDtypeStruct((nb, KVW, t), F32)
    kv5_spec = pl.BlockSpec((1, KVW, tm), lambda b, i: (b, 0, i))
    vt = jax.ShapeDtypeStruct((nb, t // QB, KVW, QB), BF16)
    vt_spec = pl.BlockSpec((1, tm // QB, KVW, QB), lambda b, i: (b, i, 0, 0))
    shapes = ([jax.ShapeDtypeStruct((nb, AW, t), BF16)] + [kv5] * 4 + [jax.ShapeDtypeStruct((n, KVW), F32)] * 4
              + [jax.ShapeDtypeStruct((n, KVW), BF16)] * 2 + [vt] * 2
              + [jax.ShapeDtypeStruct((nb, NG_ROWS, t), F32), jax.ShapeDtypeStruct((t, nb * sw), BF16),
                 jax.ShapeDtypeStruct((n, mw), BF16)])
    specs = ([pl.BlockSpec((1, AW, tm), lambda b, i: (b, 0, i))] + [kv5_spec] * 4
             + [pl.BlockSpec((tm, KVW), row)] * 6 + [vt_spec] * 2
             + [pl.BlockSpec((1, NG_ROWS, tm), lambda b, i: (b, 0, i)),
                pl.BlockSpec((tm, sw), lambda b, i: (i, b)), pl.BlockSpec((tm, mw), row)])
    return pl.pallas_call(
        _front_prompt_kernel,
        grid=(nb, nt),
        in_specs=[pl.BlockSpec((tm, d), row)] + [_const_spec(a.shape) for a in (g, wa, wng, wsu, wmg)],
        out_specs=specs,
        out_shape=shapes,
        compiler_params=_cparams(("parallel", "parallel")),
        name="front_prompt",
    )(x2d, g, wa, wng, wsu, wmg)


def _chunk_rows(load, r0, rn, pitch=CMP_STRIDE):
    return jnp.concatenate([load(pl.ds(pitch * r0 + r, rn, stride=pitch)) for r in range(CMP_STRIDE)], axis=1)


def _compress_compute(load_rows, c, w1_ref, w2_ref, pe_ref, a_scr):
    rc_n = min(c, 256)
    lo = lax.broadcasted_iota(jnp.int32, (rc_n, LANES), 1) < HD
    w1 = w1_ref[...]
    for rc in range(c // rc_n):
        x = load_rows(rc * rc_n, rc_n)
        cols = [x[:, r * LANES:(r + 1) * LANES] for r in range(CMP_STRIDE)]
        rol = [pltpu.roll(col, HD, 1) for col in cols]
        for kh in range(N_KV):
            if kh == 0:
                parts = [jnp.where(lo, cols[2 * j], rol[2 * j + 1]) for j in range(CMP_STRIDE // 2)]
            else:
                parts = [jnp.where(lo, rol[2 * j], cols[2 * j + 1]) for j in range(CMP_STRIDE // 2)]
            xh = jnp.concatenate(parts, axis=1).astype(BF16)
            a_scr[kh, rc * rc_n:(rc + 1) * rc_n, :] = _dot(xh, w1)
    pw = _dot(pe_ref[...], w1)
    peb = pw[0:1, :CMP_HIDDEN] + pw[1:2, CMP_HIDDEN:]
    w2 = w2_ref[...]
    outs = []
    for kh in range(N_KV):
        a = a_scr[kh]
        hid = a[:, :CMP_HIDDEN] + pltpu.roll(a[:, CMP_HIDDEN:], c - 1, 0) + peb
        outs.append(_dot(_gelu(hid).astype(BF16), w2))
    return jnp.concatenate(outs, axis=1)


def _compress_prompt_kernel(xk_ref, xv_ref, w1k_ref, w2k_ref, pek_ref, w1v_ref, w2v_ref, pev_ref,
                            ok_ref, ovt_ref, a_scr):
    c = xk_ref.shape[1] // CMP_STRIDE
    ok_ref[0] = _compress_compute(lambda r0, rn: _chunk_rows(lambda idx: xk_ref[0, idx, :], r0, rn), c,
                                  w1k_ref, w2k_ref, pek_ref, a_scr).astype(BF16)
    ovt_ref[0] = _compress_compute(lambda r0, rn: _chunk_rows(lambda idx: xv_ref[0, idx, :], r0, rn), c,
                                   w1v_ref, w2v_ref, pev_ref, a_scr).T.astype(BF16)


def _compress_prompt(xk, xv, cw):
    nb, t, kvw = xk.shape
    c = t // CMP_STRIDE
    wspecs = [_const_spec(w.shape) for w in cw]
    blk = pl.BlockSpec((1, t, kvw), lambda b: (b, 0, 0))
    return pl.pallas_call(
        _compress_prompt_kernel,
        grid=(nb,),
        in_specs=[blk, blk] + wspecs,
        out_specs=[pl.BlockSpec((1, c, KVW), lambda b: (b, 0, 0)), pl.BlockSpec((1, KVW, c), lambda b: (b, 0, 0))],
        out_shape=[jax.ShapeDtypeStruct((nb, c, KVW), BF16), jax.ShapeDtypeStruct((nb, KVW, c), BF16)],
        scratch_shapes=[pltpu.VMEM((N_KV, c, 2 * CMP_HIDDEN), F32)],
        compiler_params=_cparams(("parallel",)),
        name="compress_prompt",
    )(xk, xv, *cw)


def _page_copy(pool, buf, sem, page, p, slot):
    return pltpu.make_async_copy(pool.at[page], buf.at[slot, p], sem)


def _page_gather_start(pt_ref, seq, pools, bufs, sems, slot, n_pages):
    def body(p, carry):
        page = pt_ref[seq, p]
        for i, (pool, buf) in enumerate(zip(pools, bufs)):
            _page_copy(pool, buf, sems.at[i, slot], page, p, slot).start()
        return carry
    lax.fori_loop(0, n_pages, body, 0)


def _page_gather_wait(pools, bufs, sems, slot, n_pages):
    def body(p, carry):
        for i, (pool, buf) in enumerate(zip(pools, bufs)):
            _page_copy(pool, buf, sems.at[i, slot], 0, p, slot).wait()
        return carry
    lax.fori_loop(0, n_pages, body, 0)


def _paged_prefetch(pt_ref, pools, bufs, sems, n_pages):
    s = pl.program_id(0)
    slot = s % 2

    @pl.when(s == 0)
    def _():
        _page_gather_start(pt_ref, 0, pools, bufs, sems, 0, n_pages)

    @pl.when(s + 1 < pl.num_programs(0))
    def _():
        _page_gather_start(pt_ref, s + 1, pools, bufs, sems, 1 - slot, n_pages)

    _page_gather_wait(pools, bufs, sems, slot, n_pages)
    return slot


def _compress_sample_kernel(pt_ref, kpool, vpool, w1k_ref, w2k_ref, pek_ref, w1v_ref, w2v_ref, pev_ref,
                            ok_ref, ov_ref, kbuf, vbuf, sems, a_scr, rows_scr):
    n_pages = pt_ref.shape[1]
    page = kpool.shape[2]
    c = n_pages * page // CMP_STRIDE
    slot = _paged_prefetch(pt_ref, (kpool, vpool), (kbuf, vbuf), sems, n_pages)
    for buf, out_ref, w1_ref, w2_ref, pe_ref in ((kbuf, ok_ref, w1k_ref, w2k_ref, pek_ref),
                                                (vbuf, ov_ref, w1v_ref, w2v_ref, pev_ref)):
        for p in range(n_pages):
            rows = buf[slot, p].T
            for ch in range(page // CMP_STRIDE):
                r0 = (p * (page // CMP_STRIDE) + ch) * CMP_PITCH
                rows_scr[r0:r0 + CMP_STRIDE, :] = rows[ch * CMP_STRIDE:(ch + 1) * CMP_STRIDE]
        out_ref[0] = _compress_compute(
            lambda r0, rn: _chunk_rows(lambda idx: rows_scr[idx, :], r0, rn, CMP_PITCH), c,
            w1_ref, w2_ref, pe_ref, a_scr).astype(BF16)


def _compress_sample(page_table, kpool, vpool, cw):
    ns, n_pages = page_table.shape
    width, page = kpool.shape[1:]
    tokens = n_pages * page
    c = tokens // CMP_STRIDE
    buf_shape = (2, n_pages, width, page)
    any_spec = pl.BlockSpec(memory_space=pl.ANY)
    wspecs = [pl.BlockSpec(w.shape, lambda s, pt, nd=w.ndim: (0,) * nd) for w in cw]
    oblk = pl.BlockSpec((1, c, KVW), lambda s, pt: (s, 0, 0))
    return pl.pallas_call(
        _compress_sample_kernel,
        grid_spec=pltpu.PrefetchScalarGridSpec(
            num_scalar_prefetch=1,
            grid=(ns,),
            in_specs=[any_spec, any_spec] + wspecs,
            out_specs=[oblk, oblk],
            scratch_shapes=[pltpu.VMEM(buf_shape, F32), pltpu.VMEM(buf_shape, F32), pltpu.SemaphoreType.DMA((2, 2)),
                            pltpu.VMEM((N_KV, c, 2 * CMP_HIDDEN), F32), pltpu.VMEM((c * CMP_PITCH, width), F32)]),
        out_shape=[jax.ShapeDtypeStruct((ns, c, KVW), BF16)] * 2,
        compiler_params=_cparams(("arbitrary",)),
        name="compress_sample",
    )(page_table, kpool, vpool, *cw)


def _rank_select(score, blk, n_real, axis):
    size = 8 if axis == 0 else LANES
    total = score.shape[axis]
    chunk = (lambda a, c: a[c * size:(c + 1) * size]) if axis == 0 else (lambda a, c: a[:, c * size:(c + 1) * size])
    n_chunks = -(-total // size)
    sc = [chunk(score, c) for c in range(n_chunks)]
    bl = [chunk(blk, c) for c in range(n_chunks)]
    rank = [jnp.zeros(s.shape, F32) for s in sc]
    for kk in range(n_real):
        col = score[kk:kk + 1, :] if axis == 0 else score[:, kk:kk + 1]
        for c in range(n_chunks):
            other = jnp.broadcast_to(col, sc[c].shape)
            if c * size > kk:
                beats = other >= sc[c]
            elif min((c + 1) * size, total) - 1 < kk:
                beats = other > sc[c]
            else:
                beats = (other > sc[c]) | ((other == sc[c]) & (bl[c] > kk))
            rank[c] = rank[c] + jnp.where(beats, 1.0, 0.0)
    return jnp.where(jnp.concatenate(rank, axis=axis) < SEL_TOPK, 1.0, 0.0)


def _block_scores(imp, blk, t):
    cur = t // SEL_BLOCK
    forced = (blk == 0) | (blk == cur) | (blk == cur - 1)
    valid = blk * SEL_BLOCK <= t
    return jnp.where(valid, jnp.where(forced, FORCE, imp), NEG)


def _nsa_prompt_kernel(qt_ref, ngt_ref, kc_ref, vct_ref, ks_ref, vst_ref, kw_ref, vwt_ref,
                       ut_ref, at_ref, ovlt_ref, stat_ref, crow_ref, o_ref, *, n_sel):
    ib = pl.program_id(1)
    qt = qt_ref[0]
    ngt = ngt_ref[0]
    ncp = kc_ref.shape[1]
    cols = GROUP * QB
    sel_rows = LANES // 2
    t_row = ib * QB + lax.broadcasted_iota(jnp.int32, (n_sel, QB), 1)
    blk_t = lax.broadcasted_iota(jnp.int32, (n_sel, QB), 0)
    zeros_q = jnp.zeros((HD, cols), F32)
    vrows = [slice(k * HD, (k + 1) * HD) for k in range(N_KV)]
    q_sel, q_win, o_c = [], [], []
    for k in range(N_KV):
        qk = jnp.concatenate([qt[(GROUP * k + g) * HD:(GROUP * k + g + 1) * HD, :] for g in range(GROUP)],
                             axis=1).astype(F32)
        qa = jnp.concatenate([qk, zeros_q] if k == 0 else [zeros_q, qk], axis=0)

        bias_c = ut_ref[k, pl.ds(pl.multiple_of(ncp - (QB // CMP_STRIDE) * ib, 8), ncp), :]
        p_c = _softmax2_cols(_dot(kc_ref[0], qa.astype(BF16)) + bias_c)
        o_c.append(_dot(vct_ref[0][vrows[k], :], p_c.astype(BF16)))
        psum = p_c[:, 0:QB]
        for g in range(1, GROUP):
            psum = psum + p_c[:, g * QB:(g + 1) * QB]
        psum_hi = psum.astype(BF16)
        psum_lo = (psum - psum_hi.astype(F32)).astype(BF16)
        imp = _dot(ovlt_ref[...], psum_hi) + _dot(ovlt_ref[...], psum_lo)

        sel = _rank_select(_block_scores(imp[0:n_sel], blk_t, t_row), blk_t, n_sel, 0)
        selm1 = jnp.concatenate([sel - 1.0] * GROUP, axis=1)
        if n_sel < sel_rows:
            selm1 = jnp.concatenate([selm1, jnp.zeros((sel_rows - n_sel, cols), F32)], axis=0)
        tail = jnp.concatenate([crow_ref[k], jnp.zeros((LANES - sel_rows - 8, cols), F32)], axis=0)
        q_sel.append(jnp.concatenate([qa, selm1, tail], axis=0).astype(BF16))
        q_win.append(jnp.concatenate([qa, jnp.zeros((sel_rows, cols), F32), tail], axis=0).astype(BF16))

    tiles_per_step = SEL_KEYS // QB

    ones_rows = jnp.where(lax.broadcasted_iota(jnp.int32, (16, SEL_KEYS), 0) == 0, 1.0, 0.0).astype(BF16)
    step_tiles = SEL_SPLIT * tiles_per_step

    def make_sel_body(near):
        def sel_body(jp, carry):
            scores, vts = [], []
            for sp in range(SEL_SPLIT):
                j0 = (jp * SEL_SPLIT + sp) * tiles_per_step
                k0 = pl.multiple_of(j0 * QB, SEL_KEYS)
                lhs = jnp.concatenate([ks_ref[0, pl.ds(k0, SEL_KEYS), :], stat_ref[pl.ds(k0, SEL_KEYS), :]], axis=1)
                tiles = [j0 + h for h in range(tiles_per_step)]
                tidx = [jnp.where(jt == ib, 0, jnp.where(jt == ib - 1, 1, jnp.where(jt < ib, 2, 4))) for jt in tiles]
                for k in range(N_KV):
                    s = _dot(lhs, q_sel[k])
                    if near:
                        s = s + jnp.concatenate([at_ref[ti, k] for ti in tidx], axis=0)
                    scores.append(s)
                    vts.append(jnp.concatenate([vst_ref[0, jt, vrows[k], :] for jt in tiles], axis=1))
            stats = []
            for (m, _), s in zip(carry, scores):
                m_new = jnp.maximum(m, jnp.max(s, axis=0, keepdims=True))
                stats.append((m_new, jnp.exp2(m - m_new), jnp.exp2(s - m_new).astype(BF16)))
            return tuple((m_new, alpha * acc + _dot(jnp.concatenate([vt, ones_rows], axis=0), p))
                         for (_, acc), (m_new, alpha, p), vt in zip(carry, stats, vts))
        return sel_body

    init = (jnp.full((1, cols), NEG, F32), jnp.zeros((HD + ones_rows.shape[0], cols), F32))
    n_far = jnp.maximum(ib - 1, 0) // step_tiles
    sel_state = lax.fori_loop(0, n_far, make_sel_body(False), (init,) * (N_KV * SEL_SPLIT))
    sel_state = lax.fori_loop(n_far, (ib + step_tiles) // step_tiles, make_sel_body(True), sel_state)

    out_rows = []
    for k in range(N_KV):
        parts = [sel_state[sp * N_KV + k] for sp in range(SEL_SPLIT)]
        m_s = parts[0][0]
        for m_p, _ in parts[1:]:
            m_s = jnp.maximum(m_s, m_p)
        acc_s = jnp.zeros(init[1].shape, F32)
        for m_p, acc_p in parts:
            acc_s = acc_s + jnp.exp2(m_p - m_s) * acc_p
        o_s = acc_s[0:HD] / jnp.maximum(acc_s[HD:HD + 1], 1e-30)

        s_parts, tiles_j = [], []
        for w, tidx in enumerate((0, 1, None, None, 3)):
            jt = ib - w
            jc = jnp.maximum(jt, 0)
            k0 = pl.multiple_of(jc * QB, QB)
            lhs = jnp.concatenate([kw_ref[0, pl.ds(k0, QB), :], stat_ref[pl.ds(k0, QB), :]], axis=1)
            s = _dot(lhs, q_win[k])
            if tidx is not None:
                s = s + at_ref[tidx, k]
            s_parts.append(jnp.where(jt >= 0, s, NEG))
            tiles_j.append(jc)
        p_w = _softmax2_cols(jnp.concatenate(s_parts, axis=0)).astype(BF16)
        o_w = jnp.zeros((HD, cols), F32)
        for w, jc in enumerate(tiles_j):
            o_w = o_w + _dot(vwt_ref[0, jc, vrows[k], :], p_w[w * QB:(w + 1) * QB])

        def gate_row(br):
            return jnp.concatenate([ngt[(GROUP * k + g) * N_BRANCH + br:(GROUP * k + g) * N_BRANCH + br + 1, :]
                                    for g in range(GROUP)], axis=1)
        o_k = gate_row(0) * o_c[k] + gate_row(1) * o_s + gate_row(2) * o_w
        out_rows += [o_k[:, g * QB:(g + 1) * QB] for g in range(GROUP)]
    o_ref[0] = jnp.concatenate(out_rows, axis=0).T.astype(BF16)


def _nsa_prompt(qt, ngt, kc, vct, ks, vst, kw, vwt, tables):
    nb, aw, t = qt.shape
    nq = t // QB
    ncp = kc.shape[1]
    full3 = lambda b, i: (b, 0, 0)
    full4 = lambda b, i: (b, 0, 0, 0)
    return pl.pallas_call(
        functools.partial(_nsa_prompt_kernel, n_sel=t // SEL_BLOCK),
        grid=(nb, nq),
        in_specs=[pl.BlockSpec((1, aw, QB), lambda b, i: (b, 0, i)),
                  pl.BlockSpec((1, NG_ROWS, QB), lambda b, i: (b, 0, i)),
                  pl.BlockSpec((1, ncp, KVW), full3), pl.BlockSpec((1, KVW, ncp), full3),
                  pl.BlockSpec((1, t, KVW), full3), pl.BlockSpec((1, nq, KVW, QB), full4),
                  pl.BlockSpec((1, t, KVW), full3), pl.BlockSpec((1, nq, KVW, QB), full4)]
                 + [_const_spec(a.shape) for a in tables],
        out_specs=pl.BlockSpec((1, QB, aw), lambda b, i: (b, i, 0)),
        out_shape=jax.ShapeDtypeStruct((nb, t, aw), BF16),
        compiler_params=_cparams(("parallel", "arbitrary")),
        name="nsa_prompt",
    )(qt, ngt, kc, vct, ks, vst, kw, vwt, *tables)


def _nsa_sample_kernel(pt_ref, q_ref, gate_ref, kc_ref, vc_ref, kpool, vpool, ksn_ref, vsn_ref,
                       kwin_ref, vwin_ref, kwn_ref, vwn_ref, bc_ref, bs_ref, bw_ref, ovl_ref, e_ref,
                       o_ref, kbuf, vbuf, sems, *, n_sel, past, t_new):
    n_pages = pt_ref.shape[1]
    wb = kwin_ref.shape[2]
    slot = _paged_prefetch(pt_ref, (kpool, vpool), (kbuf, vbuf), sems, n_pages)
    past_t = lambda buf: jnp.concatenate([buf[slot, p] for p in range(n_pages)], axis=1).astype(BF16)
    q = q_ref[0]
    nb_past = past // SEL_BLOCK
    pad_new = jnp.zeros((QB - ksn_ref.shape[1], KVW), F32)
    new_tile = lambda ref: jnp.concatenate([ref[0], pad_new], axis=0).astype(BF16)

    p_c = _masked_softmax(_dot_t(q, kc_ref[0]) + bc_ref[...])
    o_c = _dot(p_c.astype(BF16), vc_ref[0])
    parts = []
    for k in range(N_KV):
        base = k * GROUP * t_new
        ps = p_c[base:base + t_new]
        for g in range(1, GROUP):
            ps = ps + p_c[base + g * t_new:base + (g + 1) * t_new]
        parts.append(ps)
    psum = jnp.concatenate(parts, axis=0)
    psum_hi = psum.astype(BF16)
    psum_lo = (psum - psum_hi.astype(F32)).astype(BF16)
    imp = _dot(psum_hi, ovl_ref[...]) + _dot(psum_lo, ovl_ref[...])
    blk = lax.broadcasted_iota(jnp.int32, imp.shape, 1)
    tpos = past + lax.broadcasted_iota(jnp.int32, imp.shape, 0) % t_new
    sel = _rank_select(_block_scores(imp, blk, tpos), blk, n_sel, 1)
    sel = jnp.concatenate([sel[k * t_new:(k + 1) * t_new] for k in range(N_KV) for _ in range(GROUP)], axis=0)

    mask_add = _dot((sel[:, 0:LANES] - 1.0).astype(BF16), e_ref[...])
    s_past = _dot(q, past_t(kbuf)) + bs_ref[:, 0:past] + mask_add
    s_new = _dot_t(q, new_tile(ksn_ref)) + bs_ref[:, past:]
    s_new = jnp.where(sel[:, nb_past:nb_past + 1] > 0.5, s_new, NEG)
    p_s = _masked_softmax(jnp.concatenate([s_past, s_new], axis=1)).astype(BF16)
    o_s = _dot_t(p_s[:, 0:past], past_t(vbuf)) + _dot(p_s[:, past:], new_tile(vsn_ref))

    s_w = jnp.concatenate([_dot(q, kwin_ref[0].astype(BF16)), _dot_t(q, new_tile(kwn_ref))], axis=1)
    p_w = _masked_softmax(s_w + bw_ref[...]).astype(BF16)
    o_w = _dot_t(p_w[:, 0:wb], vwin_ref[0].astype(BF16)) + _dot(p_w[:, wb:], new_tile(vwn_ref))

    gate = gate_ref[0]
    o_ref[0] = gate[:, 0:1] * o_c + gate[:, 1:2] * o_s + gate[:, 2:3] * o_w


def _nsa_sample(page_table, q, gate, kc, vc, kpool, vpool, ksn, vsn, kwin, vwin, kwn, vwn,
                bc, bs, bw, ovl, e, n_sel, past, t_new):
    ns, n_pages = page_table.shape
    rows, kvw = q.shape[1:]
    buf_shape = (2, n_pages) + kpool.shape[1:]
    seq3 = lambda s, pt: (s, 0, 0)
    any_spec = pl.BlockSpec(memory_space=pl.ANY)
    cs = lambda a: pl.BlockSpec(a.shape, lambda s, pt, nd=a.ndim: (0,) * nd, pipeline_mode=pl.Buffered(1))
    per_seq = lambda a: pl.BlockSpec((1,) + a.shape[1:], seq3)
    return pl.pallas_call(
        functools.partial(_nsa_sample_kernel, n_sel=n_sel, past=past, t_new=t_new),
        grid_spec=pltpu.PrefetchScalarGridSpec(
            num_scalar_prefetch=1,
            grid=(ns,),
            in_specs=[per_seq(q), per_seq(gate), per_seq(kc), per_seq(vc), any_spec, any_spec,
                      per_seq(ksn), per_seq(vsn), per_seq(kwin), per_seq(vwin), per_seq(kwn), per_seq(vwn),
                      cs(bc), cs(bs), cs(bw), cs(ovl), cs(e)],
            out_specs=pl.BlockSpec((1, rows, kvw), seq3),
            scratch_shapes=[pltpu.VMEM(buf_shape, F32), pltpu.VMEM(buf_shape, F32), pltpu.SemaphoreType.DMA((2, 2))]),
        out_shape=jax.ShapeDtypeStruct((ns, rows, kvw), F32),
        compiler_params=_cparams(("arbitrary",)),
        name="nsa_sample",
    )(page_table, q, gate, kc, vc, kpool, vpool, ksn, vsn, kwin, vwin, kwn, vwn, bc, bs, bw, ovl, e)


def _ssm_param_kernel(ar_ref, ai_ref, ldt_ref, br_ref, bi_ref, abr_ref, abi_ref, bbr_ref, bbi_ref):
    ar = ar_ref[...]
    ai = ai_ref[...]
    dt = jnp.exp(ldt_ref[...])
    mag = jnp.exp(ar * dt)
    abr = mag * jnp.cos(ai * dt)
    abi = mag * jnp.sin(ai * dt)
    den = ar * ar + ai * ai
    nr, ni = abr - 1.0, abi
    fr = (nr * ar + ni * ai) / den
    fi = (ni * ar - nr * ai) / den
    abr_ref[...] = abr
    abi_ref[...] = abi
    for g in range(ar.shape[0]):
        br = br_ref[g]
        bi = bi_ref[g]
        frg = fr[g:g + 1, :]
        fig = fi[g:g + 1, :]
        bbr_ref[g] = frg * br - fig * bi
        bbi_ref[g] = frg * bi + fig * br


def _ssm_params(a_re, a_im, log_dt, b_re_t, b_im_t):
    g, p = a_re.shape
    return pl.pallas_call(
        _ssm_param_kernel,
        out_shape=[jax.ShapeDtypeStruct((g, p), F32)] * 2 + [jax.ShapeDtypeStruct(b_re_t.shape, F32)] * 2,
        name="ssm_params",
    )(a_re, a_im, log_dt.reshape(g, 1), b_re_t, b_im_t)


def _ssm_kernel(u_ref, h0r_ref, h0i_ref, ar_ref, ai_ref, bd_ref, cd_ref, d_ref, wglu_ref, bglu_ref,
                so_ref, hr_ref, hi_ref, xr_scr, xi_scr, *, bt):
    i = pl.program_id(0)
    rows, width = u_ref.shape
    n_slab = bd_ref.shape[0]
    sw = bd_ref.shape[2] // 2
    u = u_ref[...]

    @pl.when(i == 0)
    def _():
        hr_ref[...] = h0r_ref[...]
        hi_ref[...] = h0i_ref[...]

    for sl in range(n_slab):
        x = _dot(u[:, sl * LANES:(sl + 1) * LANES], bd_ref[sl])
        xr_scr[:, sl * sw:(sl + 1) * sw] = x[:, :sw]
        xi_scr[:, sl * sw:(sl + 1) * sw] = x[:, sw:]

    per = 8 // math.gcd(bt, 8)
    grp = per * bt
    lc = 512
    for c0 in range(0, xr_scr.shape[1], lc):
        cl = slice(c0, c0 + lc)
        a_r = jnp.broadcast_to(ar_ref[:, cl], (bt, lc))
        a_i = jnp.broadcast_to(ai_ref[:, cl], (bt, lc))

        def step(j, carry):
            h_r, h_i = carry
            r0 = pl.multiple_of(j * grp, grp)
            xr = xr_scr[pl.ds(r0, grp), cl]
            xi = xi_scr[pl.ds(r0, grp), cl]
            out_r, out_i = [], []
            for s in range(per):
                n_r = a_r * h_r - a_i * h_i + xr[s * bt:(s + 1) * bt]
                n_i = a_r * h_i + a_i * h_r + xi[s * bt:(s + 1) * bt]
                h_r, h_i = n_r, n_i
                out_r.append(h_r)
                out_i.append(h_i)
            xr_scr[pl.ds(r0, grp), cl] = jnp.concatenate(out_r, axis=0) if per > 1 else out_r[0]
            xi_scr[pl.ds(r0, grp), cl] = jnp.concatenate(out_i, axis=0) if per > 1 else out_i[0]
            return h_r, h_i

        h_r, h_i = lax.fori_loop(0, rows // grp, step, (hr_ref[:, cl], hi_ref[:, cl]))
        hr_ref[:, cl] = h_r
        hi_ref[:, cl] = h_i

    ys = []
    for sl in range(n_slab):
        hcat = jnp.concatenate([xr_scr[:, sl * sw:(sl + 1) * sw], xi_scr[:, sl * sw:(sl + 1) * sw]], axis=1)
        ys.append(_dot(hcat.astype(BF16), cd_ref[sl]))
    y = jnp.concatenate(ys, axis=1) + d_ref[...] * u.astype(F32)
    z = _gelu(y)
    so_ref[...] = (z * _sigmoid(_dot(z.astype(BF16), wglu_ref[...]) + bglu_ref[...])).astype(so_ref.dtype)


def _ssm(u, h0r, h0i, ar, ai, bd, cd, dvec, wglu, bglu, bt, tt):
    n, width = u.shape
    rows = tt * bt
    nstate = ar.shape[1]
    cst = [_const_spec(a.shape) for a in (h0r, h0i, ar, ai, bd, cd, dvec, wglu, bglu)]
    st_spec = _const_spec((bt, nstate))
    return pl.pallas_call(
        functools.partial(_ssm_kernel, bt=bt),
        grid=(n // rows,),
        in_specs=[pl.BlockSpec((rows, width), lambda i: (i, 0))] + cst,
        out_specs=[pl.BlockSpec((rows, width), lambda i: (i, 0)), st_spec, st_spec],
        out_shape=[jax.ShapeDtypeStruct((n, width), BF16), jax.ShapeDtypeStruct((bt, nstate), F32),
                   jax.ShapeDtypeStruct((bt, nstate), F32)],
        scratch_shapes=[pltpu.VMEM((rows, nstate), F32), pltpu.VMEM((rows, nstate), F32)],
        compiler_params=_cparams(("arbitrary",)),
        name="ssm",
    )(u, h0r, h0i, ar, ai, bd, cd, dvec, wglu, bglu)


def _back_kernel(h_ref, o_ref, so_ref, mg_ref, p_ref, watt_ref, wssm_ref, wo_ref, fn_ref, wg_ref, wu_ref, wd_ref,
                 pn_ref, wpg_ref, wple_ref, fin_ref, y_ref, *, ff_chunk):
    d = h_ref.shape[1]
    a = _dot(o_ref[...], watt_ref[...])
    s = _dot(so_ref[...], wssm_ref[...])
    mg = mg_ref[...].astype(F32)
    h = h_ref[...] + _dot((mg[:, :d] * a + mg[:, d:] * s).astype(BF16), wo_ref[...])
    f = _rms(h, fn_ref[...]).astype(BF16)
    ffn = jnp.zeros_like(h)
    for c0 in range(0, wg_ref.shape[1], ff_chunk):
        gate = _dot(f, wg_ref[:, c0:c0 + ff_chunk])
        up = _dot(f, wu_ref[:, c0:c0 + ff_chunk])
        ffn = ffn + _dot((gate * _sigmoid(gate) * up).astype(BF16), wd_ref[c0:c0 + ff_chunk, :])
    h = h + ffn
    g = _sigmoid(_dot(_rms(h, pn_ref[...]).astype(BF16), wpg_ref[...]))
    h = h + g * _dot(p_ref[...].astype(BF16), wple_ref[...])
    y_ref[...] = _rms(h, fin_ref[...])


def _back(h2d, o2d, so_tb, mg, p2d, weights, nb, t, tm, ff_chunk):
    n, d = h2d.shape
    nt = t // tm
    row = lambda b, i: (b * nt + i, 0)
    wspecs = [pl.BlockSpec(w.shape, lambda b, i, nd=w.ndim: (0,) * nd, pipeline_mode=pl.Buffered(1))
              for w in weights]
    sw = o2d.shape[1]
    return pl.pallas_call(
        functools.partial(_back_kernel, ff_chunk=ff_chunk),
        grid=(nb, nt),
        in_specs=[pl.BlockSpec((tm, d), row), pl.BlockSpec((tm, sw), row),
                  pl.BlockSpec((tm, sw), lambda b, i: (i, b)),
                  pl.BlockSpec((tm, mg.shape[1]), row), pl.BlockSpec((tm, p2d.shape[1]), row)] + wspecs,
        out_specs=pl.BlockSpec((tm, d), row),
        out_shape=jax.ShapeDtypeStruct((n, d), F32),
        compiler_params=_cparams(("parallel", "parallel")),
        name="back",
    )(h2d, o2d, so_tb, mg, p2d, *weights)


def _bucket_np(dist):
    n = np.maximum(dist, 0)
    exact = N_BUCKETS // 2
    nf = np.maximum(n, 1).astype(np.float64)
    large = exact + (np.log(nf / exact) / math.log(REL_MAX_DIST / exact) * (N_BUCKETS - exact)).astype(np.int64)
    return np.where(n < exact, n, np.minimum(large, N_BUCKETS - 1)).astype(np.int32)


def _bias_table(rel_bias, dist, valid, offset=None):
    onehot = jax.nn.one_hot(jnp.asarray(_bucket_np(dist)), N_BUCKETS, dtype=F32)
    b = jnp.einsum('...b,bh->h...', onehot, rel_bias.astype(F32), precision=lax.Precision.HIGHEST)
    if offset is not None:
        b = b - offset.reshape((N_HEADS,) + (1,) * dist.ndim)
    return jnp.where(jnp.asarray(valid)[None], b, NEG)


def _prompt_tables(rel_bias, t):
    def cols(b):
        r = b.shape[1]
        return b.reshape(N_KV, GROUP, r, QB).transpose(0, 2, 1, 3).reshape(N_KV, r, GROUP * QB)
    rel_bias = rel_bias.astype(F32) * LOG2E
    c = rel_bias[N_BUCKETS - 1]
    c_hi = c.astype(BF16)
    c_lo = (c - c_hi.astype(F32)).astype(BF16)
    c_eff = c_hi.astype(F32) + c_lo.astype(F32)
    crow = jnp.stack([c_hi.astype(F32), c_lo.astype(F32)] + [jnp.zeros_like(c)] * 6, axis=1)
    crow = jnp.broadcast_to(crow[:, :, None], (N_HEADS, 8, QB))
    crow = cols(crow)
    j = np.arange(QB)[:, None]
    i = np.arange(QB)[None, :]
    ones = np.ones((QB, QB), bool)
    zeros = jnp.zeros((N_KV, QB, GROUP * QB), F32)
    at = jnp.stack([
        cols(_bias_table(rel_bias, i - j, i >= j, c_eff)),
        cols(_bias_table(rel_bias, QB + i - j, ones, c_eff)),
        zeros,
        jnp.where(jnp.asarray(np.tile(j > i, (1, GROUP)))[None], zeros, NEG),
        zeros + NEG,
    ])
    ncp = t // CMP_STRIDE
    m = np.arange(2 * ncp)[:, None] - ncp
    dist = i - CMP_STRIDE * m - (CMP_BLOCK - 1)
    ut = cols(_bias_table(rel_bias, dist, dist >= 0))
    n_sel = t // SEL_BLOCK
    n = np.arange(ncp)[None, :]
    jb = np.arange(LANES)[:, None]
    ovlt = ((n * CMP_STRIDE < jb * SEL_BLOCK + SEL_BLOCK) & (n * CMP_STRIDE + CMP_BLOCK - 1 >= jb * SEL_BLOCK)
            & (jb < n_sel) & (n < ncp - 1))
    key = np.arange(t)[:, None]
    lane = np.arange(LANES)[None, :]
    stat = np.where(lane < LANES // 2, (lane == key // SEL_BLOCK) * BIG,
                    ((lane == LANES // 2) | (lane == LANES // 2 + 1)) * 1.0).astype(np.float32)
    return ut, at, jnp.asarray(ovlt.astype(np.float32), BF16), jnp.asarray(stat, BF16), crow


def _sample_tables(rel_bias, past, t_new, win_buf):
    def rows(b):
        return b.reshape(N_HEADS * t_new, b.shape[-1])
    tok = np.arange(t_new)[:, None]
    nc = past // CMP_STRIDE
    n = np.arange(nc)[None, :]
    c_end = n * CMP_STRIDE + CMP_BLOCK - 1
    n_cmp = (past + t_new) // CMP_STRIDE - 1
    bc = rows(_bias_table(rel_bias, past + tok - c_end, (c_end <= past + tok) & (n < n_cmp)))
    js = np.arange(past + QB)[None, :]
    ds = np.where(js < past, past + tok - js, tok - (js - past))
    bs = rows(_bias_table(rel_bias, ds, np.where(js < past, True, (ds >= 0) & (js - past < t_new))))
    jw = np.arange(win_buf + QB)[None, :]
    dw = np.where(jw < win_buf, win_buf + tok - jw, tok - (jw - win_buf))
    valid = np.where(jw < win_buf, (dw >= 0) & (dw < WINDOW), (dw >= 0) & (jw - win_buf < t_new))
    bw = rows(_bias_table(rel_bias, dw, valid))
    n_sel = -(-(past + t_new) // SEL_BLOCK)
    nbp = 2 * LANES
    nn = np.arange(nc)[:, None]
    jb = np.arange(nbp)[None, :]
    ovl = ((nn * CMP_STRIDE < jb * SEL_BLOCK + SEL_BLOCK) & (nn * CMP_STRIDE + CMP_BLOCK - 1 >= jb * SEL_BLOCK)
           & (jb < n_sel) & (nn < n_cmp))
    e = (np.arange(LANES)[:, None] == np.arange(past)[None, :] // SEL_BLOCK).astype(np.float32) * BIG
    return bc, bs, bw, jnp.asarray(ovl.astype(np.float32), BF16), jnp.asarray(e, BF16), n_sel


def _block_diag(blocks):
    g, r, c = blocks.shape
    eye = jnp.eye(g, dtype=blocks.dtype)
    return jnp.einsum('grc,gh->grhc', blocks, eye).reshape(g * r, g * c)


def _layer_params(rel_bias, final_norm, attn_norm, w_in, cmp_pe_k, cmp_w1_k, cmp_w2_k, cmp_pe_v, cmp_w1_v, cmp_w2_v,
                  ssm_a_re, ssm_a_im, ssm_log_dt, ssm_b_re, ssm_b_im, ssm_c_re, ssm_c_im, ssm_d, w_glu, b_glu,
                  w_att_br, w_ssm_br, w_o, ffn_norm, w_ffn_gate, w_ffn_up, w_ffn_down, ple_norm, w_ple_gate, w_ple):
    l = 0
    d = w_in.shape[1]
    n_groups = ssm_a_re.shape[1]
    ssm_w = n_groups * SSM_CH
    nstate = n_groups * SSM_P
    assert ssm_w % LANES == 0
    w = w_in[l]
    c0 = AW + 6 * KVW
    n_gate = N_HEADS * N_BRANCH
    front_w = (attn_norm[l].reshape(1, d), w[:, :c0].astype(BF16),
               jnp.pad(w[:, c0:c0 + n_gate], ((0, 0), (0, LANES - n_gate))).astype(BF16),
               w[:, c0 + n_gate:c0 + n_gate + ssm_w].astype(BF16), w[:, c0 + n_gate + ssm_w:].astype(BF16))

    def cmp_weights(pe, w1, w2):
        half = CMP_STRIDE * HD
        w1cat = jnp.concatenate([w1[:half], w1[half:]], axis=1).astype(BF16)
        pe2 = jnp.pad(pe.reshape(2, half), ((0, 6), (0, 0))).astype(BF16)
        return w1cat, w2.astype(BF16), pe2
    cw = cmp_weights(cmp_pe_k[l], cmp_w1_k[l], cmp_w2_k[l]) + cmp_weights(cmp_pe_v[l], cmp_w1_v[l], cmp_w2_v[l])

    abr, abi, bbr_t, bbi_t = _ssm_params(ssm_a_re[l], ssm_a_im[l], ssm_log_dt[l],
                                         jnp.swapaxes(ssm_b_re[l], 1, 2), jnp.swapaxes(ssm_b_im[l], 1, 2))
    n_slab = ssm_w // LANES
    sw = nstate // n_slab
    bd_r = _block_diag(bbr_t)
    bd_i = _block_diag(bbi_t)
    bd = jnp.stack([jnp.concatenate([bd_r[s * LANES:(s + 1) * LANES, s * sw:(s + 1) * sw],
                                     bd_i[s * LANES:(s + 1) * LANES, s * sw:(s + 1) * sw]], axis=1)
                    for s in range(n_slab)]).astype(BF16)
    cd_r = _block_diag(jnp.swapaxes(ssm_c_re[l], 1, 2))
    cd_i = _block_diag(jnp.swapaxes(ssm_c_im[l], 1, 2))
    cd = jnp.stack([jnp.concatenate([cd_r[s * sw:(s + 1) * sw, s * LANES:(s + 1) * LANES],
                                     -cd_i[s * sw:(s + 1) * sw, s * LANES:(s + 1) * LANES]], axis=0)
                    for s in range(n_slab)]).astype(BF16)
    ssm_p = (abr.reshape(1, nstate), abi.reshape(1, nstate), bd, cd, ssm_d[l].reshape(1, ssm_w),
             w_glu[l].astype(BF16), b_glu[l].reshape(1, ssm_w))

    back_w = (w_att_br[l].astype(BF16), w_ssm_br[l].astype(BF16), w_o[l].astype(BF16),
              ffn_norm[l].reshape(1, d), w_ffn_gate[l].astype(BF16), w_ffn_up[l].astype(BF16),
              w_ffn_down[l].astype(BF16), ple_norm[l].reshape(1, d), w_ple_gate[l].astype(BF16),
              w_ple[l].astype(BF16), final_norm.reshape(1, d))
    d_ff = w_ffn_gate.shape[2]
    ff_chunk = d_ff // 2 if (d_ff // 2) % LANES == 0 else d_ff
    return dict(front=front_w, cmp=cw, ssm=ssm_p, back=back_w, ff_chunk=ff_chunk, rel_bias=rel_bias,
                n_groups=n_groups, ssm_w=ssm_w, nstate=nstate, n_gate=n_gate)


def _prompt_group(x_prompt, p_l, prm):
    nb, t, d = x_prompt.shape
    ssm_w, nstate = prm["ssm_w"], prm["nstate"]
    assert t % (CMP_STRIDE * LANES) == 0 and t >= WINDOW and t // SEL_BLOCK <= LANES // 2
    xp = x_prompt.reshape(nb * t, d)
    (qt, kct, vct, kst, vst32, kc, vc, kw, vw, ksb, kwb, vst, vwt, ngt, su, mg) = _front_prompt(
        xp, nb, t, prm["front"], 512)
    kcc, vcct = _compress_prompt(kc.reshape(nb, t, KVW), vc.reshape(nb, t, KVW), prm["cmp"])
    o = _nsa_prompt(qt, ngt, kcc, vcct, ksb.reshape(nb, t, KVW), vst, kwb.reshape(nb, t, KVW), vwt,
                    _prompt_tables(prm["rel_bias"], t))
    zeros_state = jnp.zeros((nb, nstate), F32)
    so, sr, si = _ssm(su.reshape(t * nb, ssm_w), zeros_state, zeros_state, *prm["ssm"], nb, 256)
    y = _back(xp, o.reshape(nb * t, AW), so.reshape(t, nb * ssm_w), mg, p_l.reshape(nb * t, -1),
              prm["back"], nb, t, 512, prm["ff_chunk"])
    return dict(y=y, o=o, so=so, rows_t=(kct, vct, kst, vst32), win=(kw, vw), state=(sr, si))


def _sample_group(x_sample, p_l, page_table, pools, wins, states, prm):
    ns, t_new, d = x_sample.shape
    tok_minor = lambda a: jnp.transpose(a, (0, 2, 3, 1)).reshape(a.shape[0], KVW, a.shape[1])
    k_cmp, v_cmp, k_sel, v_sel = pools
    k_win, v_win = wins
    n_phys, page = k_cmp.shape[:2]
    past = page_table.shape[1] * page
    win_buf = k_win.shape[1]
    ssm_w, nstate, n_gate = prm["ssm_w"], prm["nstate"], prm["n_gate"]
    assert page == QB and past % (CMP_STRIDE * LANES) == 0 and past // SEL_BLOCK <= LANES
    assert win_buf == WINDOW and past >= win_buf and t_new <= 8 and t_new < CMP_STRIDE
    n_s = ns * t_new
    xs = x_sample.reshape(n_s, d)
    q_s, kc_s, vc_s, ks_s, vs_s, kw_s, vw_s, ng_s, su_s, mg_s = _front_sample(xs, prm["front"])
    kcc_s, vcc_s = _compress_sample(page_table, tok_minor(k_cmp), tok_minor(v_cmp), prm["cmp"])
    bc, bs, bw, ovl_s, e_s, n_sel_s = _sample_tables(prm["rel_bias"], past, t_new, win_buf)
    rows_s = N_HEADS * t_new
    eye_kv = jnp.eye(N_KV, dtype=BF16)
    q_rows = q_s.reshape(ns, t_new, N_KV, GROUP, HD).transpose(0, 2, 3, 1, 4)
    q_rows = jnp.einsum('skgtd,kj->skgtjd', q_rows, eye_kv).reshape(ns, rows_s, KVW)
    gate_s = ng_s[:, :n_gate].reshape(ns, t_new, N_KV, GROUP, N_BRANCH).transpose(0, 2, 3, 1, 4)
    gate_s = jnp.pad(gate_s.reshape(ns, rows_s, N_BRANCH), ((0, 0), (0, 0), (0, LANES - N_BRANCH)))
    pad8 = lambda a: jnp.pad(a.reshape(ns, t_new, KVW), ((0, 0), (0, 8 - t_new), (0, 0)))
    o_s = _nsa_sample(page_table, q_rows, gate_s, kcc_s, vcc_s,
                      tok_minor(k_sel), tok_minor(v_sel), pad8(ks_s), pad8(vs_s), tok_minor(k_win), tok_minor(v_win),
                      pad8(kw_s), pad8(vw_s), bc, bs, bw, ovl_s, e_s, n_sel_s, past, t_new)
    o_s = o_s.reshape(ns, N_KV, GROUP, t_new, N_KV, HD)
    o_s = jnp.stack([o_s[:, k, :, :, k, :] for k in range(N_KV)], axis=1)
    o_s = o_s.transpose(0, 3, 1, 2, 4).reshape(n_s, AW).astype(BF16)
    su_ts = su_s.reshape(ns, t_new, ssm_w).transpose(1, 0, 2).reshape(n_s, ssm_w)
    so_ts, sr, si = _ssm(su_ts, states[0].reshape(ns, nstate), states[1].reshape(ns, nstate), *prm["ssm"], ns, t_new)
    so_s = so_ts.reshape(t_new, ns, ssm_w).transpose(1, 0, 2).reshape(n_s, ssm_w)
    y = _back(xs, o_s, so_s, mg_s, p_l.reshape(n_s, -1), prm["back"], 1, n_s, n_s, prm["ff_chunk"])
    return dict(y=y, o=o_s, so=so_s, rows=(kc_s, vc_s, ks_s, vs_s, kw_s, vw_s), state=(sr, si))


def kernel(x_prompt, x_sample, cache_k_cmp, cache_v_cmp, cache_k_sel, cache_v_sel, cache_k_win, cache_v_win, state_ssm_re, state_ssm_im, page_table, p_prompt, p_sample, rel_bias, final_norm, attn_norm, w_in, cmp_pe_k, cmp_w1_k, cmp_w2_k, cmp_pe_v, cmp_w1_v, cmp_w2_v, ssm_a_re, ssm_a_im, ssm_log_dt, ssm_b_re, ssm_b_im, ssm_c_re, ssm_c_im, ssm_d, w_glu, b_glu, w_att_br, w_ssm_br, w_o, ffn_norm, w_ffn_gate, w_ffn_up, w_ffn_down, ple_norm, w_ple_gate, w_ple):
    assert w_in.shape[0] == 1, "single-layer trunk"
    l = 0
    nb, t, d = x_prompt.shape
    ns, t_new = x_sample.shape[:2]
    prm = _layer_params(rel_bias, final_norm, attn_norm, w_in, cmp_pe_k, cmp_w1_k, cmp_w2_k, cmp_pe_v, cmp_w1_v,
                        cmp_w2_v, ssm_a_re, ssm_a_im, ssm_log_dt, ssm_b_re, ssm_b_im, ssm_c_re, ssm_c_im, ssm_d,
                        w_glu, b_glu, w_att_br, w_ssm_br, w_o, ffn_norm, w_ffn_gate, w_ffn_up, w_ffn_down,
                        ple_norm, w_ple_gate, w_ple)
    pg = _prompt_group(x_prompt, p_prompt[l], prm)
    sg = _sample_group(x_sample, p_sample[l], page_table,
                       (cache_k_cmp[l], cache_v_cmp[l], cache_k_sel[l], cache_v_sel[l]),
                       (cache_k_win[l], cache_v_win[l]), (state_ssm_re[l], state_ssm_im[l]), prm)

    kv5 = lambda a, b_, t_: a.reshape(1, b_, t_, N_KV, HD)
    kv5_t = lambda a: jnp.transpose(a.reshape(1, nb, N_KV, HD, t), (0, 1, 4, 2, 3))
    keep = min(WINDOW, t)
    win_p = lambda a: a.reshape(nb, t, KVW)[:, t - keep:].reshape(1, nb, keep, N_KV, HD)
    win_s = lambda cache, new: jnp.concatenate(
        [cache[l], new.reshape(ns, t_new, N_KV, HD)], axis=1)[:, t_new:][None]
    st = lambda a, b_: a.reshape(1, b_, prm["n_groups"], SSM_P)
    kc_s, vc_s, ks_s, vs_s, kw_s, vw_s = sg["rows"]
    return (pg["y"].reshape(nb, t, d), sg["y"].reshape(ns, t_new, d),
            *[kv5_t(a) for a in pg["rows_t"]], win_p(pg["win"][0]), win_p(pg["win"][1]),
            st(pg["state"][0], nb), st(pg["state"][1], nb),
            kv5(kc_s, ns, t_new), kv5(vc_s, ns, t_new), kv5(ks_s, ns, t_new), kv5(vs_s, ns, t_new),
            win_s(cache_k_win, kw_s), win_s(cache_v_win, vw_s),
            st(sg["state"][0], ns), st(sg["state"][1], ns))
```

```python
import functools
import math

import numpy as np
import jax
import jax.numpy as jnp
from jax import lax
from jax.experimental import pallas as pl
from jax.experimental.pallas import tpu as pltpu

F32 = jnp.float32
BF16 = jnp.bfloat16

N_HEADS = 8
N_KV = 2
HD = 64
GROUP = N_HEADS // N_KV
N_BRANCH = 3
CMP_BLOCK = 32
CMP_STRIDE = 16
CMP_HIDDEN = 256
CMP_PITCH = 24
SEL_BLOCK = 64
SEL_TOPK = 16
WINDOW = 512
QB = 128
SEL_KEYS = 4 * QB
SEL_SPLIT = 2
N_BUCKETS = 32
REL_MAX_DIST = 128
SSM_CH = 16
SSM_P = 64
EPS = 1e-6
NEG = -1e30
NEG_TEST = -1e29
FORCE = 1e9
BIG = 1e30
LOG2E = math.log2(math.e)
LANES = 128
VMEM_LIMIT = 56 * 1024 * 1024
AW = N_HEADS * HD
KVW = N_KV * HD
NG_ROWS = 32


def _cparams(sem):
    return pltpu.CompilerParams(dimension_semantics=sem, vmem_limit_bytes=VMEM_LIMIT)


def _const_spec(shape):
    nd = len(shape)
    return pl.BlockSpec(shape, lambda *_: (0,) * nd)


def _rms(x, g):
    return x * lax.rsqrt(jnp.mean(x * x, axis=-1, keepdims=True) + EPS) * g


def _gelu(x):
    return x * (0.5 * (1.0 + jnp.tanh(math.sqrt(2.0 / math.pi) * (x + 0.044715 * (x * x * x)))))


def _sigmoid(x):
    return 1.0 / (1.0 + jnp.exp(-x))


def _dot(a, b):
    return jnp.dot(a, b, preferred_element_type=F32)


def _dot_t(a, b):
    return lax.dot_general(a, b, (((1,), (1,)), ((), ())), preferred_element_type=F32)


def _masked_softmax(s, axis=-1):
    valid = s > NEG_TEST
    m = jnp.max(s, axis=axis, keepdims=True)
    e = jnp.where(valid, jnp.exp(s - m), 0.0)
    return e / jnp.maximum(jnp.sum(e, axis=axis, keepdims=True), 1e-30)


def _softmax2_cols(s):
    m = jnp.max(s, axis=0, keepdims=True)
    e = jnp.exp2(s - m)
    inv = jnp.where(m > NEG_TEST, 1.0 / jnp.maximum(jnp.sum(e, axis=0, keepdims=True), 1e-30), 0.0)
    return e * inv


def _front_project(x_ref, g_ref, wa_ref, wng_ref, wsu_ref, wmg_ref, su_ref, mg_ref, q_scale):
    u = _rms(x_ref[...], g_ref[...]).astype(BF16)
    za = _dot(u, wa_ref[...])
    q = za[:, :AW] * q_scale
    rows = [za[:, AW + i * KVW: AW + (i + 1) * KVW] for i in range(6)]
    ng = _sigmoid(_dot(u, wng_ref[...]))
    su_ref[...] = _dot(u, wsu_ref[...]).astype(su_ref.dtype)
    mg_ref[...] = _sigmoid(_dot(u, wmg_ref[...])).astype(BF16)
    return q, rows, ng


def _front_sample_kernel(x_ref, g_ref, wa_ref, wng_ref, wsu_ref, wmg_ref,
                         q_ref, kc_ref, vc_ref, ks_ref, vs_ref, kw_ref, vw_ref, ng_ref, su_ref, mg_ref):
    q, rows, ng = _front_project(x_ref, g_ref, wa_ref, wng_ref, wsu_ref, wmg_ref, su_ref, mg_ref, HD ** -0.5)
    q_ref[...] = q.astype(BF16)
    for ref, r in zip((kc_ref, vc_ref, ks_ref, vs_ref, kw_ref, vw_ref), rows):
        ref[...] = r
    ng_ref[...] = ng


def _front_prompt_kernel(x_ref, g_ref, wa_ref, wng_ref, wsu_ref, wmg_ref,
                         qt_ref, kct_ref, vct_ref, kst_ref, vst32_ref, kc_ref, vc_ref, kw_ref, vw_ref,
                         ksb_ref, kwb_ref, vst_ref, vwt_ref, ngt_ref, su_ref, mg_ref):
    q, rows, ng = _front_project(x_ref, g_ref, wa_ref, wng_ref, wsu_ref, wmg_ref, su_ref, mg_ref,
                                 HD ** -0.5 * LOG2E)
    kc, vc, ks, vs, kw, vw = rows
    qt_ref[0] = q.T.astype(BF16)
    for ref, r in zip((kct_ref, vct_ref, kst_ref, vst32_ref), (kc, vc, ks, vs)):
        ref[0] = r.T
    kc_ref[...] = kc
    vc_ref[...] = vc
    kw_ref[...] = kw
    vw_ref[...] = vw
    ksb_ref[...] = ks.astype(BF16)
    kwb_ref[...] = kw.astype(BF16)
    for ref, r in ((vst_ref, vs), (vwt_ref, vw)):
        rt = r.T.astype(BF16)
        for j in range(rt.shape[1] // QB):
            ref[0, j] = rt[:, j * QB:(j + 1) * QB]
    ngt_ref[0] = ng.T[0:NG_ROWS]


def _front_sample(x2d, fw):
    g, wa, wng, wsu, wmg = fw
    n, d = x2d.shape
    sw, mw = wsu.shape[1], wmg.shape[1]
    shapes = ([jax.ShapeDtypeStruct((n, AW), BF16)] + [jax.ShapeDtypeStruct((n, KVW), F32)] * 6
              + [jax.ShapeDtypeStruct((n, LANES), F32), jax.ShapeDtypeStruct((n, sw), BF16),
                 jax.ShapeDtypeStruct((n, mw), BF16)])
    return pl.pallas_call(
        _front_sample_kernel,
        grid=(1,),
        in_specs=[_const_spec(a.shape) for a in (x2d, g, wa, wng, wsu, wmg)],
        out_specs=[_const_spec(s.shape) for s in shapes],
        out_shape=shapes,
        compiler_params=_cparams(("arbitrary",)),
        name="front_sample",
    )(x2d, g, wa, wng, wsu, wmg)


def _front_prompt(x2d, nb, t, fw, tm):
    g, wa, wng, wsu, wmg = fw
    n, d = x2d.shape
    nt = t // tm
    sw, mw = wsu.shape[1], wmg.shape[1]
    row = lambda b, i: (b * nt + i, 0)
    kv5 = jax.ShapeDtypeStruct((nb, KVW, t), F32)
    kv5_spec = pl.BlockSpec((1, KVW, tm), lambda b, i: (b, 0, i))
    vt = jax.ShapeDtypeStruct((nb, t // QB, KVW, QB), BF16)
    vt_spec = pl.BlockSpec((1, tm // QB, KVW, QB), lambda b, i: (b, i, 0, 0))
    shapes = ([jax.ShapeDtypeStruct((nb, AW, t), BF16)] + [kv5] * 4 + [jax.ShapeDtypeStruct((n, KVW), F32)] * 4
              + [jax.ShapeDtypeStruct((n, KVW), BF16)] * 2 + [vt] * 2
              + [jax.ShapeDtypeStruct((nb, NG_ROWS, t), F32), jax.ShapeDtypeStruct((t, nb * sw), BF16),
                 jax.ShapeDtypeStruct((n, mw), BF16)])
    specs = ([pl.BlockSpec((1, AW, tm), lambda b, i: (b, 0, i))] + [kv5_spec] * 4
             + [pl.BlockSpec((tm, KVW), row)] * 6 + [vt_spec] * 2
             + [pl.BlockSpec((1, NG_ROWS, tm), lambda b, i: (b, 0, i)),
                pl.BlockSpec((tm, sw), lambda b, i: (i, b)), pl.BlockSpec((tm, mw), row)])
    return pl.pallas_call(
        _front_prompt_kernel,
        grid=(nb, nt),
        in_specs=[pl.BlockSpec((tm, d), row)] + [_const_spec(a.shape) for a in (g, wa, wng, wsu, wmg)],
        out_specs=specs,
        out_shape=shapes,
        compiler_params=_cparams(("parallel", "parallel")),
        name="front_prompt",
    )(x2d, g, wa, wng, wsu, wmg)


def _chunk_rows(load, r0, rn, pitch=CMP_STRIDE):
    return jnp.concatenate([load(pl.ds(pitch * r0 + r, rn, stride=pitch)) for r in range(CMP_STRIDE)], axis=1)


def _compress_compute(load_rows, c, w1_ref, w2_ref, pe_ref, a_scr):
    rc_n = min(c, 256)
    lo = lax.broadcasted_iota(jnp.int32, (rc_n, LANES), 1) < HD
    w1 = w1_ref[...]
    for rc in range(c // rc_n):
        x = load_rows(rc * rc_n, rc_n)
        cols = [x[:, r * LANES:(r + 1) * LANES] for r in range(CMP_STRIDE)]
        rol = [pltpu.roll(col, HD, 1) for col in cols]
        for kh in range(N_KV):
            if kh == 0:
                parts = [jnp.where(lo, cols[2 * j], rol[2 * j + 1]) for j in range(CMP_STRIDE // 2)]
            else:
                parts = [jnp.where(lo, rol[2 * j], cols[2 * j + 1]) for j in range(CMP_STRIDE // 2)]
            xh = jnp.concatenate(parts, axis=1).astype(BF16)
            a_scr[kh, rc * rc_n:(rc + 1) * rc_n, :] = _dot(xh, w1)
    pw = _dot(pe_ref[...], w1)
    peb = pw[0:1, :CMP_HIDDEN] + pw[1:2, CMP_HIDDEN:]
    w2 = w2_ref[...]
    outs = []
    for kh in range(N_KV):
        a = a_scr[kh]
        hid = a[:, :CMP_HIDDEN] + pltpu.roll(a[:, CMP_HIDDEN:], c - 1, 0) + peb
        outs.append(_dot(_gelu(hid).astype(BF16), w2))
    return jnp.concatenate(outs, axis=1)


def _compress_prompt_kernel(xk_ref, xv_ref, w1k_ref, w2k_ref, pek_ref, w1v_ref, w2v_ref, pev_ref,
                            ok_ref, ovt_ref, a_scr):
    c = xk_ref.shape[1] // CMP_STRIDE
    ok_ref[0] = _compress_compute(lambda r0, rn: _chunk_rows(lambda idx: xk_ref[0, idx, :], r0, rn), c,
                                  w1k_ref, w2k_ref, pek_ref, a_scr).astype(BF16)
    ovt_ref[0] = _compress_compute(lambda r0, rn: _chunk_rows(lambda idx: xv_ref[0, idx, :], r0, rn), c,
                                   w1v_ref, w2v_ref, pev_ref, a_scr).T.astype(BF16)


def _compress_prompt(xk, xv, cw):
    nb, t, kvw = xk.shape
    c = t // CMP_STRIDE
    wspecs = [_const_spec(w.shape) for w in cw]
    blk = pl.BlockSpec((1, t, kvw), lambda b: (b, 0, 0))
    return pl.pallas_call(
        _compress_prompt_kernel,
        grid=(nb,),
        in_specs=[blk, blk] + wspecs,
        out_specs=[pl.BlockSpec((1, c, KVW), lambda b: (b, 0, 0)), pl.BlockSpec((1, KVW, c), lambda b: (b, 0, 0))],
        out_shape=[jax.ShapeDtypeStruct((nb, c, KVW), BF16), jax.ShapeDtypeStruct((nb, KVW, c), BF16)],
        scratch_shapes=[pltpu.VMEM((N_KV, c, 2 * CMP_HIDDEN), F32)],
        compiler_params=_cparams(("parallel",)),
        name="compress_prompt",
    )(xk, xv, *cw)


def _page_copy(pool, buf, sem, page, p, slot):
    return pltpu.make_async_copy(pool.at[page], buf.at[slot, p], sem)


def _page_gather_start(pt_ref, seq, pools, bufs, sems, slot, n_pages):
    def body(p, carry):
        page = pt_ref[seq, p]
        for i, (pool, buf) in enumerate(zip(pools, bufs)):
            _page_copy(pool, buf, sems.at[i, slot], page, p, slot).start()
        return carry
    lax.fori_loop(0, n_pages, body, 0)


def _page_gather_wait(pools, bufs, sems, slot, n_pages):
    def body(p, carry):
        for i, (pool, buf) in enumerate(zip(pools, bufs)):
            _page_copy(pool, buf, sems.at[i, slot], 0, p, slot).wait()
        return carry
    lax.fori_loop(0, n_pages, body, 0)


def _paged_prefetch(pt_ref, pools, bufs, sems, n_pages):
    s = pl.program_id(0)
    slot = s % 2

    @pl.when(s == 0)
    def _():
        _page_gather_start(pt_ref, 0, pools, bufs, sems, 0, n_pages)

    @pl.when(s + 1 < pl.num_programs(0))
    def _():
        _page_gather_start(pt_ref, s + 1, pools, bufs, sems, 1 - slot, n_pages)

    _page_gather_wait(pools, bufs, sems, slot, n_pages)
    return slot


def _compress_sample_kernel(pt_ref, kpool, vpool, w1k_ref, w2k_ref, pek_ref, w1v_ref, w2v_ref, pev_ref,
                            ok_ref, ov_ref, kbuf, vbuf, sems, a_scr, rows_scr):
    n_pages = pt_ref.shape[1]
    page = kpool.shape[2]
    c = n_pages * page // CMP_STRIDE
    slot = _paged_prefetch(pt_ref, (kpool, vpool), (kbuf, vbuf), sems, n_pages)
    for buf, out_ref, w1_ref, w2_ref, pe_ref in ((kbuf, ok_ref, w1k_ref, w2k_ref, pek_ref),
                                                (vbuf, ov_ref, w1v_ref, w2v_ref, pev_ref)):
        for p in range(n_pages):
            rows = buf[slot, p].T
            for ch in range(page // CMP_STRIDE):
                r0 = (p * (page // CMP_STRIDE) + ch) * CMP_PITCH
                rows_scr[r0:r0 + CMP_STRIDE, :] = rows[ch * CMP_STRIDE:(ch + 1) * CMP_STRIDE]
        out_ref[0] = _compress_compute(
            lambda r0, rn: _chunk_rows(lambda idx: rows_scr[idx, :], r0, rn, CMP_PITCH), c,
            w1_ref, w2_ref, pe_ref, a_scr).astype(BF16)


def _compress_sample(page_table, kpool, vpool, cw):
    ns, n_pages = page_table.shape
    width, page = kpool.shape[1:]
    tokens = n_pages * page
    c = tokens // CMP_STRIDE
    buf_shape = (2, n_pages, width, page)
    any_spec = pl.BlockSpec(memory_space=pl.ANY)
    wspecs = [pl.BlockSpec(w.shape, lambda s, pt, nd=w.ndim: (0,) * nd) for w in cw]
    oblk = pl.BlockSpec((1, c, KVW), lambda s, pt: (s, 0, 0))
    return pl.pallas_call(
        _compress_sample_kernel,
        grid_spec=pltpu.PrefetchScalarGridSpec(
            num_scalar_prefetch=1,
            grid=(ns,),
            in_specs=[any_spec, any_spec] + wspecs,
            out_specs=[oblk, oblk],
            scratch_shapes=[pltpu.VMEM(buf_shape, F32), pltpu.VMEM(buf_shape, F32), pltpu.SemaphoreType.DMA((2, 2)),
                            pltpu.VMEM((N_KV, c, 2 * CMP_HIDDEN), F32), pltpu.VMEM((c * CMP_PITCH, width), F32)]),
        out_shape=[jax.ShapeDtypeStruct((ns, c, KVW), BF16)] * 2,
        compiler_params=_cparams(("arbitrary",)),
        name="compress_sample",
    )(page_table, kpool, vpool, *cw)


def _rank_select(score, blk, n_real, axis):
    size = 8 if axis == 0 else LANES
    total = score.shape[axis]
    chunk = (lambda a, c: a[c * size:(c + 1) * size]) if axis == 0 else (lambda a, c: a[:, c * size:(c + 1) * size])
    n_chunks = -(-total // size)
    sc = [chunk(score, c) for c in range(n_chunks)]
    bl = [chunk(blk, c) for c in range(n_chunks)]
    rank = [jnp.zeros(s.shape, F32) for s in sc]
    for kk in range(n_real):
        col = score[kk:kk + 1, :] if axis == 0 else score[:, kk:kk + 1]
        for c in range(n_chunks):
            other = jnp.broadcast_to(col, sc[c].shape)
            if c * size > kk:
                beats = other >= sc[c]
            elif min((c + 1) * size, total) - 1 < kk:
                beats = other > sc[c]
            else:
                beats = (other > sc[c]) | ((other == sc[c]) & (bl[c] > kk))
            rank[c] = rank[c] + jnp.where(beats, 1.0, 0.0)
    return jnp.where(jnp.concatenate(rank, axis=axis) < SEL_TOPK, 1.0, 0.0)


def _block_scores(imp, blk, t):
    cur = t // SEL_BLOCK
    forced = (blk == 0) | (blk == cur) | (blk == cur - 1)
    valid = blk * SEL_BLOCK <= t
    return jnp.where(valid, jnp.where(forced, FORCE, imp), NEG)


def _nsa_prompt_kernel(qt_ref, ngt_ref, kc_ref, vct_ref, ks_ref, vst_ref, kw_ref, vwt_ref,
                       ut_ref, at_ref, ovlt_ref, stat_ref, crow_ref, o_ref, *, n_sel):
    ib = pl.program_id(1)
    qt = qt_ref[0]
    ngt = ngt_ref[0]
    ncp = kc_ref.shape[1]
    cols = GROUP * QB
    sel_rows = LANES // 2
    t_row = ib * QB + lax.broadcasted_iota(jnp.int32, (n_sel, QB), 1)
    blk_t = lax.broadcasted_iota(jnp.int32, (n_sel, QB), 0)
    zeros_q = jnp.zeros((HD, cols), F32)
    vrows = [slice(k * HD, (k + 1) * HD) for k in range(N_KV)]
    q_sel, q_win, o_c, o_w = [], [], [], []
    for k in range(N_KV):
        qk = jnp.concatenate([qt[(GROUP * k + g) * HD:(GROUP * k + g + 1) * HD, :] for g in range(GROUP)],
                             axis=1).astype(F32)
        qa = jnp.concatenate([qk, zeros_q] if k == 0 else [zeros_q, qk], axis=0)

        bias_c = ut_ref[k, pl.ds(pl.multiple_of(ncp - (QB // CMP_STRIDE) * ib, 8), ncp), :]
        p_c = _softmax2_cols(_dot(kc_ref[0], qa.astype(BF16)) + bias_c)
        o_c.append(_dot(vct_ref[0][vrows[k], :], p_c.astype(BF16)))
        psum = p_c[:, 0:QB]
        for g in range(1, GROUP):
            psum = psum + p_c[:, g * QB:(g + 1) * QB]
        psum_hi = psum.astype(BF16)
        psum_lo = (psum - psum_hi.astype(F32)).astype(BF16)
        imp = _dot(ovlt_ref[...], psum_hi) + _dot(ovlt_ref[...], psum_lo)

        sel = _rank_select(_block_scores(imp[0:n_sel], blk_t, t_row), blk_t, n_sel, 0)
        selm1 = jnp.concatenate([sel - 1.0] * GROUP, axis=1)
        if n_sel < sel_rows:
            selm1 = jnp.concatenate([selm1, jnp.zeros((sel_rows - n_sel, cols), F32)], axis=0)
        tail = jnp.concatenate([crow_ref[k], jnp.zeros((LANES - sel_rows - 8, cols), F32)], axis=0)
        q_sel.append(jnp.concatenate([qa, selm1, tail], axis=0).astype(BF16))
        q_win.append(jnp.concatenate([qa, jnp.zeros((sel_rows, cols), F32), tail], axis=0).astype(BF16))

        s_parts, tiles_j = [], []
        for w, tidx in enumerate((0, 1, None, None, 3)):
            jt = ib - w
            jc = jnp.maximum(jt, 0)
            k0 = pl.multiple_of(jc * QB, QB)
            lhs = jnp.concatenate([kw_ref[0, pl.ds(k0, QB), :], stat_ref[pl.ds(k0, QB), :]], axis=1)
            s = _dot(lhs, q_win[k])
            if tidx is not None:
                s = s + at_ref[tidx, k]
            s_parts.append(jnp.where(jt >= 0, s, NEG))
            tiles_j.append(jc)
        p_w = _softmax2_cols(jnp.concatenate(s_parts, axis=0)).astype(BF16)
        o_wk = jnp.zeros((HD, cols), F32)
        for w, jc in enumerate(tiles_j):
            o_wk = o_wk + _dot(vwt_ref[0, jc, vrows[k], :], p_w[w * QB:(w + 1) * QB])
        o_w.append(o_wk)

    tiles_per_step = SEL_KEYS // QB

    ones_rows = jnp.where(lax.broadcasted_iota(jnp.int32, (16, SEL_KEYS), 0) == 0, 1.0, 0.0).astype(BF16)
    step_tiles = SEL_SPLIT * tiles_per_step

    def step_tile_ids(jp, sp):
        j0 = (jp * SEL_SPLIT + sp) * tiles_per_step
        return [j0 + h for h in range(tiles_per_step)]

    def step_scores(jp):
        out = []
        for sp in range(SEL_SPLIT):
            k0 = pl.multiple_of(step_tile_ids(jp, sp)[0] * QB, SEL_KEYS)
            lhs = jnp.concatenate([ks_ref[0, pl.ds(k0, SEL_KEYS), :], stat_ref[pl.ds(k0, SEL_KEYS), :]], axis=1)
            out += [_dot(lhs, q_sel[k]) for k in range(N_KV)]
        return tuple(out)

    def step_near_bias(jp, scores):
        out = []
        for sp in range(SEL_SPLIT):
            tidx = [jnp.where(jt == ib, 0, jnp.where(jt == ib - 1, 1, jnp.where(jt < ib, 2, 4)))
                    for jt in step_tile_ids(jp, sp)]
            out += [scores[sp * N_KV + k] + jnp.concatenate([at_ref[ti, k] for ti in tidx], axis=0)
                    for k in range(N_KV)]
        return tuple(out)

    def step_update(jp, state, scores):
        stats = []
        for (m, _), s in zip(state, scores):
            m_new = jnp.maximum(m, jnp.max(s, axis=0, keepdims=True))
            stats.append((m_new, jnp.exp2(m - m_new), jnp.exp2(s - m_new).astype(BF16)))
        vts = [jnp.concatenate([jnp.concatenate([vst_ref[0, jt, vrows[k], :] for jt in step_tile_ids(jp, sp)], axis=1),
                                ones_rows], axis=0)
               for sp in range(SEL_SPLIT) for k in range(N_KV)]
        return tuple((m_new, alpha * acc + _dot(vt, p))
                     for (_, acc), (m_new, alpha, p), vt in zip(state, stats, vts))

    def far_body(jp, state):
        return step_update(jp, state, step_scores(jp))

    def near_body(jp, state):
        return step_update(jp, state, step_near_bias(jp, step_scores(jp)))

    init = (jnp.full((1, cols), NEG, F32), jnp.zeros((HD + ones_rows.shape[0], cols), F32))
    n_far = jnp.maximum(ib - 1, 0) // step_tiles
    sel_state = lax.fori_loop(0, n_far, far_body, (init,) * (N_KV * SEL_SPLIT))
    sel_state = lax.fori_loop(n_far, (ib + step_tiles) // step_tiles, near_body, sel_state)

    out_rows = []
    for k in range(N_KV):
        parts = [sel_state[sp * N_KV + k] for sp in range(SEL_SPLIT)]
        m_s = parts[0][0]
        for m_p, _ in parts[1:]:
            m_s = jnp.maximum(m_s, m_p)
        acc_s = jnp.zeros(init[1].shape, F32)
        for m_p, acc_p in parts:
            acc_s = acc_s + jnp.exp2(m_p - m_s) * acc_p
        o_s = acc_s[0:HD] / jnp.maximum(acc_s[HD:HD + 1], 1e-30)

        def gate_row(br):
            return jnp.concatenate([ngt[(GROUP * k + g) * N_BRANCH + br:(GROUP * k + g) * N_BRANCH + br + 1, :]
                                    for g in range(GROUP)], axis=1)
        o_k = gate_row(0) * o_c[k] + gate_row(1) * o_s + gate_row(2) * o_w[k]
        out_rows += [o_k[:, g * QB:(g + 1) * QB] for g in range(GROUP)]
    o_ref[0] = jnp.concatenate(out_rows, axis=0).T.astype(BF16)


def _nsa_prompt(qt, ngt, kc, vct, ks, vst, kw, vwt, tables):
    nb, aw, t = qt.shape
    nq = t // QB
    ncp = kc.shape[1]
    full3 = lambda b, i: (b, 0, 0)
    full4 = lambda b, i: (b, 0, 0, 0)
    return pl.pallas_call(
        functools.partial(_nsa_prompt_kernel, n_sel=t // SEL_BLOCK),
        grid=(nb, nq),
        in_specs=[pl.BlockSpec((1, aw, QB), lambda b, i: (b, 0, i)),
                  pl.BlockSpec((1, NG_ROWS, QB), lambda b, i: (b, 0, i)),
                  pl.BlockSpec((1, ncp, KVW), full3), pl.BlockSpec((1, KVW, ncp), full3),
                  pl.BlockSpec((1, t, KVW), full3), pl.BlockSpec((1, nq, KVW, QB), full4),
                  pl.BlockSpec((1, t, KVW), full3), pl.BlockSpec((1, nq, KVW, QB), full4)]
                 + [_const_spec(a.shape) for a in tables],
        out_specs=pl.BlockSpec((1, QB, aw), lambda b, i: (b, i, 0)),
        out_shape=jax.ShapeDtypeStruct((nb, t, aw), BF16),
        compiler_params=_cparams(("parallel", "arbitrary")),
        name="nsa_prompt",
    )(qt, ngt, kc, vct, ks, vst, kw, vwt, *tables)


def _nsa_sample_kernel(pt_ref, q_ref, gate_ref, kc_ref, vc_ref, kpool, vpool, ksn_ref, vsn_ref,
                       kwin_ref, vwin_ref, kwn_ref, vwn_ref, bc_ref, bs_ref, bw_ref, ovl_ref, e_ref,
                       o_ref, kbuf, vbuf, sems, *, n_sel, past, t_new):
    n_pages = pt_ref.shape[1]
    wb = kwin_ref.shape[2]
    slot = _paged_prefetch(pt_ref, (kpool, vpool), (kbuf, vbuf), sems, n_pages)
    past_t = lambda buf: jnp.concatenate([buf[slot, p] for p in range(n_pages)], axis=1).astype(BF16)
    q = q_ref[0]
    nb_past = past // SEL_BLOCK
    pad_new = jnp.zeros((QB - ksn_ref.shape[1], KVW), F32)
    new_tile = lambda ref: jnp.concatenate([ref[0], pad_new], axis=0).astype(BF16)

    p_c = _masked_softmax(_dot_t(q, kc_ref[0]) + bc_ref[...])
    o_c = _dot(p_c.astype(BF16), vc_ref[0])
    parts = []
    for k in range(N_KV):
        base = k * GROUP * t_new
        ps = p_c[base:base + t_new]
        for g in range(1, GROUP):
            ps = ps + p_c[base + g * t_new:base + (g + 1) * t_new]
        parts.append(ps)
    psum = jnp.concatenate(parts, axis=0)
    psum_hi = psum.astype(BF16)
    psum_lo = (psum - psum_hi.astype(F32)).astype(BF16)
    imp = _dot(psum_hi, ovl_ref[...]) + _dot(psum_lo, ovl_ref[...])
    blk = lax.broadcasted_iota(jnp.int32, imp.shape, 1)
    tpos = past + lax.broadcasted_iota(jnp.int32, imp.shape, 0) % t_new
    sel = _rank_select(_block_scores(imp, blk, tpos), blk, n_sel, 1)
    sel = jnp.concatenate([sel[k * t_new:(k + 1) * t_new] for k in range(N_KV) for _ in range(GROUP)], axis=0)

    mask_add = _dot((sel[:, 0:LANES] - 1.0).astype(BF16), e_ref[...])
    s_past = _dot(q, past_t(kbuf)) + bs_ref[:, 0:past] + mask_add
    s_new = _dot_t(q, new_tile(ksn_ref)) + bs_ref[:, past:]
    s_new = jnp.where(sel[:, nb_past:nb_past + 1] > 0.5, s_new, NEG)
    p_s = _masked_softmax(jnp.concatenate([s_past, s_new], axis=1)).astype(BF16)
    o_s = _dot_t(p_s[:, 0:past], past_t(vbuf)) + _dot(p_s[:, past:], new_tile(vsn_ref))

    s_w = jnp.concatenate([_dot(q, kwin_ref[0].astype(BF16)), _dot_t(q, new_tile(kwn_ref))], axis=1)
    p_w = _masked_softmax(s_w + bw_ref[...]).astype(BF16)
    o_w = _dot_t(p_w[:, 0:wb], vwin_ref[0].astype(BF16)) + _dot(p_w[:, wb:], new_tile(vwn_ref))

    gate = gate_ref[0]
    o_ref[0] = gate[:, 0:1] * o_c + gate[:, 1:2] * o_s + gate[:, 2:3] * o_w


def _nsa_sample(page_table, q, gate, kc, vc, kpool, vpool, ksn, vsn, kwin, vwin, kwn, vwn,
                bc, bs, bw, ovl, e, n_sel, past, t_new):
    ns, n_pages = page_table.shape
    rows, kvw = q.shape[1:]
    buf_shape = (2, n_pages) + kpool.shape[1:]
    seq3 = lambda s, pt: (s, 0, 0)
    any_spec = pl.BlockSpec(memory_space=pl.ANY)
    cs = lambda a: pl.BlockSpec(a.shape, lambda s, pt, nd=a.ndim: (0,) * nd, pipeline_mode=pl.Buffered(1))
    per_seq = lambda a: pl.BlockSpec((1,) + a.shape[1:], seq3)
    return pl.pallas_call(
        functools.partial(_nsa_sample_kernel, n_sel=n_sel, past=past, t_new=t_new),
        grid_spec=pltpu.PrefetchScalarGridSpec(
            num_scalar_prefetch=1,
            grid=(ns,),
            in_specs=[per_seq(q), per_seq(gate), per_seq(kc), per_seq(vc), any_spec, any_spec,
                      per_seq(ksn), per_seq(vsn), per_seq(kwin), per_seq(vwin), per_seq(kwn), per_seq(vwn),
                      cs(bc), cs(bs), cs(bw), cs(ovl), cs(e)],
            out_specs=pl.BlockSpec((1, rows, kvw), seq3),
            scratch_shapes=[pltpu.VMEM(buf_shape, F32), pltpu.VMEM(buf_shape, F32), pltpu.SemaphoreType.DMA((2, 2))]),
        out_shape=jax.ShapeDtypeStruct((ns, rows, kvw), F32),
        compiler_params=_cparams(("arbitrary",)),
        name="nsa_sample",
    )(page_table, q, gate, kc, vc, kpool, vpool, ksn, vsn, kwin, vwin, kwn, vwn, bc, bs, bw, ovl, e)


def _ssm_param_kernel(ar_ref, ai_ref, ldt_ref, br_ref, bi_ref, abr_ref, abi_ref, bbr_ref, bbi_ref):
    ar = ar_ref[...]
    ai = ai_ref[...]
    dt = jnp.exp(ldt_ref[...])
    mag = jnp.exp(ar * dt)
    abr = mag * jnp.cos(ai * dt)
    abi = mag * jnp.sin(ai * dt)
    den = ar * ar + ai * ai
    nr, ni = abr - 1.0, abi
    fr = (nr * ar + ni * ai) / den
    fi = (ni * ar - nr * ai) / den
    abr_ref[...] = abr
    abi_ref[...] = abi
    for g in range(ar.shape[0]):
        br = br_ref[g]
        bi = bi_ref[g]
        frg = fr[g:g + 1, :]
        fig = fi[g:g + 1, :]
        bbr_ref[g] = frg * br - fig * bi
        bbi_ref[g] = frg * bi + fig * br


def _ssm_params(a_re, a_im, log_dt, b_re_t, b_im_t):
    g, p = a_re.shape
    return pl.pallas_call(
        _ssm_param_kernel,
        out_shape=[jax.ShapeDtypeStruct((g, p), F32)] * 2 + [jax.ShapeDtypeStruct(b_re_t.shape, F32)] * 2,
        name="ssm_params",
    )(a_re, a_im, log_dt.reshape(g, 1), b_re_t, b_im_t)


def _ssm_kernel(u_ref, h0r_ref, h0i_ref, ar_ref, ai_ref, bd_ref, cd_ref, d_ref, wglu_ref, bglu_ref,
                so_ref, hr_ref, hi_ref, xr_scr, xi_scr, *slab_scr, bt):
    i = pl.program_id(0)
    n_slab = bd_ref.shape[0]
    width = n_slab * LANES
    sw = bd_ref.shape[2] // 2
    if slab_scr:
        slab = slab_scr[0]
        tt = u_ref.shape[0]
        rows = tt * bt
        u_wide = u_ref[...].astype(F32)
        for b in range(bt):
            for sl in range(n_slab):
                lanes = slice(b * width + sl * LANES, b * width + (sl + 1) * LANES)
                slab[sl, pl.ds(b, tt, stride=bt), :] = u_wide[:, lanes]
        u_slabs = [slab[sl] for sl in range(n_slab)]
    else:
        rows = u_ref.shape[0]
        u_rows = u_ref[...].astype(F32)
        u_slabs = [u_rows[:, sl * LANES:(sl + 1) * LANES] for sl in range(n_slab)]

    @pl.when(i == 0)
    def _():
        hr_ref[...] = h0r_ref[...]
        hi_ref[...] = h0i_ref[...]

    for sl in range(n_slab):
        x = _dot(u_slabs[sl].astype(BF16), bd_ref[sl])
        xr_scr[:, sl * sw:(sl + 1) * sw] = x[:, :sw]
        xi_scr[:, sl * sw:(sl + 1) * sw] = x[:, sw:]

    per = 8 // math.gcd(bt, 8)
    grp = per * bt
    lc = 512
    for c0 in range(0, xr_scr.shape[1], lc):
        cl = slice(c0, c0 + lc)
        a_r = jnp.broadcast_to(ar_ref[:, cl], (bt, lc))
        a_i = jnp.broadcast_to(ai_ref[:, cl], (bt, lc))

        def step(j, carry):
            h_r, h_i = carry
            r0 = pl.multiple_of(j * grp, grp)
            xr = xr_scr[pl.ds(r0, grp), cl]
            xi = xi_scr[pl.ds(r0, grp), cl]
            out_r, out_i = [], []
            for s in range(per):
                n_r = a_r * h_r - a_i * h_i + xr[s * bt:(s + 1) * bt]
                n_i = a_r * h_i + a_i * h_r + xi[s * bt:(s + 1) * bt]
                h_r, h_i = n_r, n_i
                out_r.append(h_r)
                out_i.append(h_i)
            xr_scr[pl.ds(r0, grp), cl] = jnp.concatenate(out_r, axis=0) if per > 1 else out_r[0]
            xi_scr[pl.ds(r0, grp), cl] = jnp.concatenate(out_i, axis=0) if per > 1 else out_i[0]
            return h_r, h_i

        h_r, h_i = lax.fori_loop(0, rows // grp, step, (hr_ref[:, cl], hi_ref[:, cl]))
        hr_ref[:, cl] = h_r
        hi_ref[:, cl] = h_i

    ys = []
    for sl in range(n_slab):
        hcat = jnp.concatenate([xr_scr[:, sl * sw:(sl + 1) * sw], xi_scr[:, sl * sw:(sl + 1) * sw]], axis=1)
        ys.append(_dot(hcat.astype(BF16), cd_ref[sl]))
    y = jnp.concatenate(ys, axis=1) + d_ref[...] * jnp.concatenate(u_slabs, axis=1)
    z = _gelu(y)
    so = z * _sigmoid(_dot(z.astype(BF16), wglu_ref[...]) + bglu_ref[...])
    if slab_scr:
        for sl in range(n_slab):
            slab[sl] = so[:, sl * LANES:(sl + 1) * LANES]
        for b in range(bt):
            for sl in range(n_slab):
                lanes = slice(b * width + sl * LANES, b * width + (sl + 1) * LANES)
                so_ref[:, lanes] = slab[sl, pl.ds(b, tt, stride=bt), :].astype(so_ref.dtype)
    else:
        so_ref[...] = so.astype(so_ref.dtype)


def _ssm(u, h0r, h0i, ar, ai, bd, cd, dvec, wglu, bglu, bt, tt, time_major):
    width = bd.shape[0] * LANES
    rows = tt * bt
    nstate = ar.shape[1]
    blk = (tt, bt * width) if time_major else (rows, width)
    cst = [_const_spec(a.shape) for a in (h0r, h0i, ar, ai, bd, cd, dvec, wglu, bglu)]
    st_spec = _const_spec((bt, nstate))
    scratch = [pltpu.VMEM((rows, nstate), F32), pltpu.VMEM((rows, nstate), F32)]
    if time_major:
        scratch.append(pltpu.VMEM((bd.shape[0], rows, LANES), F32))
    return pl.pallas_call(
        functools.partial(_ssm_kernel, bt=bt),
        grid=(u.shape[0] // blk[0],),
        in_specs=[pl.BlockSpec(blk, lambda i: (i, 0))] + cst,
        out_specs=[pl.BlockSpec(blk, lambda i: (i, 0)), st_spec, st_spec],
        out_shape=[jax.ShapeDtypeStruct(u.shape, BF16), jax.ShapeDtypeStruct((bt, nstate), F32),
                   jax.ShapeDtypeStruct((bt, nstate), F32)],
        scratch_shapes=scratch,
        compiler_params=_cparams(("arbitrary",)),
        name="ssm",
    )(u, h0r, h0i, ar, ai, bd, cd, dvec, wglu, bglu)


def _back_kernel(h_ref, o_ref, so_ref, mg_ref, p_ref, watt_ref, wssm_ref, wo_ref, fn_ref, wg_ref, wu_ref, wd_ref,
                 pn_ref, wpg_ref, wple_ref, fin_ref, y_ref, *, ff_chunk):
    d = h_ref.shape[1]
    a = _dot(o_ref[...], watt_ref[...])
    s = _dot(so_ref[...], wssm_ref[...])
    mg = mg_ref[...].astype(F32)
    h = h_ref[...] + _dot((mg[:, :d] * a + mg[:, d:] * s).astype(BF16), wo_ref[...])
    f = _rms(h, fn_ref[...]).astype(BF16)
    ffn = jnp.zeros_like(h)
    for c0 in range(0, wg_ref.shape[1], ff_chunk):
        gate = _dot(f, wg_ref[:, c0:c0 + ff_chunk])
        up = _dot(f, wu_ref[:, c0:c0 + ff_chunk])
        ffn = ffn + _dot((gate * _sigmoid(gate) * up).astype(BF16), wd_ref[c0:c0 + ff_chunk, :])
    h = h + ffn
    g = _sigmoid(_dot(_rms(h, pn_ref[...]).astype(BF16), wpg_ref[...]))
    h = h + g * _dot(p_ref[...].astype(BF16), wple_ref[...])
    y_ref[...] = _rms(h, fin_ref[...])


def _back(h2d, o2d, so_tb, mg, p2d, weights, nb, t, tm, ff_chunk):
    n, d = h2d.shape
    nt = t // tm
    row = lambda b, i: (b * nt + i, 0)
    wspecs = [pl.BlockSpec(w.shape, lambda b, i, nd=w.ndim: (0,) * nd, pipeline_mode=pl.Buffered(1))
              for w in weights]
    sw = o2d.shape[1]
    return pl.pallas_call(
        functools.partial(_back_kernel, ff_chunk=ff_chunk),
        grid=(nb, nt),
        in_specs=[pl.BlockSpec((tm, d), row), pl.BlockSpec((tm, sw), row),
                  pl.BlockSpec((tm, sw), lambda b, i: (i, b)),
                  pl.BlockSpec((tm, mg.shape[1]), row), pl.BlockSpec((tm, p2d.shape[1]), row)] + wspecs,
        out_specs=pl.BlockSpec((tm, d), row),
        out_shape=jax.ShapeDtypeStruct((n, d), F32),
        compiler_params=_cparams(("parallel", "parallel")),
        name="back",
    )(h2d, o2d, so_tb, mg, p2d, *weights)


def _bucket_np(dist):
    n = np.maximum(dist, 0)
    exact = N_BUCKETS // 2
    nf = np.maximum(n, 1).astype(np.float64)
    large = exact + (np.log(nf / exact) / math.log(REL_MAX_DIST / exact) * (N_BUCKETS - exact)).astype(np.int64)
    return np.where(n < exact, n, np.minimum(large, N_BUCKETS - 1)).astype(np.int32)


def _bias_table(rel_bias, dist, valid, offset=None):
    onehot = jax.nn.one_hot(jnp.asarray(_bucket_np(dist)), N_BUCKETS, dtype=F32)
    b = jnp.einsum('...b,bh->h...', onehot, rel_bias.astype(F32), precision=lax.Precision.HIGHEST)
    if offset is not None:
        b = b - offset.reshape((N_HEADS,) + (1,) * dist.ndim)
    return jnp.where(jnp.asarray(valid)[None], b, NEG)


def _prompt_tables(rel_bias, t):
    def cols(b):
        r = b.shape[1]
        return b.reshape(N_KV, GROUP, r, QB).transpose(0, 2, 1, 3).reshape(N_KV, r, GROUP * QB)
    rel_bias = rel_bias.astype(F32) * LOG2E
    c = rel_bias[N_BUCKETS - 1]
    c_hi = c.astype(BF16)
    c_lo = (c - c_hi.astype(F32)).astype(BF16)
    c_eff = c_hi.astype(F32) + c_lo.astype(F32)
    crow = jnp.stack([c_hi.astype(F32), c_lo.astype(F32)] + [jnp.zeros_like(c)] * 6, axis=1)
    crow = jnp.broadcast_to(crow[:, :, None], (N_HEADS, 8, QB))
    crow = cols(crow)
    j = np.arange(QB)[:, None]
    i = np.arange(QB)[None, :]
    ones = np.ones((QB, QB), bool)
    zeros = jnp.zeros((N_KV, QB, GROUP * QB), F32)
    at = jnp.stack([
        cols(_bias_table(rel_bias, i - j, i >= j, c_eff)),
        cols(_bias_table(rel_bias, QB + i - j, ones, c_eff)),
        zeros,
        jnp.where(jnp.asarray(np.tile(j > i, (1, GROUP)))[None], zeros, NEG),
        zeros + NEG,
    ])
    ncp = t // CMP_STRIDE
    m = np.arange(2 * ncp)[:, None] - ncp
    dist = i - CMP_STRIDE * m - (CMP_BLOCK - 1)
    ut = cols(_bias_table(rel_bias, dist, dist >= 0))
    n_sel = t // SEL_BLOCK
    n = np.arange(ncp)[None, :]
    jb = np.arange(LANES)[:, None]
    ovlt = ((n * CMP_STRIDE < jb * SEL_BLOCK + SEL_BLOCK) & (n * CMP_STRIDE + CMP_BLOCK - 1 >= jb * SEL_BLOCK)
            & (jb < n_sel) & (n < ncp - 1))
    key = np.arange(t)[:, None]
    lane = np.arange(LANES)[None, :]
    stat = np.where(lane < LANES // 2, (lane == key // SEL_BLOCK) * BIG,
                    ((lane == LANES // 2) | (lane == LANES // 2 + 1)) * 1.0).astype(np.float32)
    return ut, at, jnp.asarray(ovlt.astype(np.float32), BF16), jnp.asarray(stat, BF16), crow


def _sample_tables(rel_bias, past, t_new, win_buf):
    def rows(b):
        return b.reshape(N_HEADS * t_new, b.shape[-1])
    tok = np.arange(t_new)[:, None]
    nc = past // CMP_STRIDE
    n = np.arange(nc)[None, :]
    c_end = n * CMP_STRIDE + CMP_BLOCK - 1
    n_cmp = (past + t_new) // CMP_STRIDE - 1
    bc = rows(_bias_table(rel_bias, past + tok - c_end, (c_end <= past + tok) & (n < n_cmp)))
    js = np.arange(past + QB)[None, :]
    ds = np.where(js < past, past + tok - js, tok - (js - past))
    bs = rows(_bias_table(rel_bias, ds, np.where(js < past, True, (ds >= 0) & (js - past < t_new))))
    jw = np.arange(win_buf + QB)[None, :]
    dw = np.where(jw < win_buf, win_buf + tok - jw, tok - (jw - win_buf))
    valid = np.where(jw < win_buf, (dw >= 0) & (dw < WINDOW), (dw >= 0) & (jw - win_buf < t_new))
    bw = rows(_bias_table(rel_bias, dw, valid))
    n_sel = -(-(past + t_new) // SEL_BLOCK)
    nbp = 2 * LANES
    nn = np.arange(nc)[:, None]
    jb = np.arange(nbp)[None, :]
    ovl = ((nn * CMP_STRIDE < jb * SEL_BLOCK + SEL_BLOCK) & (nn * CMP_STRIDE + CMP_BLOCK - 1 >= jb * SEL_BLOCK)
           & (jb < n_sel) & (nn < n_cmp))
    e = (np.arange(LANES)[:, None] == np.arange(past)[None, :] // SEL_BLOCK).astype(np.float32) * BIG
    return bc, bs, bw, jnp.asarray(ovl.astype(np.float32), BF16), jnp.asarray(e, BF16), n_sel


def _slab_diag(blocks, n_slab):
    g, r, c = blocks.shape
    gps = g // n_slab
    eye = jnp.eye(gps, dtype=blocks.dtype)
    return jnp.einsum('sgrc,gh->sgrhc', blocks.reshape(n_slab, gps, r, c), eye).reshape(n_slab, gps * r, gps * c)


def _layer_params(rel_bias, final_norm, attn_norm, w_in, cmp_pe_k, cmp_w1_k, cmp_w2_k, cmp_pe_v, cmp_w1_v, cmp_w2_v,
                  ssm_a_re, ssm_a_im, ssm_log_dt, ssm_b_re, ssm_b_im, ssm_c_re, ssm_c_im, ssm_d, w_glu, b_glu,
                  w_att_br, w_ssm_br, w_o, ffn_norm, w_ffn_gate, w_ffn_up, w_ffn_down, ple_norm, w_ple_gate, w_ple):
    l = 0
    d = w_in.shape[1]
    n_groups = ssm_a_re.shape[1]
    ssm_w = n_groups * SSM_CH
    nstate = n_groups * SSM_P
    assert ssm_w % LANES == 0
    w = w_in[l]
    c0 = AW + 6 * KVW
    n_gate = N_HEADS * N_BRANCH
    front_w = (attn_norm[l].reshape(1, d), w[:, :c0].astype(BF16),
               jnp.pad(w[:, c0:c0 + n_gate], ((0, 0), (0, LANES - n_gate))).astype(BF16),
               w[:, c0 + n_gate:c0 + n_gate + ssm_w].astype(BF16), w[:, c0 + n_gate + ssm_w:].astype(BF16))

    def cmp_weights(pe, w1, w2):
        half = CMP_STRIDE * HD
        w1cat = jnp.concatenate([w1[:half], w1[half:]], axis=1).astype(BF16)
        pe2 = jnp.pad(pe.reshape(2, half), ((0, 6), (0, 0))).astype(BF16)
        return w1cat, w2.astype(BF16), pe2
    cw = cmp_weights(cmp_pe_k[l], cmp_w1_k[l], cmp_w2_k[l]) + cmp_weights(cmp_pe_v[l], cmp_w1_v[l], cmp_w2_v[l])

    abr, abi, bbr_t, bbi_t = _ssm_params(ssm_a_re[l], ssm_a_im[l], ssm_log_dt[l],
                                         jnp.swapaxes(ssm_b_re[l], 1, 2), jnp.swapaxes(ssm_b_im[l], 1, 2))
    n_slab = ssm_w // LANES
    bd = jnp.concatenate([_slab_diag(bbr_t, n_slab), _slab_diag(bbi_t, n_slab)], axis=2).astype(BF16)
    cd = jnp.concatenate([_slab_diag(jnp.swapaxes(ssm_c_re[l], 1, 2), n_slab),
                          -_slab_diag(jnp.swapaxes(ssm_c_im[l], 1, 2), n_slab)], axis=1).astype(BF16)
    ssm_p = (abr.reshape(1, nstate), abi.reshape(1, nstate), bd, cd, ssm_d[l].reshape(1, ssm_w),
             w_glu[l].astype(BF16), b_glu[l].reshape(1, ssm_w))

    back_w = (w_att_br[l].astype(BF16), w_ssm_br[l].astype(BF16), w_o[l].astype(BF16),
              ffn_norm[l].reshape(1, d), w_ffn_gate[l].astype(BF16), w_ffn_up[l].astype(BF16),
              w_ffn_down[l].astype(BF16), ple_norm[l].reshape(1, d), w_ple_gate[l].astype(BF16),
              w_ple[l].astype(BF16), final_norm.reshape(1, d))
    d_ff = w_ffn_gate.shape[2]
    ff_chunk = d_ff // 2 if (d_ff // 2) % LANES == 0 else d_ff
    return dict(front=front_w, cmp=cw, ssm=ssm_p, back=back_w, ff_chunk=ff_chunk, rel_bias=rel_bias,
                n_groups=n_groups, ssm_w=ssm_w, nstate=nstate, n_gate=n_gate)


def _prompt_group(x_prompt, p_l, prm):
    nb, t, d = x_prompt.shape
    ssm_w, nstate = prm["ssm_w"], prm["nstate"]
    assert t % (CMP_STRIDE * LANES) == 0 and t >= WINDOW and t // SEL_BLOCK <= LANES // 2
    xp = x_prompt.reshape(nb * t, d)
    (qt, kct, vct, kst, vst32, kc, vc, kw, vw, ksb, kwb, vst, vwt, ngt, su, mg) = _front_prompt(
        xp, nb, t, prm["front"], 512)
    kcc, vcct = _compress_prompt(kc.reshape(nb, t, KVW), vc.reshape(nb, t, KVW), prm["cmp"])
    o = _nsa_prompt(qt, ngt, kcc, vcct, ksb.reshape(nb, t, KVW), vst, kwb.reshape(nb, t, KVW), vwt,
                    _prompt_tables(prm["rel_bias"], t))
    zeros_state = jnp.zeros((nb, nstate), F32)
    so, sr, si = _ssm(su, zeros_state, zeros_state, *prm["ssm"], nb, 256, True)
    y = _back(xp, o.reshape(nb * t, AW), so, mg, p_l.reshape(nb * t, -1),
              prm["back"], nb, t, 512, prm["ff_chunk"])
    return dict(y=y, o=o, so=so, rows_t=(kct, vct, kst, vst32), win=(kw, vw), state=(sr, si))


def _sample_group(x_sample, p_l, page_table, pools, wins, states, prm):
    ns, t_new, d = x_sample.shape
    tok_minor = lambda a: jnp.transpose(a, (0, 2, 3, 1)).reshape(a.shape[0], KVW, a.shape[1])
    k_cmp, v_cmp, k_sel, v_sel = pools
    k_win, v_win = wins
    n_phys, page = k_cmp.shape[:2]
    past = page_table.shape[1] * page
    win_buf = k_win.shape[1]
    ssm_w, nstate, n_gate = prm["ssm_w"], prm["nstate"], prm["n_gate"]
    assert page == QB and past % (CMP_STRIDE * LANES) == 0 and past // SEL_BLOCK <= LANES
    assert win_buf == WINDOW and past >= win_buf and t_new <= 8 and t_new < CMP_STRIDE
    n_s = ns * t_new
    xs = x_sample.reshape(n_s, d)
    q_s, kc_s, vc_s, ks_s, vs_s, kw_s, vw_s, ng_s, su_s, mg_s = _front_sample(xs, prm["front"])
    kcc_s, vcc_s = _compress_sample(page_table, tok_minor(k_cmp), tok_minor(v_cmp), prm["cmp"])
    bc, bs, bw, ovl_s, e_s, n_sel_s = _sample_tables(prm["rel_bias"], past, t_new, win_buf)
    rows_s = N_HEADS * t_new
    eye_kv = jnp.eye(N_KV, dtype=BF16)
    q_rows = q_s.reshape(ns, t_new, N_KV, GROUP, HD).transpose(0, 2, 3, 1, 4)
    q_rows = jnp.einsum('skgtd,kj->skgtjd', q_rows, eye_kv).reshape(ns, rows_s, KVW)
    gate_s = ng_s[:, :n_gate].reshape(ns, t_new, N_KV, GROUP, N_BRANCH).transpose(0, 2, 3, 1, 4)
    gate_s = jnp.pad(gate_s.reshape(ns, rows_s, N_BRANCH), ((0, 0), (0, 0), (0, LANES - N_BRANCH)))
    pad8 = lambda a: jnp.pad(a.reshape(ns, t_new, KVW), ((0, 0), (0, 8 - t_new), (0, 0)))
    o_s = _nsa_sample(page_table, q_rows, gate_s, kcc_s, vcc_s,
                      tok_minor(k_sel), tok_minor(v_sel), pad8(ks_s), pad8(vs_s), tok_minor(k_win), tok_minor(v_win),
                      pad8(kw_s), pad8(vw_s), bc, bs, bw, ovl_s, e_s, n_sel_s, past, t_new)
    o_s = o_s.reshape(ns, N_KV, GROUP, t_new, N_KV, HD)
    o_s = jnp.stack([o_s[:, k, :, :, k, :] for k in range(N_KV)], axis=1)
    o_s = o_s.transpose(0, 3, 1, 2, 4).reshape(n_s, AW).astype(BF16)
    su_ts = su_s.reshape(ns, t_new, ssm_w).transpose(1, 0, 2).reshape(n_s, ssm_w)
    so_ts, sr, si = _ssm(su_ts, states[0].reshape(ns, nstate), states[1].reshape(ns, nstate), *prm["ssm"], ns, t_new,
                         False)
    so_s = so_ts.reshape(t_new, ns, ssm_w).transpose(1, 0, 2).reshape(n_s, ssm_w)
    y = _back(xs, o_s, so_s, mg_s, p_l.reshape(n_s, -1), prm["back"], 1, n_s, n_s, prm["ff_chunk"])
    return dict(y=y, o=o_s, so=so_s, rows=(kc_s, vc_s, ks_s, vs_s, kw_s, vw_s), state=(sr, si))


def kernel(x_prompt, x_sample, cache_k_cmp, cache_v_cmp, cache_k_sel, cache_v_sel, cache_k_win, cache_v_win, state_ssm_re, state_ssm_im, page_table, p_prompt, p_sample, rel_bias, final_norm, attn_norm, w_in, cmp_pe_k, cmp_w1_k, cmp_w2_k, cmp_pe_v, cmp_w1_v, cmp_w2_v, ssm_a_re, ssm_a_im, ssm_log_dt, ssm_b_re, ssm_b_im, ssm_c_re, ssm_c_im, ssm_d, w_glu, b_glu, w_att_br, w_ssm_br, w_o, ffn_norm, w_ffn_gate, w_ffn_up, w_ffn_down, ple_norm, w_ple_gate, w_ple):
    assert w_in.shape[0] == 1, "single-layer trunk"
    l = 0
    nb, t, d = x_prompt.shape
    ns, t_new = x_sample.shape[:2]
    prm = _layer_params(rel_bias, final_norm, attn_norm, w_in, cmp_pe_k, cmp_w1_k, cmp_w2_k, cmp_pe_v, cmp_w1_v,
                        cmp_w2_v, ssm_a_re, ssm_a_im, ssm_log_dt, ssm_b_re, ssm_b_im, ssm_c_re, ssm_c_im, ssm_d,
                        w_glu, b_glu, w_att_br, w_ssm_br, w_o, ffn_norm, w_ffn_gate, w_ffn_up, w_ffn_down,
                        ple_norm, w_ple_gate, w_ple)
    pg = _prompt_group(x_prompt, p_prompt[l], prm)
    sg = _sample_group(x_sample, p_sample[l], page_table,
                       (cache_k_cmp[l], cache_v_cmp[l], cache_k_sel[l], cache_v_sel[l]),
                       (cache_k_win[l], cache_v_win[l]), (state_ssm_re[l], state_ssm_im[l]), prm)

    kv5 = lambda a, b_, t_: a.reshape(1, b_, t_, N_KV, HD)
    kv5_t = lambda a: jnp.transpose(a.reshape(1, nb, N_KV, HD, t), (0, 1, 4, 2, 3))
    keep = min(WINDOW, t)
    win_p = lambda a: a.reshape(nb, t, KVW)[:, t - keep:].reshape(1, nb, keep, N_KV, HD)
    win_s = lambda cache, new: jnp.concatenate(
        [cache[l], new.reshape(ns, t_new, N_KV, HD)], axis=1)[:, t_new:][None]
    st = lambda a, b_: a.reshape(1, b_, prm["n_groups"], SSM_P)
    kc_s, vc_s, ks_s, vs_s, kw_s, vw_s = sg["rows"]
    return (pg["y"].reshape(nb, t, d), sg["y"].reshape(ns, t_new, d),
            *[kv5_t(a) for a in pg["rows_t"]], win_p(pg["win"][0]), win_p(pg["win"][1]),
            st(pg["state"][0], nb), st(pg["state"][1], nb),
            kv5(kc_s, ns, t_new), kv5(vc_s, ns, t_new), kv5(ks_s, ns, t_new), kv5(vs_s, ns, t_new),
            win_s(cache_k_win, kw_s), win_s(cache_v_win, vw_s),
            st(sg["state"][0], ns), st(sg["state"][1], ns))
```

```python
import functools
import math

import numpy as np
import jax
import jax.numpy as jnp
from jax import lax
from jax.experimental import pallas as pl
from jax.experimental.pallas import tpu as pltpu

F32 = jnp.float32
BF16 = jnp.bfloat16

N_HEADS = 8
N_KV = 2
HD = 64
GROUP = N_HEADS // N_KV
N_BRANCH = 3
CMP_BLOCK = 32
CMP_STRIDE = 16
CMP_HIDDEN = 256
CMP_PITCH = 24
SEL_BLOCK = 64
SEL_TOPK = 16
WINDOW = 512
QB = 128
SEL_KEYS = 4 * QB
SEL_SPLIT = 2
N_BUCKETS = 32
REL_MAX_DIST = 128
SSM_CH = 16
SSM_P = 64
EPS = 1e-6
NEG = -1e30
NEG_TEST = -1e29
FORCE = 1e9
BIG = 1e30
LOG2E = math.log2(math.e)
LANES = 128
VMEM_LIMIT = 56 * 1024 * 1024
AW = N_HEADS * HD
KVW = N_KV * HD
NG_ROWS = 32


def _cparams(sem):
    return pltpu.CompilerParams(dimension_semantics=sem, vmem_limit_bytes=VMEM_LIMIT)


def _const_spec(shape):
    nd = len(shape)
    return pl.BlockSpec(shape, lambda *_: (0,) * nd)


def _rms(x, g):
    return x * lax.rsqrt(jnp.mean(x * x, axis=-1, keepdims=True) + EPS) * g


def _gelu(x):
    return x * (0.5 * (1.0 + jnp.tanh(math.sqrt(2.0 / math.pi) * (x + 0.044715 * (x * x * x)))))


def _sigmoid(x):
    return 1.0 / (1.0 + jnp.exp(-x))


def _dot(a, b):
    return jnp.dot(a, b, preferred_element_type=F32)


def _dot_t(a, b):
    return lax.dot_general(a, b, (((1,), (1,)), ((), ())), preferred_element_type=F32)


def _masked_softmax(s, axis=-1):
    valid = s > NEG_TEST
    m = jnp.max(s, axis=axis, keepdims=True)
    e = jnp.where(valid, jnp.exp(s - m), 0.0)
    return e / jnp.maximum(jnp.sum(e, axis=axis, keepdims=True), 1e-30)


def _softmax2_cols(s):
    m = jnp.max(s, axis=0, keepdims=True)
    e = jnp.exp2(s - m)
    inv = jnp.where(m > NEG_TEST, 1.0 / jnp.maximum(jnp.sum(e, axis=0, keepdims=True), 1e-30), 0.0)
    return e * inv


def _front_project(x_ref, g_ref, wa_ref, wng_ref, wsu_ref, wmg_ref, su_ref, mg_ref, q_scale):
    u = _rms(x_ref[...], g_ref[...]).astype(BF16)
    za = _dot(u, wa_ref[...])
    q = za[:, :AW] * q_scale
    rows = [za[:, AW + i * KVW: AW + (i + 1) * KVW] for i in range(6)]
    ng = _sigmoid(_dot(u, wng_ref[...]))
    su_ref[...] = _dot(u, wsu_ref[...]).astype(su_ref.dtype)
    mg_ref[...] = _sigmoid(_dot(u, wmg_ref[...])).astype(BF16)
    return q, rows, ng


def _front_sample_kernel(x_ref, g_ref, wa_ref, wng_ref, wsu_ref, wmg_ref,
                         q_ref, kc_ref, vc_ref, ks_ref, vs_ref, kw_ref, vw_ref, ng_ref, su_ref, mg_ref):
    q, rows, ng = _front_project(x_ref, g_ref, wa_ref, wng_ref, wsu_ref, wmg_ref, su_ref, mg_ref, HD ** -0.5)
    q_ref[...] = q.astype(BF16)
    for ref, r in zip((kc_ref, vc_ref, ks_ref, vs_ref, kw_ref, vw_ref), rows):
        ref[...] = r
    ng_ref[...] = ng


def _front_prompt_kernel(x_ref, g_ref, wa_ref, wng_ref, wsu_ref, wmg_ref,
                         qt_ref, kct_ref, vct_ref, kst_ref, vst32_ref, kc_ref, vc_ref, kw_ref, vw_ref,
                         ksb_ref, kwb_ref, vst_ref, vwt_ref, ngt_ref, su_ref, mg_ref):
    q, rows, ng = _front_project(x_ref, g_ref, wa_ref, wng_ref, wsu_ref, wmg_ref, su_ref, mg_ref,
                                 HD ** -0.5 * LOG2E)
    kc, vc, ks, vs, kw, vw = rows
    qt_ref[0] = q.T.astype(BF16)
    for ref, r in zip((kct_ref, vct_ref, kst_ref, vst32_ref), (kc, vc, ks, vs)):
        ref[0] = r.T
    kc_ref[...] = kc
    vc_ref[...] = vc
    kw_ref[...] = kw
    vw_ref[...] = vw
    ksb_ref[...] = ks.astype(BF16)
    kwb_ref[...] = kw.astype(BF16)
    for ref, r in ((vst_ref, vs), (vwt_ref, vw)):
        rt = r.T.astype(BF16)
        for j in range(rt.shape[1] // QB):
            ref[0, j] = rt[:, j * QB:(j + 1) * QB]
    ngt_ref[0] = ng.T[0:NG_ROWS]


def _front_sample(x2d, fw):
    g, wa, wng, wsu, wmg = fw
    n, d = x2d.shape
    sw, mw = wsu.shape[1], wmg.shape[1]
    shapes = ([jax.ShapeDtypeStruct((n, AW), BF16)] + [jax.ShapeDtypeStruct((n, KVW), F32)] * 6
              + [jax.ShapeDtypeStruct((n, LANES), F32), jax.ShapeDtypeStruct((n, sw), BF16),
                 jax.ShapeDtypeStruct((n, mw), BF16)])
    return pl.pallas_call(
        _front_sample_kernel,
        grid=(1,),
        in_specs=[_const_spec(a.shape) for a in (x2d, g, wa, wng, wsu, wmg)],
        out_specs=[_const_spec(s.shape) for s in shapes],
        out_shape=shapes,
        compiler_params=_cparams(("arbitrary",)),
        name="front_sample",
    )(x2d, g, wa, wng, wsu, wmg)


def _front_prompt(x2d, nb, t, fw, tm):
    g, wa, wng, wsu, wmg = fw
    n, d = x2d.shape
    nt = t // tm
    sw, mw = wsu.shape[1], wmg.shape[1]
    row = lambda b, i: (b * nt + i, 0)
    kv5 = jax.ShapeDtypeStruct((nb, KVW, t), F32)
    kv5_spec = pl.BlockSpec((1, KVW, tm), lambda b, i: (b, 0, i))
    vt = jax.ShapeDtypeStruct((nb, t // QB, KVW, QB), BF16)
    vt_spec = pl.BlockSpec((1, tm // QB, KVW, QB), lambda b, i: (b, i, 0, 0))
    shapes = ([jax.ShapeDtypeStruct((nb, AW, t), BF16)] + [kv5] * 4 + [jax.ShapeDtypeStruct((n, KVW), F32)] * 4
              + [jax.ShapeDtypeStruct((n, KVW), BF16)] * 2 + [vt] * 2
              + [jax.ShapeDtypeStruct((nb, NG_ROWS, t), F32), jax.ShapeDtypeStruct((t, nb * sw), BF16),
                 jax.ShapeDtypeStruct((n, mw), BF16)])
    specs = ([pl.BlockSpec((1, AW, tm), lambda b, i: (b, 0, i))] + [kv5_spec] * 4
             + [pl.BlockSpec((tm, KVW), row)] * 6 + [vt_spec] * 2
             + [pl.BlockSpec((1, NG_ROWS, tm), lambda b, i: (b, 0, i)),
                pl.BlockSpec((tm, sw), lambda b, i: (i, b)), pl.BlockSpec((tm, mw), row)])
    return pl.pallas_call(
        _front_prompt_kernel,
        grid=(nb, nt),
        in_specs=[pl.BlockSpec((tm, d), row)] + [_const_spec(a.shape) for a in (g, wa, wng, wsu, wmg)],
        out_specs=specs,
        out_shape=shapes,
        compiler_params=_cparams(("parallel", "parallel")),
        name="front_prompt",
    )(x2d, g, wa, wng, wsu, wmg)


def _chunk_rows(load, r0, rn, pitch=CMP_STRIDE):
    return jnp.concatenate([load(pl.ds(pitch * r0 + r, rn, stride=pitch)) for r in range(CMP_STRIDE)], axis=1)


def _compress_compute(load_rows, c, w1_ref, w2_ref, pe_ref, a_scr):
    rc_n = min(c, 256)
    lo = lax.broadcasted_iota(jnp.int32, (rc_n, LANES), 1) < HD
    w1 = w1_ref[...]
    for rc in range(c // rc_n):
        x = load_rows(rc * rc_n, rc_n)
        cols = [x[:, r * LANES:(r + 1) * LANES] for r in range(CMP_STRIDE)]
        rol = [pltpu.roll(col, HD, 1) for col in cols]
        for kh in range(N_KV):
            if kh == 0:
                parts = [jnp.where(lo, cols[2 * j], rol[2 * j + 1]) for j in range(CMP_STRIDE // 2)]
            else:
                parts = [jnp.where(lo, rol[2 * j], cols[2 * j + 1]) for j in range(CMP_STRIDE // 2)]
            xh = jnp.concatenate(parts, axis=1).astype(BF16)
            a_scr[kh, rc * rc_n:(rc + 1) * rc_n, :] = _dot(xh, w1)
    pw = _dot(pe_ref[...], w1)
    peb = pw[0:1, :CMP_HIDDEN] + pw[1:2, CMP_HIDDEN:]
    w2 = w2_ref[...]
    outs = []
    for kh in range(N_KV):
        a = a_scr[kh]
        hid = a[:, :CMP_HIDDEN] + pltpu.roll(a[:, CMP_HIDDEN:], c - 1, 0) + peb
        outs.append(_dot(_gelu(hid).astype(BF16), w2))
    return jnp.concatenate(outs, axis=1)


def _compress_prompt_kernel(xk_ref, xv_ref, w1k_ref, w2k_ref, pek_ref, w1v_ref, w2v_ref, pev_ref,
                            ok_ref, ovt_ref, a_scr):
    c = xk_ref.shape[1] // CMP_STRIDE
    ok_ref[0] = _compress_compute(lambda r0, rn: _chunk_rows(lambda idx: xk_ref[0, idx, :], r0, rn), c,
                                  w1k_ref, w2k_ref, pek_ref, a_scr).astype(BF16)
    ovt_ref[0] = _compress_compute(lambda r0, rn: _chunk_rows(lambda idx: xv_ref[0, idx, :], r0, rn), c,
                                   w1v_ref, w2v_ref, pev_ref, a_scr).T.astype(BF16)


def _compress_prompt(xk, xv, cw):
    nb, t, kvw = xk.shape
    c = t // CMP_STRIDE
    wspecs = [_const_spec(w.shape) for w in cw]
    blk = pl.BlockSpec((1, t, kvw), lambda b: (b, 0, 0))
    return pl.pallas_call(
        _compress_prompt_kernel,
        grid=(nb,),
        in_specs=[blk, blk] + wspecs,
        out_specs=[pl.BlockSpec((1, c, KVW), lambda b: (b, 0, 0)), pl.BlockSpec((1, KVW, c), lambda b: (b, 0, 0))],
        out_shape=[jax.ShapeDtypeStruct((nb, c, KVW), BF16), jax.ShapeDtypeStruct((nb, KVW, c), BF16)],
        scratch_shapes=[pltpu.VMEM((N_KV, c, 2 * CMP_HIDDEN), F32)],
        compiler_params=_cparams(("parallel",)),
        name="compress_prompt",
    )(xk, xv, *cw)


def _page_copy(pool, buf, sem, page, p, slot):
    return pltpu.make_async_copy(pool.at[page], buf.at[slot, p], sem)


def _page_gather_start(pt_ref, seq, pools, bufs, sems, slot, n_pages):
    def body(p, carry):
        page = pt_ref[seq, p]
        for i, (pool, buf) in enumerate(zip(pools, bufs)):
            _page_copy(pool, buf, sems.at[i, slot], page, p, slot).start()
        return carry
    lax.fori_loop(0, n_pages, body, 0)


def _page_gather_wait(pools, bufs, sems, slot, n_pages):
    def body(p, carry):
        for i, (pool, buf) in enumerate(zip(pools, bufs)):
            _page_copy(pool, buf, sems.at[i, slot], 0, p, slot).wait()
        return carry
    lax.fori_loop(0, n_pages, body, 0)


def _paged_prefetch(pt_ref, pools, bufs, sems, n_pages):
    s = pl.program_id(0)
    slot = s % 2

    @pl.when(s == 0)
    def _():
        _page_gather_start(pt_ref, 0, pools, bufs, sems, 0, n_pages)

    @pl.when(s + 1 < pl.num_programs(0))
    def _():
        _page_gather_start(pt_ref, s + 1, pools, bufs, sems, 1 - slot, n_pages)

    _page_gather_wait(pools, bufs, sems, slot, n_pages)
    return slot


def _compress_sample_kernel(pt_ref, kpool, vpool, w1k_ref, w2k_ref, pek_ref, w1v_ref, w2v_ref, pev_ref,
                            ok_ref, ov_ref, kbuf, vbuf, sems, a_scr, rows_scr):
    n_pages = pt_ref.shape[1]
    page = kpool.shape[2]
    c = n_pages * page // CMP_STRIDE
    slot = _paged_prefetch(pt_ref, (kpool, vpool), (kbuf, vbuf), sems, n_pages)
    for buf, out_ref, w1_ref, w2_ref, pe_ref in ((kbuf, ok_ref, w1k_ref, w2k_ref, pek_ref),
                                                (vbuf, ov_ref, w1v_ref, w2v_ref, pev_ref)):
        for p in range(n_pages):
            rows = buf[slot, p].T
            for ch in range(page // CMP_STRIDE):
                r0 = (p * (page // CMP_STRIDE) + ch) * CMP_PITCH
                rows_scr[r0:r0 + CMP_STRIDE, :] = rows[ch * CMP_STRIDE:(ch + 1) * CMP_STRIDE]
        out_ref[0] = _compress_compute(
            lambda r0, rn: _chunk_rows(lambda idx: rows_scr[idx, :], r0, rn, CMP_PITCH), c,
            w1_ref, w2_ref, pe_ref, a_scr).astype(BF16)


def _compress_sample(page_table, kpool, vpool, cw):
    ns, n_pages = page_table.shape
    width, page = kpool.shape[1:]
    tokens = n_pages * page
    c = tokens // CMP_STRIDE
    buf_shape = (2, n_pages, width, page)
    any_spec = pl.BlockSpec(memory_space=pl.ANY)
    wspecs = [pl.BlockSpec(w.shape, lambda s, pt, nd=w.ndim: (0,) * nd) for w in cw]
    oblk = pl.BlockSpec((1, c, KVW), lambda s, pt: (s, 0, 0))
    return pl.pallas_call(
        _compress_sample_kernel,
        grid_spec=pltpu.PrefetchScalarGridSpec(
            num_scalar_prefetch=1,
            grid=(ns,),
            in_specs=[any_spec, any_spec] + wspecs,
            out_specs=[oblk, oblk],
            scratch_shapes=[pltpu.VMEM(buf_shape, F32), pltpu.VMEM(buf_shape, F32), pltpu.SemaphoreType.DMA((2, 2)),
                            pltpu.VMEM((N_KV, c, 2 * CMP_HIDDEN), F32), pltpu.VMEM((c * CMP_PITCH, width), F32)]),
        out_shape=[jax.ShapeDtypeStruct((ns, c, KVW), BF16)] * 2,
        compiler_params=_cparams(("arbitrary",)),
        name="compress_sample",
    )(page_table, kpool, vpool, *cw)


def _rank_select(score, blk, n_real, axis):
    size = 8 if axis == 0 else LANES
    total = score.shape[axis]
    chunk = (lambda a, c: a[c * size:(c + 1) * size]) if axis == 0 else (lambda a, c: a[:, c * size:(c + 1) * size])
    n_chunks = -(-total // size)
    sc = [chunk(score, c) for c in range(n_chunks)]
    bl = [chunk(blk, c) for c in range(n_chunks)]
    rank = [jnp.zeros(s.shape, F32) for s in sc]
    for kk in range(n_real):
        col = score[kk:kk + 1, :] if axis == 0 else score[:, kk:kk + 1]
        for c in range(n_chunks):
            other = jnp.broadcast_to(col, sc[c].shape)
            if c * size > kk:
                beats = other >= sc[c]
            elif min((c + 1) * size, total) - 1 < kk:
                beats = other > sc[c]
            else:
                beats = (other > sc[c]) | ((other == sc[c]) & (bl[c] > kk))
            rank[c] = rank[c] + jnp.where(beats, 1.0, 0.0)
    return jnp.where(jnp.concatenate(rank, axis=axis) < SEL_TOPK, 1.0, 0.0)


def _block_scores(imp, blk, t):
    cur = t // SEL_BLOCK
    forced = (blk == 0) | (blk == cur) | (blk == cur - 1)
    valid = blk * SEL_BLOCK <= t
    return jnp.where(valid, jnp.where(forced, FORCE, imp), NEG)


def _nsa_prompt_kernel(qt_ref, ngt_ref, kc_ref, vct_ref, ks_ref, vst_ref, kw_ref, vwt_ref,
                       ut_ref, at_ref, ovlt_ref, stat_ref, crow_ref, o_ref, *, n_sel):
    ib = pl.program_id(1)
    qt = qt_ref[0]
    ngt = ngt_ref[0]
    ncp = kc_ref.shape[1]
    cols = GROUP * QB
    sel_rows = LANES // 2
    t_row = ib * QB + lax.broadcasted_iota(jnp.int32, (n_sel, QB), 1)
    blk_t = lax.broadcasted_iota(jnp.int32, (n_sel, QB), 0)
    zeros_q = jnp.zeros((HD, cols), F32)
    vrows = [slice(k * HD, (k + 1) * HD) for k in range(N_KV)]
    q_sel, q_win, o_c, o_w = [], [], [], []
    for k in range(N_KV):
        qk = jnp.concatenate([qt[(GROUP * k + g) * HD:(GROUP * k + g + 1) * HD, :] for g in range(GROUP)],
                             axis=1).astype(F32)
        qa = jnp.concatenate([qk, zeros_q] if k == 0 else [zeros_q, qk], axis=0)

        bias_c = ut_ref[k, pl.ds(pl.multiple_of(ncp - (QB // CMP_STRIDE) * ib, 8), ncp), :]
        p_c = _softmax2_cols(_dot(kc_ref[0], qa.astype(BF16)) + bias_c)
        o_c.append(_dot(vct_ref[0][vrows[k], :], p_c.astype(BF16)))
        psum = p_c[:, 0:QB]
        for g in range(1, GROUP):
            psum = psum + p_c[:, g * QB:(g + 1) * QB]
        psum_hi = psum.astype(BF16)
        psum_lo = (psum - psum_hi.astype(F32)).astype(BF16)
        imp = _dot(ovlt_ref[...], psum_hi) + _dot(ovlt_ref[...], psum_lo)

        sel = _rank_select(_block_scores(imp[0:n_sel], blk_t, t_row), blk_t, n_sel, 0)
        selm1 = jnp.concatenate([sel - 1.0] * GROUP, axis=1)
        if n_sel < sel_rows:
            selm1 = jnp.concatenate([selm1, jnp.zeros((sel_rows - n_sel, cols), F32)], axis=0)
        tail = jnp.concatenate([crow_ref[k], jnp.zeros((LANES - sel_rows - 8, cols), F32)], axis=0)
        q_sel.append(jnp.concatenate([qa, selm1, tail], axis=0).astype(BF16))
        q_win.append(jnp.concatenate([qa, jnp.zeros((sel_rows, cols), F32), tail], axis=0).astype(BF16))

        s_parts, tiles_j = [], []
        for w, tidx in enumerate((0, 1, None, None, 3)):
            jt = ib - w
            jc = jnp.maximum(jt, 0)
            k0 = pl.multiple_of(jc * QB, QB)
            lhs = jnp.concatenate([kw_ref[0, pl.ds(k0, QB), :], stat_ref[pl.ds(k0, QB), :]], axis=1)
            s = _dot(lhs, q_win[k])
            if tidx is not None:
                s = s + at_ref[tidx, k]
            s_parts.append(jnp.where(jt >= 0, s, NEG))
            tiles_j.append(jc)
        p_w = _softmax2_cols(jnp.concatenate(s_parts, axis=0)).astype(BF16)
        o_wk = jnp.zeros((HD, cols), F32)
        for w, jc in enumerate(tiles_j):
            o_wk = o_wk + _dot(vwt_ref[0, jc, vrows[k], :], p_w[w * QB:(w + 1) * QB])
        o_w.append(o_wk)

    tiles_per_step = SEL_KEYS // QB

    ones_rows = jnp.where(lax.broadcasted_iota(jnp.int32, (16, SEL_KEYS), 0) == 0, 1.0, 0.0).astype(BF16)
    step_tiles = SEL_SPLIT * tiles_per_step

    def step_tile_ids(jp, sp):
        j0 = (jp * SEL_SPLIT + sp) * tiles_per_step
        return [j0 + h for h in range(tiles_per_step)]

    def step_scores(jp):
        out = []
        for sp in range(SEL_SPLIT):
            k0 = pl.multiple_of(step_tile_ids(jp, sp)[0] * QB, SEL_KEYS)
            lhs = jnp.concatenate([ks_ref[0, pl.ds(k0, SEL_KEYS), :], stat_ref[pl.ds(k0, SEL_KEYS), :]], axis=1)
            out += [_dot(lhs, q_sel[k]) for k in range(N_KV)]
        return tuple(out)

    def step_near_bias(jp, scores):
        out = []
        for sp in range(SEL_SPLIT):
            tidx = [jnp.where(jt == ib, 0, jnp.where(jt == ib - 1, 1, jnp.where(jt < ib, 2, 4)))
                    for jt in step_tile_ids(jp, sp)]
            out += [scores[sp * N_KV + k] + jnp.concatenate([at_ref[ti, k] for ti in tidx], axis=0)
                    for k in range(N_KV)]
        return tuple(out)

    def step_update(jp, state, scores):
        stats = []
        for (m, _), s in zip(state, scores):
            m_new = jnp.maximum(m, jnp.max(s, axis=0, keepdims=True))
            stats.append((m_new, jnp.exp2(m - m_new), jnp.exp2(s - m_new).astype(BF16)))
        vts = [jnp.concatenate([jnp.concatenate([vst_ref[0, jt, vrows[k], :] for jt in step_tile_ids(jp, sp)], axis=1),
                                ones_rows], axis=0)
               for sp in range(SEL_SPLIT) for k in range(N_KV)]
        return tuple((m_new, alpha * acc + _dot(vt, p))
                     for (_, acc), (m_new, alpha, p), vt in zip(state, stats, vts))

    def far_body(jp, state):
        return step_update(jp, state, step_scores(jp))

    def near_body(jp, state):
        return step_update(jp, state, step_near_bias(jp, step_scores(jp)))

    init = (jnp.full((1, cols), NEG, F32), jnp.zeros((HD + ones_rows.shape[0], cols), F32))
    n_far = jnp.maximum(ib - 1, 0) // step_tiles
    sel_state = lax.fori_loop(0, n_far, far_body, (init,) * (N_KV * SEL_SPLIT))
    sel_state = lax.fori_loop(n_far, (ib + step_tiles) // step_tiles, near_body, sel_state)

    out_rows = []
    for k in range(N_KV):
        parts = [sel_state[sp * N_KV + k] for sp in range(SEL_SPLIT)]
        m_s = parts[0][0]
        for m_p, _ in parts[1:]:
            m_s = jnp.maximum(m_s, m_p)
        acc_s = jnp.zeros(init[1].shape, F32)
        for m_p, acc_p in parts:
            acc_s = acc_s + jnp.exp2(m_p - m_s) * acc_p
        o_s = acc_s[0:HD] / jnp.maximum(acc_s[HD:HD + 1], 1e-30)

        def gate_row(br):
            return jnp.concatenate([ngt[(GROUP * k + g) * N_BRANCH + br:(GROUP * k + g) * N_BRANCH + br + 1, :]
                                    for g in range(GROUP)], axis=1)
        o_k = gate_row(0) * o_c[k] + gate_row(1) * o_s + gate_row(2) * o_w[k]
        out_rows += [o_k[:, g * QB:(g + 1) * QB] for g in range(GROUP)]
    o_ref[0] = jnp.concatenate(out_rows, axis=0).T.astype(BF16)


def _nsa_prompt(qt, ngt, kc, vct, ks, vst, kw, vwt, tables):
    nb, aw, t = qt.shape
    nq = t // QB
    ncp = kc.shape[1]
    full3 = lambda b, i: (b, 0, 0)
    full4 = lambda b, i: (b, 0, 0, 0)
    return pl.pallas_call(
        functools.partial(_nsa_prompt_kernel, n_sel=t // SEL_BLOCK),
        grid=(nb, nq),
        in_specs=[pl.BlockSpec((1, aw, QB), lambda b, i: (b, 0, i)),
                  pl.BlockSpec((1, NG_ROWS, QB), lambda b, i: (b, 0, i)),
                  pl.BlockSpec((1, ncp, KVW), full3), pl.BlockSpec((1, KVW, ncp), full3),
                  pl.BlockSpec((1, t, KVW), full3), pl.BlockSpec((1, nq, KVW, QB), full4),
                  pl.BlockSpec((1, t, KVW), full3), pl.BlockSpec((1, nq, KVW, QB), full4)]
                 + [_const_spec(a.shape) for a in tables],
        out_specs=pl.BlockSpec((1, QB, aw), lambda b, i: (b, i, 0)),
        out_shape=jax.ShapeDtypeStruct((nb, t, aw), BF16),
        compiler_params=_cparams(("parallel", "arbitrary")),
        name="nsa_prompt",
    )(qt, ngt, kc, vct, ks, vst, kw, vwt, *tables)


def _nsa_sample_kernel(pt_ref, q_ref, gate_ref, kc_ref, vc_ref, kpool, vpool, ksn_ref, vsn_ref,
                       kwin_ref, vwin_ref, kwn_ref, vwn_ref, bc_ref, bs_ref, bw_ref, ovl_ref, e_ref,
                       o_ref, kwo_ref, vwo_ref, kbuf, vbuf, sems, *, n_sel, past, t_new):
    n_pages = pt_ref.shape[1]
    wb = kwin_ref.shape[2]
    slot = _paged_prefetch(pt_ref, (kpool, vpool), (kbuf, vbuf), sems, n_pages)
    past_t = lambda buf: jnp.concatenate([buf[slot, p] for p in range(n_pages)], axis=1).astype(BF16)
    q = q_ref[0]
    nb_past = past // SEL_BLOCK
    pad_new = jnp.zeros((QB - ksn_ref.shape[1], KVW), F32)
    new_tile = lambda ref: jnp.concatenate([ref[0], pad_new], axis=0).astype(BF16)

    p_c = _masked_softmax(_dot_t(q, kc_ref[0]) + bc_ref[...])
    o_c = _dot(p_c.astype(BF16), vc_ref[0])
    parts = []
    for k in range(N_KV):
        base = k * GROUP * t_new
        ps = p_c[base:base + t_new]
        for g in range(1, GROUP):
            ps = ps + p_c[base + g * t_new:base + (g + 1) * t_new]
        parts.append(ps)
    psum = jnp.concatenate(parts, axis=0)
    psum_hi = psum.astype(BF16)
    psum_lo = (psum - psum_hi.astype(F32)).astype(BF16)
    imp = _dot(psum_hi, ovl_ref[...]) + _dot(psum_lo, ovl_ref[...])
    blk = lax.broadcasted_iota(jnp.int32, imp.shape, 1)
    tpos = past + lax.broadcasted_iota(jnp.int32, imp.shape, 0) % t_new
    sel = _rank_select(_block_scores(imp, blk, tpos), blk, n_sel, 1)
    sel = jnp.concatenate([sel[k * t_new:(k + 1) * t_new] for k in range(N_KV) for _ in range(GROUP)], axis=0)

    mask_add = _dot((sel[:, 0:LANES] - 1.0).astype(BF16), e_ref[...])
    s_past = _dot(q, past_t(kbuf)) + bs_ref[:, 0:past] + mask_add
    s_new = _dot_t(q, new_tile(ksn_ref)) + bs_ref[:, past:]
    s_new = jnp.where(sel[:, nb_past:nb_past + 1] > 0.5, s_new, NEG)
    p_s = _masked_softmax(jnp.concatenate([s_past, s_new], axis=1)).astype(BF16)
    o_s = _dot_t(p_s[:, 0:past], past_t(vbuf)) + _dot(p_s[:, past:], new_tile(vsn_ref))

    s_w = jnp.concatenate([_dot(q, kwin_ref[0].astype(BF16)), _dot_t(q, new_tile(kwn_ref))], axis=1)
    p_w = _masked_softmax(s_w + bw_ref[...]).astype(BF16)
    o_w = _dot_t(p_w[:, 0:wb], vwin_ref[0].astype(BF16)) + _dot(p_w[:, wb:], new_tile(vwn_ref))

    gate = gate_ref[0]
    o_ref[0] = gate[:, 0:1] * o_c + gate[:, 1:2] * o_s + gate[:, 2:3] * o_w

    lane = lax.broadcasted_iota(jnp.int32, (KVW, wb), 1)
    for win_ref, new_ref, out_ref in ((kwin_ref, kwn_ref, kwo_ref), (vwin_ref, vwn_ref, vwo_ref)):
        new_t = jnp.concatenate([new_ref[0], pad_new], axis=0).T
        tail = pltpu.roll(jnp.concatenate([jnp.zeros((KVW, wb - QB), F32), new_t], axis=1), QB - t_new, 1)
        out_ref[0] = jnp.where(lane >= wb - t_new, tail, pltpu.roll(win_ref[0], wb - t_new, 1))


def _nsa_sample(page_table, q, gate, kc, vc, kpool, vpool, ksn, vsn, kwin, vwin, kwn, vwn,
                bc, bs, bw, ovl, e, n_sel, past, t_new):
    ns, n_pages = page_table.shape
    rows, kvw = q.shape[1:]
    buf_shape = (2, n_pages) + kpool.shape[1:]
    seq3 = lambda s, pt: (s, 0, 0)
    any_spec = pl.BlockSpec(memory_space=pl.ANY)
    cs = lambda a: pl.BlockSpec(a.shape, lambda s, pt, nd=a.ndim: (0,) * nd, pipeline_mode=pl.Buffered(1))
    per_seq = lambda a: pl.BlockSpec((1,) + a.shape[1:], seq3)
    return pl.pallas_call(
        functools.partial(_nsa_sample_kernel, n_sel=n_sel, past=past, t_new=t_new),
        grid_spec=pltpu.PrefetchScalarGridSpec(
            num_scalar_prefetch=1,
            grid=(ns,),
            in_specs=[per_seq(q), per_seq(gate), per_seq(kc), per_seq(vc), any_spec, any_spec,
                      per_seq(ksn), per_seq(vsn), per_seq(kwin), per_seq(vwin), per_seq(kwn), per_seq(vwn),
                      cs(bc), cs(bs), cs(bw), cs(ovl), cs(e)],
            out_specs=[pl.BlockSpec((1, rows, kvw), seq3), per_seq(kwin), per_seq(vwin)],
            scratch_shapes=[pltpu.VMEM(buf_shape, F32), pltpu.VMEM(buf_shape, F32), pltpu.SemaphoreType.DMA((2, 2))]),
        out_shape=[jax.ShapeDtypeStruct((ns, rows, kvw), F32), jax.ShapeDtypeStruct(kwin.shape, F32),
                   jax.ShapeDtypeStruct(vwin.shape, F32)],
        compiler_params=_cparams(("arbitrary",)),
        name="nsa_sample",
    )(page_table, q, gate, kc, vc, kpool, vpool, ksn, vsn, kwin, vwin, kwn, vwn, bc, bs, bw, ovl, e)


def _ssm_param_kernel(ar_ref, ai_ref, ldt_ref, br_ref, bi_ref, abr_ref, abi_ref, bbr_ref, bbi_ref):
    ar = ar_ref[...]
    ai = ai_ref[...]
    dt = jnp.exp(ldt_ref[...])
    mag = jnp.exp(ar * dt)
    abr = mag * jnp.cos(ai * dt)
    abi = mag * jnp.sin(ai * dt)
    den = ar * ar + ai * ai
    nr, ni = abr - 1.0, abi
    fr = (nr * ar + ni * ai) / den
    fi = (ni * ar - nr * ai) / den
    abr_ref[...] = abr
    abi_ref[...] = abi
    for g in range(ar.shape[0]):
        br = br_ref[g]
        bi = bi_ref[g]
        frg = fr[g:g + 1, :]
        fig = fi[g:g + 1, :]
        bbr_ref[g] = frg * br - fig * bi
        bbi_ref[g] = frg * bi + fig * br


def _ssm_params(a_re, a_im, log_dt, b_re_t, b_im_t):
    g, p = a_re.shape
    return pl.pallas_call(
        _ssm_param_kernel,
        out_shape=[jax.ShapeDtypeStruct((g, p), F32)] * 2 + [jax.ShapeDtypeStruct(b_re_t.shape, F32)] * 2,
        name="ssm_params",
    )(a_re, a_im, log_dt.reshape(g, 1), b_re_t, b_im_t)


def _ssm_kernel(u_ref, h0r_ref, h0i_ref, ar_ref, ai_ref, bd_ref, cd_ref, d_ref, wglu_ref, bglu_ref,
                so_ref, hr_ref, hi_ref, xr_scr, xi_scr, *slab_scr, bt):
    i = pl.program_id(0)
    n_slab = bd_ref.shape[0]
    width = n_slab * LANES
    sw = bd_ref.shape[2] // 2
    if slab_scr:
        slab = slab_scr[0]
        tt = u_ref.shape[0]
        rows = tt * bt
        u_wide = u_ref[...].astype(F32)
        for b in range(bt):
            for sl in range(n_slab):
                lanes = slice(b * width + sl * LANES, b * width + (sl + 1) * LANES)
                slab[sl, pl.ds(b, tt, stride=bt), :] = u_wide[:, lanes]
        u_slabs = [slab[sl] for sl in range(n_slab)]
    else:
        rows = u_ref.shape[0]
        u_rows = u_ref[...].astype(F32)
        u_slabs = [u_rows[:, sl * LANES:(sl + 1) * LANES] for sl in range(n_slab)]

    @pl.when(i == 0)
    def _():
        hr_ref[...] = h0r_ref[...]
        hi_ref[...] = h0i_ref[...]

    for sl in range(n_slab):
        x = _dot(u_slabs[sl].astype(BF16), bd_ref[sl])
        xr_scr[:, sl * sw:(sl + 1) * sw] = x[:, :sw]
        xi_scr[:, sl * sw:(sl + 1) * sw] = x[:, sw:]

    per = 8 // math.gcd(bt, 8)
    grp = per * bt
    lc = 512
    for c0 in range(0, xr_scr.shape[1], lc):
        cl = slice(c0, c0 + lc)
        a_r = jnp.broadcast_to(ar_ref[:, cl], (bt, lc))
        a_i = jnp.broadcast_to(ai_ref[:, cl], (bt, lc))

        def step(j, carry):
            h_r, h_i = carry
            r0 = pl.multiple_of(j * grp, grp)
            xr = xr_scr[pl.ds(r0, grp), cl]
            xi = xi_scr[pl.ds(r0, grp), cl]
            out_r, out_i = [], []
            for s in range(per):
                n_r = a_r * h_r - a_i * h_i + xr[s * bt:(s + 1) * bt]
                n_i = a_r * h_i + a_i * h_r + xi[s * bt:(s + 1) * bt]
                h_r, h_i = n_r, n_i
                out_r.append(h_r)
                out_i.append(h_i)
            xr_scr[pl.ds(r0, grp), cl] = jnp.concatenate(out_r, axis=0) if per > 1 else out_r[0]
            xi_scr[pl.ds(r0, grp), cl] = jnp.concatenate(out_i, axis=0) if per > 1 else out_i[0]
            return h_r, h_i

        h_r, h_i = lax.fori_loop(0, rows // grp, step, (hr_ref[:, cl], hi_ref[:, cl]))
        hr_ref[:, cl] = h_r
        hi_ref[:, cl] = h_i

    ys = []
    for sl in range(n_slab):
        hcat = jnp.concatenate([xr_scr[:, sl * sw:(sl + 1) * sw], xi_scr[:, sl * sw:(sl + 1) * sw]], axis=1)
        ys.append(_dot(hcat.astype(BF16), cd_ref[sl]))
    y = jnp.concatenate(ys, axis=1) + d_ref[...] * jnp.concatenate(u_slabs, axis=1)
    z = _gelu(y)
    so = z * _sigmoid(_dot(z.astype(BF16), wglu_ref[...]) + bglu_ref[...])
    if slab_scr:
        for sl in range(n_slab):
            slab[sl] = so[:, sl * LANES:(sl + 1) * LANES]
        for b in range(bt):
            for sl in range(n_slab):
                lanes = slice(b * width + sl * LANES, b * width + (sl + 1) * LANES)
                so_ref[:, lanes] = slab[sl, pl.ds(b, tt, stride=bt), :].astype(so_ref.dtype)
    else:
        so_ref[...] = so.astype(so_ref.dtype)


def _ssm(u, h0r, h0i, ar, ai, bd, cd, dvec, wglu, bglu, bt, tt, time_major):
    width = bd.shape[0] * LANES
    rows = tt * bt
    nstate = ar.shape[1]
    blk = (tt, bt * width) if time_major else (rows, width)
    cst = [_const_spec(a.shape) for a in (h0r, h0i, ar, ai, bd, cd, dvec, wglu, bglu)]
    st_spec = _const_spec((bt, nstate))
    scratch = [pltpu.VMEM((rows, nstate), F32), pltpu.VMEM((rows, nstate), F32)]
    if time_major:
        scratch.append(pltpu.VMEM((bd.shape[0], rows, LANES), F32))
    return pl.pallas_call(
        functools.partial(_ssm_kernel, bt=bt),
        grid=(u.shape[0] // blk[0],),
        in_specs=[pl.BlockSpec(blk, lambda i: (i, 0))] + cst,
        out_specs=[pl.BlockSpec(blk, lambda i: (i, 0)), st_spec, st_spec],
        out_shape=[jax.ShapeDtypeStruct(u.shape, BF16), jax.ShapeDtypeStruct((bt, nstate), F32),
                   jax.ShapeDtypeStruct((bt, nstate), F32)],
        scratch_shapes=scratch,
        compiler_params=_cparams(("arbitrary",)),
        name="ssm",
    )(u, h0r, h0i, ar, ai, bd, cd, dvec, wglu, bglu)


def _back_kernel(h_ref, o_ref, so_ref, mg_ref, p_ref, watt_ref, wssm_ref, wo_ref, fn_ref, wg_ref, wu_ref, wd_ref,
                 pn_ref, wpg_ref, wple_ref, fin_ref, y_ref, *, ff_chunk):
    d = h_ref.shape[1]
    a = _dot(o_ref[...], watt_ref[...])
    s = _dot(so_ref[...], wssm_ref[...])
    mg = mg_ref[...].astype(F32)
    h = h_ref[...] + _dot((mg[:, :d] * a + mg[:, d:] * s).astype(BF16), wo_ref[...])
    f = _rms(h, fn_ref[...]).astype(BF16)
    ffn = jnp.zeros_like(h)
    for c0 in range(0, wg_ref.shape[1], ff_chunk):
        gate = _dot(f, wg_ref[:, c0:c0 + ff_chunk])
        up = _dot(f, wu_ref[:, c0:c0 + ff_chunk])
        ffn = ffn + _dot((gate * _sigmoid(gate) * up).astype(BF16), wd_ref[c0:c0 + ff_chunk, :])
    h = h + ffn
    g = _sigmoid(_dot(_rms(h, pn_ref[...]).astype(BF16), wpg_ref[...]))
    h = h + g * _dot(p_ref[...].astype(BF16), wple_ref[...])
    y_ref[...] = _rms(h, fin_ref[...])


def _back(h2d, o2d, so_tb, mg, p2d, weights, nb, t, tm, ff_chunk):
    n, d = h2d.shape
    nt = t // tm
    row = lambda b, i: (b * nt + i, 0)
    wspecs = [pl.BlockSpec(w.shape, lambda b, i, nd=w.ndim: (0,) * nd, pipeline_mode=pl.Buffered(1))
              for w in weights]
    sw = o2d.shape[1]
    return pl.pallas_call(
        functools.partial(_back_kernel, ff_chunk=ff_chunk),
        grid=(nb, nt),
        in_specs=[pl.BlockSpec((tm, d), row), pl.BlockSpec((tm, sw), row),
                  pl.BlockSpec((tm, sw), lambda b, i: (i, b)),
                  pl.BlockSpec((tm, mg.shape[1]), row), pl.BlockSpec((tm, p2d.shape[1]), row)] + wspecs,
        out_specs=pl.BlockSpec((tm, d), row),
        out_shape=jax.ShapeDtypeStruct((n, d), F32),
        compiler_params=_cparams(("parallel", "parallel")),
        name="back",
    )(h2d, o2d, so_tb, mg, p2d, *weights)


def _bucket_np(dist):
    n = np.maximum(dist, 0)
    exact = N_BUCKETS // 2
    nf = np.maximum(n, 1).astype(np.float64)
    large = exact + (np.log(nf / exact) / math.log(REL_MAX_DIST / exact) * (N_BUCKETS - exact)).astype(np.int64)
    return np.where(n < exact, n, np.minimum(large, N_BUCKETS - 1)).astype(np.int32)


def _bias_table(rel_bias, dist, valid, offset=None):
    onehot = jax.nn.one_hot(jnp.asarray(_bucket_np(dist)), N_BUCKETS, dtype=F32)
    b = jnp.einsum('...b,bh->h...', onehot, rel_bias.astype(F32), precision=lax.Precision.HIGHEST)
    if offset is not None:
        b = b - offset.reshape((N_HEADS,) + (1,) * dist.ndim)
    return jnp.where(jnp.asarray(valid)[None], b, NEG)


def _prompt_tables(rel_bias, t):
    def cols(b):
        r = b.shape[1]
        return b.reshape(N_KV, GROUP, r, QB).transpose(0, 2, 1, 3).reshape(N_KV, r, GROUP * QB)
    rel_bias = rel_bias.astype(F32) * LOG2E
    c = rel_bias[N_BUCKETS - 1]
    c_hi = c.astype(BF16)
    c_lo = (c - c_hi.astype(F32)).astype(BF16)
    c_eff = c_hi.astype(F32) + c_lo.astype(F32)
    crow = jnp.stack([c_hi.astype(F32), c_lo.astype(F32)] + [jnp.zeros_like(c)] * 6, axis=1)
    crow = jnp.broadcast_to(crow[:, :, None], (N_HEADS, 8, QB))
    crow = cols(crow)
    j = np.arange(QB)[:, None]
    i = np.arange(QB)[None, :]
    ones = np.ones((QB, QB), bool)
    zeros = jnp.zeros((N_KV, QB, GROUP * QB), F32)
    at = jnp.stack([
        cols(_bias_table(rel_bias, i - j, i >= j, c_eff)),
        cols(_bias_table(rel_bias, QB + i - j, ones, c_eff)),
        zeros,
        jnp.where(jnp.asarray(np.tile(j > i, (1, GROUP)))[None], zeros, NEG),
        zeros + NEG,
    ])
    ncp = t // CMP_STRIDE
    m = np.arange(2 * ncp)[:, None] - ncp
    dist = i - CMP_STRIDE * m - (CMP_BLOCK - 1)
    ut = cols(_bias_table(rel_bias, dist, dist >= 0))
    n_sel = t // SEL_BLOCK
    n = np.arange(ncp)[None, :]
    jb = np.arange(LANES)[:, None]
    ovlt = ((n * CMP_STRIDE < jb * SEL_BLOCK + SEL_BLOCK) & (n * CMP_STRIDE + CMP_BLOCK - 1 >= jb * SEL_BLOCK)
            & (jb < n_sel) & (n < ncp - 1))
    key = np.arange(t)[:, None]
    lane = np.arange(LANES)[None, :]
    stat = np.where(lane < LANES // 2, (lane == key // SEL_BLOCK) * BIG,
                    ((lane == LANES // 2) | (lane == LANES // 2 + 1)) * 1.0).astype(np.float32)
    return ut, at, jnp.asarray(ovlt.astype(np.float32), BF16), jnp.asarray(stat, BF16), crow


def _sample_tables(rel_bias, past, t_new, win_buf):
    def rows(b):
        return b.reshape(N_HEADS * t_new, b.shape[-1])
    tok = np.arange(t_new)[:, None]
    nc = past // CMP_STRIDE
    n = np.arange(nc)[None, :]
    c_end = n * CMP_STRIDE + CMP_BLOCK - 1
    n_cmp = (past + t_new) // CMP_STRIDE - 1
    bc = rows(_bias_table(rel_bias, past + tok - c_end, (c_end <= past + tok) & (n < n_cmp)))
    js = np.arange(past + QB)[None, :]
    ds = np.where(js < past, past + tok - js, tok - (js - past))
    bs = rows(_bias_table(rel_bias, ds, np.where(js < past, True, (ds >= 0) & (js - past < t_new))))
    jw = np.arange(win_buf + QB)[None, :]
    dw = np.where(jw < win_buf, win_buf + tok - jw, tok - (jw - win_buf))
    valid = np.where(jw < win_buf, (dw >= 0) & (dw < WINDOW), (dw >= 0) & (jw - win_buf < t_new))
    bw = rows(_bias_table(rel_bias, dw, valid))
    n_sel = -(-(past + t_new) // SEL_BLOCK)
    nbp = 2 * LANES
    nn = np.arange(nc)[:, None]
    jb = np.arange(nbp)[None, :]
    ovl = ((nn * CMP_STRIDE < jb * SEL_BLOCK + SEL_BLOCK) & (nn * CMP_STRIDE + CMP_BLOCK - 1 >= jb * SEL_BLOCK)
           & (jb < n_sel) & (nn < n_cmp))
    e = (np.arange(LANES)[:, None] == np.arange(past)[None, :] // SEL_BLOCK).astype(np.float32) * BIG
    return bc, bs, bw, jnp.asarray(ovl.astype(np.float32), BF16), jnp.asarray(e, BF16), n_sel


def _slab_diag(blocks, n_slab):
    g, r, c = blocks.shape
    gps = g // n_slab
    eye = jnp.eye(gps, dtype=blocks.dtype)
    return jnp.einsum('sgrc,gh->sgrhc', blocks.reshape(n_slab, gps, r, c), eye).reshape(n_slab, gps * r, gps * c)


def _layer_params(rel_bias, final_norm, attn_norm, w_in, cmp_pe_k, cmp_w1_k, cmp_w2_k, cmp_pe_v, cmp_w1_v, cmp_w2_v,
                  ssm_a_re, ssm_a_im, ssm_log_dt, ssm_b_re, ssm_b_im, ssm_c_re, ssm_c_im, ssm_d, w_glu, b_glu,
                  w_att_br, w_ssm_br, w_o, ffn_norm, w_ffn_gate, w_ffn_up, w_ffn_down, ple_norm, w_ple_gate, w_ple):
    l = 0
    d = w_in.shape[1]
    n_groups = ssm_a_re.shape[1]
    ssm_w = n_groups * SSM_CH
    nstate = n_groups * SSM_P
    assert ssm_w % LANES == 0
    w = w_in[l]
    c0 = AW + 6 * KVW
    n_gate = N_HEADS * N_BRANCH
    front_w = (attn_norm[l].reshape(1, d), w[:, :c0].astype(BF16),
               jnp.pad(w[:, c0:c0 + n_gate], ((0, 0), (0, LANES - n_gate))).astype(BF16),
               w[:, c0 + n_gate:c0 + n_gate + ssm_w].astype(BF16), w[:, c0 + n_gate + ssm_w:].astype(BF16))

    def cmp_weights(pe, w1, w2):
        half = CMP_STRIDE * HD
        w1cat = jnp.concatenate([w1[:half], w1[half:]], axis=1).astype(BF16)
        pe2 = jnp.pad(pe.reshape(2, half), ((0, 6), (0, 0))).astype(BF16)
        return w1cat, w2.astype(BF16), pe2
    cw = cmp_weights(cmp_pe_k[l], cmp_w1_k[l], cmp_w2_k[l]) + cmp_weights(cmp_pe_v[l], cmp_w1_v[l], cmp_w2_v[l])

    abr, abi, bbr_t, bbi_t = _ssm_params(ssm_a_re[l], ssm_a_im[l], ssm_log_dt[l],
                                         jnp.swapaxes(ssm_b_re[l], 1, 2), jnp.swapaxes(ssm_b_im[l], 1, 2))
    n_slab = ssm_w // LANES
    bd = jnp.concatenate([_slab_diag(bbr_t, n_slab), _slab_diag(bbi_t, n_slab)], axis=2).astype(BF16)
    cd = jnp.concatenate([_slab_diag(jnp.swapaxes(ssm_c_re[l], 1, 2), n_slab),
                          -_slab_diag(jnp.swapaxes(ssm_c_im[l], 1, 2), n_slab)], axis=1).astype(BF16)
    ssm_p = (abr.reshape(1, nstate), abi.reshape(1, nstate), bd, cd, ssm_d[l].reshape(1, ssm_w),
             w_glu[l].astype(BF16), b_glu[l].reshape(1, ssm_w))

    back_w = (w_att_br[l].astype(BF16), w_ssm_br[l].astype(BF16), w_o[l].astype(BF16),
              ffn_norm[l].reshape(1, d), w_ffn_gate[l].astype(BF16), w_ffn_up[l].astype(BF16),
              w_ffn_down[l].astype(BF16), ple_norm[l].reshape(1, d), w_ple_gate[l].astype(BF16),
              w_ple[l].astype(BF16), final_norm.reshape(1, d))
    d_ff = w_ffn_gate.shape[2]
    ff_chunk = d_ff // 2 if (d_ff // 2) % LANES == 0 else d_ff
    return dict(front=front_w, cmp=cw, ssm=ssm_p, back=back_w, ff_chunk=ff_chunk, rel_bias=rel_bias,
                n_groups=n_groups, ssm_w=ssm_w, nstate=nstate, n_gate=n_gate)


def _prompt_group(x_prompt, p_l, prm):
    nb, t, d = x_prompt.shape
    ssm_w, nstate = prm["ssm_w"], prm["nstate"]
    assert t % (CMP_STRIDE * LANES) == 0 and t >= WINDOW and t // SEL_BLOCK <= LANES // 2
    xp = x_prompt.reshape(nb * t, d)
    (qt, kct, vct, kst, vst32, kc, vc, kw, vw, ksb, kwb, vst, vwt, ngt, su, mg) = _front_prompt(
        xp, nb, t, prm["front"], 1024)
    kcc, vcct = _compress_prompt(kc.reshape(nb, t, KVW), vc.reshape(nb, t, KVW), prm["cmp"])
    o = _nsa_prompt(qt, ngt, kcc, vcct, ksb.reshape(nb, t, KVW), vst, kwb.reshape(nb, t, KVW), vwt,
                    _prompt_tables(prm["rel_bias"], t))
    zeros_state = jnp.zeros((nb, nstate), F32)
    so, sr, si = _ssm(su, zeros_state, zeros_state, *prm["ssm"], nb, 256, True)
    y = _back(xp, o.reshape(nb * t, AW), so, mg, p_l.reshape(nb * t, -1),
              prm["back"], nb, t, 512, prm["ff_chunk"])
    return dict(y=y, o=o, so=so, rows_t=(kct, vct, kst, vst32), win=(kw, vw), state=(sr, si))


def _sample_group(x_sample, p_l, page_table, pools, wins, states, prm):
    ns, t_new, d = x_sample.shape
    tok_minor = lambda a: jnp.transpose(a, (0, 2, 3, 1)).reshape(a.shape[0], KVW, a.shape[1])
    k_cmp, v_cmp, k_sel, v_sel = pools
    k_win, v_win = wins
    n_phys, page = k_cmp.shape[:2]
    past = page_table.shape[1] * page
    win_buf = k_win.shape[1]
    ssm_w, nstate, n_gate = prm["ssm_w"], prm["nstate"], prm["n_gate"]
    assert page == QB and past % (CMP_STRIDE * LANES) == 0 and past // SEL_BLOCK <= LANES
    assert win_buf == WINDOW and past >= win_buf and t_new <= 8 and t_new < CMP_STRIDE
    n_s = ns * t_new
    xs = x_sample.reshape(n_s, d)
    q_s, kc_s, vc_s, ks_s, vs_s, kw_s, vw_s, ng_s, su_s, mg_s = _front_sample(xs, prm["front"])
    kcc_s, vcc_s = _compress_sample(page_table, tok_minor(k_cmp), tok_minor(v_cmp), prm["cmp"])
    bc, bs, bw, ovl_s, e_s, n_sel_s = _sample_tables(prm["rel_bias"], past, t_new, win_buf)
    rows_s = N_HEADS * t_new
    eye_kv = jnp.eye(N_KV, dtype=BF16)
    q_rows = q_s.reshape(ns, t_new, N_KV, GROUP, HD).transpose(0, 2, 3, 1, 4)
    q_rows = jnp.einsum('skgtd,kj->skgtjd', q_rows, eye_kv).reshape(ns, rows_s, KVW)
    gate_s = ng_s[:, :n_gate].reshape(ns, t_new, N_KV, GROUP, N_BRANCH).transpose(0, 2, 3, 1, 4)
    gate_s = jnp.pad(gate_s.reshape(ns, rows_s, N_BRANCH), ((0, 0), (0, 0), (0, LANES - N_BRANCH)))
    pad8 = lambda a: jnp.pad(a.reshape(ns, t_new, KVW), ((0, 0), (0, 8 - t_new), (0, 0)))
    o_s, kwin_new, vwin_new = _nsa_sample(
        page_table, q_rows, gate_s, kcc_s, vcc_s,
        tok_minor(k_sel), tok_minor(v_sel), pad8(ks_s), pad8(vs_s), tok_minor(k_win), tok_minor(v_win),
        pad8(kw_s), pad8(vw_s), bc, bs, bw, ovl_s, e_s, n_sel_s, past, t_new)
    o_s = o_s.reshape(ns, N_KV, GROUP, t_new, N_KV, HD)
    o_s = jnp.stack([o_s[:, k, :, :, k, :] for k in range(N_KV)], axis=1)
    o_s = o_s.transpose(0, 3, 1, 2, 4).reshape(n_s, AW).astype(BF16)
    su_ts = su_s.reshape(ns, t_new, ssm_w).transpose(1, 0, 2).reshape(n_s, ssm_w)
    so_ts, sr, si = _ssm(su_ts, states[0].reshape(ns, nstate), states[1].reshape(ns, nstate), *prm["ssm"], ns, t_new,
                         False)
    so_s = so_ts.reshape(t_new, ns, ssm_w).transpose(1, 0, 2).reshape(n_s, ssm_w)
    y = _back(xs, o_s, so_s, mg_s, p_l.reshape(n_s, -1), prm["back"], 1, n_s, n_s, prm["ff_chunk"])
    return dict(y=y, o=o_s, so=so_s, rows=(kc_s, vc_s, ks_s, vs_s), win_t=(kwin_new, vwin_new), state=(sr, si))


def kernel(x_prompt, x_sample, cache_k_cmp, cache_v_cmp, cache_k_sel, cache_v_sel, cache_k_win, cache_v_win, state_ssm_re, state_ssm_im, page_table, p_prompt, p_sample, rel_bias, final_norm, attn_norm, w_in, cmp_pe_k, cmp_w1_k, cmp_w2_k, cmp_pe_v, cmp_w1_v, cmp_w2_v, ssm_a_re, ssm_a_im, ssm_log_dt, ssm_b_re, ssm_b_im, ssm_c_re, ssm_c_im, ssm_d, w_glu, b_glu, w_att_br, w_ssm_br, w_o, ffn_norm, w_ffn_gate, w_ffn_up, w_ffn_down, ple_norm, w_ple_gate, w_ple):
    assert w_in.shape[0] == 1, "single-layer trunk"
    l = 0
    nb, t, d = x_prompt.shape
    ns, t_new = x_sample.shape[:2]
    prm = _layer_params(rel_bias, final_norm, attn_norm, w_in, cmp_pe_k, cmp_w1_k, cmp_w2_k, cmp_pe_v, cmp_w1_v,
                        cmp_w2_v, ssm_a_re, ssm_a_im, ssm_log_dt, ssm_b_re, ssm_b_im, ssm_c_re, ssm_c_im, ssm_d,
                        w_glu, b_glu, w_att_br, w_ssm_br, w_o, ffn_norm, w_ffn_gate, w_ffn_up, w_ffn_down,
                        ple_norm, w_ple_gate, w_ple)
    pg = _prompt_group(x_prompt, p_prompt[l], prm)
    sg = _sample_group(x_sample, p_sample[l], page_table,
                       (cache_k_cmp[l], cache_v_cmp[l], cache_k_sel[l], cache_v_sel[l]),
                       (cache_k_win[l], cache_v_win[l]), (state_ssm_re[l], state_ssm_im[l]), prm)

    kv5 = lambda a, b_, t_: a.reshape(1, b_, t_, N_KV, HD)
    kv5_t = lambda a: jnp.transpose(a.reshape(1, nb, N_KV, HD, t), (0, 1, 4, 2, 3))
    keep = min(WINDOW, t)
    win_p = lambda a: a.reshape(nb, t, KVW)[:, t - keep:].reshape(1, nb, keep, N_KV, HD)
    win_s = lambda a: jnp.transpose(a.reshape(1, ns, N_KV, HD, a.shape[-1]), (0, 1, 4, 2, 3))
    st = lambda a, b_: a.reshape(1, b_, prm["n_groups"], SSM_P)
    kc_s, vc_s, ks_s, vs_s = sg["rows"]
    return (pg["y"].reshape(nb, t, d), sg["y"].reshape(ns, t_new, d),
            *[kv5_t(a) for a in pg["rows_t"]], win_p(pg["win"][0]), win_p(pg["win"][1]),
            st(pg["state"][0], nb), st(pg["state"][1], nb),
            kv5(kc_s, ns, t_new), kv5(vc_s, ns, t_new), kv5(ks_s, ns, t_new), kv5(vs_s, ns, t_new),
            win_s(sg["win_t"][0]), win_s(sg["win_t"][1]),
            st(sg["state"][0], ns), st(sg["state"][1], ns))
```

```python
import functools
import math

import numpy as np
import jax
import jax.numpy as jnp
from jax import lax
from jax.experimental import pallas as pl
from jax.experimental.pallas import tpu as pltpu

F32 = jnp.float32
BF16 = jnp.bfloat16

N_HEADS = 8
N_KV = 2
HD = 64
GROUP = N_HEADS // N_KV
N_BRANCH = 3
CMP_BLOCK = 32
CMP_STRIDE = 16
CMP_HIDDEN = 256
CMP_PITCH = 24
SEL_BLOCK = 64
SEL_TOPK = 16
WINDOW = 512
QB = 128
SEL_KEYS = 4 * QB
SEL_KEYS_NEAR = 2 * QB
SEL_SPLIT = 2
N_BUCKETS = 32
REL_MAX_DIST = 128
SSM_CH = 16
SSM_P = 64
EPS = 1e-6
NEG = -1e30
NEG_TEST = -1e29
FORCE = 1e9
BIG = 1e30
LOG2E = math.log2(math.e)
LANES = 128
VMEM_LIMIT = 56 * 1024 * 1024
FRONT_ROWS = 1024
BACK_ROWS = 512
SSM_STEPS = 256
AW = N_HEADS * HD
KVW = N_KV * HD
NG_ROWS = 32


def _cparams(sem):
    return pltpu.CompilerParams(dimension_semantics=sem, vmem_limit_bytes=VMEM_LIMIT)


def _const_spec(shape):
    nd = len(shape)
    return pl.BlockSpec(shape, lambda *_: (0,) * nd)


def _rms(x, g):
    return x * lax.rsqrt(jnp.mean(x * x, axis=-1, keepdims=True) + EPS) * g


def _gelu(x):
    return x * (0.5 * (1.0 + jnp.tanh(math.sqrt(2.0 / math.pi) * (x + 0.044715 * (x * x * x)))))


def _sigmoid(x):
    return 1.0 / (1.0 + jnp.exp(-x))


def _dot(a, b):
    return jnp.dot(a, b, preferred_element_type=F32)


def _dot_t(a, b):
    return lax.dot_general(a, b, (((1,), (1,)), ((), ())), preferred_element_type=F32)


def _masked_softmax(s, axis=-1):
    valid = s > NEG_TEST
    m = jnp.max(s, axis=axis, keepdims=True)
    e = jnp.where(valid, jnp.exp(s - m), 0.0)
    return e / jnp.maximum(jnp.sum(e, axis=axis, keepdims=True), 1e-30)


def _softmax2_cols(s):
    m = jnp.max(s, axis=0, keepdims=True)
    e = jnp.exp2(s - m)
    inv = jnp.where(m > NEG_TEST, 1.0 / jnp.maximum(jnp.sum(e, axis=0, keepdims=True), 1e-30), 0.0)
    return e * inv


def _front_project(x_ref, g_ref, wa_ref, wng_ref, wsu_ref, wmg_ref, su_ref, mg_ref, q_scale):
    u = _rms(x_ref[...], g_ref[...]).astype(BF16)
    za = _dot(u, wa_ref[...])
    q = za[:, :AW] * q_scale
    rows = [za[:, AW + i * KVW: AW + (i + 1) * KVW] for i in range(6)]
    ng = _sigmoid(_dot(u, wng_ref[...]))
    su_ref[...] = _dot(u, wsu_ref[...]).astype(su_ref.dtype)
    mg_ref[...] = _sigmoid(_dot(u, wmg_ref[...])).astype(BF16)
    return q, rows, ng


def _front_sample_kernel(x_ref, g_ref, wa_ref, wng_ref, wsu_ref, wmg_ref,
                         q_ref, kc_ref, vc_ref, ks_ref, vs_ref, kw_ref, vw_ref, ng_ref, su_ref, mg_ref):
    q, rows, ng = _front_project(x_ref, g_ref, wa_ref, wng_ref, wsu_ref, wmg_ref, su_ref, mg_ref, HD ** -0.5)
    q_ref[...] = q.astype(BF16)
    for ref, r in zip((kc_ref, vc_ref, ks_ref, vs_ref, kw_ref, vw_ref), rows):
        ref[...] = r
    ng_ref[...] = ng


def _front_prompt_kernel(x_ref, g_ref, wa_ref, wng_ref, wsu_ref, wmg_ref,
                         qt_ref, kct_ref, vct_ref, kst_ref, vst32_ref, kc_ref, vc_ref, kw_ref, vw_ref,
                         ksb_ref, kwb_ref, vst_ref, vwt_ref, ngt_ref, su_ref, mg_ref):
    q, rows, ng = _front_project(x_ref, g_ref, wa_ref, wng_ref, wsu_ref, wmg_ref, su_ref, mg_ref,
                                 HD ** -0.5 * LOG2E)
    kc, vc, ks, vs, kw, vw = rows
    qt_ref[0] = q.T.astype(BF16)
    for ref, r in zip((kct_ref, vct_ref, kst_ref, vst32_ref), (kc, vc, ks, vs)):
        ref[0] = r.T
    kc_ref[...] = kc
    vc_ref[...] = vc
    kw_ref[...] = kw
    vw_ref[...] = vw
    ksb_ref[...] = ks.astype(BF16)
    kwb_ref[...] = kw.astype(BF16)
    for ref, r in ((vst_ref, vs), (vwt_ref, vw)):
        rt = r.T.astype(BF16)
        for j in range(rt.shape[1] // QB):
            ref[0, j] = rt[:, j * QB:(j + 1) * QB]
    ngt_ref[0] = ng.T[0:NG_ROWS]


def _front_sample(x2d, fw):
    g, wa, wng, wsu, wmg = fw
    n, d = x2d.shape
    sw, mw = wsu.shape[1], wmg.shape[1]
    shapes = ([jax.ShapeDtypeStruct((n, AW), BF16)] + [jax.ShapeDtypeStruct((n, KVW), F32)] * 6
              + [jax.ShapeDtypeStruct((n, LANES), F32), jax.ShapeDtypeStruct((n, sw), BF16),
                 jax.ShapeDtypeStruct((n, mw), BF16)])
    return pl.pallas_call(
        _front_sample_kernel,
        grid=(1,),
        in_specs=[_const_spec(a.shape) for a in (x2d, g, wa, wng, wsu, wmg)],
        out_specs=[_const_spec(s.shape) for s in shapes],
        out_shape=shapes,
        compiler_params=_cparams(("arbitrary",)),
        name="front_sample",
    )(x2d, g, wa, wng, wsu, wmg)


def _front_prompt(x2d, nb, t, fw, tm):
    g, wa, wng, wsu, wmg = fw
    n, d = x2d.shape
    nt = t // tm
    sw, mw = wsu.shape[1], wmg.shape[1]
    row = lambda b, i: (b * nt + i, 0)
    kv5 = jax.ShapeDtypeStruct((nb, KVW, t), F32)
    kv5_spec = pl.BlockSpec((1, KVW, tm), lambda b, i: (b, 0, i))
    vt = jax.ShapeDtypeStruct((nb, t // QB, KVW, QB), BF16)
    vt_spec = pl.BlockSpec((1, tm // QB, KVW, QB), lambda b, i: (b, i, 0, 0))
    shapes = ([jax.ShapeDtypeStruct((nb, AW, t), BF16)] + [kv5] * 4 + [jax.ShapeDtypeStruct((n, KVW), F32)] * 4
              + [jax.ShapeDtypeStruct((n, KVW), BF16)] * 2 + [vt] * 2
              + [jax.ShapeDtypeStruct((nb, NG_ROWS, t), F32), jax.ShapeDtypeStruct((t, nb * sw), BF16),
                 jax.ShapeDtypeStruct((n, mw), BF16)])
    specs = ([pl.BlockSpec((1, AW, tm), lambda b, i: (b, 0, i))] + [kv5_spec] * 4
             + [pl.BlockSpec((tm, KVW), row)] * 6 + [vt_spec] * 2
             + [pl.BlockSpec((1, NG_ROWS, tm), lambda b, i: (b, 0, i)),
                pl.BlockSpec((tm, sw), lambda b, i: (i, b)), pl.BlockSpec((tm, mw), row)])
    return pl.pallas_call(
        _front_prompt_kernel,
        grid=(nb, nt),
        in_specs=[pl.BlockSpec((tm, d), row)] + [_const_spec(a.shape) for a in (g, wa, wng, wsu, wmg)],
        out_specs=specs,
        out_shape=shapes,
        compiler_params=_cparams(("parallel", "parallel")),
        name="front_prompt",
    )(x2d, g, wa, wng, wsu, wmg)


def _chunk_rows(load, r0, rn, pitch=CMP_STRIDE):
    return jnp.concatenate([load(pl.ds(pitch * r0 + r, rn, stride=pitch)) for r in range(CMP_STRIDE)], axis=1)


def _compress_compute(load_rows, c, w1_ref, w2_ref, pe_ref, a_scr):
    rc_n = min(c, 256)
    lo = lax.broadcasted_iota(jnp.int32, (rc_n, LANES), 1) < HD
    w1 = w1_ref[...]
    for rc in range(c // rc_n):
        x = load_rows(rc * rc_n, rc_n)
        cols = [x[:, r * LANES:(r + 1) * LANES] for r in range(CMP_STRIDE)]
        rol = [pltpu.roll(col, HD, 1) for col in cols]
        for kh in range(N_KV):
            if kh == 0:
                parts = [jnp.where(lo, cols[2 * j], rol[2 * j + 1]) for j in range(CMP_STRIDE // 2)]
            else:
                parts = [jnp.where(lo, rol[2 * j], cols[2 * j + 1]) for j in range(CMP_STRIDE // 2)]
            xh = jnp.concatenate(parts, axis=1).astype(BF16)
            a_scr[kh, rc * rc_n:(rc + 1) * rc_n, :] = _dot(xh, w1)
    pw = _dot(pe_ref[...], w1)
    peb = pw[0:1, :CMP_HIDDEN] + pw[1:2, CMP_HIDDEN:]
    w2 = w2_ref[...]
    outs = []
    for kh in range(N_KV):
        a = a_scr[kh]
        hid = a[:, :CMP_HIDDEN] + pltpu.roll(a[:, CMP_HIDDEN:], c - 1, 0) + peb
        outs.append(_dot(_gelu(hid).astype(BF16), w2))
    return jnp.concatenate(outs, axis=1)


def _compress_prompt_kernel(xk_ref, xv_ref, w1k_ref, w2k_ref, pek_ref, w1v_ref, w2v_ref, pev_ref,
                            ok_ref, ovt_ref, a_scr):
    c = xk_ref.shape[1] // CMP_STRIDE
    ok_ref[0] = _compress_compute(lambda r0, rn: _chunk_rows(lambda idx: xk_ref[0, idx, :], r0, rn), c,
                                  w1k_ref, w2k_ref, pek_ref, a_scr).astype(BF16)
    ovt_ref[0] = _compress_compute(lambda r0, rn: _chunk_rows(lambda idx: xv_ref[0, idx, :], r0, rn), c,
                                   w1v_ref, w2v_ref, pev_ref, a_scr).T.astype(BF16)


def _compress_prompt(xk, xv, cw):
    nb, t, kvw = xk.shape
    c = t // CMP_STRIDE
    wspecs = [_const_spec(w.shape) for w in cw]
    blk = pl.BlockSpec((1, t, kvw), lambda b: (b, 0, 0))
    return pl.pallas_call(
        _compress_prompt_kernel,
        grid=(nb,),
        in_specs=[blk, blk] + wspecs,
        out_specs=[pl.BlockSpec((1, c, KVW), lambda b: (b, 0, 0)), pl.BlockSpec((1, KVW, c), lambda b: (b, 0, 0))],
        out_shape=[jax.ShapeDtypeStruct((nb, c, KVW), BF16), jax.ShapeDtypeStruct((nb, KVW, c), BF16)],
        scratch_shapes=[pltpu.VMEM((N_KV, c, 2 * CMP_HIDDEN), F32)],
        compiler_params=_cparams(("parallel",)),
        name="compress_prompt",
    )(xk, xv, *cw)


def _page_copy(pool, buf, sem, page, p, slot):
    return pltpu.make_async_copy(pool.at[page], buf.at[slot, p], sem)


def _page_gather_start(pt_ref, seq, pools, bufs, sems, slot, n_pages):
    def body(p, carry):
        page = pt_ref[seq, p]
        for i, (pool, buf) in enumerate(zip(pools, bufs)):
            _page_copy(pool, buf, sems.at[i, slot], page, p, slot).start()
        return carry
    lax.fori_loop(0, n_pages, body, 0)


def _page_gather_wait(pools, bufs, sems, slot, n_pages):
    def body(p, carry):
        for i, (pool, buf) in enumerate(zip(pools, bufs)):
            _page_copy(pool, buf, sems.at[i, slot], 0, p, slot).wait()
        return carry
    lax.fori_loop(0, n_pages, body, 0)


def _paged_prefetch(pt_ref, pools, bufs, sems, n_pages):
    s = pl.program_id(0)
    slot = s % 2

    @pl.when(s == 0)
    def _():
        _page_gather_start(pt_ref, 0, pools, bufs, sems, 0, n_pages)

    @pl.when(s + 1 < pl.num_programs(0))
    def _():
        _page_gather_start(pt_ref, s + 1, pools, bufs, sems, 1 - slot, n_pages)

    _page_gather_wait(pools, bufs, sems, slot, n_pages)
    return slot


def _compress_sample_kernel(pt_ref, kpool, vpool, w1k_ref, w2k_ref, pek_ref, w1v_ref, w2v_ref, pev_ref,
                            ok_ref, ov_ref, kbuf, vbuf, sems, a_scr, rows_scr):
    n_pages = pt_ref.shape[1]
    page = kpool.shape[2]
    c = n_pages * page // CMP_STRIDE
    slot = _paged_prefetch(pt_ref, (kpool, vpool), (kbuf, vbuf), sems, n_pages)
    for buf, out_ref, w1_ref, w2_ref, pe_ref in ((kbuf, ok_ref, w1k_ref, w2k_ref, pek_ref),
                                                (vbuf, ov_ref, w1v_ref, w2v_ref, pev_ref)):
        for p in range(n_pages):
            rows = buf[slot, p].T
            for ch in range(page // CMP_STRIDE):
                r0 = (p * (page // CMP_STRIDE) + ch) * CMP_PITCH
                rows_scr[r0:r0 + CMP_STRIDE, :] = rows[ch * CMP_STRIDE:(ch + 1) * CMP_STRIDE]
        out_ref[0] = _compress_compute(
            lambda r0, rn: _chunk_rows(lambda idx: rows_scr[idx, :], r0, rn, CMP_PITCH), c,
            w1_ref, w2_ref, pe_ref, a_scr).astype(BF16)


def _compress_sample(page_table, kpool, vpool, cw):
    ns, n_pages = page_table.shape
    width, page = kpool.shape[1:]
    tokens = n_pages * page
    c = tokens // CMP_STRIDE
    buf_shape = (2, n_pages, width, page)
    any_spec = pl.BlockSpec(memory_space=pl.ANY)
    wspecs = [pl.BlockSpec(w.shape, lambda s, pt, nd=w.ndim: (0,) * nd) for w in cw]
    oblk = pl.BlockSpec((1, c, KVW), lambda s, pt: (s, 0, 0))
    return pl.pallas_call(
        _compress_sample_kernel,
        grid_spec=pltpu.PrefetchScalarGridSpec(
            num_scalar_prefetch=1,
            grid=(ns,),
            in_specs=[any_spec, any_spec] + wspecs,
            out_specs=[oblk, oblk],
            scratch_shapes=[pltpu.VMEM(buf_shape, F32), pltpu.VMEM(buf_shape, F32), pltpu.SemaphoreType.DMA((2, 2)),
                            pltpu.VMEM((N_KV, c, 2 * CMP_HIDDEN), F32), pltpu.VMEM((c * CMP_PITCH, width), F32)]),
        out_shape=[jax.ShapeDtypeStruct((ns, c, KVW), BF16)] * 2,
        compiler_params=_cparams(("arbitrary",)),
        name="compress_sample",
    )(page_table, kpool, vpool, *cw)


def _rank_select(score, blk, n_real, axis):
    size = 8 if axis == 0 else LANES
    total = score.shape[axis]
    chunk = (lambda a, c: a[c * size:(c + 1) * size]) if axis == 0 else (lambda a, c: a[:, c * size:(c + 1) * size])
    n_chunks = -(-total // size)
    sc = [chunk(score, c) for c in range(n_chunks)]
    bl = [chunk(blk, c) for c in range(n_chunks)]
    rank = [jnp.zeros(s.shape, F32) for s in sc]
    for kk in range(n_real):
        col = score[kk:kk + 1, :] if axis == 0 else score[:, kk:kk + 1]
        for c in range(n_chunks):
            other = jnp.broadcast_to(col, sc[c].shape)
            if c * size > kk:
                beats = other >= sc[c]
            elif min((c + 1) * size, total) - 1 < kk:
                beats = other > sc[c]
            else:
                beats = (other > sc[c]) | ((other == sc[c]) & (bl[c] > kk))
            rank[c] = rank[c] + jnp.where(beats, 1.0, 0.0)
    return jnp.where(jnp.concatenate(rank, axis=axis) < SEL_TOPK, 1.0, 0.0)


def _block_scores(imp, blk, t):
    cur = t // SEL_BLOCK
    forced = (blk == 0) | (blk == cur) | (blk == cur - 1)
    valid = blk * SEL_BLOCK <= t
    return jnp.where(valid, jnp.where(forced, FORCE, imp), NEG)


def _nsa_prompt_kernel(qt_ref, ngt_ref, kc_ref, vct_ref, ks_ref, vst_ref, kw_ref, vwt_ref,
                       ut_ref, at_ref, ovlt_ref, stat_ref, crow_ref, o_ref, *, n_sel):
    ib = pl.program_id(1)
    qt = qt_ref[0]
    ngt = ngt_ref[0]
    ncp = kc_ref.shape[1]
    cols = GROUP * QB
    sel_rows = LANES // 2
    t_row = ib * QB + lax.broadcasted_iota(jnp.int32, (n_sel, QB), 1)
    blk_t = lax.broadcasted_iota(jnp.int32, (n_sel, QB), 0)
    zeros_q = jnp.zeros((HD, cols), F32)
    vrows = [slice(k * HD, (k + 1) * HD) for k in range(N_KV)]
    q_sel, q_win, o_c, o_w = [], [], [], []
    for k in range(N_KV):
        qk = jnp.concatenate([qt[(GROUP * k + g) * HD:(GROUP * k + g + 1) * HD, :] for g in range(GROUP)],
                             axis=1).astype(F32)
        qa = jnp.concatenate([qk, zeros_q] if k == 0 else [zeros_q, qk], axis=0)

        bias_c = ut_ref[k, pl.ds(pl.multiple_of(ncp - (QB // CMP_STRIDE) * ib, 8), ncp), :]
        p_c = _softmax2_cols(_dot(kc_ref[0], qa.astype(BF16)) + bias_c)
        o_c.append(_dot(vct_ref[0][vrows[k], :], p_c.astype(BF16)))
        psum = p_c[:, 0:QB]
        for g in range(1, GROUP):
            psum = psum + p_c[:, g * QB:(g + 1) * QB]
        psum_hi = psum.astype(BF16)
        psum_lo = (psum - psum_hi.astype(F32)).astype(BF16)
        imp = _dot(ovlt_ref[...], psum_hi) + _dot(ovlt_ref[...], psum_lo)

        sel = _rank_select(_block_scores(imp[0:n_sel], blk_t, t_row), blk_t, n_sel, 0)
        selm1 = jnp.concatenate([sel - 1.0] * GROUP, axis=1)
        if n_sel < sel_rows:
            selm1 = jnp.concatenate([selm1, jnp.zeros((sel_rows - n_sel, cols), F32)], axis=0)
        tail = jnp.concatenate([crow_ref[k], jnp.zeros((LANES - sel_rows - 8, cols), F32)], axis=0)
        q_sel.append(jnp.concatenate([qa, selm1, tail], axis=0).astype(BF16))
        q_win.append(jnp.concatenate([qa, jnp.zeros((sel_rows, cols), F32), tail], axis=0).astype(BF16))

        s_parts, tiles_j = [], []
        for w, tidx in enumerate((0, 1, None, None, 3)):
            jt = ib - w
            jc = jnp.maximum(jt, 0)
            k0 = pl.multiple_of(jc * QB, QB)
            lhs = jnp.concatenate([kw_ref[0, pl.ds(k0, QB), :], stat_ref[pl.ds(k0, QB), :]], axis=1)
            s = _dot(lhs, q_win[k])
            if tidx is not None:
                s = s + at_ref[tidx, k]
            s_parts.append(jnp.where(jt >= 0, s, NEG))
            tiles_j.append(jc)
        p_w = _softmax2_cols(jnp.concatenate(s_parts, axis=0)).astype(BF16)
        o_wk = jnp.zeros((HD, cols), F32)
        for w, jc in enumerate(tiles_j):
            o_wk = o_wk + _dot(vwt_ref[0, jc, vrows[k], :], p_w[w * QB:(w + 1) * QB])
        o_w.append(o_wk)

    ones_pad = 16
    acc_rows = HD + ones_pad

    def make_step(keys, base, near):
        tps = keys // QB
        ones_rows = jnp.where(lax.broadcasted_iota(jnp.int32, (ones_pad, keys), 0) == 0, 1.0, 0.0).astype(BF16)

        def tile_ids(jp, sp):
            j0 = base + (jp * SEL_SPLIT + sp) * tps
            return [j0 + h for h in range(tps)]

        def step(jp, state):
            scores = []
            for sp in range(SEL_SPLIT):
                k0 = pl.multiple_of(tile_ids(jp, sp)[0] * QB, keys)
                lhs = jnp.concatenate([ks_ref[0, pl.ds(k0, keys), :], stat_ref[pl.ds(k0, keys), :]], axis=1)
                for k in range(N_KV):
                    s = _dot(lhs, q_sel[k])
                    if near:
                        tidx = [jnp.where(jt == ib, 0, jnp.where(jt == ib - 1, 1, jnp.where(jt < ib, 2, 4)))
                                for jt in tile_ids(jp, sp)]
                        s = s + jnp.concatenate([at_ref[ti, k] for ti in tidx], axis=0)
                    scores.append(s)
            stats = []
            for (m, _), s in zip(state, scores):
                m_new = jnp.maximum(m, jnp.max(s, axis=0, keepdims=True))
                stats.append((m_new, jnp.exp2(m - m_new), jnp.exp2(s - m_new).astype(BF16)))
            vts = [jnp.concatenate([jnp.concatenate([vst_ref[0, jt, vrows[k], :] for jt in tile_ids(jp, sp)], axis=1),
                                    ones_rows], axis=0)
                   for sp in range(SEL_SPLIT) for k in range(N_KV)]
            return tuple((m_new, alpha * acc + _dot(vt, p))
                         for (_, acc), (m_new, alpha, p), vt in zip(state, stats, vts))
        return step

    far_tiles = SEL_SPLIT * SEL_KEYS // QB
    near_tiles = SEL_SPLIT * SEL_KEYS_NEAR // QB
    init = (jnp.full((1, cols), NEG, F32), jnp.zeros((acc_rows, cols), F32))
    n_far = jnp.maximum(ib - 1, 0) // far_tiles
    near_base = n_far * far_tiles
    n_near = (ib - near_base + near_tiles) // near_tiles
    sel_state = lax.fori_loop(0, n_far, make_step(SEL_KEYS, 0, False), (init,) * (N_KV * SEL_SPLIT))
    sel_state = lax.fori_loop(0, n_near, make_step(SEL_KEYS_NEAR, near_base, True), sel_state)

    out_rows = []
    for k in range(N_KV):
        parts = [sel_state[sp * N_KV + k] for sp in range(SEL_SPLIT)]
        m_s = parts[0][0]
        for m_p, _ in parts[1:]:
            m_s = jnp.maximum(m_s, m_p)
        acc_s = jnp.zeros(init[1].shape, F32)
        for m_p, acc_p in parts:
            acc_s = acc_s + jnp.exp2(m_p - m_s) * acc_p
        o_s = acc_s[0:HD] / jnp.maximum(acc_s[HD:HD + 1], 1e-30)

        def gate_row(br):
            return jnp.concatenate([ngt[(GROUP * k + g) * N_BRANCH + br:(GROUP * k + g) * N_BRANCH + br + 1, :]
                                    for g in range(GROUP)], axis=1)
        o_k = gate_row(0) * o_c[k] + gate_row(1) * o_s + gate_row(2) * o_w[k]
        out_rows += [o_k[:, g * QB:(g + 1) * QB] for g in range(GROUP)]
    o_ref[0] = jnp.concatenate(out_rows, axis=0).T.astype(BF16)


def _nsa_prompt(qt, ngt, kc, vct, ks, vst, kw, vwt, tables):
    nb, aw, t = qt.shape
    nq = t // QB
    ncp = kc.shape[1]
    full3 = lambda b, i: (b, 0, 0)
    full4 = lambda b, i: (b, 0, 0, 0)
    return pl.pallas_call(
        functools.partial(_nsa_prompt_kernel, n_sel=t // SEL_BLOCK),
        grid=(nb, nq),
        in_specs=[pl.BlockSpec((1, aw, QB), lambda b, i: (b, 0, i)),
                  pl.BlockSpec((1, NG_ROWS, QB), lambda b, i: (b, 0, i)),
                  pl.BlockSpec((1, ncp, KVW), full3), pl.BlockSpec((1, KVW, ncp), full3),
                  pl.BlockSpec((1, t, KVW), full3), pl.BlockSpec((1, nq, KVW, QB), full4),
                  pl.BlockSpec((1, t, KVW), full3), pl.BlockSpec((1, nq, KVW, QB), full4)]
                 + [_const_spec(a.shape) for a in tables],
        out_specs=pl.BlockSpec((1, QB, aw), lambda b, i: (b, i, 0)),
        out_shape=jax.ShapeDtypeStruct((nb, t, aw), BF16),
        compiler_params=_cparams(("parallel", "arbitrary")),
        name="nsa_prompt",
    )(qt, ngt, kc, vct, ks, vst, kw, vwt, *tables)


def _nsa_sample_kernel(pt_ref, q_ref, gate_ref, kc_ref, vc_ref, kpool, vpool, ksn_ref, vsn_ref,
                       kwin_ref, vwin_ref, kwn_ref, vwn_ref, bc_ref, bs_ref, bw_ref, ovl_ref, e_ref,
                       o_ref, kwo_ref, vwo_ref, kbuf, vbuf, sems, *, n_sel, past, t_new):
    n_pages = pt_ref.shape[1]
    wb = kwin_ref.shape[2]
    slot = _paged_prefetch(pt_ref, (kpool, vpool), (kbuf, vbuf), sems, n_pages)
    past_t = lambda buf: jnp.concatenate([buf[slot, p] for p in range(n_pages)], axis=1).astype(BF16)
    q = q_ref[0]
    nb_past = past // SEL_BLOCK
    pad_new = jnp.zeros((QB - ksn_ref.shape[1], KVW), F32)
    new_tile = lambda ref: jnp.concatenate([ref[0], pad_new], axis=0).astype(BF16)

    p_c = _masked_softmax(_dot_t(q, kc_ref[0]) + bc_ref[...])
    o_c = _dot(p_c.astype(BF16), vc_ref[0])
    parts = []
    for k in range(N_KV):
        base = k * GROUP * t_new
        ps = p_c[base:base + t_new]
        for g in range(1, GROUP):
            ps = ps + p_c[base + g * t_new:base + (g + 1) * t_new]
        parts.append(ps)
    psum = jnp.concatenate(parts, axis=0)
    psum_hi = psum.astype(BF16)
    psum_lo = (psum - psum_hi.astype(F32)).astype(BF16)
    imp = _dot(psum_hi, ovl_ref[...]) + _dot(psum_lo, ovl_ref[...])
    blk = lax.broadcasted_iota(jnp.int32, imp.shape, 1)
    tpos = past + lax.broadcasted_iota(jnp.int32, imp.shape, 0) % t_new
    sel = _rank_select(_block_scores(imp, blk, tpos), blk, n_sel, 1)
    sel = jnp.concatenate([sel[k * t_new:(k + 1) * t_new] for k in range(N_KV) for _ in range(GROUP)], axis=0)

    mask_add = _dot((sel[:, 0:LANES] - 1.0).astype(BF16), e_ref[...])
    s_past = _dot(q, past_t(kbuf)) + bs_ref[:, 0:past] + mask_add
    s_new = _dot_t(q, new_tile(ksn_ref)) + bs_ref[:, past:]
    s_new = jnp.where(sel[:, nb_past:nb_past + 1] > 0.5, s_new, NEG)
    p_s = _masked_softmax(jnp.concatenate([s_past, s_new], axis=1)).astype(BF16)
    o_s = _dot_t(p_s[:, 0:past], past_t(vbuf)) + _dot(p_s[:, past:], new_tile(vsn_ref))

    s_w = jnp.concatenate([_dot(q, kwin_ref[0].astype(BF16)), _dot_t(q, new_tile(kwn_ref))], axis=1)
    p_w = _masked_softmax(s_w + bw_ref[...]).astype(BF16)
    o_w = _dot_t(p_w[:, 0:wb], vwin_ref[0].astype(BF16)) + _dot(p_w[:, wb:], new_tile(vwn_ref))

    gate = gate_ref[0]
    o_ref[0] = gate[:, 0:1] * o_c + gate[:, 1:2] * o_s + gate[:, 2:3] * o_w

    lane = lax.broadcasted_iota(jnp.int32, (KVW, wb), 1)
    for win_ref, new_ref, out_ref in ((kwin_ref, kwn_ref, kwo_ref), (vwin_ref, vwn_ref, vwo_ref)):
        new_t = jnp.concatenate([new_ref[0], pad_new], axis=0).T
        tail = pltpu.roll(jnp.concatenate([jnp.zeros((KVW, wb - QB), F32), new_t], axis=1), QB - t_new, 1)
        out_ref[0] = jnp.where(lane >= wb - t_new, tail, pltpu.roll(win_ref[0], wb - t_new, 1))


def _nsa_sample(page_table, q, gate, kc, vc, kpool, vpool, ksn, vsn, kwin, vwin, kwn, vwn,
                bc, bs, bw, ovl, e, n_sel, past, t_new):
    ns, n_pages = page_table.shape
    rows, kvw = q.shape[1:]
    buf_shape = (2, n_pages) + kpool.shape[1:]
    seq3 = lambda s, pt: (s, 0, 0)
    any_spec = pl.BlockSpec(memory_space=pl.ANY)
    cs = lambda a: pl.BlockSpec(a.shape, lambda s, pt, nd=a.ndim: (0,) * nd, pipeline_mode=pl.Buffered(1))
    per_seq = lambda a: pl.BlockSpec((1,) + a.shape[1:], seq3)
    return pl.pallas_call(
        functools.partial(_nsa_sample_kernel, n_sel=n_sel, past=past, t_new=t_new),
        grid_spec=pltpu.PrefetchScalarGridSpec(
            num_scalar_prefetch=1,
            grid=(ns,),
            in_specs=[per_seq(q), per_seq(gate), per_seq(kc), per_seq(vc), any_spec, any_spec,
                      per_seq(ksn), per_seq(vsn), per_seq(kwin), per_seq(vwin), per_seq(kwn), per_seq(vwn),
                      cs(bc), cs(bs), cs(bw), cs(ovl), cs(e)],
            out_specs=[pl.BlockSpec((1, rows, kvw), seq3), per_seq(kwin), per_seq(vwin)],
            scratch_shapes=[pltpu.VMEM(buf_shape, F32), pltpu.VMEM(buf_shape, F32), pltpu.SemaphoreType.DMA((2, 2))]),
        out_shape=[jax.ShapeDtypeStruct((ns, rows, kvw), F32), jax.ShapeDtypeStruct(kwin.shape, F32),
                   jax.ShapeDtypeStruct(vwin.shape, F32)],
        compiler_params=_cparams(("arbitrary",)),
        name="nsa_sample",
    )(page_table, q, gate, kc, vc, kpool, vpool, ksn, vsn, kwin, vwin, kwn, vwn, bc, bs, bw, ovl, e)


def _ssm_param_kernel(ar_ref, ai_ref, ldt_ref, br_ref, bi_ref, abr_ref, abi_ref, bbr_ref, bbi_ref):
    ar = ar_ref[...]
    ai = ai_ref[...]
    dt = jnp.exp(ldt_ref[...])
    mag = jnp.exp(ar * dt)
    abr = mag * jnp.cos(ai * dt)
    abi = mag * jnp.sin(ai * dt)
    den = ar * ar + ai * ai
    nr, ni = abr - 1.0, abi
    fr = (nr * ar + ni * ai) / den
    fi = (ni * ar - nr * ai) / den
    abr_ref[...] = abr
    abi_ref[...] = abi
    for g in range(ar.shape[0]):
        br = br_ref[g]
        bi = bi_ref[g]
        frg = fr[g:g + 1, :]
        fig = fi[g:g + 1, :]
        bbr_ref[g] = frg * br - fig * bi
        bbi_ref[g] = frg * bi + fig * br


def _ssm_params(a_re, a_im, log_dt, b_re_t, b_im_t):
    g, p = a_re.shape
    return pl.pallas_call(
        _ssm_param_kernel,
        out_shape=[jax.ShapeDtypeStruct((g, p), F32)] * 2 + [jax.ShapeDtypeStruct(b_re_t.shape, F32)] * 2,
        name="ssm_params",
    )(a_re, a_im, log_dt.reshape(g, 1), b_re_t, b_im_t)


def _ssm_kernel(u_ref, h0r_ref, h0i_ref, ar_ref, ai_ref, bd_ref, cd_ref, d_ref, wglu_ref, bglu_ref,
                so_ref, hr_ref, hi_ref, xr_scr, xi_scr, *slab_scr, bt):
    i = pl.program_id(0)
    n_slab = bd_ref.shape[0]
    width = n_slab * LANES
    sw = bd_ref.shape[2] // 2
    if slab_scr:
        slab = slab_scr[0]
        tt = u_ref.shape[0]
        rows = tt * bt
        u_wide = u_ref[...].astype(F32)
        for b in range(bt):
            for sl in range(n_slab):
                lanes = slice(b * width + sl * LANES, b * width + (sl + 1) * LANES)
                slab[sl, pl.ds(b, tt, stride=bt), :] = u_wide[:, lanes]
        u_slabs = [slab[sl] for sl in range(n_slab)]
    else:
        rows = u_ref.shape[0]
        u_rows = u_ref[...].astype(F32)
        u_slabs = [u_rows[:, sl * LANES:(sl + 1) * LANES] for sl in range(n_slab)]

    @pl.when(i == 0)
    def _():
        hr_ref[...] = h0r_ref[...]
        hi_ref[...] = h0i_ref[...]

    for sl in range(n_slab):
        x = _dot(u_slabs[sl].astype(BF16), bd_ref[sl])
        xr_scr[:, sl * sw:(sl + 1) * sw] = x[:, :sw]
        xi_scr[:, sl * sw:(sl + 1) * sw] = x[:, sw:]

    per = 8 // math.gcd(bt, 8)
    grp = per * bt
    lc = 512
    for c0 in range(0, xr_scr.shape[1], lc):
        cl = slice(c0, c0 + lc)
        a_r = jnp.broadcast_to(ar_ref[:, cl], (bt, lc))
        a_i = jnp.broadcast_to(ai_ref[:, cl], (bt, lc))

        def step(j, carry):
            h_r, h_i = carry
            r0 = pl.multiple_of(j * grp, grp)
            xr = xr_scr[pl.ds(r0, grp), cl]
            xi = xi_scr[pl.ds(r0, grp), cl]
            out_r, out_i = [], []
            for s in range(per):
                n_r = a_r * h_r - a_i * h_i + xr[s * bt:(s + 1) * bt]
                n_i = a_r * h_i + a_i * h_r + xi[s * bt:(s + 1) * bt]
                h_r, h_i = n_r, n_i
                out_r.append(h_r)
                out_i.append(h_i)
            xr_scr[pl.ds(r0, grp), cl] = jnp.concatenate(out_r, axis=0) if per > 1 else out_r[0]
            xi_scr[pl.ds(r0, grp), cl] = jnp.concatenate(out_i, axis=0) if per > 1 else out_i[0]
            return h_r, h_i

        h_r, h_i = lax.fori_loop(0, rows // grp, step, (hr_ref[:, cl], hi_ref[:, cl]))
        hr_ref[:, cl] = h_r
        hi_ref[:, cl] = h_i

    ys = []
    for sl in range(n_slab):
        hcat = jnp.concatenate([xr_scr[:, sl * sw:(sl + 1) * sw], xi_scr[:, sl * sw:(sl + 1) * sw]], axis=1)
        ys.append(_dot(hcat.astype(BF16), cd_ref[sl]))
    y = jnp.concatenate(ys, axis=1) + d_ref[...] * jnp.concatenate(u_slabs, axis=1)
    z = _gelu(y)
    so = z * _sigmoid(_dot(z.astype(BF16), wglu_ref[...]) + bglu_ref[...])
    if slab_scr:
        for sl in range(n_slab):
            slab[sl] = so[:, sl * LANES:(sl + 1) * LANES]
        for b in range(bt):
            for sl in range(n_slab):
                lanes = slice(b * width + sl * LANES, b * width + (sl + 1) * LANES)
                so_ref[:, lanes] = slab[sl, pl.ds(b, tt, stride=bt), :].astype(so_ref.dtype)
    else:
        so_ref[...] = so.astype(so_ref.dtype)


def _ssm(u, h0r, h0i, ar, ai, bd, cd, dvec, wglu, bglu, bt, tt, time_major):
    width = bd.shape[0] * LANES
    rows = tt * bt
    nstate = ar.shape[1]
    blk = (tt, bt * width) if time_major else (rows, width)
    cst = [_const_spec(a.shape) for a in (h0r, h0i, ar, ai, bd, cd, dvec, wglu, bglu)]
    st_spec = _const_spec((bt, nstate))
    scratch = [pltpu.VMEM((rows, nstate), F32), pltpu.VMEM((rows, nstate), F32)]
    if time_major:
        scratch.append(pltpu.VMEM((bd.shape[0], rows, LANES), F32))
    return pl.pallas_call(
        functools.partial(_ssm_kernel, bt=bt),
        grid=(u.shape[0] // blk[0],),
        in_specs=[pl.BlockSpec(blk, lambda i: (i, 0))] + cst,
        out_specs=[pl.BlockSpec(blk, lambda i: (i, 0)), st_spec, st_spec],
        out_shape=[jax.ShapeDtypeStruct(u.shape, BF16), jax.ShapeDtypeStruct((bt, nstate), F32),
                   jax.ShapeDtypeStruct((bt, nstate), F32)],
        scratch_shapes=scratch,
        compiler_params=_cparams(("arbitrary",)),
        name="ssm",
    )(u, h0r, h0i, ar, ai, bd, cd, dvec, wglu, bglu)


def _back_kernel(h_ref, o_ref, so_ref, mg_ref, p_ref, watt_ref, wssm_ref, wo_ref, fn_ref, wg_ref, wu_ref, wd_ref,
                 pn_ref, wpg_ref, wple_ref, fin_ref, y_ref, *, ff_chunk):
    d = h_ref.shape[1]
    a = _dot(o_ref[...], watt_ref[...])
    s = _dot(so_ref[...], wssm_ref[...])
    mg = mg_ref[...].astype(F32)
    h = h_ref[...] + _dot((mg[:, :d] * a + mg[:, d:] * s).astype(BF16), wo_ref[...])
    f = _rms(h, fn_ref[...]).astype(BF16)
    ffn = jnp.zeros_like(h)
    for c0 in range(0, wg_ref.shape[1], ff_chunk):
        gate = _dot(f, wg_ref[:, c0:c0 + ff_chunk])
        up = _dot(f, wu_ref[:, c0:c0 + ff_chunk])
        ffn = ffn + _dot((gate * _sigmoid(gate) * up).astype(BF16), wd_ref[c0:c0 + ff_chunk, :])
    h = h + ffn
    g = _sigmoid(_dot(_rms(h, pn_ref[...]).astype(BF16), wpg_ref[...]))
    h = h + g * _dot(p_ref[...].astype(BF16), wple_ref[...])
    y_ref[...] = _rms(h, fin_ref[...])


def _back(h2d, o2d, so_tb, mg, p2d, weights, nb, t, tm, ff_chunk):
    n, d = h2d.shape
    nt = t // tm
    row = lambda b, i: (b * nt + i, 0)
    wspecs = [pl.BlockSpec(w.shape, lambda b, i, nd=w.ndim: (0,) * nd, pipeline_mode=pl.Buffered(1))
              for w in weights]
    sw = o2d.shape[1]
    return pl.pallas_call(
        functools.partial(_back_kernel, ff_chunk=ff_chunk),
        grid=(nb, nt),
        in_specs=[pl.BlockSpec((tm, d), row), pl.BlockSpec((tm, sw), row),
                  pl.BlockSpec((tm, sw), lambda b, i: (i, b)),
                  pl.BlockSpec((tm, mg.shape[1]), row), pl.BlockSpec((tm, p2d.shape[1]), row)] + wspecs,
        out_specs=pl.BlockSpec((tm, d), row),
        out_shape=jax.ShapeDtypeStruct((n, d), F32),
        compiler_params=_cparams(("parallel", "parallel")),
        name="back",
    )(h2d, o2d, so_tb, mg, p2d, *weights)


def _bucket_np(dist):
    n = np.maximum(dist, 0)
    exact = N_BUCKETS // 2
    nf = np.maximum(n, 1).astype(np.float64)
    large = exact + (np.log(nf / exact) / math.log(REL_MAX_DIST / exact) * (N_BUCKETS - exact)).astype(np.int64)
    return np.where(n < exact, n, np.minimum(large, N_BUCKETS - 1)).astype(np.int32)


def _bias_table(rel_bias, dist, valid, offset=None):
    onehot = jax.nn.one_hot(jnp.asarray(_bucket_np(dist)), N_BUCKETS, dtype=F32)
    b = jnp.einsum('...b,bh->h...', onehot, rel_bias.astype(F32), precision=lax.Precision.HIGHEST)
    if offset is not None:
        b = b - offset.reshape((N_HEADS,) + (1,) * dist.ndim)
    return jnp.where(jnp.asarray(valid)[None], b, NEG)


def _prompt_tables(rel_bias, t):
    def cols(b):
        r = b.shape[1]
        return b.reshape(N_KV, GROUP, r, QB).transpose(0, 2, 1, 3).reshape(N_KV, r, GROUP * QB)
    rel_bias = rel_bias.astype(F32) * LOG2E
    c = rel_bias[N_BUCKETS - 1]
    c_hi = c.astype(BF16)
    c_lo = (c - c_hi.astype(F32)).astype(BF16)
    c_eff = c_hi.astype(F32) + c_lo.astype(F32)
    crow = jnp.stack([c_hi.astype(F32), c_lo.astype(F32)] + [jnp.zeros_like(c)] * 6, axis=1)
    crow = jnp.broadcast_to(crow[:, :, None], (N_HEADS, 8, QB))
    crow = cols(crow)
    j = np.arange(QB)[:, None]
    i = np.arange(QB)[None, :]
    ones = np.ones((QB, QB), bool)
    zeros = jnp.zeros((N_KV, QB, GROUP * QB), F32)
    at = jnp.stack([
        cols(_bias_table(rel_bias, i - j, i >= j, c_eff)),
        cols(_bias_table(rel_bias, QB + i - j, ones, c_eff)),
        zeros,
        jnp.where(jnp.asarray(np.tile(j > i, (1, GROUP)))[None], zeros, NEG),
        zeros + NEG,
    ])
    ncp = t // CMP_STRIDE
    m = np.arange(2 * ncp)[:, None] - ncp
    dist = i - CMP_STRIDE * m - (CMP_BLOCK - 1)
    ut = cols(_bias_table(rel_bias, dist, dist >= 0))
    n_sel = t // SEL_BLOCK
    n = np.arange(ncp)[None, :]
    jb = np.arange(LANES)[:, None]
    ovlt = ((n * CMP_STRIDE < jb * SEL_BLOCK + SEL_BLOCK) & (n * CMP_STRIDE + CMP_BLOCK - 1 >= jb * SEL_BLOCK)
            & (jb < n_sel) & (n < ncp - 1))
    key = np.arange(t)[:, None]
    lane = np.arange(LANES)[None, :]
    stat = np.where(lane < LANES // 2, (lane == key // SEL_BLOCK) * BIG,
                    ((lane == LANES // 2) | (lane == LANES // 2 + 1)) * 1.0).astype(np.float32)
    return ut, at, jnp.asarray(ovlt.astype(np.float32), BF16), jnp.asarray(stat, BF16), crow


def _sample_tables(rel_bias, past, t_new, win_buf):
    def rows(b):
        return b.reshape(N_HEADS * t_new, b.shape[-1])
    tok = np.arange(t_new)[:, None]
    nc = past // CMP_STRIDE
    n = np.arange(nc)[None, :]
    c_end = n * CMP_STRIDE + CMP_BLOCK - 1
    n_cmp = (past + t_new) // CMP_STRIDE - 1
    bc = rows(_bias_table(rel_bias, past + tok - c_end, (c_end <= past + tok) & (n < n_cmp)))
    js = np.arange(past + QB)[None, :]
    ds = np.where(js < past, past + tok - js, tok - (js - past))
    bs = rows(_bias_table(rel_bias, ds, np.where(js < past, True, (ds >= 0) & (js - past < t_new))))
    jw = np.arange(win_buf + QB)[None, :]
    dw = np.where(jw < win_buf, win_buf + tok - jw, tok - (jw - win_buf))
    valid = np.where(jw < win_buf, (dw >= 0) & (dw < WINDOW), (dw >= 0) & (jw - win_buf < t_new))
    bw = rows(_bias_table(rel_bias, dw, valid))
    n_sel = -(-(past + t_new) // SEL_BLOCK)
    nbp = 2 * LANES
    nn = np.arange(nc)[:, None]
    jb = np.arange(nbp)[None, :]
    ovl = ((nn * CMP_STRIDE < jb * SEL_BLOCK + SEL_BLOCK) & (nn * CMP_STRIDE + CMP_BLOCK - 1 >= jb * SEL_BLOCK)
           & (jb < n_sel) & (nn < n_cmp))
    e = (np.arange(LANES)[:, None] == np.arange(past)[None, :] // SEL_BLOCK).astype(np.float32) * BIG
    return bc, bs, bw, jnp.asarray(ovl.astype(np.float32), BF16), jnp.asarray(e, BF16), n_sel


def _slab_diag(blocks, n_slab):
    g, r, c = blocks.shape
    gps = g // n_slab
    eye = jnp.eye(gps, dtype=blocks.dtype)
    return jnp.einsum('sgrc,gh->sgrhc', blocks.reshape(n_slab, gps, r, c), eye).reshape(n_slab, gps * r, gps * c)


def _layer_params(rel_bias, final_norm, attn_norm, w_in, cmp_pe_k, cmp_w1_k, cmp_w2_k, cmp_pe_v, cmp_w1_v, cmp_w2_v,
                  ssm_a_re, ssm_a_im, ssm_log_dt, ssm_b_re, ssm_b_im, ssm_c_re, ssm_c_im, ssm_d, w_glu, b_glu,
                  w_att_br, w_ssm_br, w_o, ffn_norm, w_ffn_gate, w_ffn_up, w_ffn_down, ple_norm, w_ple_gate, w_ple):
    l = 0
    d = w_in.shape[1]
    n_groups = ssm_a_re.shape[1]
    ssm_w = n_groups * SSM_CH
    nstate = n_groups * SSM_P
    assert ssm_w % LANES == 0
    w = w_in[l]
    c0 = AW + 6 * KVW
    n_gate = N_HEADS * N_BRANCH
    front_w = (attn_norm[l].reshape(1, d), w[:, :c0].astype(BF16),
               jnp.pad(w[:, c0:c0 + n_gate], ((0, 0), (0, LANES - n_gate))).astype(BF16),
               w[:, c0 + n_gate:c0 + n_gate + ssm_w].astype(BF16), w[:, c0 + n_gate + ssm_w:].astype(BF16))

    def cmp_weights(pe, w1, w2):
        half = CMP_STRIDE * HD
        w1cat = jnp.concatenate([w1[:half], w1[half:]], axis=1).astype(BF16)
        pe2 = jnp.pad(pe.reshape(2, half), ((0, 6), (0, 0))).astype(BF16)
        return w1cat, w2.astype(BF16), pe2
    cw = cmp_weights(cmp_pe_k[l], cmp_w1_k[l], cmp_w2_k[l]) + cmp_weights(cmp_pe_v[l], cmp_w1_v[l], cmp_w2_v[l])

    abr, abi, bbr_t, bbi_t = _ssm_params(ssm_a_re[l], ssm_a_im[l], ssm_log_dt[l],
                                         jnp.swapaxes(ssm_b_re[l], 1, 2), jnp.swapaxes(ssm_b_im[l], 1, 2))
    n_slab = ssm_w // LANES
    bd = jnp.concatenate([_slab_diag(bbr_t, n_slab), _slab_diag(bbi_t, n_slab)], axis=2).astype(BF16)
    cd = jnp.concatenate([_slab_diag(jnp.swapaxes(ssm_c_re[l], 1, 2), n_slab),
                          -_slab_diag(jnp.swapaxes(ssm_c_im[l], 1, 2), n_slab)], axis=1).astype(BF16)
    ssm_p = (abr.reshape(1, nstate), abi.reshape(1, nstate), bd, cd, ssm_d[l].reshape(1, ssm_w),
             w_glu[l].astype(BF16), b_glu[l].reshape(1, ssm_w))

    back_w = (w_att_br[l].astype(BF16), w_ssm_br[l].astype(BF16), w_o[l].astype(BF16),
              ffn_norm[l].reshape(1, d), w_ffn_gate[l].astype(BF16), w_ffn_up[l].astype(BF16),
              w_ffn_down[l].astype(BF16), ple_norm[l].reshape(1, d), w_ple_gate[l].astype(BF16),
              w_ple[l].astype(BF16), final_norm.reshape(1, d))
    d_ff = w_ffn_gate.shape[2]
    ff_chunk = d_ff // 2 if (d_ff // 2) % LANES == 0 else d_ff
    return dict(front=front_w, cmp=cw, ssm=ssm_p, back=back_w, ff_chunk=ff_chunk, rel_bias=rel_bias,
                n_groups=n_groups, ssm_w=ssm_w, nstate=nstate, n_gate=n_gate)


def _prompt_group(x_prompt, p_l, prm):
    nb, t, d = x_prompt.shape
    ssm_w, nstate = prm["ssm_w"], prm["nstate"]
    assert t % (CMP_STRIDE * LANES) == 0 and t >= WINDOW and t // SEL_BLOCK <= LANES // 2
    xp = x_prompt.reshape(nb * t, d)
    (qt, kct, vct, kst, vst32, kc, vc, kw, vw, ksb, kwb, vst, vwt, ngt, su, mg) = _front_prompt(
        xp, nb, t, prm["front"], min(FRONT_ROWS, t))
    kcc, vcct = _compress_prompt(kc.reshape(nb, t, KVW), vc.reshape(nb, t, KVW), prm["cmp"])
    o = _nsa_prompt(qt, ngt, kcc, vcct, ksb.reshape(nb, t, KVW), vst, kwb.reshape(nb, t, KVW), vwt,
                    _prompt_tables(prm["rel_bias"], t))
    zeros_state = jnp.zeros((nb, nstate), F32)
    so, sr, si = _ssm(su, zeros_state, zeros_state, *prm["ssm"], nb, SSM_STEPS, True)
    y = _back(xp, o.reshape(nb * t, AW), so, mg, p_l.reshape(nb * t, -1),
              prm["back"], nb, t, BACK_ROWS, prm["ff_chunk"])
    return dict(y=y, o=o, so=so, rows_t=(kct, vct, kst, vst32), win=(kw, vw), state=(sr, si))


def _sample_group(x_sample, p_l, page_table, pools, wins, states, prm):
    ns, t_new, d = x_sample.shape
    tok_minor = lambda a: jnp.transpose(a, (0, 2, 3, 1)).reshape(a.shape[0], KVW, a.shape[1])
    k_cmp, v_cmp, k_sel, v_sel = pools
    k_win, v_win = wins
    n_phys, page = k_cmp.shape[:2]
    past = page_table.shape[1] * page
    win_buf = k_win.shape[1]
    ssm_w, nstate, n_gate = prm["ssm_w"], prm["nstate"], prm["n_gate"]
    assert page == QB and past % (CMP_STRIDE * LANES) == 0 and past // SEL_BLOCK <= LANES
    assert win_buf == WINDOW and past >= win_buf and t_new <= 8 and t_new < CMP_STRIDE
    n_s = ns * t_new
    xs = x_sample.reshape(n_s, d)
    q_s, kc_s, vc_s, ks_s, vs_s, kw_s, vw_s, ng_s, su_s, mg_s = _front_sample(xs, prm["front"])
    kcc_s, vcc_s = _compress_sample(page_table, tok_minor(k_cmp), tok_minor(v_cmp), prm["cmp"])
    bc, bs, bw, ovl_s, e_s, n_sel_s = _sample_tables(prm["rel_bias"], past, t_new, win_buf)
    rows_s = N_HEADS * t_new
    eye_kv = jnp.eye(N_KV, dtype=BF16)
    q_rows = q_s.reshape(ns, t_new, N_KV, GROUP, HD).transpose(0, 2, 3, 1, 4)
    q_rows = jnp.einsum('skgtd,kj->skgtjd', q_rows, eye_kv).reshape(ns, rows_s, KVW)
    gate_s = ng_s[:, :n_gate].reshape(ns, t_new, N_KV, GROUP, N_BRANCH).transpose(0, 2, 3, 1, 4)
    gate_s = jnp.pad(gate_s.reshape(ns, rows_s, N_BRANCH), ((0, 0), (0, 0), (0, LANES - N_BRANCH)))
    pad8 = lambda a: jnp.pad(a.reshape(ns, t_new, KVW), ((0, 0), (0, 8 - t_new), (0, 0)))
    o_s, kwin_new, vwin_new = _nsa_sample(
        page_table, q_rows, gate_s, kcc_s, vcc_s,
        tok_minor(k_sel), tok_minor(v_sel), pad8(ks_s), pad8(vs_s), tok_minor(k_win), tok_minor(v_win),
        pad8(kw_s), pad8(vw_s), bc, bs, bw, ovl_s, e_s, n_sel_s, past, t_new)
    o_s = o_s.reshape(ns, N_KV, GROUP, t_new, N_KV, HD)
    o_s = jnp.stack([o_s[:, k, :, :, k, :] for k in range(N_KV)], axis=1)
    o_s = o_s.transpose(0, 3, 1, 2, 4).reshape(n_s, AW).astype(BF16)
    su_ts = su_s.reshape(ns, t_new, ssm_w).transpose(1, 0, 2).reshape(n_s, ssm_w)
    so_ts, sr, si = _ssm(su_ts, states[0].reshape(ns, nstate), states[1].reshape(ns, nstate), *prm["ssm"], ns, t_new,
                         False)
    so_s = so_ts.reshape(t_new, ns, ssm_w).transpose(1, 0, 2).reshape(n_s, ssm_w)
    y = _back(xs, o_s, so_s, mg_s, p_l.reshape(n_s, -1), prm["back"], 1, n_s, n_s, prm["ff_chunk"])
    return dict(y=y, o=o_s, so=so_s, rows=(kc_s, vc_s, ks_s, vs_s), win_t=(kwin_new, vwin_new), state=(sr, si))


def kernel(x_prompt, x_sample, cache_k_cmp, cache_v_cmp, cache_k_sel, cache_v_sel, cache_k_win, cache_v_win, state_ssm_re, state_ssm_im, page_table, p_prompt, p_sample, rel_bias, final_norm, attn_norm, w_in, cmp_pe_k, cmp_w1_k, cmp_w2_k, cmp_pe_v, cmp_w1_v, cmp_w2_v, ssm_a_re, ssm_a_im, ssm_log_dt, ssm_b_re, ssm_b_im, ssm_c_re, ssm_c_im, ssm_d, w_glu, b_glu, w_att_br, w_ssm_br, w_o, ffn_norm, w_ffn_gate, w_ffn_up, w_ffn_down, ple_norm, w_ple_gate, w_ple):
    assert w_in.shape[0] == 1, "single-layer trunk"
    l = 0
    nb, t, d = x_prompt.shape
    ns, t_new = x_sample.shape[:2]
    prm = _layer_params(rel_bias, final_norm, attn_norm, w_in, cmp_pe_k, cmp_w1_k, cmp_w2_k, cmp_pe_v, cmp_w1_v,
                        cmp_w2_v, ssm_a_re, ssm_a_im, ssm_log_dt, ssm_b_re, ssm_b_im, ssm_c_re, ssm_c_im, ssm_d,
                        w_glu, b_glu, w_att_br, w_ssm_br, w_o, ffn_norm, w_ffn_gate, w_ffn_up, w_ffn_down,
                        ple_norm, w_ple_gate, w_ple)
    pg = _prompt_group(x_prompt, p_prompt[l], prm)
    sg = _sample_group(x_sample, p_sample[l], page_table,
                       (cache_k_cmp[l], cache_v_cmp[l], cache_k_sel[l], cache_v_sel[l]),
                       (cache_k_win[l], cache_v_win[l]), (state_ssm_re[l], state_ssm_im[l]), prm)

    kv5 = lambda a, b_, t_: a.reshape(1, b_, t_, N_KV, HD)
    kv5_t = lambda a: jnp.transpose(a.reshape(1, nb, N_KV, HD, t), (0, 1, 4, 2, 3))
    keep = min(WINDOW, t)
    win_p = lambda a: a.reshape(nb, t, KVW)[:, t - keep:].reshape(1, nb, keep, N_KV, HD)
    win_s = lambda a: jnp.transpose(a.reshape(1, ns, N_KV, HD, a.shape[-1]), (0, 1, 4, 2, 3))
    st = lambda a, b_: a.reshape(1, b_, prm["n_groups"], SSM_P)
    kc_s, vc_s, ks_s, vs_s = sg["rows"]
    return (pg["y"].reshape(nb, t, d), sg["y"].reshape(ns, t_new, d),
            *[kv5_t(a) for a in pg["rows_t"]], win_p(pg["win"][0]), win_p(pg["win"][1]),
            st(pg["state"][0], nb), st(pg["state"][1], nb),
            kv5(kc_s, ns, t_new), kv5(vc_s, ns, t_new), kv5(ks_s, ns, t_new), kv5(vs_s, ns, t_new),
            win_s(sg["win_t"][0]), win_s(sg["win_t"][1]),
            st(sg["state"][0], ns), st(sg["state"][1], ns))
```

```python
import functools
import math

import numpy as np
import jax
import jax.numpy as jnp
from jax import lax
from jax.experimental import pallas as pl
from jax.experimental.pallas import tpu as pltpu

F32 = jnp.float32
BF16 = jnp.bfloat16

N_HEADS = 8
N_KV = 2
HD = 64
GROUP = N_HEADS // N_KV
N_BRANCH = 3
CMP_BLOCK = 32
CMP_STRIDE = 16
CMP_HIDDEN = 256
CMP_PITCH = 24
SEL_BLOCK = 64
SEL_TOPK = 16
WINDOW = 512
QB = 128
SEL_KEYS = 4 * QB
SEL_KEYS_NEAR = 2 * QB
SEL_SPLIT = 2
N_BUCKETS = 32
REL_MAX_DIST = 128
SSM_CH = 16
SSM_P = 64
EPS = 1e-6
NEG = -1e30
NEG_TEST = -1e29
FORCE = 1e9
BIG = 1e30
LOG2E = math.log2(math.e)
LANES = 128
VMEM_LIMIT = 56 * 1024 * 1024
FRONT_ROWS = 1024
BACK_ROWS = 512
SSM_STEPS = 256
AW = N_HEADS * HD
KVW = N_KV * HD
NG_ROWS = 32


def _cparams(sem):
    return pltpu.CompilerParams(dimension_semantics=sem, vmem_limit_bytes=VMEM_LIMIT)


def _const_spec(shape):
    nd = len(shape)
    return pl.BlockSpec(shape, lambda *_: (0,) * nd)


def _rms(x, g):
    return x * lax.rsqrt(jnp.mean(x * x, axis=-1, keepdims=True) + EPS) * g


def _gelu(x):
    return x * (0.5 * (1.0 + jnp.tanh(math.sqrt(2.0 / math.pi) * (x + 0.044715 * (x * x * x)))))


def _sigmoid(x):
    return 1.0 / (1.0 + jnp.exp(-x))


def _dot(a, b):
    return jnp.dot(a, b, preferred_element_type=F32)


def _dot_t(a, b):
    return lax.dot_general(a, b, (((1,), (1,)), ((), ())), preferred_element_type=F32)


def _masked_softmax(s, axis=-1):
    valid = s > NEG_TEST
    m = jnp.max(s, axis=axis, keepdims=True)
    e = jnp.where(valid, jnp.exp(s - m), 0.0)
    return e / jnp.maximum(jnp.sum(e, axis=axis, keepdims=True), 1e-30)


def _softmax2_cols(s):
    m = jnp.max(s, axis=0, keepdims=True)
    e = jnp.exp2(s - m)
    inv = jnp.where(m > NEG_TEST, 1.0 / jnp.maximum(jnp.sum(e, axis=0, keepdims=True), 1e-30), 0.0)
    return e * inv


def _front_project(x_ref, g_ref, wa_ref, wng_ref, wsu_ref, wmg_ref, su_ref, mg_ref, q_scale):
    u = _rms(x_ref[...], g_ref[...]).astype(BF16)
    za = _dot(u, wa_ref[...])
    q = za[:, :AW] * q_scale
    rows = [za[:, AW + i * KVW: AW + (i + 1) * KVW] for i in range(6)]
    ng = _sigmoid(_dot(u, wng_ref[...]))
    su_ref[...] = _dot(u, wsu_ref[...]).astype(su_ref.dtype)
    mg_ref[...] = _sigmoid(_dot(u, wmg_ref[...])).astype(BF16)
    return q, rows, ng


def _front_sample_kernel(x_ref, g_ref, wa_ref, wng_ref, wsu_ref, wmg_ref,
                         q_ref, kc_ref, vc_ref, ks_ref, vs_ref, kw_ref, vw_ref, ng_ref, su_ref, mg_ref):
    q, rows, ng = _front_project(x_ref, g_ref, wa_ref, wng_ref, wsu_ref, wmg_ref, su_ref, mg_ref, HD ** -0.5)
    q_ref[...] = q.astype(BF16)
    for ref, r in zip((kc_ref, vc_ref, ks_ref, vs_ref, kw_ref, vw_ref), rows):
        ref[...] = r
    ng_ref[...] = ng


def _front_prompt_kernel(x_ref, g_ref, wa_ref, wng_ref, wsu_ref, wmg_ref,
                         qt_ref, kct_ref, vct_ref, kst_ref, vst32_ref, kc_ref, vc_ref, kw_ref, vw_ref,
                         ksb_ref, kwb_ref, vst_ref, vwt_ref, ngt_ref, su_ref, mg_ref):
    q, rows, ng = _front_project(x_ref, g_ref, wa_ref, wng_ref, wsu_ref, wmg_ref, su_ref, mg_ref,
                                 HD ** -0.5 * LOG2E)
    kc, vc, ks, vs, kw, vw = rows
    qt_ref[0] = q.T.astype(BF16)
    for ref, r in zip((kct_ref, vct_ref, kst_ref, vst32_ref), (kc, vc, ks, vs)):
        ref[0] = r.T
    kc_ref[...] = kc
    vc_ref[...] = vc
    kw_ref[...] = kw
    vw_ref[...] = vw
    ksb_ref[...] = ks.astype(BF16)
    kwb_ref[...] = kw.astype(BF16)
    for ref, r in ((vst_ref, vs), (vwt_ref, vw)):
        rt = r.T.astype(BF16)
        for j in range(rt.shape[1] // QB):
            ref[0, j] = rt[:, j * QB:(j + 1) * QB]
    ngt_ref[0] = ng.T[0:NG_ROWS]


def _front_sample(x2d, fw):
    g, wa, wng, wsu, wmg = fw
    n, d = x2d.shape
    sw, mw = wsu.shape[1], wmg.shape[1]
    shapes = ([jax.ShapeDtypeStruct((n, AW), BF16)] + [jax.ShapeDtypeStruct((n, KVW), F32)] * 6
              + [jax.ShapeDtypeStruct((n, LANES), F32), jax.ShapeDtypeStruct((n, sw), BF16),
                 jax.ShapeDtypeStruct((n, mw), BF16)])
    return pl.pallas_call(
        _front_sample_kernel,
        grid=(1,),
        in_specs=[_const_spec(a.shape) for a in (x2d, g, wa, wng, wsu, wmg)],
        out_specs=[_const_spec(s.shape) for s in shapes],
        out_shape=shapes,
        compiler_params=_cparams(("arbitrary",)),
        name="front_sample",
    )(x2d, g, wa, wng, wsu, wmg)


def _front_prompt(x2d, nb, t, fw, tm):
    g, wa, wng, wsu, wmg = fw
    n, d = x2d.shape
    nt = t // tm
    sw, mw = wsu.shape[1], wmg.shape[1]
    row = lambda b, i: (b * nt + i, 0)
    kv5 = jax.ShapeDtypeStruct((nb, KVW, t), F32)
    kv5_spec = pl.BlockSpec((1, KVW, tm), lambda b, i: (b, 0, i))
    vt = jax.ShapeDtypeStruct((nb, t // QB, KVW, QB), BF16)
    vt_spec = pl.BlockSpec((1, tm // QB, KVW, QB), lambda b, i: (b, i, 0, 0))
    shapes = ([jax.ShapeDtypeStruct((nb, AW, t), BF16)] + [kv5] * 4 + [jax.ShapeDtypeStruct((n, KVW), F32)] * 4
              + [jax.ShapeDtypeStruct((n, KVW), BF16)] * 2 + [vt] * 2
              + [jax.ShapeDtypeStruct((nb, NG_ROWS, t), F32), jax.ShapeDtypeStruct((t, nb * sw), BF16),
                 jax.ShapeDtypeStruct((n, mw), BF16)])
    specs = ([pl.BlockSpec((1, AW, tm), lambda b, i: (b, 0, i))] + [kv5_spec] * 4
             + [pl.BlockSpec((tm, KVW), row)] * 6 + [vt_spec] * 2
             + [pl.BlockSpec((1, NG_ROWS, tm), lambda b, i: (b, 0, i)),
                pl.BlockSpec((tm, sw), lambda b, i: (i, b)), pl.BlockSpec((tm, mw), row)])
    return pl.pallas_call(
        _front_prompt_kernel,
        grid=(nb, nt),
        in_specs=[pl.BlockSpec((tm, d), row)] + [_const_spec(a.shape) for a in (g, wa, wng, wsu, wmg)],
        out_specs=specs,
        out_shape=shapes,
        compiler_params=_cparams(("parallel", "parallel")),
        name="front_prompt",
    )(x2d, g, wa, wng, wsu, wmg)


def _chunk_rows(load, r0, rn, pitch=CMP_STRIDE):
    return jnp.concatenate([load(pl.ds(pitch * r0 + r, rn, stride=pitch)) for r in range(CMP_STRIDE)], axis=1)


def _compress_compute(load_rows, c, w1_ref, w2_ref, pe_ref, a_scr):
    rc_n = min(c, 256)
    lo = lax.broadcasted_iota(jnp.int32, (rc_n, LANES), 1) < HD
    w1 = w1_ref[...]
    for rc in range(c // rc_n):
        x = load_rows(rc * rc_n, rc_n)
        cols = [x[:, r * LANES:(r + 1) * LANES] for r in range(CMP_STRIDE)]
        rol = [pltpu.roll(col, HD, 1) for col in cols]
        for kh in range(N_KV):
            if kh == 0:
                parts = [jnp.where(lo, cols[2 * j], rol[2 * j + 1]) for j in range(CMP_STRIDE // 2)]
            else:
                parts = [jnp.where(lo, rol[2 * j], cols[2 * j + 1]) for j in range(CMP_STRIDE // 2)]
            xh = jnp.concatenate(parts, axis=1).astype(BF16)
            a_scr[kh, rc * rc_n:(rc + 1) * rc_n, :] = _dot(xh, w1)
    pw = _dot(pe_ref[...], w1)
    peb = pw[0:1, :CMP_HIDDEN] + pw[1:2, CMP_HIDDEN:]
    w2 = w2_ref[...]
    outs = []
    for kh in range(N_KV):
        a = a_scr[kh]
        hid = a[:, :CMP_HIDDEN] + pltpu.roll(a[:, CMP_HIDDEN:], c - 1, 0) + peb
        outs.append(_dot(_gelu(hid).astype(BF16), w2))
    return jnp.concatenate(outs, axis=1)


def _compress_prompt_kernel(xk_ref, xv_ref, w1k_ref, w2k_ref, pek_ref, w1v_ref, w2v_ref, pev_ref,
                            ok_ref, ovt_ref, a_scr):
    c = xk_ref.shape[1] // CMP_STRIDE
    ok_ref[0] = _compress_compute(lambda r0, rn: _chunk_rows(lambda idx: xk_ref[0, idx, :], r0, rn), c,
                                  w1k_ref, w2k_ref, pek_ref, a_scr).astype(BF16)
    ovt_ref[0] = _compress_compute(lambda r0, rn: _chunk_rows(lambda idx: xv_ref[0, idx, :], r0, rn), c,
                                   w1v_ref, w2v_ref, pev_ref, a_scr).T.astype(BF16)


def _compress_prompt(xk, xv, cw):
    nb, t, kvw = xk.shape
    c = t // CMP_STRIDE
    wspecs = [_const_spec(w.shape) for w in cw]
    blk = pl.BlockSpec((1, t, kvw), lambda b: (b, 0, 0))
    return pl.pallas_call(
        _compress_prompt_kernel,
        grid=(nb,),
        in_specs=[blk, blk] + wspecs,
        out_specs=[pl.BlockSpec((1, c, KVW), lambda b: (b, 0, 0)), pl.BlockSpec((1, KVW, c), lambda b: (b, 0, 0))],
        out_shape=[jax.ShapeDtypeStruct((nb, c, KVW), BF16), jax.ShapeDtypeStruct((nb, KVW, c), BF16)],
        scratch_shapes=[pltpu.VMEM((N_KV, c, 2 * CMP_HIDDEN), F32)],
        compiler_params=_cparams(("parallel",)),
        name="compress_prompt",
    )(xk, xv, *cw)


def _page_copy(pool, buf, sem, page, p, slot):
    return pltpu.make_async_copy(pool.at[page], buf.at[slot, p], sem)


def _page_gather_start(pt_ref, seq, pools, bufs, sems, slot, n_pages):
    def body(p, carry):
        page = pt_ref[seq, p]
        for i, (pool, buf) in enumerate(zip(pools, bufs)):
            _page_copy(pool, buf, sems.at[i, slot], page, p, slot).start()
        return carry
    lax.fori_loop(0, n_pages, body, 0)


def _page_gather_wait(pools, bufs, sems, slot, n_pages):
    def body(p, carry):
        for i, (pool, buf) in enumerate(zip(pools, bufs)):
            _page_copy(pool, buf, sems.at[i, slot], 0, p, slot).wait()
        return carry
    lax.fori_loop(0, n_pages, body, 0)


def _paged_prefetch(pt_ref, pools, bufs, sems, n_pages):
    s = pl.program_id(0)
    slot = s % 2

    @pl.when(s == 0)
    def _():
        _page_gather_start(pt_ref, 0, pools, bufs, sems, 0, n_pages)

    @pl.when(s + 1 < pl.num_programs(0))
    def _():
        _page_gather_start(pt_ref, s + 1, pools, bufs, sems, 1 - slot, n_pages)

    _page_gather_wait(pools, bufs, sems, slot, n_pages)
    return slot


def _compress_sample_kernel(pt_ref, kpool, vpool, w1k_ref, w2k_ref, pek_ref, w1v_ref, w2v_ref, pev_ref,
                            ok_ref, ov_ref, kbuf, vbuf, sems, a_scr, rows_scr):
    n_pages = pt_ref.shape[1]
    page = kpool.shape[2]
    c = n_pages * page // CMP_STRIDE
    slot = _paged_prefetch(pt_ref, (kpool, vpool), (kbuf, vbuf), sems, n_pages)
    for buf, out_ref, w1_ref, w2_ref, pe_ref in ((kbuf, ok_ref, w1k_ref, w2k_ref, pek_ref),
                                                (vbuf, ov_ref, w1v_ref, w2v_ref, pev_ref)):
        for p in range(n_pages):
            rows = buf[slot, p].T
            for ch in range(page // CMP_STRIDE):
                r0 = (p * (page // CMP_STRIDE) + ch) * CMP_PITCH
                rows_scr[r0:r0 + CMP_STRIDE, :] = rows[ch * CMP_STRIDE:(ch + 1) * CMP_STRIDE]
        out_ref[0] = _compress_compute(
            lambda r0, rn: _chunk_rows(lambda idx: rows_scr[idx, :], r0, rn, CMP_PITCH), c,
            w1_ref, w2_ref, pe_ref, a_scr).astype(BF16)


def _compress_sample(page_table, kpool, vpool, cw):
    ns, n_pages = page_table.shape
    width, page = kpool.shape[1:]
    tokens = n_pages * page
    c = tokens // CMP_STRIDE
    buf_shape = (2, n_pages, width, page)
    any_spec = pl.BlockSpec(memory_space=pl.ANY)
    wspecs = [pl.BlockSpec(w.shape, lambda s, pt, nd=w.ndim: (0,) * nd) for w in cw]
    oblk = pl.BlockSpec((1, c, KVW), lambda s, pt: (s, 0, 0))
    return pl.pallas_call(
        _compress_sample_kernel,
        grid_spec=pltpu.PrefetchScalarGridSpec(
            num_scalar_prefetch=1,
            grid=(ns,),
            in_specs=[any_spec, any_spec] + wspecs,
            out_specs=[oblk, oblk],
            scratch_shapes=[pltpu.VMEM(buf_shape, F32), pltpu.VMEM(buf_shape, F32), pltpu.SemaphoreType.DMA((2, 2)),
                            pltpu.VMEM((N_KV, c, 2 * CMP_HIDDEN), F32), pltpu.VMEM((c * CMP_PITCH, width), F32)]),
        out_shape=[jax.ShapeDtypeStruct((ns, c, KVW), BF16)] * 2,
        compiler_params=_cparams(("arbitrary",)),
        name="compress_sample",
    )(page_table, kpool, vpool, *cw)


def _rank_select(score, blk, n_real, axis):
    size = 8 if axis == 0 else LANES
    total = score.shape[axis]
    chunk = (lambda a, c: a[c * size:(c + 1) * size]) if axis == 0 else (lambda a, c: a[:, c * size:(c + 1) * size])
    n_chunks = -(-total // size)
    sc = [chunk(score, c) for c in range(n_chunks)]
    bl = [chunk(blk, c) for c in range(n_chunks)]
    rank = [jnp.zeros(s.shape, F32) for s in sc]
    for kk in range(n_real):
        col = score[kk:kk + 1, :] if axis == 0 else score[:, kk:kk + 1]
        for c in range(n_chunks):
            other = jnp.broadcast_to(col, sc[c].shape)
            if c * size > kk:
                beats = other >= sc[c]
            elif min((c + 1) * size, total) - 1 < kk:
                beats = other > sc[c]
            else:
                beats = (other > sc[c]) | ((other == sc[c]) & (bl[c] > kk))
            rank[c] = rank[c] + jnp.where(beats, 1.0, 0.0)
    return jnp.where(jnp.concatenate(rank, axis=axis) < SEL_TOPK, 1.0, 0.0)


def _block_scores(imp, blk, t):
    cur = t // SEL_BLOCK
    forced = (blk == 0) | (blk == cur) | (blk == cur - 1)
    valid = blk * SEL_BLOCK <= t
    return jnp.where(valid, jnp.where(forced, FORCE, imp), NEG)


def _nsa_prompt_kernel(qt_ref, ngt_ref, kc_ref, vct_ref, ks_ref, vst_ref, kw_ref, vwt_ref,
                       ut_ref, at_ref, ovlt_ref, stat_ref, crow_ref, o_ref, *, n_sel):
    ib = pl.program_id(1)
    qt = qt_ref[0]
    ngt = ngt_ref[0]
    ncp = kc_ref.shape[1]
    cols = GROUP * QB
    sel_rows = LANES // 2
    t_row = ib * QB + lax.broadcasted_iota(jnp.int32, (n_sel, QB), 1)
    blk_t = lax.broadcasted_iota(jnp.int32, (n_sel, QB), 0)
    zeros_q = jnp.zeros((HD, cols), F32)
    vrows = [slice(k * HD, (k + 1) * HD) for k in range(N_KV)]
    q_sel, q_win, o_c, o_w = [], [], [], []
    for k in range(N_KV):
        qk = jnp.concatenate([qt[(GROUP * k + g) * HD:(GROUP * k + g + 1) * HD, :] for g in range(GROUP)],
                             axis=1).astype(F32)
        qa = jnp.concatenate([qk, zeros_q] if k == 0 else [zeros_q, qk], axis=0)

        bias_c = ut_ref[k, pl.ds(pl.multiple_of(ncp - (QB // CMP_STRIDE) * ib, 8), ncp), :]
        p_c = _softmax2_cols(_dot(kc_ref[0], qa.astype(BF16)) + bias_c)
        o_c.append(_dot(vct_ref[0][vrows[k], :], p_c.astype(BF16)))
        psum = p_c[:, 0:QB]
        for g in range(1, GROUP):
            psum = psum + p_c[:, g * QB:(g + 1) * QB]
        psum_hi = psum.astype(BF16)
        psum_lo = (psum - psum_hi.astype(F32)).astype(BF16)
        imp = _dot(ovlt_ref[...], psum_hi) + _dot(ovlt_ref[...], psum_lo)

        sel = _rank_select(_block_scores(imp[0:n_sel], blk_t, t_row), blk_t, n_sel, 0)
        selm1 = jnp.concatenate([sel - 1.0] * GROUP, axis=1)
        if n_sel < sel_rows:
            selm1 = jnp.concatenate([selm1, jnp.zeros((sel_rows - n_sel, cols), F32)], axis=0)
        tail = jnp.concatenate([crow_ref[k], jnp.zeros((LANES - sel_rows - 8, cols), F32)], axis=0)
        q_sel.append(jnp.concatenate([qa, selm1, tail], axis=0).astype(BF16))
        q_win.append(jnp.concatenate([qa, jnp.zeros((sel_rows, cols), F32), tail], axis=0).astype(BF16))

        s_parts, tiles_j = [], []
        for w, tidx in enumerate((0, 1, None, None, 3)):
            jt = ib - w
            jc = jnp.maximum(jt, 0)
            k0 = pl.multiple_of(jc * QB, QB)
            lhs = jnp.concatenate([kw_ref[0, pl.ds(k0, QB), :], stat_ref[pl.ds(k0, QB), :]], axis=1)
            s = _dot(lhs, q_win[k])
            if tidx is not None:
                s = s + at_ref[tidx, k]
            s_parts.append(jnp.where(jt >= 0, s, NEG))
            tiles_j.append(jc)
        p_w = _softmax2_cols(jnp.concatenate(s_parts, axis=0)).astype(BF16)
        o_wk = jnp.zeros((HD, cols), F32)
        for w, jc in enumerate(tiles_j):
            o_wk = o_wk + _dot(vwt_ref[0, jc, vrows[k], :], p_w[w * QB:(w + 1) * QB])
        o_w.append(o_wk)

    ones_pad = 16
    acc_rows = HD + ones_pad

    def make_step(keys, base, near):
        tps = keys // QB
        ones_rows = jnp.where(lax.broadcasted_iota(jnp.int32, (ones_pad, keys), 0) == 0, 1.0, 0.0).astype(BF16)

        def tile_ids(jp, sp):
            j0 = base + (jp * SEL_SPLIT + sp) * tps
            return [j0 + h for h in range(tps)]

        def step(jp, state):
            scores = []
            for sp in range(SEL_SPLIT):
                k0 = pl.multiple_of(tile_ids(jp, sp)[0] * QB, keys)
                lhs = jnp.concatenate([ks_ref[0, pl.ds(k0, keys), :], stat_ref[pl.ds(k0, keys), :]], axis=1)
                for k in range(N_KV):
                    s = _dot(lhs, q_sel[k])
                    if near:
                        tidx = [jnp.where(jt == ib, 0, jnp.where(jt == ib - 1, 1, jnp.where(jt < ib, 2, 4)))
                                for jt in tile_ids(jp, sp)]
                        s = s + jnp.concatenate([at_ref[ti, k] for ti in tidx], axis=0)
                    scores.append(s)
            stats = []
            for (m, _), s in zip(state, scores):
                m_new = jnp.maximum(m, jnp.max(s, axis=0, keepdims=True))
                stats.append((m_new, jnp.exp2(m - m_new), jnp.exp2(s - m_new).astype(BF16)))
            vts = [jnp.concatenate([jnp.concatenate([vst_ref[0, jt, vrows[k], :] for jt in tile_ids(jp, sp)], axis=1),
                                    ones_rows], axis=0)
                   for sp in range(SEL_SPLIT) for k in range(N_KV)]
            return tuple((m_new, alpha * acc + _dot(vt, p))
                         for (_, acc), (m_new, alpha, p), vt in zip(state, stats, vts))
        return step

    far_tiles = SEL_SPLIT * SEL_KEYS // QB
    near_tiles = SEL_SPLIT * SEL_KEYS_NEAR // QB
    init = (jnp.full((1, cols), NEG, F32), jnp.zeros((acc_rows, cols), F32))
    n_far = jnp.maximum(ib - 1, 0) // far_tiles
    near_base = n_far * far_tiles
    n_near = (ib - near_base + near_tiles) // near_tiles
    sel_state = lax.fori_loop(0, n_far, make_step(SEL_KEYS, 0, False), (init,) * (N_KV * SEL_SPLIT))
    sel_state = lax.fori_loop(0, n_near, make_step(SEL_KEYS_NEAR, near_base, True), sel_state)

    out_rows = []
    for k in range(N_KV):
        parts = [sel_state[sp * N_KV + k] for sp in range(SEL_SPLIT)]
        m_s = parts[0][0]
        for m_p, _ in parts[1:]:
            m_s = jnp.maximum(m_s, m_p)
        acc_s = jnp.zeros(init[1].shape, F32)
        for m_p, acc_p in parts:
            acc_s = acc_s + jnp.exp2(m_p - m_s) * acc_p
        o_s = acc_s[0:HD] / jnp.maximum(acc_s[HD:HD + 1], 1e-30)

        def gate_row(br):
            return jnp.concatenate([ngt[(GROUP * k + g) * N_BRANCH + br:(GROUP * k + g) * N_BRANCH + br + 1, :]
                                    for g in range(GROUP)], axis=1)
        o_k = gate_row(0) * o_c[k] + gate_row(1) * o_s + gate_row(2) * o_w[k]
        out_rows += [o_k[:, g * QB:(g + 1) * QB] for g in range(GROUP)]
    o_ref[0] = jnp.concatenate(out_rows, axis=0).T.astype(BF16)


def _nsa_prompt(qt, ngt, kc, vct, ks, vst, kw, vwt, tables):
    nb, aw, t = qt.shape
    nq = t // QB
    ncp = kc.shape[1]
    full3 = lambda b, i: (b, 0, 0)
    full4 = lambda b, i: (b, 0, 0, 0)
    return pl.pallas_call(
        functools.partial(_nsa_prompt_kernel, n_sel=t // SEL_BLOCK),
        grid=(nb, nq),
        in_specs=[pl.BlockSpec((1, aw, QB), lambda b, i: (b, 0, i)),
                  pl.BlockSpec((1, NG_ROWS, QB), lambda b, i: (b, 0, i)),
                  pl.BlockSpec((1, ncp, KVW), full3), pl.BlockSpec((1, KVW, ncp), full3),
                  pl.BlockSpec((1, t, KVW), full3), pl.BlockSpec((1, nq, KVW, QB), full4),
                  pl.BlockSpec((1, t, KVW), full3), pl.BlockSpec((1, nq, KVW, QB), full4)]
                 + [_const_spec(a.shape) for a in tables],
        out_specs=pl.BlockSpec((1, QB, aw), lambda b, i: (b, i, 0)),
        out_shape=jax.ShapeDtypeStruct((nb, t, aw), BF16),
        compiler_params=_cparams(("parallel", "arbitrary")),
        name="nsa_prompt",
    )(qt, ngt, kc, vct, ks, vst, kw, vwt, *tables)


def _nsa_sample_kernel(pt_ref, q_ref, gate_ref, kc_ref, vc_ref, kpool, vpool, ksn_ref, vsn_ref,
                       kwin_ref, vwin_ref, kwn_ref, vwn_ref, bc_ref, bs_ref, bw_ref, ovl_ref, e_ref,
                       o_ref, kwo_ref, vwo_ref, kbuf, vbuf, sems, *, n_sel, past, t_new):
    n_pages = pt_ref.shape[1]
    wb = kwin_ref.shape[2]
    slot = _paged_prefetch(pt_ref, (kpool, vpool), (kbuf, vbuf), sems, n_pages)
    past_t = lambda buf: jnp.concatenate([buf[slot, p] for p in range(n_pages)], axis=1).astype(BF16)
    q = q_ref[0]
    nb_past = past // SEL_BLOCK
    pad_new = jnp.zeros((QB - ksn_ref.shape[1], KVW), F32)
    new_tile = lambda ref: jnp.concatenate([ref[0], pad_new], axis=0).astype(BF16)

    p_c = _masked_softmax(_dot_t(q, kc_ref[0]) + bc_ref[...])
    o_c = _dot(p_c.astype(BF16), vc_ref[0])
    parts = []
    for k in range(N_KV):
        base = k * GROUP * t_new
        ps = p_c[base:base + t_new]
        for g in range(1, GROUP):
            ps = ps + p_c[base + g * t_new:base + (g + 1) * t_new]
        parts.append(ps)
    psum = jnp.concatenate(parts, axis=0)
    psum_hi = psum.astype(BF16)
    psum_lo = (psum - psum_hi.astype(F32)).astype(BF16)
    imp = _dot(psum_hi, ovl_ref[...]) + _dot(psum_lo, ovl_ref[...])
    blk = lax.broadcasted_iota(jnp.int32, imp.shape, 1)
    tpos = past + lax.broadcasted_iota(jnp.int32, imp.shape, 0) % t_new
    sel = _rank_select(_block_scores(imp, blk, tpos), blk, n_sel, 1)
    sel = jnp.concatenate([sel[k * t_new:(k + 1) * t_new] for k in range(N_KV) for _ in range(GROUP)], axis=0)

    mask_add = _dot((sel[:, 0:LANES] - 1.0).astype(BF16), e_ref[...])
    s_past = _dot(q, past_t(kbuf)) + bs_ref[:, 0:past] + mask_add
    s_new = _dot_t(q, new_tile(ksn_ref)) + bs_ref[:, past:]
    s_new = jnp.where(sel[:, nb_past:nb_past + 1] > 0.5, s_new, NEG)
    p_s = _masked_softmax(jnp.concatenate([s_past, s_new], axis=1)).astype(BF16)
    o_s = _dot_t(p_s[:, 0:past], past_t(vbuf)) + _dot(p_s[:, past:], new_tile(vsn_ref))

    s_w = jnp.concatenate([_dot(q, kwin_ref[0].astype(BF16)), _dot_t(q, new_tile(kwn_ref))], axis=1)
    p_w = _masked_softmax(s_w + bw_ref[...]).astype(BF16)
    o_w = _dot_t(p_w[:, 0:wb], vwin_ref[0].astype(BF16)) + _dot(p_w[:, wb:], new_tile(vwn_ref))

    gate = gate_ref[0]
    o_ref[0] = gate[:, 0:1] * o_c + gate[:, 1:2] * o_s + gate[:, 2:3] * o_w

    lane = lax.broadcasted_iota(jnp.int32, (KVW, wb), 1)
    for win_ref, new_ref, out_ref in ((kwin_ref, kwn_ref, kwo_ref), (vwin_ref, vwn_ref, vwo_ref)):
        new_t = jnp.concatenate([new_ref[0], pad_new], axis=0).T
        tail = pltpu.roll(jnp.concatenate([jnp.zeros((KVW, wb - QB), F32), new_t], axis=1), QB - t_new, 1)
        out_ref[0] = jnp.where(lane >= wb - t_new, tail, pltpu.roll(win_ref[0], wb - t_new, 1))


def _nsa_sample(page_table, q, gate, kc, vc, kpool, vpool, ksn, vsn, kwin, vwin, kwn, vwn,
                bc, bs, bw, ovl, e, n_sel, past, t_new):
    ns, n_pages = page_table.shape
    rows, kvw = q.shape[1:]
    buf_shape = (2, n_pages) + kpool.shape[1:]
    seq3 = lambda s, pt: (s, 0, 0)
    any_spec = pl.BlockSpec(memory_space=pl.ANY)
    cs = lambda a: pl.BlockSpec(a.shape, lambda s, pt, nd=a.ndim: (0,) * nd, pipeline_mode=pl.Buffered(1))
    per_seq = lambda a: pl.BlockSpec((1,) + a.shape[1:], seq3)
    return pl.pallas_call(
        functools.partial(_nsa_sample_kernel, n_sel=n_sel, past=past, t_new=t_new),
        grid_spec=pltpu.PrefetchScalarGridSpec(
            num_scalar_prefetch=1,
            grid=(ns,),
            in_specs=[per_seq(q), per_seq(gate), per_seq(kc), per_seq(vc), any_spec, any_spec,
                      per_seq(ksn), per_seq(vsn), per_seq(kwin), per_seq(vwin), per_seq(kwn), per_seq(vwn),
                      cs(bc), cs(bs), cs(bw), cs(ovl), cs(e)],
            out_specs=[pl.BlockSpec((1, rows, kvw), seq3), per_seq(kwin), per_seq(vwin)],
            scratch_shapes=[pltpu.VMEM(buf_shape, F32), pltpu.VMEM(buf_shape, F32), pltpu.SemaphoreType.DMA((2, 2))]),
        out_shape=[jax.ShapeDtypeStruct((ns, rows, kvw), F32), jax.ShapeDtypeStruct(kwin.shape, F32),
                   jax.ShapeDtypeStruct(vwin.shape, F32)],
        compiler_params=_cparams(("arbitrary",)),
        name="nsa_sample",
    )(page_table, q, gate, kc, vc, kpool, vpool, ksn, vsn, kwin, vwin, kwn, vwn, bc, bs, bw, ovl, e)


def _ssm_param_kernel(ar_ref, ai_ref, ldt_ref, br_ref, bi_ref, abr_ref, abi_ref, bbr_ref, bbi_ref):
    ar = ar_ref[...]
    ai = ai_ref[...]
    dt = jnp.exp(ldt_ref[...])
    mag = jnp.exp(ar * dt)
    abr = mag * jnp.cos(ai * dt)
    abi = mag * jnp.sin(ai * dt)
    den = ar * ar + ai * ai
    nr, ni = abr - 1.0, abi
    fr = (nr * ar + ni * ai) / den
    fi = (ni * ar - nr * ai) / den
    abr_ref[...] = abr
    abi_ref[...] = abi
    for g in range(ar.shape[0]):
        br = br_ref[g]
        bi = bi_ref[g]
        frg = fr[g:g + 1, :]
        fig = fi[g:g + 1, :]
        bbr_ref[g] = frg * br - fig * bi
        bbi_ref[g] = frg * bi + fig * br


def _ssm_params(a_re, a_im, log_dt, b_re_t, b_im_t):
    g, p = a_re.shape
    return pl.pallas_call(
        _ssm_param_kernel,
        out_shape=[jax.ShapeDtypeStruct((g, p), F32)] * 2 + [jax.ShapeDtypeStruct(b_re_t.shape, F32)] * 2,
        name="ssm_params",
    )(a_re, a_im, log_dt.reshape(g, 1), b_re_t, b_im_t)


def _ssm_kernel(u_ref, h0r_ref, h0i_ref, ar_ref, ai_ref, bd_ref, cd_ref, d_ref, wglu_ref, bglu_ref,
                so_ref, hr_ref, hi_ref, xr_scr, xi_scr, *slab_scr, bt):
    i = pl.program_id(0)
    n_slab = bd_ref.shape[0]
    width = n_slab * LANES
    sw = bd_ref.shape[2] // 2
    if slab_scr:
        slab = slab_scr[0]
        tt = u_ref.shape[0]
        rows = tt * bt
        u_wide = u_ref[...].astype(F32)
        for b in range(bt):
            for sl in range(n_slab):
                lanes = slice(b * width + sl * LANES, b * width + (sl + 1) * LANES)
                slab[sl, pl.ds(b, tt, stride=bt), :] = u_wide[:, lanes]
        u_slabs = [slab[sl] for sl in range(n_slab)]
    else:
        rows = u_ref.shape[0]
        u_rows = u_ref[...].astype(F32)
        u_slabs = [u_rows[:, sl * LANES:(sl + 1) * LANES] for sl in range(n_slab)]

    @pl.when(i == 0)
    def _():
        hr_ref[...] = h0r_ref[...]
        hi_ref[...] = h0i_ref[...]

    for sl in range(n_slab):
        x = _dot(u_slabs[sl].astype(BF16), bd_ref[sl])
        xr_scr[:, sl * sw:(sl + 1) * sw] = x[:, :sw]
        xi_scr[:, sl * sw:(sl + 1) * sw] = x[:, sw:]

    per = 8 // math.gcd(bt, 8)
    grp = per * bt
    lc = 512
    for c0 in range(0, xr_scr.shape[1], lc):
        cl = slice(c0, c0 + lc)
        a_r = jnp.broadcast_to(ar_ref[:, cl], (bt, lc))
        a_i = jnp.broadcast_to(ai_ref[:, cl], (bt, lc))

        def step(j, carry):
            h_r, h_i = carry
            r0 = pl.multiple_of(j * grp, grp)
            xr = xr_scr[pl.ds(r0, grp), cl]
            xi = xi_scr[pl.ds(r0, grp), cl]
            out_r, out_i = [], []
            for s in range(per):
                n_r = a_r * h_r - a_i * h_i + xr[s * bt:(s + 1) * bt]
                n_i = a_r * h_i + a_i * h_r + xi[s * bt:(s + 1) * bt]
                h_r, h_i = n_r, n_i
                out_r.append(h_r)
                out_i.append(h_i)
            xr_scr[pl.ds(r0, grp), cl] = jnp.concatenate(out_r, axis=0) if per > 1 else out_r[0]
            xi_scr[pl.ds(r0, grp), cl] = jnp.concatenate(out_i, axis=0) if per > 1 else out_i[0]
            return h_r, h_i

        h_r, h_i = lax.fori_loop(0, rows // grp, step, (hr_ref[:, cl], hi_ref[:, cl]))
        hr_ref[:, cl] = h_r
        hi_ref[:, cl] = h_i

    ys = []
    for sl in range(n_slab):
        hcat = jnp.concatenate([xr_scr[:, sl * sw:(sl + 1) * sw], xi_scr[:, sl * sw:(sl + 1) * sw]], axis=1)
        ys.append(_dot(hcat.astype(BF16), cd_ref[sl]))
    y = jnp.concatenate(ys, axis=1) + d_ref[...] * jnp.concatenate(u_slabs, axis=1)
    z = _gelu(y)
    so = z * _sigmoid(_dot(z.astype(BF16), wglu_ref[...]) + bglu_ref[...])
    if slab_scr:
        for sl in range(n_slab):
            slab[sl] = so[:, sl * LANES:(sl + 1) * LANES]
        for b in range(bt):
            for sl in range(n_slab):
                lanes = slice(b * width + sl * LANES, b * width + (sl + 1) * LANES)
                so_ref[:, lanes] = slab[sl, pl.ds(b, tt, stride=bt), :].astype(so_ref.dtype)
    else:
        so_ref[...] = so.astype(so_ref.dtype)


def _ssm(u, h0r, h0i, ar, ai, bd, cd, dvec, wglu, bglu, bt, tt, time_major):
    width = bd.shape[0] * LANES
    rows = tt * bt
    nstate = ar.shape[1]
    blk = (tt, bt * width) if time_major else (rows, width)
    cst = [_const_spec(a.shape) for a in (h0r, h0i, ar, ai, bd, cd, dvec, wglu, bglu)]
    st_spec = _const_spec((bt, nstate))
    scratch = [pltpu.VMEM((rows, nstate), F32), pltpu.VMEM((rows, nstate), F32)]
    if time_major:
        scratch.append(pltpu.VMEM((bd.shape[0], rows, LANES), F32))
    return pl.pallas_call(
        functools.partial(_ssm_kernel, bt=bt),
        grid=(u.shape[0] // blk[0],),
        in_specs=[pl.BlockSpec(blk, lambda i: (i, 0))] + cst,
        out_specs=[pl.BlockSpec(blk, lambda i: (i, 0)), st_spec, st_spec],
        out_shape=[jax.ShapeDtypeStruct(u.shape, BF16), jax.ShapeDtypeStruct((bt, nstate), F32),
                   jax.ShapeDtypeStruct((bt, nstate), F32)],
        scratch_shapes=scratch,
        compiler_params=_cparams(("arbitrary",)),
        name="ssm",
    )(u, h0r, h0i, ar, ai, bd, cd, dvec, wglu, bglu)


def _back_kernel(h_ref, o_ref, so_ref, mg_ref, p_ref, watt_ref, wssm_ref, wo_ref, fn_ref, wg_ref, wu_ref, wd_ref,
                 pn_ref, wpg_ref, wple_ref, fin_ref, y_ref):
    d = h_ref.shape[1]
    a = _dot(o_ref[...], watt_ref[...])
    s = _dot(so_ref[...], wssm_ref[...])
    mg = mg_ref[...].astype(F32)
    h = h_ref[...] + _dot((mg[:, :d] * a + mg[:, d:] * s).astype(BF16), wo_ref[...])
    f = _rms(h, fn_ref[...]).astype(BF16)
    gate = _dot(f, wg_ref[...])
    up = _dot(f, wu_ref[...])
    h = h + _dot((gate * _sigmoid(gate) * up).astype(BF16), wd_ref[...])
    g = _sigmoid(_dot(_rms(h, pn_ref[...]).astype(BF16), wpg_ref[...]))
    h = h + g * _dot(p_ref[...].astype(BF16), wple_ref[...])
    y_ref[...] = _rms(h, fin_ref[...])


def _back(h2d, o2d, so_tb, mg, p2d, weights, nb, t, tm):
    n, d = h2d.shape
    nt = t // tm
    row = lambda b, i: (b * nt + i, 0)
    wspecs = [pl.BlockSpec(w.shape, lambda b, i, nd=w.ndim: (0,) * nd, pipeline_mode=pl.Buffered(1))
              for w in weights]
    sw = o2d.shape[1]
    return pl.pallas_call(
        _back_kernel,
        grid=(nb, nt),
        in_specs=[pl.BlockSpec((tm, d), row), pl.BlockSpec((tm, sw), row),
                  pl.BlockSpec((tm, sw), lambda b, i: (i, b)),
                  pl.BlockSpec((tm, mg.shape[1]), row), pl.BlockSpec((tm, p2d.shape[1]), row)] + wspecs,
        out_specs=pl.BlockSpec((tm, d), row),
        out_shape=jax.ShapeDtypeStruct((n, d), F32),
        compiler_params=_cparams(("parallel", "parallel")),
        name="back",
    )(h2d, o2d, so_tb, mg, p2d, *weights)


def _bucket_np(dist):
    n = np.maximum(dist, 0)
    exact = N_BUCKETS // 2
    nf = np.maximum(n, 1).astype(np.float64)
    large = exact + (np.log(nf / exact) / math.log(REL_MAX_DIST / exact) * (N_BUCKETS - exact)).astype(np.int64)
    return np.where(n < exact, n, np.minimum(large, N_BUCKETS - 1)).astype(np.int32)


def _bias_table(rel_bias, dist, valid, offset=None):
    onehot = jax.nn.one_hot(jnp.asarray(_bucket_np(dist)), N_BUCKETS, dtype=F32)
    b = jnp.einsum('...b,bh->h...', onehot, rel_bias.astype(F32), precision=lax.Precision.HIGHEST)
    if offset is not None:
        b = b - offset.reshape((N_HEADS,) + (1,) * dist.ndim)
    return jnp.where(jnp.asarray(valid)[None], b, NEG)


def _prompt_tables(rel_bias, t):
    def cols(b):
        r = b.shape[1]
        return b.reshape(N_KV, GROUP, r, QB).transpose(0, 2, 1, 3).reshape(N_KV, r, GROUP * QB)
    rel_bias = rel_bias.astype(F32) * LOG2E
    c = rel_bias[N_BUCKETS - 1]
    c_hi = c.astype(BF16)
    c_lo = (c - c_hi.astype(F32)).astype(BF16)
    c_eff = c_hi.astype(F32) + c_lo.astype(F32)
    crow = jnp.stack([c_hi.astype(F32), c_lo.astype(F32)] + [jnp.zeros_like(c)] * 6, axis=1)
    crow = jnp.broadcast_to(crow[:, :, None], (N_HEADS, 8, QB))
    crow = cols(crow)
    j = np.arange(QB)[:, None]
    i = np.arange(QB)[None, :]
    ones = np.ones((QB, QB), bool)
    zeros = jnp.zeros((N_KV, QB, GROUP * QB), F32)
    at = jnp.stack([
        cols(_bias_table(rel_bias, i - j, i >= j, c_eff)),
        cols(_bias_table(rel_bias, QB + i - j, ones, c_eff)),
        zeros,
        jnp.where(jnp.asarray(np.tile(j > i, (1, GROUP)))[None], zeros, NEG),
        zeros + NEG,
    ])
    ncp = t // CMP_STRIDE
    m = np.arange(2 * ncp)[:, None] - ncp
    dist = i - CMP_STRIDE * m - (CMP_BLOCK - 1)
    ut = cols(_bias_table(rel_bias, dist, dist >= 0))
    n_sel = t // SEL_BLOCK
    n = np.arange(ncp)[None, :]
    jb = np.arange(LANES)[:, None]
    ovlt = ((n * CMP_STRIDE < jb * SEL_BLOCK + SEL_BLOCK) & (n * CMP_STRIDE + CMP_BLOCK - 1 >= jb * SEL_BLOCK)
            & (jb < n_sel) & (n < ncp - 1))
    key = np.arange(t)[:, None]
    lane = np.arange(LANES)[None, :]
    stat = np.where(lane < LANES // 2, (lane == key // SEL_BLOCK) * BIG,
                    ((lane == LANES // 2) | (lane == LANES // 2 + 1)) * 1.0).astype(np.float32)
    return ut, at, jnp.asarray(ovlt.astype(np.float32), BF16), jnp.asarray(stat, BF16), crow


def _sample_tables(rel_bias, past, t_new, win_buf):
    def rows(b):
        return b.reshape(N_HEADS * t_new, b.shape[-1])
    tok = np.arange(t_new)[:, None]
    nc = past // CMP_STRIDE
    n = np.arange(nc)[None, :]
    c_end = n * CMP_STRIDE + CMP_BLOCK - 1
    n_cmp = (past + t_new) // CMP_STRIDE - 1
    bc = rows(_bias_table(rel_bias, past + tok - c_end, (c_end <= past + tok) & (n < n_cmp)))
    js = np.arange(past + QB)[None, :]
    ds = np.where(js < past, past + tok - js, tok - (js - past))
    bs = rows(_bias_table(rel_bias, ds, np.where(js < past, True, (ds >= 0) & (js - past < t_new))))
    jw = np.arange(win_buf + QB)[None, :]
    dw = np.where(jw < win_buf, win_buf + tok - jw, tok - (jw - win_buf))
    valid = np.where(jw < win_buf, (dw >= 0) & (dw < WINDOW), (dw >= 0) & (jw - win_buf < t_new))
    bw = rows(_bias_table(rel_bias, dw, valid))
    n_sel = -(-(past + t_new) // SEL_BLOCK)
    nbp = 2 * LANES
    nn = np.arange(nc)[:, None]
    jb = np.arange(nbp)[None, :]
    ovl = ((nn * CMP_STRIDE < jb * SEL_BLOCK + SEL_BLOCK) & (nn * CMP_STRIDE + CMP_BLOCK - 1 >= jb * SEL_BLOCK)
           & (jb < n_sel) & (nn < n_cmp))
    e = (np.arange(LANES)[:, None] == np.arange(past)[None, :] // SEL_BLOCK).astype(np.float32) * BIG
    return bc, bs, bw, jnp.asarray(ovl.astype(np.float32), BF16), jnp.asarray(e, BF16), n_sel


def _slab_diag(blocks, n_slab):
    g, r, c = blocks.shape
    gps = g // n_slab
    eye = jnp.eye(gps, dtype=blocks.dtype)
    return jnp.einsum('sgrc,gh->sgrhc', blocks.reshape(n_slab, gps, r, c), eye).reshape(n_slab, gps * r, gps * c)


def _layer_params(rel_bias, final_norm, attn_norm, w_in, cmp_pe_k, cmp_w1_k, cmp_w2_k, cmp_pe_v, cmp_w1_v, cmp_w2_v,
                  ssm_a_re, ssm_a_im, ssm_log_dt, ssm_b_re, ssm_b_im, ssm_c_re, ssm_c_im, ssm_d, w_glu, b_glu,
                  w_att_br, w_ssm_br, w_o, ffn_norm, w_ffn_gate, w_ffn_up, w_ffn_down, ple_norm, w_ple_gate, w_ple):
    l = 0
    d = w_in.shape[1]
    n_groups = ssm_a_re.shape[1]
    ssm_w = n_groups * SSM_CH
    nstate = n_groups * SSM_P
    assert ssm_w % LANES == 0
    w = w_in[l]
    c0 = AW + 6 * KVW
    n_gate = N_HEADS * N_BRANCH
    front_w = (attn_norm[l].reshape(1, d), w[:, :c0].astype(BF16),
               jnp.pad(w[:, c0:c0 + n_gate], ((0, 0), (0, LANES - n_gate))).astype(BF16),
               w[:, c0 + n_gate:c0 + n_gate + ssm_w].astype(BF16), w[:, c0 + n_gate + ssm_w:].astype(BF16))

    def cmp_weights(pe, w1, w2):
        half = CMP_STRIDE * HD
        w1cat = jnp.concatenate([w1[:half], w1[half:]], axis=1).astype(BF16)
        pe2 = jnp.pad(pe.reshape(2, half), ((0, 6), (0, 0))).astype(BF16)
        return w1cat, w2.astype(BF16), pe2
    cw = cmp_weights(cmp_pe_k[l], cmp_w1_k[l], cmp_w2_k[l]) + cmp_weights(cmp_pe_v[l], cmp_w1_v[l], cmp_w2_v[l])

    abr, abi, bbr_t, bbi_t = _ssm_params(ssm_a_re[l], ssm_a_im[l], ssm_log_dt[l],
                                         jnp.swapaxes(ssm_b_re[l], 1, 2), jnp.swapaxes(ssm_b_im[l], 1, 2))
    n_slab = ssm_w // LANES
    bd = jnp.concatenate([_slab_diag(bbr_t, n_slab), _slab_diag(bbi_t, n_slab)], axis=2).astype(BF16)
    cd = jnp.concatenate([_slab_diag(jnp.swapaxes(ssm_c_re[l], 1, 2), n_slab),
                          -_slab_diag(jnp.swapaxes(ssm_c_im[l], 1, 2), n_slab)], axis=1).astype(BF16)
    ssm_p = (abr.reshape(1, nstate), abi.reshape(1, nstate), bd, cd, ssm_d[l].reshape(1, ssm_w),
             w_glu[l].astype(BF16), b_glu[l].reshape(1, ssm_w))

    back_w = (w_att_br[l].astype(BF16), w_ssm_br[l].astype(BF16), w_o[l].astype(BF16),
              ffn_norm[l].reshape(1, d), w_ffn_gate[l].astype(BF16), w_ffn_up[l].astype(BF16),
              w_ffn_down[l].astype(BF16), ple_norm[l].reshape(1, d), w_ple_gate[l].astype(BF16),
              w_ple[l].astype(BF16), final_norm.reshape(1, d))
    return dict(front=front_w, cmp=cw, ssm=ssm_p, back=back_w, rel_bias=rel_bias,
                n_groups=n_groups, ssm_w=ssm_w, nstate=nstate, n_gate=n_gate)


def _prompt_group(x_prompt, p_l, prm):
    nb, t, d = x_prompt.shape
    ssm_w, nstate = prm["ssm_w"], prm["nstate"]
    assert t % (CMP_STRIDE * LANES) == 0 and t >= WINDOW and t // SEL_BLOCK <= LANES // 2
    xp = x_prompt.reshape(nb * t, d)
    (qt, kct, vct, kst, vst32, kc, vc, kw, vw, ksb, kwb, vst, vwt, ngt, su, mg) = _front_prompt(
        xp, nb, t, prm["front"], min(FRONT_ROWS, t))
    kcc, vcct = _compress_prompt(kc.reshape(nb, t, KVW), vc.reshape(nb, t, KVW), prm["cmp"])
    o = _nsa_prompt(qt, ngt, kcc, vcct, ksb.reshape(nb, t, KVW), vst, kwb.reshape(nb, t, KVW), vwt,
                    _prompt_tables(prm["rel_bias"], t))
    zeros_state = jnp.zeros((nb, nstate), F32)
    so, sr, si = _ssm(su, zeros_state, zeros_state, *prm["ssm"], nb, SSM_STEPS, True)
    y = _back(xp, o.reshape(nb * t, AW), so, mg, p_l.reshape(nb * t, -1),
              prm["back"], nb, t, BACK_ROWS)
    return dict(y=y, o=o, so=so, rows_t=(kct, vct, kst, vst32), win=(kw, vw), state=(sr, si))


def _sample_group(x_sample, p_l, page_table, pools, wins, states, prm):
    ns, t_new, d = x_sample.shape
    tok_minor = lambda a: jnp.transpose(a, (0, 2, 3, 1)).reshape(a.shape[0], KVW, a.shape[1])
    k_cmp, v_cmp, k_sel, v_sel = pools
    k_win, v_win = wins
    n_phys, page = k_cmp.shape[:2]
    past = page_table.shape[1] * page
    win_buf = k_win.shape[1]
    ssm_w, nstate, n_gate = prm["ssm_w"], prm["nstate"], prm["n_gate"]
    assert page == QB and past % (CMP_STRIDE * LANES) == 0 and past // SEL_BLOCK <= LANES
    assert win_buf == WINDOW and past >= win_buf and t_new <= 8 and t_new < CMP_STRIDE
    n_s = ns * t_new
    xs = x_sample.reshape(n_s, d)
    q_s, kc_s, vc_s, ks_s, vs_s, kw_s, vw_s, ng_s, su_s, mg_s = _front_sample(xs, prm["front"])
    kcc_s, vcc_s = _compress_sample(page_table, tok_minor(k_cmp), tok_minor(v_cmp), prm["cmp"])
    bc, bs, bw, ovl_s, e_s, n_sel_s = _sample_tables(prm["rel_bias"], past, t_new, win_buf)
    rows_s = N_HEADS * t_new
    eye_kv = jnp.eye(N_KV, dtype=BF16)
    q_rows = q_s.reshape(ns, t_new, N_KV, GROUP, HD).transpose(0, 2, 3, 1, 4)
    q_rows = jnp.einsum('skgtd,kj->skgtjd', q_rows, eye_kv).reshape(ns, rows_s, KVW)
    gate_s = ng_s[:, :n_gate].reshape(ns, t_new, N_KV, GROUP, N_BRANCH).transpose(0, 2, 3, 1, 4)
    gate_s = jnp.pad(gate_s.reshape(ns, rows_s, N_BRANCH), ((0, 0), (0, 0), (0, LANES - N_BRANCH)))
    pad8 = lambda a: jnp.pad(a.reshape(ns, t_new, KVW), ((0, 0), (0, 8 - t_new), (0, 0)))
    o_s, kwin_new, vwin_new = _nsa_sample(
        page_table, q_rows, gate_s, kcc_s, vcc_s,
        tok_minor(k_sel), tok_minor(v_sel), pad8(ks_s), pad8(vs_s), tok_minor(k_win), tok_minor(v_win),
        pad8(kw_s), pad8(vw_s), bc, bs, bw, ovl_s, e_s, n_sel_s, past, t_new)
    o_s = o_s.reshape(ns, N_KV, GROUP, t_new, N_KV, HD)
    o_s = jnp.stack([o_s[:, k, :, :, k, :] for k in range(N_KV)], axis=1)
    o_s = o_s.transpose(0, 3, 1, 2, 4).reshape(n_s, AW).astype(BF16)
    su_ts = su_s.reshape(ns, t_new, ssm_w).transpose(1, 0, 2).reshape(n_s, ssm_w)
    so_ts, sr, si = _ssm(su_ts, states[0].reshape(ns, nstate), states[1].reshape(ns, nstate), *prm["ssm"], ns, t_new,
                         False)
    so_s = so_ts.reshape(t_new, ns, ssm_w).transpose(1, 0, 2).reshape(n_s, ssm_w)
    y = _back(xs, o_s, so_s, mg_s, p_l.reshape(n_s, -1), prm["back"], 1, n_s, n_s)
    return dict(y=y, o=o_s, so=so_s, rows=(kc_s, vc_s, ks_s, vs_s), win_t=(kwin_new, vwin_new), state=(sr, si))


def kernel(x_prompt, x_sample, cache_k_cmp, cache_v_cmp, cache_k_sel, cache_v_sel, cache_k_win, cache_v_win, state_ssm_re, state_ssm_im, page_table, p_prompt, p_sample, rel_bias, final_norm, attn_norm, w_in, cmp_pe_k, cmp_w1_k, cmp_w2_k, cmp_pe_v, cmp_w1_v, cmp_w2_v, ssm_a_re, ssm_a_im, ssm_log_dt, ssm_b_re, ssm_b_im, ssm_c_re, ssm_c_im, ssm_d, w_glu, b_glu, w_att_br, w_ssm_br, w_o, ffn_norm, w_ffn_gate, w_ffn_up, w_ffn_down, ple_norm, w_ple_gate, w_ple):
    assert w_in.shape[0] == 1, "single-layer trunk"
    l = 0
    nb, t, d = x_prompt.shape
    ns, t_new = x_sample.shape[:2]
    prm = _layer_params(rel_bias, final_norm, attn_norm, w_in, cmp_pe_k, cmp_w1_k, cmp_w2_k, cmp_pe_v, cmp_w1_v,
                        cmp_w2_v, ssm_a_re, ssm_a_im, ssm_log_dt, ssm_b_re, ssm_b_im, ssm_c_re, ssm_c_im, ssm_d,
                        w_glu, b_glu, w_att_br, w_ssm_br, w_o, ffn_norm, w_ffn_gate, w_ffn_up, w_ffn_down,
                        ple_norm, w_ple_gate, w_ple)
    pg = _prompt_group(x_prompt, p_prompt[l], prm)
    sg = _sample_group(x_sample, p_sample[l], page_table,
                       (cache_k_cmp[l], cache_v_cmp[l], cache_k_sel[l], cache_v_sel[l]),
                       (cache_k_win[l], cache_v_win[l]), (state_ssm_re[l], state_ssm_im[l]), prm)

    kv5 = lambda a, b_, t_: a.reshape(1, b_, t_, N_KV, HD)
    kv5_t = lambda a: jnp.transpose(a.reshape(1, nb, N_KV, HD, t), (0, 1, 4, 2, 3))
    keep = min(WINDOW, t)
    win_p = lambda a: a.reshape(nb, t, KVW)[:, t - keep:].reshape(1, nb, keep, N_KV, HD)
    win_s = lambda a: jnp.transpose(a.reshape(1, ns, N_KV, HD, a.shape[-1]), (0, 1, 4, 2, 3))
    st = lambda a, b_: a.reshape(1, b_, prm["n_groups"], SSM_P)
    kc_s, vc_s, ks_s, vs_s = sg["rows"]
    return (pg["y"].reshape(nb, t, d), sg["y"].reshape(ns, t_new, d),
            *[kv5_t(a) for a in pg["rows_t"]], win_p(pg["win"][0]), win_p(pg["win"][1]),
            st(pg["state"][0], nb), st(pg["state"][1], nb),
            kv5(kc_s, ns, t_new), kv5(vc_s, ns, t_new), kv5(ks_s, ns, t_new), kv5(vs_s, ns, t_new),
            win_s(sg["win_t"][0]), win_s(sg["win_t"][1]),
            st(sg["state"][0], ns), st(sg["state"][1], ns))
```

```python
import functools
import math

import numpy as np
import jax
import jax.numpy as jnp
from jax import lax
from jax.experimental import pallas as pl
from jax.experimental.pallas import tpu as pltpu

F32 = jnp.float32
BF16 = jnp.bfloat16

N_HEADS = 8
N_KV = 2
HD = 64
GROUP = N_HEADS // N_KV
N_BRANCH = 3
CMP_BLOCK = 32
CMP_STRIDE = 16
CMP_HIDDEN = 256
CMP_PITCH = 20
SEL_BLOCK = 64
SEL_TOPK = 16
WINDOW = 512
QB = 128
SEL_KEYS = 4 * QB
SEL_KEYS_NEAR = 2 * QB
SEL_SPLIT = 2
N_BUCKETS = 32
REL_MAX_DIST = 128
SSM_CH = 16
SSM_P = 64
EPS = 1e-6
NEG = -1e30
NEG_TEST = -1e29
FORCE = 1e9
BIG = 1e30
LOG2E = math.log2(math.e)
LANES = 128
VMEM_LIMIT = 56 * 1024 * 1024
FRONT_ROWS = 1024
BACK_ROWS = 512
SSM_STEPS = 256
AW = N_HEADS * HD
KVW = N_KV * HD
NG_ROWS = 32


def _cparams(sem):
    return pltpu.CompilerParams(dimension_semantics=sem, vmem_limit_bytes=VMEM_LIMIT)


def _const_spec(shape):
    nd = len(shape)
    return pl.BlockSpec(shape, lambda *_: (0,) * nd)


def _rms(x, g):
    return x * lax.rsqrt(jnp.mean(x * x, axis=-1, keepdims=True) + EPS) * g


def _gelu(x):
    return x * (0.5 * (1.0 + jnp.tanh(math.sqrt(2.0 / math.pi) * (x + 0.044715 * (x * x * x)))))


def _sigmoid(x):
    return 1.0 / (1.0 + jnp.exp(-x))


def _dot(a, b):
    return jnp.dot(a, b, preferred_element_type=F32)


def _dot_t(a, b):
    return lax.dot_general(a, b, (((1,), (1,)), ((), ())), preferred_element_type=F32)


def _masked_softmax(s, axis=-1):
    valid = s > NEG_TEST
    m = jnp.max(s, axis=axis, keepdims=True)
    e = jnp.where(valid, jnp.exp(s - m), 0.0)
    return e / jnp.maximum(jnp.sum(e, axis=axis, keepdims=True), 1e-30)


def _softmax2_cols(s):
    m = jnp.max(s, axis=0, keepdims=True)
    e = jnp.exp2(s - m)
    inv = jnp.where(m > NEG_TEST, 1.0 / jnp.maximum(jnp.sum(e, axis=0, keepdims=True), 1e-30), 0.0)
    return e * inv


def _front_project(x_ref, g_ref, wa_ref, wng_ref, wsu_ref, wmg_ref, su_ref, mg_ref, q_scale):
    u = _rms(x_ref[...], g_ref[...]).astype(BF16)
    za = _dot(u, wa_ref[...])
    q = za[:, :AW] * q_scale
    rows = [za[:, AW + i * KVW: AW + (i + 1) * KVW] for i in range(6)]
    ng = _sigmoid(_dot(u, wng_ref[...]))
    su_ref[...] = _dot(u, wsu_ref[...]).astype(su_ref.dtype)
    mg_ref[...] = _sigmoid(_dot(u, wmg_ref[...])).astype(BF16)
    return q, rows, ng


def _front_sample_kernel(x_ref, g_ref, wa_ref, wng_ref, wsu_ref, wmg_ref,
                         q_ref, kc_ref, vc_ref, ks_ref, vs_ref, kw_ref, vw_ref, ng_ref, su_ref, mg_ref):
    q, rows, ng = _front_project(x_ref, g_ref, wa_ref, wng_ref, wsu_ref, wmg_ref, su_ref, mg_ref, HD ** -0.5)
    q_ref[...] = q.astype(BF16)
    for ref, r in zip((kc_ref, vc_ref, ks_ref, vs_ref, kw_ref, vw_ref), rows):
        ref[...] = r
    ng_ref[...] = ng


def _front_prompt_kernel(x_ref, g_ref, wa_ref, wng_ref, wsu_ref, wmg_ref,
                         qt_ref, kct_ref, vct_ref, kst_ref, vst32_ref, kc_ref, vc_ref, kw_ref, vw_ref,
                         ksb_ref, kwb_ref, vst_ref, vwt_ref, ngt_ref, su_ref, mg_ref):
    q, rows, ng = _front_project(x_ref, g_ref, wa_ref, wng_ref, wsu_ref, wmg_ref, su_ref, mg_ref,
                                 HD ** -0.5 * LOG2E)
    kc, vc, ks, vs, kw, vw = rows
    qt_ref[0] = q.T.astype(BF16)
    for ref, r in zip((kct_ref, vct_ref, kst_ref, vst32_ref), (kc, vc, ks, vs)):
        ref[0] = r.T
    kc_ref[...] = kc
    vc_ref[...] = vc
    kw_ref[...] = kw
    vw_ref[...] = vw
    ksb_ref[...] = ks.astype(BF16)
    kwb_ref[...] = kw.astype(BF16)
    for ref, r in ((vst_ref, vs), (vwt_ref, vw)):
        rt = r.T.astype(BF16)
        for j in range(rt.shape[1] // QB):
            ref[0, j] = rt[:, j * QB:(j + 1) * QB]
    ngt_ref[0] = ng.T[0:NG_ROWS]


def _front_sample(x2d, fw):
    g, wa, wng, wsu, wmg = fw
    n, d = x2d.shape
    sw, mw = wsu.shape[1], wmg.shape[1]
    shapes = ([jax.ShapeDtypeStruct((n, AW), BF16)] + [jax.ShapeDtypeStruct((n, KVW), F32)] * 6
              + [jax.ShapeDtypeStruct((n, LANES), F32), jax.ShapeDtypeStruct((n, sw), BF16),
                 jax.ShapeDtypeStruct((n, mw), BF16)])
    return pl.pallas_call(
        _front_sample_kernel,
        grid=(1,),
        in_specs=[_const_spec(a.shape) for a in (x2d, g, wa, wng, wsu, wmg)],
        out_specs=[_const_spec(s.shape) for s in shapes],
        out_shape=shapes,
        compiler_params=_cparams(("arbitrary",)),
        name="front_sample",
    )(x2d, g, wa, wng, wsu, wmg)


def _front_prompt(x2d, nb, t, fw, tm):
    g, wa, wng, wsu, wmg = fw
    n, d = x2d.shape
    nt = t // tm
    sw, mw = wsu.shape[1], wmg.shape[1]
    row = lambda b, i: (b * nt + i, 0)
    kv5 = jax.ShapeDtypeStruct((nb, KVW, t), F32)
    kv5_spec = pl.BlockSpec((1, KVW, tm), lambda b, i: (b, 0, i))
    vt = jax.ShapeDtypeStruct((nb, t // QB, KVW, QB), BF16)
    vt_spec = pl.BlockSpec((1, tm // QB, KVW, QB), lambda b, i: (b, i, 0, 0))
    shapes = ([jax.ShapeDtypeStruct((nb, AW, t), BF16)] + [kv5] * 4 + [jax.ShapeDtypeStruct((n, KVW), F32)] * 4
              + [jax.ShapeDtypeStruct((n, KVW), BF16)] * 2 + [vt] * 2
              + [jax.ShapeDtypeStruct((nb, NG_ROWS, t), F32), jax.ShapeDtypeStruct((t, nb * sw), BF16),
                 jax.ShapeDtypeStruct((n, mw), BF16)])
    specs = ([pl.BlockSpec((1, AW, tm), lambda b, i: (b, 0, i))] + [kv5_spec] * 4
             + [pl.BlockSpec((tm, KVW), row)] * 6 + [vt_spec] * 2
             + [pl.BlockSpec((1, NG_ROWS, tm), lambda b, i: (b, 0, i)),
                pl.BlockSpec((tm, sw), lambda b, i: (i, b)), pl.BlockSpec((tm, mw), row)])
    return pl.pallas_call(
        _front_prompt_kernel,
        grid=(nb, nt),
        in_specs=[pl.BlockSpec((tm, d), row)] + [_const_spec(a.shape) for a in (g, wa, wng, wsu, wmg)],
        out_specs=specs,
        out_shape=shapes,
        compiler_params=_cparams(("parallel", "parallel")),
        name="front_prompt",
    )(x2d, g, wa, wng, wsu, wmg)


def _chunk_rows(load, r0, rn, pitch=CMP_STRIDE):
    return jnp.concatenate([load(pl.ds(pitch * r0 + r, rn, stride=pitch)) for r in range(CMP_STRIDE)], axis=1)


def _compress_compute(load_rows, c, w1_ref, w2_ref, pe_ref, a_scr):
    rc_n = min(c, 256)
    lo = lax.broadcasted_iota(jnp.int32, (rc_n, LANES), 1) < HD
    w1 = w1_ref[...]
    for rc in range(c // rc_n):
        x = load_rows(rc * rc_n, rc_n)
        cols = [x[:, r * LANES:(r + 1) * LANES] for r in range(CMP_STRIDE)]
        rol = [pltpu.roll(col, HD, 1) for col in cols]
        for kh in range(N_KV):
            if kh == 0:
                parts = [jnp.where(lo, cols[2 * j], rol[2 * j + 1]) for j in range(CMP_STRIDE // 2)]
            else:
                parts = [jnp.where(lo, rol[2 * j], cols[2 * j + 1]) for j in range(CMP_STRIDE // 2)]
            xh = jnp.concatenate(parts, axis=1).astype(BF16)
            a_scr[kh, rc * rc_n:(rc + 1) * rc_n, :] = _dot(xh, w1)
    pw = _dot(pe_ref[...], w1)
    peb = pw[0:1, :CMP_HIDDEN] + pw[1:2, CMP_HIDDEN:]
    w2 = w2_ref[...]
    outs = []
    for kh in range(N_KV):
        a = a_scr[kh]
        hid = a[:, :CMP_HIDDEN] + pltpu.roll(a[:, CMP_HIDDEN:], c - 1, 0) + peb
        outs.append(_dot(_gelu(hid).astype(BF16), w2))
    return jnp.concatenate(outs, axis=1)


def _compress_prompt_kernel(xk_ref, xv_ref, w1k_ref, w2k_ref, pek_ref, w1v_ref, w2v_ref, pev_ref,
                            ok_ref, ovt_ref, a_scr):
    c = xk_ref.shape[1] // CMP_STRIDE
    ok_ref[0] = _compress_compute(lambda r0, rn: _chunk_rows(lambda idx: xk_ref[0, idx, :], r0, rn), c,
                                  w1k_ref, w2k_ref, pek_ref, a_scr).astype(BF16)
    ovt_ref[0] = _compress_compute(lambda r0, rn: _chunk_rows(lambda idx: xv_ref[0, idx, :], r0, rn), c,
                                   w1v_ref, w2v_ref, pev_ref, a_scr).T.astype(BF16)


def _compress_prompt(xk, xv, cw):
    nb, t, kvw = xk.shape
    c = t // CMP_STRIDE
    wspecs = [_const_spec(w.shape) for w in cw]
    blk = pl.BlockSpec((1, t, kvw), lambda b: (b, 0, 0))
    return pl.pallas_call(
        _compress_prompt_kernel,
        grid=(nb,),
        in_specs=[blk, blk] + wspecs,
        out_specs=[pl.BlockSpec((1, c, KVW), lambda b: (b, 0, 0)), pl.BlockSpec((1, KVW, c), lambda b: (b, 0, 0))],
        out_shape=[jax.ShapeDtypeStruct((nb, c, KVW), BF16), jax.ShapeDtypeStruct((nb, KVW, c), BF16)],
        scratch_shapes=[pltpu.VMEM((N_KV, c, 2 * CMP_HIDDEN), F32)],
        compiler_params=_cparams(("parallel",)),
        name="compress_prompt",
    )(xk, xv, *cw)


def _page_copy(pool, buf, sem, page, p, slot):
    return pltpu.make_async_copy(pool.at[page], buf.at[slot, p], sem)


def _page_gather_start(pt_ref, seq, pools, bufs, sems, slot, n_pages):
    def body(p, carry):
        page = pt_ref[seq, p]
        for i, (pool, buf) in enumerate(zip(pools, bufs)):
            _page_copy(pool, buf, sems.at[i, slot], page, p, slot).start()
        return carry
    lax.fori_loop(0, n_pages, body, 0)


def _page_gather_wait(pools, bufs, sems, slot, n_pages):
    def body(p, carry):
        for i, (pool, buf) in enumerate(zip(pools, bufs)):
            _page_copy(pool, buf, sems.at[i, slot], 0, p, slot).wait()
        return carry
    lax.fori_loop(0, n_pages, body, 0)


def _paged_prefetch(pt_ref, pools, bufs, sems, n_pages):
    s = pl.program_id(0)
    slot = s % 2

    @pl.when(s == 0)
    def _():
        _page_gather_start(pt_ref, 0, pools, bufs, sems, 0, n_pages)

    @pl.when(s + 1 < pl.num_programs(0))
    def _():
        _page_gather_start(pt_ref, s + 1, pools, bufs, sems, 1 - slot, n_pages)

    _page_gather_wait(pools, bufs, sems, slot, n_pages)
    return slot


def _compress_sample_kernel(pt_ref, kpool, vpool, w1k_ref, w2k_ref, pek_ref, w1v_ref, w2v_ref, pev_ref,
                            ok_ref, ov_ref, kbuf, vbuf, sems, a_scr, rows_scr):
    n_pages = pt_ref.shape[1]
    page = kpool.shape[2]
    c = n_pages * page // CMP_STRIDE
    slot = _paged_prefetch(pt_ref, (kpool, vpool), (kbuf, vbuf), sems, n_pages)
    for buf, out_ref, w1_ref, w2_ref, pe_ref in ((kbuf, ok_ref, w1k_ref, w2k_ref, pek_ref),
                                                (vbuf, ov_ref, w1v_ref, w2v_ref, pev_ref)):
        for p in range(n_pages):
            rows = buf[slot, p].T
            for ch in range(page // CMP_STRIDE):
                r0 = (p * (page // CMP_STRIDE) + ch) * CMP_PITCH
                rows_scr[r0:r0 + CMP_STRIDE, :] = rows[ch * CMP_STRIDE:(ch + 1) * CMP_STRIDE]
        out_ref[0] = _compress_compute(
            lambda r0, rn: _chunk_rows(lambda idx: rows_scr[idx, :], r0, rn, CMP_PITCH), c,
            w1_ref, w2_ref, pe_ref, a_scr).astype(BF16)


def _compress_sample(page_table, kpool, vpool, cw):
    ns, n_pages = page_table.shape
    width, page = kpool.shape[1:]
    tokens = n_pages * page
    c = tokens // CMP_STRIDE
    buf_shape = (2, n_pages, width, page)
    any_spec = pl.BlockSpec(memory_space=pl.ANY)
    wspecs = [pl.BlockSpec(w.shape, lambda s, pt, nd=w.ndim: (0,) * nd) for w in cw]
    oblk = pl.BlockSpec((1, c, KVW), lambda s, pt: (s, 0, 0))
    return pl.pallas_call(
        _compress_sample_kernel,
        grid_spec=pltpu.PrefetchScalarGridSpec(
            num_scalar_prefetch=1,
            grid=(ns,),
            in_specs=[any_spec, any_spec] + wspecs,
            out_specs=[oblk, oblk],
            scratch_shapes=[pltpu.VMEM(buf_shape, F32), pltpu.VMEM(buf_shape, F32), pltpu.SemaphoreType.DMA((2, 2)),
                            pltpu.VMEM((N_KV, c, 2 * CMP_HIDDEN), F32), pltpu.VMEM((c * CMP_PITCH, width), F32)]),
        out_shape=[jax.ShapeDtypeStruct((ns, c, KVW), BF16)] * 2,
        compiler_params=_cparams(("arbitrary",)),
        name="compress_sample",
    )(page_table, kpool, vpool, *cw)


def _rank_select(score, blk, n_real, axis):
    size = 8 if axis == 0 else LANES
    total = score.shape[axis]
    chunk = (lambda a, c: a[c * size:(c + 1) * size]) if axis == 0 else (lambda a, c: a[:, c * size:(c + 1) * size])
    n_chunks = -(-total // size)
    sc = [chunk(score, c) for c in range(n_chunks)]
    bl = [chunk(blk, c) for c in range(n_chunks)]
    rank = [jnp.zeros(s.shape, F32) for s in sc]
    for kk in range(n_real):
        col = score[kk:kk + 1, :] if axis == 0 else score[:, kk:kk + 1]
        for c in range(n_chunks):
            other = jnp.broadcast_to(col, sc[c].shape)
            if c * size > kk:
                beats = other >= sc[c]
            elif min((c + 1) * size, total) - 1 < kk:
                beats = other > sc[c]
            else:
                beats = (other > sc[c]) | ((other == sc[c]) & (bl[c] > kk))
            rank[c] = rank[c] + jnp.where(beats, 1.0, 0.0)
    return jnp.where(jnp.concatenate(rank, axis=axis) < SEL_TOPK, 1.0, 0.0)


def _block_scores(imp, blk, t):
    cur = t // SEL_BLOCK
    forced = (blk == 0) | (blk == cur) | (blk == cur - 1)
    valid = blk * SEL_BLOCK <= t
    return jnp.where(valid, jnp.where(forced, FORCE, imp), NEG)


def _nsa_prompt_kernel(qt_ref, ngt_ref, kc_ref, vct_ref, ks_ref, vst_ref, kw_ref, vwt_ref,
                       ut_ref, at_ref, ovlt_ref, stat_ref, crow_ref, o_ref, *, n_sel):
    ib = pl.program_id(1)
    qt = qt_ref[0]
    ngt = ngt_ref[0]
    ncp = kc_ref.shape[1]
    cols = GROUP * QB
    sel_rows = LANES // 2
    t_row = ib * QB + lax.broadcasted_iota(jnp.int32, (n_sel, QB), 1)
    blk_t = lax.broadcasted_iota(jnp.int32, (n_sel, QB), 0)
    zeros_q = jnp.zeros((HD, cols), F32)
    vrows = [slice(k * HD, (k + 1) * HD) for k in range(N_KV)]
    q_sel, q_win, o_c, o_w = [], [], [], []
    for k in range(N_KV):
        qk = jnp.concatenate([qt[(GROUP * k + g) * HD:(GROUP * k + g + 1) * HD, :] for g in range(GROUP)],
                             axis=1).astype(F32)
        qa = jnp.concatenate([qk, zeros_q] if k == 0 else [zeros_q, qk], axis=0)

        bias_c = ut_ref[k, pl.ds(pl.multiple_of(ncp - (QB // CMP_STRIDE) * ib, 8), ncp), :]
        p_c = _softmax2_cols(_dot(kc_ref[0], qa.astype(BF16)) + bias_c)
        o_c.append(_dot(vct_ref[0][vrows[k], :], p_c.astype(BF16)))
        psum = p_c[:, 0:QB]
        for g in range(1, GROUP):
            psum = psum + p_c[:, g * QB:(g + 1) * QB]
        psum_hi = psum.astype(BF16)
        psum_lo = (psum - psum_hi.astype(F32)).astype(BF16)
        imp = _dot(ovlt_ref[...], psum_hi) + _dot(ovlt_ref[...], psum_lo)

        sel = _rank_select(_block_scores(imp[0:n_sel], blk_t, t_row), blk_t, n_sel, 0)
        selm1 = jnp.concatenate([sel - 1.0] * GROUP, axis=1)
        if n_sel < sel_rows:
            selm1 = jnp.concatenate([selm1, jnp.zeros((sel_rows - n_sel, cols), F32)], axis=0)
        tail = jnp.concatenate([crow_ref[k], jnp.zeros((LANES - sel_rows - 8, cols), F32)], axis=0)
        q_sel.append(jnp.concatenate([qa, selm1, tail], axis=0).astype(BF16))
        q_win.append(jnp.concatenate([qa, jnp.zeros((sel_rows, cols), F32), tail], axis=0).astype(BF16))

        s_parts, tiles_j = [], []
        for w, tidx in enumerate((0, 1, None, None, 3)):
            jt = ib - w
            jc = jnp.maximum(jt, 0)
            k0 = pl.multiple_of(jc * QB, QB)
            lhs = jnp.concatenate([kw_ref[0, pl.ds(k0, QB), :], stat_ref[pl.ds(k0, QB), :]], axis=1)
            s = _dot(lhs, q_win[k])
            if tidx is not None:
                s = s + at_ref[tidx, k]
            s_parts.append(jnp.where(jt >= 0, s, NEG))
            tiles_j.append(jc)
        p_w = _softmax2_cols(jnp.concatenate(s_parts, axis=0)).astype(BF16)
        o_wk = jnp.zeros((HD, cols), F32)
        for w, jc in enumerate(tiles_j):
            o_wk = o_wk + _dot(vwt_ref[0, jc, vrows[k], :], p_w[w * QB:(w + 1) * QB])
        o_w.append(o_wk)

    ones_pad = 16
    acc_rows = HD + ones_pad

    def make_step(keys, base, near):
        tps = keys // QB
        ones_rows = jnp.where(lax.broadcasted_iota(jnp.int32, (ones_pad, keys), 0) == 0, 1.0, 0.0).astype(BF16)

        def tile_ids(jp, sp):
            j0 = base + (jp * SEL_SPLIT + sp) * tps
            return [j0 + h for h in range(tps)]

        def step(jp, state):
            scores = []
            for sp in range(SEL_SPLIT):
                k0 = pl.multiple_of(tile_ids(jp, sp)[0] * QB, keys)
                lhs = jnp.concatenate([ks_ref[0, pl.ds(k0, keys), :], stat_ref[pl.ds(k0, keys), :]], axis=1)
                for k in range(N_KV):
                    s = _dot(lhs, q_sel[k])
                    if near:
                        tidx = [jnp.where(jt == ib, 0, jnp.where(jt == ib - 1, 1, jnp.where(jt < ib, 2, 4)))
                                for jt in tile_ids(jp, sp)]
                        s = s + jnp.concatenate([at_ref[ti, k] for ti in tidx], axis=0)
                    scores.append(s)
            stats = []
            for (m, _), s in zip(state, scores):
                m_new = jnp.maximum(m, jnp.max(s, axis=0, keepdims=True))
                stats.append((m_new, jnp.exp2(m - m_new), jnp.exp2(s - m_new).astype(BF16)))
            vts = [jnp.concatenate([jnp.concatenate([vst_ref[0, jt, vrows[k], :] for jt in tile_ids(jp, sp)], axis=1),
                                    ones_rows], axis=0)
                   for sp in range(SEL_SPLIT) for k in range(N_KV)]
            return tuple((m_new, alpha * acc + _dot(vt, p))
                         for (_, acc), (m_new, alpha, p), vt in zip(state, stats, vts))
        return step

    far_tiles = SEL_SPLIT * SEL_KEYS // QB
    near_tiles = SEL_SPLIT * SEL_KEYS_NEAR // QB
    init = (jnp.full((1, cols), NEG, F32), jnp.zeros((acc_rows, cols), F32))
    n_far = jnp.maximum(ib - 1, 0) // far_tiles
    near_base = n_far * far_tiles
    n_near = (ib - near_base + near_tiles) // near_tiles
    sel_state = lax.fori_loop(0, n_far, make_step(SEL_KEYS, 0, False), (init,) * (N_KV * SEL_SPLIT))
    sel_state = lax.fori_loop(0, n_near, make_step(SEL_KEYS_NEAR, near_base, True), sel_state)

    out_rows = []
    for k in range(N_KV):
        parts = [sel_state[sp * N_KV + k] for sp in range(SEL_SPLIT)]
        m_s = parts[0][0]
        for m_p, _ in parts[1:]:
            m_s = jnp.maximum(m_s, m_p)
        acc_s = jnp.zeros(init[1].shape, F32)
        for m_p, acc_p in parts:
            acc_s = acc_s + jnp.exp2(m_p - m_s) * acc_p
        o_s = acc_s[0:HD] / jnp.maximum(acc_s[HD:HD + 1], 1e-30)

        def gate_row(br):
            return jnp.concatenate([ngt[(GROUP * k + g) * N_BRANCH + br:(GROUP * k + g) * N_BRANCH + br + 1, :]
                                    for g in range(GROUP)], axis=1)
        o_k = gate_row(0) * o_c[k] + gate_row(1) * o_s + gate_row(2) * o_w[k]
        out_rows += [o_k[:, g * QB:(g + 1) * QB] for g in range(GROUP)]
    o_ref[0] = jnp.concatenate(out_rows, axis=0).T.astype(BF16)


def _nsa_prompt(qt, ngt, kc, vct, ks, vst, kw, vwt, tables):
    nb, aw, t = qt.shape
    nq = t // QB
    ncp = kc.shape[1]
    full3 = lambda b, i: (b, 0, 0)
    full4 = lambda b, i: (b, 0, 0, 0)
    return pl.pallas_call(
        functools.partial(_nsa_prompt_kernel, n_sel=t // SEL_BLOCK),
        grid=(nb, nq),
        in_specs=[pl.BlockSpec((1, aw, QB), lambda b, i: (b, 0, i)),
                  pl.BlockSpec((1, NG_ROWS, QB), lambda b, i: (b, 0, i)),
                  pl.BlockSpec((1, ncp, KVW), full3), pl.BlockSpec((1, KVW, ncp), full3),
                  pl.BlockSpec((1, t, KVW), full3), pl.BlockSpec((1, nq, KVW, QB), full4),
                  pl.BlockSpec((1, t, KVW), full3), pl.BlockSpec((1, nq, KVW, QB), full4)]
                 + [_const_spec(a.shape) for a in tables],
        out_specs=pl.BlockSpec((1, QB, aw), lambda b, i: (b, i, 0)),
        out_shape=jax.ShapeDtypeStruct((nb, t, aw), BF16),
        compiler_params=_cparams(("parallel", "arbitrary")),
        name="nsa_prompt",
    )(qt, ngt, kc, vct, ks, vst, kw, vwt, *tables)


def _nsa_sample_kernel(pt_ref, q_ref, gate_ref, kc_ref, vc_ref, kpool, vpool, ksn_ref, vsn_ref,
                       kwin_ref, vwin_ref, kwn_ref, vwn_ref, bc_ref, bs_ref, bw_ref, ovl_ref, e_ref,
                       o_ref, kwo_ref, vwo_ref, kbuf, vbuf, sems, *, n_sel, past, t_new):
    n_pages = pt_ref.shape[1]
    wb = kwin_ref.shape[2]
    slot = _paged_prefetch(pt_ref, (kpool, vpool), (kbuf, vbuf), sems, n_pages)
    past_t = lambda buf: jnp.concatenate([buf[slot, p] for p in range(n_pages)], axis=1).astype(BF16)
    q = q_ref[0]
    nb_past = past // SEL_BLOCK
    pad_new = jnp.zeros((QB - ksn_ref.shape[1], KVW), F32)
    new_tile = lambda ref: jnp.concatenate([ref[0], pad_new], axis=0).astype(BF16)

    p_c = _masked_softmax(_dot_t(q, kc_ref[0]) + bc_ref[...])
    o_c = _dot(p_c.astype(BF16), vc_ref[0])
    parts = []
    for k in range(N_KV):
        base = k * GROUP * t_new
        ps = p_c[base:base + t_new]
        for g in range(1, GROUP):
            ps = ps + p_c[base + g * t_new:base + (g + 1) * t_new]
        parts.append(ps)
    psum = jnp.concatenate(parts, axis=0)
    psum_hi = psum.astype(BF16)
    psum_lo = (psum - psum_hi.astype(F32)).astype(BF16)
    imp = _dot(psum_hi, ovl_ref[...]) + _dot(psum_lo, ovl_ref[...])
    blk = lax.broadcasted_iota(jnp.int32, imp.shape, 1)
    tpos = past + lax.broadcasted_iota(jnp.int32, imp.shape, 0) % t_new
    sel = _rank_select(_block_scores(imp, blk, tpos), blk, n_sel, 1)
    sel = jnp.concatenate([sel[k * t_new:(k + 1) * t_new] for k in range(N_KV) for _ in range(GROUP)], axis=0)

    mask_add = _dot((sel[:, 0:LANES] - 1.0).astype(BF16), e_ref[...])
    s_past = _dot(q, past_t(kbuf)) + bs_ref[:, 0:past] + mask_add
    s_new = _dot_t(q, new_tile(ksn_ref)) + bs_ref[:, past:]
    s_new = jnp.where(sel[:, nb_past:nb_past + 1] > 0.5, s_new, NEG)
    p_s = _masked_softmax(jnp.concatenate([s_past, s_new], axis=1)).astype(BF16)
    o_s = _dot_t(p_s[:, 0:past], past_t(vbuf)) + _dot(p_s[:, past:], new_tile(vsn_ref))

    s_w = jnp.concatenate([_dot(q, kwin_ref[0].astype(BF16)), _dot_t(q, new_tile(kwn_ref))], axis=1)
    p_w = _masked_softmax(s_w + bw_ref[...]).astype(BF16)
    o_w = _dot_t(p_w[:, 0:wb], vwin_ref[0].astype(BF16)) + _dot(p_w[:, wb:], new_tile(vwn_ref))

    gate = gate_ref[0]
    o_ref[0] = gate[:, 0:1] * o_c + gate[:, 1:2] * o_s + gate[:, 2:3] * o_w

    lane = lax.broadcasted_iota(jnp.int32, (KVW, wb), 1)
    for win_ref, new_ref, out_ref in ((kwin_ref, kwn_ref, kwo_ref), (vwin_ref, vwn_ref, vwo_ref)):
        new_t = jnp.concatenate([new_ref[0], pad_new], axis=0).T
        tail = pltpu.roll(jnp.concatenate([jnp.zeros((KVW, wb - QB), F32), new_t], axis=1), QB - t_new, 1)
        out_ref[0] = jnp.where(lane >= wb - t_new, tail, pltpu.roll(win_ref[0], wb - t_new, 1))


def _nsa_sample(page_table, q, gate, kc, vc, kpool, vpool, ksn, vsn, kwin, vwin, kwn, vwn,
                bc, bs, bw, ovl, e, n_sel, past, t_new):
    ns, n_pages = page_table.shape
    rows, kvw = q.shape[1:]
    buf_shape = (2, n_pages) + kpool.shape[1:]
    seq3 = lambda s, pt: (s, 0, 0)
    any_spec = pl.BlockSpec(memory_space=pl.ANY)
    cs = lambda a: pl.BlockSpec(a.shape, lambda s, pt, nd=a.ndim: (0,) * nd, pipeline_mode=pl.Buffered(1))
    per_seq = lambda a: pl.BlockSpec((1,) + a.shape[1:], seq3)
    return pl.pallas_call(
        functools.partial(_nsa_sample_kernel, n_sel=n_sel, past=past, t_new=t_new),
        grid_spec=pltpu.PrefetchScalarGridSpec(
            num_scalar_prefetch=1,
            grid=(ns,),
            in_specs=[per_seq(q), per_seq(gate), per_seq(kc), per_seq(vc), any_spec, any_spec,
                      per_seq(ksn), per_seq(vsn), per_seq(kwin), per_seq(vwin), per_seq(kwn), per_seq(vwn),
                      cs(bc), cs(bs), cs(bw), cs(ovl), cs(e)],
            out_specs=[pl.BlockSpec((1, rows, kvw), seq3), per_seq(kwin), per_seq(vwin)],
            scratch_shapes=[pltpu.VMEM(buf_shape, F32), pltpu.VMEM(buf_shape, F32), pltpu.SemaphoreType.DMA((2, 2))]),
        out_shape=[jax.ShapeDtypeStruct((ns, rows, kvw), F32), jax.ShapeDtypeStruct(kwin.shape, F32),
                   jax.ShapeDtypeStruct(vwin.shape, F32)],
        compiler_params=_cparams(("arbitrary",)),
        name="nsa_sample",
    )(page_table, q, gate, kc, vc, kpool, vpool, ksn, vsn, kwin, vwin, kwn, vwn, bc, bs, bw, ovl, e)


def _ssm_param_kernel(ar_ref, ai_ref, ldt_ref, br_ref, bi_ref, abr_ref, abi_ref, bbr_ref, bbi_ref):
    ar = ar_ref[...]
    ai = ai_ref[...]
    dt = jnp.exp(ldt_ref[...])
    mag = jnp.exp(ar * dt)
    abr = mag * jnp.cos(ai * dt)
    abi = mag * jnp.sin(ai * dt)
    den = ar * ar + ai * ai
    nr, ni = abr - 1.0, abi
    fr = (nr * ar + ni * ai) / den
    fi = (ni * ar - nr * ai) / den
    abr_ref[...] = abr
    abi_ref[...] = abi
    for g in range(ar.shape[0]):
        br = br_ref[g]
        bi = bi_ref[g]
        frg = fr[g:g + 1, :]
        fig = fi[g:g + 1, :]
        bbr_ref[g] = frg * br - fig * bi
        bbi_ref[g] = frg * bi + fig * br


def _ssm_params(a_re, a_im, log_dt, b_re_t, b_im_t):
    g, p = a_re.shape
    return pl.pallas_call(
        _ssm_param_kernel,
        out_shape=[jax.ShapeDtypeStruct((g, p), F32)] * 2 + [jax.ShapeDtypeStruct(b_re_t.shape, F32)] * 2,
        name="ssm_params",
    )(a_re, a_im, log_dt.reshape(g, 1), b_re_t, b_im_t)


def _ssm_kernel(u_ref, h0r_ref, h0i_ref, ar_ref, ai_ref, bd_ref, cd_ref, d_ref, wglu_ref, bglu_ref,
                so_ref, hr_ref, hi_ref, xr_scr, xi_scr, *slab_scr, bt):
    i = pl.program_id(0)
    n_slab = bd_ref.shape[0]
    width = n_slab * LANES
    sw = bd_ref.shape[2] // 2
    if slab_scr:
        slab = slab_scr[0]
        tt = u_ref.shape[0]
        rows = tt * bt
        u_wide = u_ref[...].astype(F32)
        for b in range(bt):
            for sl in range(n_slab):
                lanes = slice(b * width + sl * LANES, b * width + (sl + 1) * LANES)
                slab[sl, pl.ds(b, tt, stride=bt), :] = u_wide[:, lanes]
        u_slabs = [slab[sl] for sl in range(n_slab)]
    else:
        rows = u_ref.shape[0]
        u_rows = u_ref[...].astype(F32)
        u_slabs = [u_rows[:, sl * LANES:(sl + 1) * LANES] for sl in range(n_slab)]

    @pl.when(i == 0)
    def _():
        hr_ref[...] = h0r_ref[...]
        hi_ref[...] = h0i_ref[...]

    for sl in range(n_slab):
        x = _dot(u_slabs[sl].astype(BF16), bd_ref[sl])
        xr_scr[:, sl * sw:(sl + 1) * sw] = x[:, :sw]
        xi_scr[:, sl * sw:(sl + 1) * sw] = x[:, sw:]

    per = 8 // math.gcd(bt, 8)
    grp = per * bt
    lc = 512
    for c0 in range(0, xr_scr.shape[1], lc):
        cl = slice(c0, c0 + lc)
        a_r = jnp.broadcast_to(ar_ref[:, cl], (bt, lc))
        a_i = jnp.broadcast_to(ai_ref[:, cl], (bt, lc))

        def step(j, carry):
            h_r, h_i = carry
            r0 = pl.multiple_of(j * grp, grp)
            xr = xr_scr[pl.ds(r0, grp), cl]
            xi = xi_scr[pl.ds(r0, grp), cl]
            out_r, out_i = [], []
            for s in range(per):
                n_r = a_r * h_r - a_i * h_i + xr[s * bt:(s + 1) * bt]
                n_i = a_r * h_i + a_i * h_r + xi[s * bt:(s + 1) * bt]
                h_r, h_i = n_r, n_i
                out_r.append(h_r)
                out_i.append(h_i)
            xr_scr[pl.ds(r0, grp), cl] = jnp.concatenate(out_r, axis=0) if per > 1 else out_r[0]
            xi_scr[pl.ds(r0, grp), cl] = jnp.concatenate(out_i, axis=0) if per > 1 else out_i[0]
            return h_r, h_i

        h_r, h_i = lax.fori_loop(0, rows // grp, step, (hr_ref[:, cl], hi_ref[:, cl]))
        hr_ref[:, cl] = h_r
        hi_ref[:, cl] = h_i

    ys = []
    for sl in range(n_slab):
        hcat = jnp.concatenate([xr_scr[:, sl * sw:(sl + 1) * sw], xi_scr[:, sl * sw:(sl + 1) * sw]], axis=1)
        ys.append(_dot(hcat.astype(BF16), cd_ref[sl]))
    y = jnp.concatenate(ys, axis=1) + d_ref[...] * jnp.concatenate(u_slabs, axis=1)
    z = _gelu(y)
    so = z * _sigmoid(_dot(z.astype(BF16), wglu_ref[...]) + bglu_ref[...])
    if slab_scr:
        for sl in range(n_slab):
            slab[sl] = so[:, sl * LANES:(sl + 1) * LANES]
        for b in range(bt):
            for sl in range(n_slab):
                lanes = slice(b * width + sl * LANES, b * width + (sl + 1) * LANES)
                so_ref[:, lanes] = slab[sl, pl.ds(b, tt, stride=bt), :].astype(so_ref.dtype)
    else:
        so_ref[...] = so.astype(so_ref.dtype)


def _ssm(u, h0r, h0i, ar, ai, bd, cd, dvec, wglu, bglu, bt, tt, time_major):
    width = bd.shape[0] * LANES
    rows = tt * bt
    nstate = ar.shape[1]
    blk = (tt, bt * width) if time_major else (rows, width)
    cst = [_const_spec(a.shape) for a in (h0r, h0i, ar, ai, bd, cd, dvec, wglu, bglu)]
    st_spec = _const_spec((bt, nstate))
    scratch = [pltpu.VMEM((rows, nstate), F32), pltpu.VMEM((rows, nstate), F32)]
    if time_major:
        scratch.append(pltpu.VMEM((bd.shape[0], rows, LANES), F32))
    return pl.pallas_call(
        functools.partial(_ssm_kernel, bt=bt),
        grid=(u.shape[0] // blk[0],),
        in_specs=[pl.BlockSpec(blk, lambda i: (i, 0))] + cst,
        out_specs=[pl.BlockSpec(blk, lambda i: (i, 0)), st_spec, st_spec],
        out_shape=[jax.ShapeDtypeStruct(u.shape, BF16), jax.ShapeDtypeStruct((bt, nstate), F32),
                   jax.ShapeDtypeStruct((bt, nstate), F32)],
        scratch_shapes=scratch,
        compiler_params=_cparams(("arbitrary",)),
        name="ssm",
    )(u, h0r, h0i, ar, ai, bd, cd, dvec, wglu, bglu)


def _back_kernel(h_ref, o_ref, so_ref, mg_ref, p_ref, watt_ref, wssm_ref, wo_ref, fn_ref, wg_ref, wu_ref, wd_ref,
                 pn_ref, wpg_ref, wple_ref, fin_ref, y_ref):
    d = h_ref.shape[1]
    a = _dot(o_ref[...], watt_ref[...])
    s = _dot(so_ref[...], wssm_ref[...])
    mg = mg_ref[...].astype(F32)
    h = h_ref[...] + _dot((mg[:, :d] * a + mg[:, d:] * s).astype(BF16), wo_ref[...])
    f = _rms(h, fn_ref[...]).astype(BF16)
    gate = _dot(f, wg_ref[...])
    up = _dot(f, wu_ref[...])
    h = h + _dot((gate * _sigmoid(gate) * up).astype(BF16), wd_ref[...])
    g = _sigmoid(_dot(_rms(h, pn_ref[...]).astype(BF16), wpg_ref[...]))
    h = h + g * _dot(p_ref[...].astype(BF16), wple_ref[...])
    y_ref[...] = _rms(h, fin_ref[...])


def _back(h2d, o2d, so_tb, mg, p2d, weights, nb, t, tm):
    n, d = h2d.shape
    nt = t // tm
    row = lambda b, i: (b * nt + i, 0)
    wspecs = [pl.BlockSpec(w.shape, lambda b, i, nd=w.ndim: (0,) * nd, pipeline_mode=pl.Buffered(1))
              for w in weights]
    sw = o2d.shape[1]
    return pl.pallas_call(
        _back_kernel,
        grid=(nb, nt),
        in_specs=[pl.BlockSpec((tm, d), row), pl.BlockSpec((tm, sw), row),
                  pl.BlockSpec((tm, sw), lambda b, i: (i, b)),
                  pl.BlockSpec((tm, mg.shape[1]), row), pl.BlockSpec((tm, p2d.shape[1]), row)] + wspecs,
        out_specs=pl.BlockSpec((tm, d), row),
        out_shape=jax.ShapeDtypeStruct((n, d), F32),
        compiler_params=_cparams(("parallel", "parallel")),
        name="back",
    )(h2d, o2d, so_tb, mg, p2d, *weights)


def _bucket_np(dist):
    n = np.maximum(dist, 0)
    exact = N_BUCKETS // 2
    nf = np.maximum(n, 1).astype(np.float64)
    large = exact + (np.log(nf / exact) / math.log(REL_MAX_DIST / exact) * (N_BUCKETS - exact)).astype(np.int64)
    return np.where(n < exact, n, np.minimum(large, N_BUCKETS - 1)).astype(np.int32)


def _bias_table(rel_bias, dist, valid, offset=None):
    onehot = jax.nn.one_hot(jnp.asarray(_bucket_np(dist)), N_BUCKETS, dtype=F32)
    b = jnp.einsum('...b,bh->h...', onehot, rel_bias.astype(F32), precision=lax.Precision.HIGHEST)
    if offset is not None:
        b = b - offset.reshape((N_HEADS,) + (1,) * dist.ndim)
    return jnp.where(jnp.asarray(valid)[None], b, NEG)


def _prompt_tables(rel_bias, t):
    def cols(b):
        r = b.shape[1]
        return b.reshape(N_KV, GROUP, r, QB).transpose(0, 2, 1, 3).reshape(N_KV, r, GROUP * QB)
    rel_bias = rel_bias.astype(F32) * LOG2E
    c = rel_bias[N_BUCKETS - 1]
    c_hi = c.astype(BF16)
    c_lo = (c - c_hi.astype(F32)).astype(BF16)
    c_eff = c_hi.astype(F32) + c_lo.astype(F32)
    crow = jnp.stack([c_hi.astype(F32), c_lo.astype(F32)] + [jnp.zeros_like(c)] * 6, axis=1)
    crow = jnp.broadcast_to(crow[:, :, None], (N_HEADS, 8, QB))
    crow = cols(crow)
    j = np.arange(QB)[:, None]
    i = np.arange(QB)[None, :]
    ones = np.ones((QB, QB), bool)
    zeros = jnp.zeros((N_KV, QB, GROUP * QB), F32)
    at = jnp.stack([
        cols(_bias_table(rel_bias, i - j, i >= j, c_eff)),
        cols(_bias_table(rel_bias, QB + i - j, ones, c_eff)),
        zeros,
        jnp.where(jnp.asarray(np.tile(j > i, (1, GROUP)))[None], zeros, NEG),
        zeros + NEG,
    ])
    ncp = t // CMP_STRIDE
    m = np.arange(2 * ncp)[:, None] - ncp
    dist = i - CMP_STRIDE * m - (CMP_BLOCK - 1)
    ut = cols(_bias_table(rel_bias, dist, dist >= 0))
    n_sel = t // SEL_BLOCK
    n = np.arange(ncp)[None, :]
    jb = np.arange(LANES)[:, None]
    ovlt = ((n * CMP_STRIDE < jb * SEL_BLOCK + SEL_BLOCK) & (n * CMP_STRIDE + CMP_BLOCK - 1 >= jb * SEL_BLOCK)
            & (jb < n_sel) & (n < ncp - 1))
    key = np.arange(t)[:, None]
    lane = np.arange(LANES)[None, :]
    stat = np.where(lane < LANES // 2, (lane == key // SEL_BLOCK) * BIG,
                    ((lane == LANES // 2) | (lane == LANES // 2 + 1)) * 1.0).astype(np.float32)
    return ut, at, jnp.asarray(ovlt.astype(np.float32), BF16), jnp.asarray(stat, BF16), crow


def _sample_tables(rel_bias, past, t_new, win_buf):
    def rows(b):
        return b.reshape(N_HEADS * t_new, b.shape[-1])
    tok = np.arange(t_new)[:, None]
    nc = past // CMP_STRIDE
    n = np.arange(nc)[None, :]
    c_end = n * CMP_STRIDE + CMP_BLOCK - 1
    n_cmp = (past + t_new) // CMP_STRIDE - 1
    bc = rows(_bias_table(rel_bias, past + tok - c_end, (c_end <= past + tok) & (n < n_cmp)))
    js = np.arange(past + QB)[None, :]
    ds = np.where(js < past, past + tok - js, tok - (js - past))
    bs = rows(_bias_table(rel_bias, ds, np.where(js < past, True, (ds >= 0) & (js - past < t_new))))
    jw = np.arange(win_buf + QB)[None, :]
    dw = np.where(jw < win_buf, win_buf + tok - jw, tok - (jw - win_buf))
    valid = np.where(jw < win_buf, (dw >= 0) & (dw < WINDOW), (dw >= 0) & (jw - win_buf < t_new))
    bw = rows(_bias_table(rel_bias, dw, valid))
    n_sel = -(-(past + t_new) // SEL_BLOCK)
    nbp = 2 * LANES
    nn = np.arange(nc)[:, None]
    jb = np.arange(nbp)[None, :]
    ovl = ((nn * CMP_STRIDE < jb * SEL_BLOCK + SEL_BLOCK) & (nn * CMP_STRIDE + CMP_BLOCK - 1 >= jb * SEL_BLOCK)
           & (jb < n_sel) & (nn < n_cmp))
    e = (np.arange(LANES)[:, None] == np.arange(past)[None, :] // SEL_BLOCK).astype(np.float32) * BIG
    return bc, bs, bw, jnp.asarray(ovl.astype(np.float32), BF16), jnp.asarray(e, BF16), n_sel


def _slab_diag(blocks, n_slab):
    g, r, c = blocks.shape
    gps = g // n_slab
    eye = jnp.eye(gps, dtype=blocks.dtype)
    return jnp.einsum('sgrc,gh->sgrhc', blocks.reshape(n_slab, gps, r, c), eye).reshape(n_slab, gps * r, gps * c)


def _layer_params(rel_bias, final_norm, attn_norm, w_in, cmp_pe_k, cmp_w1_k, cmp_w2_k, cmp_pe_v, cmp_w1_v, cmp_w2_v,
                  ssm_a_re, ssm_a_im, ssm_log_dt, ssm_b_re, ssm_b_im, ssm_c_re, ssm_c_im, ssm_d, w_glu, b_glu,
                  w_att_br, w_ssm_br, w_o, ffn_norm, w_ffn_gate, w_ffn_up, w_ffn_down, ple_norm, w_ple_gate, w_ple):
    l = 0
    d = w_in.shape[1]
    n_groups = ssm_a_re.shape[1]
    ssm_w = n_groups * SSM_CH
    nstate = n_groups * SSM_P
    assert ssm_w % LANES == 0
    w = w_in[l]
    c0 = AW + 6 * KVW
    n_gate = N_HEADS * N_BRANCH
    front_w = (attn_norm[l].reshape(1, d), w[:, :c0].astype(BF16),
               jnp.pad(w[:, c0:c0 + n_gate], ((0, 0), (0, LANES - n_gate))).astype(BF16),
               w[:, c0 + n_gate:c0 + n_gate + ssm_w].astype(BF16), w[:, c0 + n_gate + ssm_w:].astype(BF16))

    def cmp_weights(pe, w1, w2):
        half = CMP_STRIDE * HD
        w1cat = jnp.concatenate([w1[:half], w1[half:]], axis=1).astype(BF16)
        pe2 = jnp.pad(pe.reshape(2, half), ((0, 6), (0, 0))).astype(BF16)
        return w1cat, w2.astype(BF16), pe2
    cw = cmp_weights(cmp_pe_k[l], cmp_w1_k[l], cmp_w2_k[l]) + cmp_weights(cmp_pe_v[l], cmp_w1_v[l], cmp_w2_v[l])

    abr, abi, bbr_t, bbi_t = _ssm_params(ssm_a_re[l], ssm_a_im[l], ssm_log_dt[l],
                                         jnp.swapaxes(ssm_b_re[l], 1, 2), jnp.swapaxes(ssm_b_im[l], 1, 2))
    n_slab = ssm_w // LANES
    bd = jnp.concatenate([_slab_diag(bbr_t, n_slab), _slab_diag(bbi_t, n_slab)], axis=2).astype(BF16)
    cd = jnp.concatenate([_slab_diag(jnp.swapaxes(ssm_c_re[l], 1, 2), n_slab),
                          -_slab_diag(jnp.swapaxes(ssm_c_im[l], 1, 2), n_slab)], axis=1).astype(BF16)
    ssm_p = (abr.reshape(1, nstate), abi.reshape(1, nstate), bd, cd, ssm_d[l].reshape(1, ssm_w),
             w_glu[l].astype(BF16), b_glu[l].reshape(1, ssm_w))

    back_w = (w_att_br[l].astype(BF16), w_ssm_br[l].astype(BF16), w_o[l].astype(BF16),
              ffn_norm[l].reshape(1, d), w_ffn_gate[l].astype(BF16), w_ffn_up[l].astype(BF16),
              w_ffn_down[l].astype(BF16), ple_norm[l].reshape(1, d), w_ple_gate[l].astype(BF16),
              w_ple[l].astype(BF16), final_norm.reshape(1, d))
    return dict(front=front_w, cmp=cw, ssm=ssm_p, back=back_w, rel_bias=rel_bias,
                n_groups=n_groups, ssm_w=ssm_w, nstate=nstate, n_gate=n_gate)


def _prompt_group(x_prompt, p_l, prm):
    nb, t, d = x_prompt.shape
    ssm_w, nstate = prm["ssm_w"], prm["nstate"]
    assert t % (CMP_STRIDE * LANES) == 0 and t >= WINDOW and t // SEL_BLOCK <= LANES // 2
    xp = x_prompt.reshape(nb * t, d)
    (qt, kct, vct, kst, vst32, kc, vc, kw, vw, ksb, kwb, vst, vwt, ngt, su, mg) = _front_prompt(
        xp, nb, t, prm["front"], min(FRONT_ROWS, t))
    kcc, vcct = _compress_prompt(kc.reshape(nb, t, KVW), vc.reshape(nb, t, KVW), prm["cmp"])
    o = _nsa_prompt(qt, ngt, kcc, vcct, ksb.reshape(nb, t, KVW), vst, kwb.reshape(nb, t, KVW), vwt,
                    _prompt_tables(prm["rel_bias"], t))
    zeros_state = jnp.zeros((nb, nstate), F32)
    so, sr, si = _ssm(su, zeros_state, zeros_state, *prm["ssm"], nb, SSM_STEPS, True)
    y = _back(xp, o.reshape(nb * t, AW), so, mg, p_l.reshape(nb * t, -1),
              prm["back"], nb, t, BACK_ROWS)
    return dict(y=y, o=o, so=so, rows_t=(kct, vct, kst, vst32), win=(kw, vw), state=(sr, si))


def _sample_group(x_sample, p_l, page_table, pools, wins, states, prm):
    ns, t_new, d = x_sample.shape
    tok_minor = lambda a: jnp.transpose(a, (0, 2, 3, 1)).reshape(a.shape[0], KVW, a.shape[1])
    k_cmp, v_cmp, k_sel, v_sel = pools
    k_win, v_win = wins
    n_phys, page = k_cmp.shape[:2]
    past = page_table.shape[1] * page
    win_buf = k_win.shape[1]
    ssm_w, nstate, n_gate = prm["ssm_w"], prm["nstate"], prm["n_gate"]
    assert page == QB and past % (CMP_STRIDE * LANES) == 0 and past // SEL_BLOCK <= LANES
    assert win_buf == WINDOW and past >= win_buf and t_new <= 8 and t_new < CMP_STRIDE
    n_s = ns * t_new
    xs = x_sample.reshape(n_s, d)
    q_s, kc_s, vc_s, ks_s, vs_s, kw_s, vw_s, ng_s, su_s, mg_s = _front_sample(xs, prm["front"])
    kcc_s, vcc_s = _compress_sample(page_table, tok_minor(k_cmp), tok_minor(v_cmp), prm["cmp"])
    bc, bs, bw, ovl_s, e_s, n_sel_s = _sample_tables(prm["rel_bias"], past, t_new, win_buf)
    rows_s = N_HEADS * t_new
    eye_kv = jnp.eye(N_KV, dtype=BF16)
    q_rows = q_s.reshape(ns, t_new, N_KV, GROUP, HD).transpose(0, 2, 3, 1, 4)
    q_rows = jnp.einsum('skgtd,kj->skgtjd', q_rows, eye_kv).reshape(ns, rows_s, KVW)
    gate_s = ng_s[:, :n_gate].reshape(ns, t_new, N_KV, GROUP, N_BRANCH).transpose(0, 2, 3, 1, 4)
    gate_s = jnp.pad(gate_s.reshape(ns, rows_s, N_BRANCH), ((0, 0), (0, 0), (0, LANES - N_BRANCH)))
    pad8 = lambda a: jnp.pad(a.reshape(ns, t_new, KVW), ((0, 0), (0, 8 - t_new), (0, 0)))
    o_s, kwin_new, vwin_new = _nsa_sample(
        page_table, q_rows, gate_s, kcc_s, vcc_s,
        tok_minor(k_sel), tok_minor(v_sel), pad8(ks_s), pad8(vs_s), tok_minor(k_win), tok_minor(v_win),
        pad8(kw_s), pad8(vw_s), bc, bs, bw, ovl_s, e_s, n_sel_s, past, t_new)
    o_s = o_s.reshape(ns, N_KV, GROUP, t_new, N_KV, HD)
    o_s = jnp.stack([o_s[:, k, :, :, k, :] for k in range(N_KV)], axis=1)
    o_s = o_s.transpose(0, 3, 1, 2, 4).reshape(n_s, AW).astype(BF16)
    su_ts = su_s.reshape(ns, t_new, ssm_w).transpose(1, 0, 2).reshape(n_s, ssm_w)
    so_ts, sr, si = _ssm(su_ts, states[0].reshape(ns, nstate), states[1].reshape(ns, nstate), *prm["ssm"], ns, t_new,
                         False)
    so_s = so_ts.reshape(t_new, ns, ssm_w).transpose(1, 0, 2).reshape(n_s, ssm_w)
    y = _back(xs, o_s, so_s, mg_s, p_l.reshape(n_s, -1), prm["back"], 1, n_s, n_s)
    return dict(y=y, o=o_s, so=so_s, rows=(kc_s, vc_s, ks_s, vs_s), win_t=(kwin_new, vwin_new), state=(sr, si))


def kernel(x_prompt, x_sample, cache_k_cmp, cache_v_cmp, cache_k_sel, cache_v_sel, cache_k_win, cache_v_win, state_ssm_re, state_ssm_im, page_table, p_prompt, p_sample, rel_bias, final_norm, attn_norm, w_in, cmp_pe_k, cmp_w1_k, cmp_w2_k, cmp_pe_v, cmp_w1_v, cmp_w2_v, ssm_a_re, ssm_a_im, ssm_log_dt, ssm_b_re, ssm_b_im, ssm_c_re, ssm_c_im, ssm_d, w_glu, b_glu, w_att_br, w_ssm_br, w_o, ffn_norm, w_ffn_gate, w_ffn_up, w_ffn_down, ple_norm, w_ple_gate, w_ple):
    assert w_in.shape[0] == 1, "single-layer trunk"
    l = 0
    nb, t, d = x_prompt.shape
    ns, t_new = x_sample.shape[:2]
    prm = _layer_params(rel_bias, final_norm, attn_norm, w_in, cmp_pe_k, cmp_w1_k, cmp_w2_k, cmp_pe_v, cmp_w1_v,
                        cmp_w2_v, ssm_a_re, ssm_a_im, ssm_log_dt, ssm_b_re, ssm_b_im, ssm_c_re, ssm_c_im, ssm_d,
                        w_glu, b_glu, w_att_br, w_ssm_br, w_o, ffn_norm, w_ffn_gate, w_ffn_up, w_ffn_down,
                        ple_norm, w_ple_gate, w_ple)
    pg = _prompt_group(x_prompt, p_prompt[l], prm)
    sg = _sample_group(x_sample, p_sample[l], page_table,
                       (cache_k_cmp[l], cache_v_cmp[l], cache_k_sel[l], cache_v_sel[l]),
                       (cache_k_win[l], cache_v_win[l]), (state_ssm_re[l], state_ssm_im[l]), prm)

    kv5 = lambda a, b_, t_: a.reshape(1, b_, t_, N_KV, HD)
    kv5_t = lambda a: jnp.transpose(a.reshape(1, nb, N_KV, HD, t), (0, 1, 4, 2, 3))
    keep = min(WINDOW, t)
    win_p = lambda a: a.reshape(nb, t, KVW)[:, t - keep:].reshape(1, nb, keep, N_KV, HD)
    win_s = lambda a: jnp.transpose(a.reshape(1, ns, N_KV, HD, a.shape[-1]), (0, 1, 4, 2, 3))
    st = lambda a, b_: a.reshape(1, b_, prm["n_groups"], SSM_P)
    kc_s, vc_s, ks_s, vs_s = sg["rows"]
    return (pg["y"].reshape(nb, t, d), sg["y"].reshape(ns, t_new, d),
            *[kv5_t(a) for a in pg["rows_t"]], win_p(pg["win"][0]), win_p(pg["win"][1]),
            st(pg["state"][0], nb), st(pg["state"][1], nb),
            kv5(kc_s, ns, t_new), kv5(vc_s, ns, t_new), kv5(ks_s, ns, t_new), kv5(vs_s, ns, t_new),
            win_s(sg["win_t"][0]), win_s(sg["win_t"][1]),
            st(sg["state"][0], ns), st(sg["state"][1], ns))
```
